```python
import math
import jax, jax.numpy as jnp
from jax import lax
import numpy as np

D_MODEL = 1024
BATCH = 16
SEQ = 256
DEPTH = 4
DEC_BATCH = 2
DEC_SEQ = 2048
PAST_LEN = 512

GRID_W = 64
N_MIXERS = 2
N_SSM_LAYERS = (DEPTH + 1) // 2
N_FOURIER_LAYERS = DEPTH // 2
S5_GROUP_CH = 16
S5_GROUPS = D_MODEL // S5_GROUP_CH
S5_STATE = 64
FOURIER_GROUPS = 4
FOURIER_GROUP_CH = D_MODEL // FOURIER_GROUPS
D_FF = 2816
CONV_W = 3
EPS = 1e-6
DT_MIN = 1e-3
DT_MAX = 1e-1

kernel_name = "s5_fnet_convffn_diffusion_step"


def rms_norm(x, g):
    xf = x.astype(jnp.float32)
    y = xf * lax.rsqrt(jnp.mean(xf * xf, axis=-1, keepdims=True) + EPS)
    return (y * g.astype(jnp.float32)).astype(x.dtype)


def _linear_recurrence(e1, e2):
    a1, b1 = e1
    a2, b2 = e2
    return a1 * a2, a2 * b1 + b2


def s5_direction(u, lam_re, lam_im, log_dt, b_re, b_im, c_re, c_im, h0, reverse):
    lam = lax.complex(lam_re.astype(jnp.float32), lam_im.astype(jnp.float32))
    dt = jnp.exp(log_dt.astype(jnp.float32))[:, None]
    abar = jnp.exp(lam * dt)
    bmat = lax.complex(b_re.astype(jnp.float32), b_im.astype(jnp.float32))
    bbar = ((abar - 1.0) / lam)[..., None] * bmat
    bu = jnp.einsum('blgc,gpc->blgp', u.astype(jnp.complex64), bbar)
    edge = -1 if reverse else 0
    bu = bu.at[:, edge].add(abar * h0)
    a = jnp.broadcast_to(abar, bu.shape)
    _, h = lax.associative_scan(_linear_recurrence, (a, bu), axis=1, reverse=reverse)
    cmat = lax.complex(c_re.astype(jnp.float32), c_im.astype(jnp.float32))
    y = jnp.real(jnp.einsum('gcp,blgp->blgc', cmat, h))
    return y, h[:, edge]


def s5_mixer(h, j, p, h0_re, h0_im):
    bsz, seq, dm = h.shape
    u = h.astype(jnp.float32).reshape(bsz, seq, S5_GROUPS, S5_GROUP_CH)
    y_sum = None
    fin_re, fin_im = [], []
    for d in range(2):
        h0 = lax.complex(h0_re[:, d].astype(jnp.float32), h0_im[:, d].astype(jnp.float32))
        y_d, fin = s5_direction(u, p['ssm_lam_re'][j, d], p['ssm_lam_im'][j, d], p['ssm_log_dt'][j, d],
                                p['ssm_b_re'][j, d], p['ssm_b_im'][j, d], p['ssm_c_re'][j, d], p['ssm_c_im'][j, d],
                                h0, reverse=(d == 1))
        y_sum = y_d if y_sum is None else y_sum + y_d
        fin_re.append(jnp.real(fin))
        fin_im.append(jnp.imag(fin))
    y = y_sum.reshape(bsz, seq, dm) + p['ssm_d'][j].astype(jnp.float32) * h.astype(jnp.float32)
    y = jax.nn.gelu(y).astype(h.dtype)
    z = y @ p['w_glu'][j] + p['b_glu'][j]
    out = z[..., :dm] * jax.nn.sigmoid(z[..., dm:])
    return out, jnp.stack(fin_re, axis=1), jnp.stack(fin_im, axis=1)


def fourier_mixer(h, j, p):
    bsz, seq, dm = h.shape
    hg = h.astype(jnp.float32).reshape(bsz, seq, FOURIER_GROUPS, FOURIER_GROUP_CH)
    f = jnp.real(jnp.fft.fft2(hg, axes=(1, 3), norm='ortho')).reshape(bsz, seq, dm).astype(h.dtype)
    return f @ p['w_fourier'][j] + p['b_fourier'][j]


def dwconv_rows(h, w, b, n_rows):
    bsz, seq, ch = h.shape
    hr = h.reshape(bsz, n_rows, seq // n_rows, ch)
    hp = jnp.pad(hr, ((0, 0), (0, 0), (1, 1), (0, 0)))
    out = w[0] * hp[:, :, :-2] + w[1] * hp[:, :, 1:-1] + w[2] * hp[:, :, 2:] + b
    return out.reshape(bsz, seq, ch)


def conv_ffn(h, i, p, n_rows):
    up = h @ p['w_up'][i]
    up = dwconv_rows(up, p['conv_w'][i], p['conv_b'][i], n_rows)
    gate, val = up[..., :D_FF], up[..., D_FF:]
    return (jax.nn.silu(gate) * val) @ p['w_down'][i]


def trunk(x, cond, h0_re, h0_im, n_rows, p, return_states):
    st_re, st_im = [], []
    for i in range(DEPTH):
        mod = (jax.nn.silu(cond) @ p['w_ada'][i] + p['b_ada'][i])[:, None, :]
        sh1, sc1, g1, sh2, sc2, g2 = jnp.split(mod, 6, axis=-1)
        h = rms_norm(x, p['g_mix'][i]) * (1.0 + sc1) + sh1
        j = i // N_MIXERS
        if i % N_MIXERS == 0:
            out, fr, fi = s5_mixer(h, j, p, h0_re[:, j], h0_im[:, j])
            if return_states:
                st_re.append(fr)
                st_im.append(fi)
        else:
            out = fourier_mixer(h, j, p)
        x = x + g1 * out
        h = rms_norm(x, p['g_ffn'][i]) * (1.0 + sc2) + sh2
        x = x + g2 * conv_ffn(h, i, p, n_rows)
    y = rms_norm(x, p['g_final'])
    if return_states:
        return y, jnp.stack(st_re, axis=1), jnp.stack(st_im, axis=1)
    return y


def setup_inputs(seed: int = 0) -> dict:
    key = jax.random.key(seed)
    ks = jax.random.split(key, 32)
    f32 = jnp.float32
    D = D_MODEL
    nrm = lambda k, shape, s: jax.random.normal(k, shape, f32) * s
    n_idx = jnp.arange(S5_STATE, dtype=f32)
    lam_re = -0.5 + nrm(ks[6], (N_SSM_LAYERS, 2, S5_GROUPS, S5_STATE), 0.01)
    lam_im = math.pi * n_idx + nrm(ks[7], (N_SSM_LAYERS, 2, S5_GROUPS, S5_STATE), 0.01)
    log_dt = jax.random.uniform(ks[8], (N_SSM_LAYERS, 2, S5_GROUPS), f32, math.log(DT_MIN), math.log(DT_MAX))
    bscale = (2.0 * S5_GROUP_CH) ** -0.5
    cscale = (2.0 * S5_STATE) ** -0.5
    return {
        'x_prompt': nrm(ks[0], (BATCH, SEQ, D), 1.0),
        'x_sample': nrm(ks[1], (DEC_BATCH, DEC_SEQ, D), 1.0),
        'state_ssm_re': nrm(ks[2], (DEC_BATCH, N_SSM_LAYERS, 2, S5_GROUPS, S5_STATE), 0.1),
        'state_ssm_im': nrm(ks[3], (DEC_BATCH, N_SSM_LAYERS, 2, S5_GROUPS, S5_STATE), 0.1),
        'c': nrm(ks[4], (DEC_BATCH, D), 1.0),
        'c_ctx': nrm(ks[5], (D,), 1.0),
        'w_ada': nrm(ks[9], (DEPTH, D, 6 * D), 0.5 * D ** -0.5),
        'b_ada': nrm(ks[10], (DEPTH, 6 * D), 0.01),
        'g_mix': 1.0 + nrm(ks[11], (DEPTH, D), 0.02),
        'g_ffn': 1.0 + nrm(ks[12], (DEPTH, D), 0.02),
        'ssm_lam_re': lam_re,
        'ssm_lam_im': lam_im,
        'ssm_log_dt': log_dt,
        'ssm_b_re': nrm(ks[13], (N_SSM_LAYERS, 2, S5_GROUPS, S5_STATE, S5_GROUP_CH), bscale),
        'ssm_b_im': nrm(ks[14], (N_SSM_LAYERS, 2, S5_GROUPS, S5_STATE, S5_GROUP_CH), bscale),
        'ssm_c_re': nrm(ks[15], (N_SSM_LAYERS, 2, S5_GROUPS, S5_GROUP_CH, S5_STATE), cscale),
        'ssm_c_im': nrm(ks[16], (N_SSM_LAYERS, 2, S5_GROUPS, S5_GROUP_CH, S5_STATE), cscale),
        'ssm_d': nrm(ks[17], (N_SSM_LAYERS, D), 1.0),
        'w_glu': nrm(ks[18], (N_SSM_LAYERS, D, 2 * D), D ** -0.5),
        'b_glu': nrm(ks[19], (N_SSM_LAYERS, 2 * D), 0.01),
        'w_fourier': nrm(ks[20], (N_FOURIER_LAYERS, D, D), D ** -0.5),
        'b_fourier': nrm(ks[21], (N_FOURIER_LAYERS, D), 0.01),
        'w_up': nrm(ks[22], (DEPTH, D, 2 * D_FF), D ** -0.5),
        'conv_w': nrm(ks[23], (DEPTH, CONV_W, 2 * D_FF), CONV_W ** -0.5),
        'conv_b': nrm(ks[24], (DEPTH, 2 * D_FF), 0.01),
        'w_down': nrm(ks[25], (DEPTH, D_FF, D), D_FF ** -0.5),
        'g_final': 1.0 + nrm(ks[26], (D,), 0.02),
    }


def reference(x_prompt, x_sample, state_ssm_re, state_ssm_im, c, c_ctx,
              w_ada, b_ada, g_mix, g_ffn,
              ssm_lam_re, ssm_lam_im, ssm_log_dt, ssm_b_re, ssm_b_im, ssm_c_re, ssm_c_im, ssm_d,
              w_glu, b_glu, w_fourier, b_fourier,
              w_up, conv_w, conv_b, w_down, g_final):
    p = {
        'w_ada': w_ada, 'b_ada': b_ada, 'g_mix': g_mix, 'g_ffn': g_ffn,
        'ssm_lam_re': ssm_lam_re, 'ssm_lam_im': ssm_lam_im, 'ssm_log_dt': ssm_log_dt,
        'ssm_b_re': ssm_b_re, 'ssm_b_im': ssm_b_im, 'ssm_c_re': ssm_c_re, 'ssm_c_im': ssm_c_im,
        'ssm_d': ssm_d, 'w_glu': w_glu, 'b_glu': b_glu,
        'w_fourier': w_fourier, 'b_fourier': b_fourier,
        'w_up': w_up, 'conv_w': conv_w, 'conv_b': conv_b, 'w_down': w_down, 'g_final': g_final,
    }
    bsz = x_prompt.shape[0]
    zeros = jnp.zeros((bsz, N_SSM_LAYERS, 2, S5_GROUPS, S5_STATE), jnp.float32)
    y_prompt, new_ssm_re, new_ssm_im = trunk(x_prompt, c_ctx[None, :], zeros, zeros, 1, p, True)
    rows = x_sample.shape[1] // GRID_W
    y_sample = trunk(x_sample, c, state_ssm_re, state_ssm_im, rows, p, False)
    return (y_prompt, y_sample, new_ssm_re, new_ssm_im)
```

```python
import functools
import math

import jax
import jax.numpy as jnp
from jax import lax
from jax.experimental import pallas as pl
from jax.experimental.pallas import tpu as pltpu

F32 = jnp.float32
BF16 = jnp.bfloat16

D = 1024
TILE = 256
NCOL = 32
NPC = 16
SEG = 8
DEPTH = 4
GS = 64
GC = 16
PS = 64
NST = GS * PS
FG = 4
FGC = 256
DFF = 2816
EPS = 1e-6
GRID_W = 64

TT = 32
NCH = TILE // TT
SW = 512
FC = 256

VMEM_LIMIT = 56 * 1024 * 1024


def _cparams(n_axes):
    return pltpu.CompilerParams(dimension_semantics=("arbitrary",) * n_axes,
                                vmem_limit_bytes=VMEM_LIMIT)


def _normmod(x, g, sc, sh):
    ms = jnp.mean(x * x, axis=-1, keepdims=True)
    return x * lax.rsqrt(ms + EPS) * g * (1.0 + sc) + sh


def _sigmoid(x):
    return 1.0 / (1.0 + jnp.exp(-x))


def _cond_of_col(i):
    return jnp.where(i < NPC, 0, 1 + (i - NPC) // SEG)


def _ada_kernel(c_ref, w_ref, b_ref, o_ref):
    c = c_ref[...]
    s = (c * _sigmoid(c)).astype(BF16)
    o_ref[...] = jnp.dot(s, w_ref[...].astype(BF16), preferred_element_type=F32) + b_ref[...]


def _ada(cond8, w_ada, b_ada):
    tn = 1536
    return pl.pallas_call(
        _ada_kernel,
        grid=(DEPTH, 6 * D // tn),
        in_specs=[pl.BlockSpec((8, D), lambda l, n: (0, 0)),
                  pl.BlockSpec((None, D, tn), lambda l, n: (l, 0, n)),
                  pl.BlockSpec((None, 1, tn), lambda l, n: (l, 0, n))],
        out_specs=pl.BlockSpec((None, 8, tn), lambda l, n: (l, 0, n)),
        out_shape=jax.ShapeDtypeStruct((DEPTH, 8, 6 * D), F32),
        compiler_params=_cparams(2),
        name="ada",
    )(cond8, w_ada, b_ada.reshape(DEPTH, 1, 6 * D))


def _cast_kernel(w_ref, o_ref):
    o_ref[...] = w_ref[...].astype(BF16)


def _cast_bf16(w2d, block_rows):
    r, c = w2d.shape
    return pl.pallas_call(
        _cast_kernel,
        grid=(r // block_rows,),
        in_specs=[pl.BlockSpec((block_rows, c), lambda i: (i, 0))],
        out_specs=pl.BlockSpec((block_rows, c), lambda i: (i, 0)),
        out_shape=jax.ShapeDtypeStruct((r, c), BF16),
        compiler_params=_cparams(1),
        name="cast",
    )(w2d)


def _abar(lr, li, logdt):
    dt = jnp.exp(logdt)
    mag = jnp.exp(lr * dt)
    return mag * jnp.cos(li * dt), mag * jnp.sin(li * dt)


def _s5_kernel(x_ref, sc_ref, sh_ref, g_ref, lr_ref, li_ref, dt_ref, btr_ref, bti_ref,
               cwr_ref, cwi_ref, h0r_ref, h0i_ref, *rest, emit_y):
    if emit_y:
        y_ref, er_ref, ei_ref, fr_ref, fi_ref = rest[:5]
        scratch = rest[5:]
    else:
        y_ref = None
        er_ref, ei_ref, fr_ref, fi_ref = rest[:4]
        scratch = rest[4:]
    bu_re, bu_im, bt_s, ct_s, ar_s, ai_s, hs_re, hs_im = scratch

    d = pl.program_id(1)
    i = pl.program_id(2)
    rev = d == 1

    @pl.when(i == 0)
    def _prep():
        lr = lr_ref[...]
        li = li_ref[...]
        ar, ai = _abar(lr, li, dt_ref[...])
        ar_s[...] = jnp.broadcast_to(ar, (NPC, NST))
        ai_s[...] = jnp.broadcast_to(ai, (NPC, NST))
        xr = ar - 1.0
        den = lr * lr + li * li
        fr = (xr * lr + ai * li) / den
        fi = (ai * lr - xr * li) / den
        btr = btr_ref[...]
        bti = bti_ref[...]
        bbr = fr * btr - fi * bti
        bbi = fr * bti + fi * btr
        r = lax.broadcasted_iota(jnp.int32, (128, SW), 0)
        c = lax.broadcasted_iota(jnp.int32, (128, SW), 1)
        bmask = ((r >> 4) == (c >> 6)).astype(F32)
        for m in range(8):
            sl = slice(m * SW, (m + 1) * SW)
            tr = jnp.concatenate([bbr[:, sl]] * 8, axis=0) * bmask
            ti = jnp.concatenate([bbi[:, sl]] * 8, axis=0) * bmask
            bt_s[m, :, 0:SW] = tr.astype(BF16)
            bt_s[m, :, SW:2 * SW] = ti.astype(BF16)
        r2 = lax.broadcasted_iota(jnp.int32, (SW, 128), 0)
        c2 = lax.broadcasted_iota(jnp.int32, (SW, 128), 1)
        cmask = ((r2 >> 6) == (c2 >> 4)).astype(F32)
        for n in range(8):
            sl = slice(n * SW, (n + 1) * SW)
            ct_s[n, 0:SW, :] = (cwr_ref[sl, :] * cmask).astype(BF16)
            ct_s[n, SW:2 * SW, :] = (-(cwi_ref[sl, :] * cmask)).astype(BF16)
        hs_re[...] = h0r_ref[...]
        hs_im[...] = h0i_ref[...]

    x3 = x_ref[...]
    ms = jnp.mean(x3 * x3, axis=-1, keepdims=True)
    u3 = x3 * lax.rsqrt(ms + EPS) * g_ref[...][None] * (1.0 + sc_ref[...][None]) + sh_ref[...][None]
    u = u3.reshape(TT * NPC, D).astype(BF16)
    for m in range(8):
        res = jnp.dot(u[:, m * 128:(m + 1) * 128], bt_s[m], preferred_element_type=F32)
        bu_re[:, m * SW:(m + 1) * SW] = res[:, 0:SW]
        bu_im[:, m * SW:(m + 1) * SW] = res[:, SW:2 * SW]

    for s in range(NST // SW):
        lanes = slice(s * SW, (s + 1) * SW)
        ar = ar_s[:, lanes]
        ai = ai_s[:, lanes]

        def body(t, carry, lanes=lanes, ar=ar, ai=ai):
            hr, hi = carry
            tt = jnp.where(rev, TT - 1 - t, t)
            r0 = pl.multiple_of(tt * NPC, NPC)
            nr = ar * hr - ai * hi + bu_re[pl.ds(r0, NPC), lanes]
            ni = ar * hi + ai * hr + bu_im[pl.ds(r0, NPC), lanes]
            bu_re[pl.ds(r0, NPC), lanes] = nr
            bu_im[pl.ds(r0, NPC), lanes] = ni
            return nr, ni

        hr, hi = lax.fori_loop(0, TT, body, (hs_re[:, lanes], hs_im[:, lanes]))
        hs_re[:, lanes] = hr
        hs_im[:, lanes] = hi

    @pl.when(i == 0)
    def _edge():
        e0 = pl.multiple_of(jnp.where(rev, TT - 1, 0) * NPC, NPC)
        er_ref[...] = bu_re[pl.ds(e0, NPC), :]
        ei_ref[...] = bu_im[pl.ds(e0, NPC), :]

    @pl.when(i == NCH - 1)
    def _fin():
        fr_ref[...] = hs_re[...]
        fi_ref[...] = hs_im[...]

    if emit_y:
        for n in range(8):
            sl = slice(n * SW, (n + 1) * SW)
            yn = (jnp.dot(bu_re[:, sl].astype(BF16), ct_s[n, 0:SW, :], preferred_element_type=F32)
                  + jnp.dot(bu_im[:, sl].astype(BF16), ct_s[n, SW:2 * SW, :], preferred_element_type=F32))
            y_ref[:, :, n * 128:(n + 1) * 128] = yn.reshape(TT, NPC, 128)


def _s5_scan(x, sc, sh, g, lr, li, dt, btr, bti, cwr, cwi, h0r, h0i, *, ntr, col_off, emit_y):
    def chunk(d, i):
        return jnp.where(d == 0, i, NCH - 1 - i)

    par = lambda k, d, i: (d, 0, 0)
    st = lambda k, d, i: (d, k, 0)
    in_specs = [
        pl.BlockSpec((TT, NPC, D), lambda k, d, i: (chunk(d, i), k + col_off, 0)),
        pl.BlockSpec((NPC, D), lambda k, d, i: (k + col_off, 0)),
        pl.BlockSpec((NPC, D), lambda k, d, i: (k + col_off, 0)),
        pl.BlockSpec((1, D), lambda k, d, i: (0, 0)),
        pl.BlockSpec((None, 1, NST), par),
        pl.BlockSpec((None, 1, NST), par),
        pl.BlockSpec((None, 1, NST), par),
        pl.BlockSpec((None, GC, NST), par),
        pl.BlockSpec((None, GC, NST), par),
        pl.BlockSpec((None, NST, 128), par),
        pl.BlockSpec((None, NST, 128), par),
        pl.BlockSpec((None, NPC, NST), st),
        pl.BlockSpec((None, NPC, NST), st),
    ]
    st_shape = jax.ShapeDtypeStruct((2, NPC * ntr, NST), F32)
    out_specs = [pl.BlockSpec((None, NPC, NST), st)] * 4
    out_shape = [st_shape] * 4
    if emit_y:
        out_specs = [pl.BlockSpec((None, TT, NPC, D), lambda k, d, i: (d, chunk(d, i), k, 0))] + out_specs
        out_shape = [jax.ShapeDtypeStruct((2, TILE, NPC * ntr, D), F32)] + out_shape
    return pl.pallas_call(
        functools.partial(_s5_kernel, emit_y=emit_y),
        grid=(ntr, 2, NCH),
        in_specs=in_specs,
        out_specs=out_specs,
        out_shape=out_shape,
        scratch_shapes=[pltpu.VMEM((TT * NPC, NST), F32), pltpu.VMEM((TT * NPC, NST), F32),
                        pltpu.VMEM((8, 128, 2 * SW), BF16), pltpu.VMEM((8, 2 * SW, 128), BF16),
                        pltpu.VMEM((NPC, NST), F32), pltpu.VMEM((NPC, NST), F32),
                        pltpu.VMEM((NPC, NST), F32), pltpu.VMEM((NPC, NST), F32)],
        compiler_params=_cparams(3),
        name="s5_scan" if emit_y else "s5_states",
    )(x, sc, sh, g, lr, li, dt, btr, bti, cwr, cwi, h0r, h0i)


def _carry_kernel(fr_ref, fi_ref, h0r_ref, h0i_ref, lr_ref, li_ref, dt_ref, or_ref, oi_ref):
    for d in range(2):
        ar, ai = _abar(lr_ref[d], li_ref[d], dt_ref[d])
        for _ in range(8):
            ar, ai = ar * ar - ai * ai, 2.0 * ar * ai
        for b in range(2):
            order = list(range(SEG)) if d == 0 else list(range(SEG - 1, -1, -1))
            r = b * SEG + order[0]
            or_ref[d, r:r + 1, :] = h0r_ref[d, r:r + 1, :]
            oi_ref[d, r:r + 1, :] = h0i_ref[d, r:r + 1, :]
            tr = fr_ref[d, r:r + 1, :]
            ti = fi_ref[d, r:r + 1, :]
            for s in order[1:]:
                r = b * SEG + s
                or_ref[d, r:r + 1, :] = tr
                oi_ref[d, r:r + 1, :] = ti
                tr, ti = (fr_ref[d, r:r + 1, :] + ar * tr - ai * ti,
                          fi_ref[d, r:r + 1, :] + ar * ti + ai * tr)


def _carry(fr, fi, h0r, h0i, lr, li, dt):
    shp = jax.ShapeDtypeStruct((2, NPC, NST), F32)
    return pl.pallas_call(_carry_kernel, out_shape=[shp, shp], name="s5_carry")(fr, fi, h0r, h0i, lr, li, dt)


def _glu_kernel(x_ref, y_ref, mod_ref, g_ref, d_ref, w_ref, b_ref, o_ref, wbf):
    i = pl.program_id(0)

    @pl.when(i == 0)
    def _():
        wbf[...] = w_ref[...].astype(BF16)

    x = x_ref[...]
    mod = mod_ref[...]
    h = _normmod(x, g_ref[...], mod[1:2], mod[0:1])
    y = y_ref[0] + y_ref[1] + d_ref[...] * h
    ge = 0.5 * y * (1.0 + jnp.tanh(math.sqrt(2.0 / math.pi) * (y + 0.044715 * (y * y * y))))
    z = jnp.dot(ge.astype(BF16), wbf[...], preferred_element_type=F32) + b_ref[...]
    out = z[:, 0:D] * _sigmoid(z[:, D:2 * D])
    o_ref[...] = x + mod[2:3] * out


def _glu(x, y, mods, layer, g, dvec, w, b):
    return pl.pallas_call(
        _glu_kernel,
        grid=(NCOL,),
        in_specs=[pl.BlockSpec((TILE, D), lambda i: (i, 0)),
                  pl.BlockSpec((2, None, TILE, D), lambda i: (0, i, 0, 0)),
                  pl.BlockSpec((None, None, 6, D), lambda i: (layer, _cond_of_col(i), 0, 0)),
                  pl.BlockSpec((1, D), lambda i: (0, 0)),
                  pl.BlockSpec((1, D), lambda i: (0, 0)),
                  pl.BlockSpec((None, D, 2 * D), lambda i: (layer // 2, 0, 0)),
                  pl.BlockSpec((1, 2 * D), lambda i: (0, 0))],
        out_specs=pl.BlockSpec((TILE, D), lambda i: (i, 0)),
        out_shape=jax.ShapeDtypeStruct((NCOL * TILE, D), F32),
        scratch_shapes=[pltpu.VMEM((D, 2 * D), BF16)],
        compiler_params=_cparams(1),
        name="s5_glu",
    )(x, y, mods, g, dvec, w, b)


def _fourier_kernel(x_ref, mod_ref, g_ref, cc_ref, cs_ref, cl_ref, sl_ref, w_ref, b_ref,
                    o_ref, xc_s, xs_s, *, scale):
    ph = pl.program_id(1)
    s = pl.program_id(2)
    mod = mod_ref[...]

    @pl.when(ph == 0)
    def _channel_dft():
        h = _normmod(x_ref[...], g_ref[...], mod[1:2], mod[0:1]).astype(BF16)
        r0 = pl.multiple_of(s * TILE, TILE)
        for q in range(FG):
            hq = h[:, q * FGC:(q + 1) * FGC]
            xc_s[pl.ds(r0, TILE), q * FGC:(q + 1) * FGC] = jnp.dot(
                hq, cc_ref[...], preferred_element_type=F32).astype(BF16)
            xs_s[pl.ds(r0, TILE), q * FGC:(q + 1) * FGC] = jnp.dot(
                hq, cs_ref[...], preferred_element_type=F32).astype(BF16)

    @pl.when(ph == 1)
    def _position_dft():
        f = (jnp.dot(cl_ref[...], xc_s[...], preferred_element_type=F32)
             - jnp.dot(sl_ref[...], xs_s[...], preferred_element_type=F32)) * scale
        o = jnp.dot(f.astype(BF16), w_ref[...], preferred_element_type=F32) + b_ref[...]
        o_ref[...] = x_ref[...] + mod[2:3] * o


def _fourier(x, mods, layer, g, cc, cs, cl, sl, w, b, *, nseq, nseg, col_off, cond_off, cond_stride):
    ln = nseg * TILE
    col = lambda q, ph, s: (col_off + q * nseg + s, 0)
    return pl.pallas_call(
        functools.partial(_fourier_kernel, scale=1.0 / math.sqrt(ln * FGC)),
        grid=(nseq, 2, nseg),
        in_specs=[pl.BlockSpec((TILE, D), col),
                  pl.BlockSpec((None, None, 6, D), lambda q, ph, s: (layer, cond_off + q * cond_stride, 0, 0)),
                  pl.BlockSpec((1, D), lambda q, ph, s: (0, 0)),
                  pl.BlockSpec((FGC, FGC), lambda q, ph, s: (0, 0)),
                  pl.BlockSpec((FGC, FGC), lambda q, ph, s: (0, 0)),
                  pl.BlockSpec((TILE, ln), lambda q, ph, s: (s * ph, 0)),
                  pl.BlockSpec((TILE, ln), lambda q, ph, s: (s * ph, 0)),
                  pl.BlockSpec((None, D, D), lambda q, ph, s: (layer // 2, 0, 0)),
                  pl.BlockSpec((1, D), lambda q, ph, s: (0, 0))],
        out_specs=pl.BlockSpec((TILE, D), lambda q, ph, s: (col_off + q * nseg + s * ph, 0)),
        out_shape=jax.ShapeDtypeStruct((NCOL * TILE, D), F32),
        scratch_shapes=[pltpu.VMEM((ln, D), BF16), pltpu.VMEM((ln, D), BF16)],
        input_output_aliases={0: 0},
        compiler_params=_cparams(3),
        name="fourier%d" % nseg,
    )(x, mods, g, cc, cs, cl, sl, w, b)


def _dft_mats(n):
    k = jnp.arange(n, dtype=jnp.int32)
    ang = ((k[:, None] * k[None, :]) % n).astype(F32) * (2.0 * math.pi / n)
    return jnp.cos(ang).astype(BF16), jnp.sin(ang).astype(BF16)


def _ffn_kernel(x_ref, mod_ref, g_ref, wup_ref, cw_ref, cb_ref, wdn_ref, gf_ref, o_ref, *, final):
    i = pl.program_id(0)
    x = x_ref[...]
    mod = mod_ref[...]
    h = _normmod(x, g_ref[...], mod[4:5], mod[3:4]).astype(BF16)
    period = jnp.where(i < NPC, TILE, GRID_W)
    pos = lax.broadcasted_iota(jnp.int32, (TILE, FC), 0) & (period - 1)
    mprev = (pos != 0).astype(F32)
    mnext = (pos != period - 1).astype(F32)

    def conv(up, off):
        w = cw_ref[:, off:off + FC]
        prev = pltpu.roll(up, 1, 0) * mprev
        nxt = pltpu.roll(up, TILE - 1, 0) * mnext
        return w[0:1] * prev + w[1:2] * up + w[2:3] * nxt + cb_ref[:, off:off + FC]

    acc = jnp.zeros((TILE, D), F32)
    for c in range(DFF // FC):
        og = c * FC
        ov = DFF + c * FC
        gate = conv(jnp.dot(h, wup_ref[:, og:og + FC], preferred_element_type=F32), og)
        val = conv(jnp.dot(h, wup_ref[:, ov:ov + FC], preferred_element_type=F32), ov)
        act = (gate * _sigmoid(gate) * val).astype(BF16)
        acc = acc + jnp.dot(act, wdn_ref[og:og + FC, :], preferred_element_type=F32)
    y = x + mod[5:6] * acc
    if final:
        ms = jnp.mean(y * y, axis=-1, keepdims=True)
        y = y * lax.rsqrt(ms + EPS) * gf_ref[...]
    o_ref[...] = y


def _ffn(x, mods, layer, g, wup, cw, cb, wdn, gf, *, final):
    out_spec = pl.BlockSpec((TILE, D), lambda i: (i, 0))
    out_shape = jax.ShapeDtypeStruct((NCOL * TILE, D), F32)
    return pl.pallas_call(
        functools.partial(_ffn_kernel, final=final),
        grid=(NCOL,),
        in_specs=[pl.BlockSpec((TILE, D), lambda i: (i, 0)),
                  pl.BlockSpec((None, None, 6, D), lambda i: (layer, _cond_of_col(i), 0, 0)),
                  pl.BlockSpec((1, D), lambda i: (0, 0)),
                  pl.BlockSpec((None, D, 2 * DFF), lambda i: (layer, 0, 0)),
                  pl.BlockSpec((None, 3, 2 * DFF), lambda i: (layer, 0, 0)),
                  pl.BlockSpec((None, 1, 2 * DFF), lambda i: (layer, 0, 0)),
                  pl.BlockSpec((None, DFF, D), lambda i: (layer, 0, 0)),
                  pl.BlockSpec((1, D), lambda i: (0, 0))],
        out_specs=out_spec,
        out_shape=out_shape,
        compiler_params=_cparams(1),
        name="ffn",
    )(x, mods, g, wup, cw, cb, wdn, gf)


def kernel(x_prompt, x_sample, state_ssm_re, state_ssm_im, c, c_ctx, w_ada, b_ada, g_mix, g_ffn,
           ssm_lam_re, ssm_lam_im, ssm_log_dt, ssm_b_re, ssm_b_im, ssm_c_re, ssm_c_im, ssm_d,
           w_glu, b_glu, w_fourier, b_fourier, w_up, conv_w, conv_b, w_down, g_final):
    nb = x_prompt.shape[0]
    x = jnp.concatenate([x_prompt.reshape(NPC * TILE, D), x_sample.reshape(NPC * TILE, D)], axis=0)

    cond8 = jnp.concatenate([c_ctx[None, :], c, jnp.zeros((5, D), F32)], axis=0)
    mods = _ada(cond8, w_ada, b_ada).reshape(DEPTH, 8, 6, D)
    col_cond = jnp.asarray([0] * NPC + [1] * SEG + [2] * SEG, jnp.int32)

    wup_bf = _cast_bf16(w_up.reshape(DEPTH * D, 2 * DFF), 256).reshape(DEPTH, D, 2 * DFF)
    wdn_bf = _cast_bf16(w_down.reshape(DEPTH * DFF, D), DFF).reshape(DEPTH, DFF, D)
    wf_bf = _cast_bf16(w_fourier.reshape(2 * D, D), D).reshape(2, D, D)

    cc, cs = _dft_mats(FGC)
    cl1, sl1 = cc, cs
    cl8, sl8 = _dft_mats(SEG * TILE)

    st_re, st_im = [], []
    for i in range(DEPTH):
        j = i // 2
        if i % 2 == 0:
            sh1 = mods[i, :, 0, :][col_cond]
            sc1 = mods[i, :, 1, :][col_cond]
            g = g_mix[i][None, :]
            lr = ssm_lam_re[j].reshape(2, 1, NST)
            li = ssm_lam_im[j].reshape(2, 1, NST)
            dt = jnp.repeat(ssm_log_dt[j], PS, axis=-1).reshape(2, 1, NST)
            btr = ssm_b_re[j].transpose(0, 3, 1, 2).reshape(2, GC, NST)
            bti = ssm_b_im[j].transpose(0, 3, 1, 2).reshape(2, GC, NST)
            cwr = jnp.tile(ssm_c_re[j].transpose(0, 1, 3, 2).reshape(2, NST, GC), (1, 1, 8))
            cwi = jnp.tile(ssm_c_im[j].transpose(0, 1, 3, 2).reshape(2, NST, GC), (1, 1, 8))
            sre = state_ssm_re[:, j].reshape(2, 2, NST)
            sim = state_ssm_im[:, j].reshape(2, 2, NST)
            h0r = jnp.zeros((2, 2, SEG, NST), F32)
            h0i = jnp.zeros((2, 2, SEG, NST), F32)
            h0r = h0r.at[0, :, 0].set(sre[:, 0]).at[1, :, SEG - 1].set(sre[:, 1]).reshape(2, NPC, NST)
            h0i = h0i.at[0, :, 0].set(sim[:, 0]).at[1, :, SEG - 1].set(sim[:, 1]).reshape(2, NPC, NST)
            x_tm = x.reshape(NCOL, TILE, D).transpose(1, 0, 2)
            args = (x_tm, sc1, sh1, g, lr, li, dt, btr, bti, cwr, cwi)
            _, _, fr, fi = _s5_scan(*args, h0r, h0i, ntr=1, col_off=1, emit_y=False)
            ir, ii = _carry(fr, fi, h0r, h0i, lr, li, dt)
            z = jnp.zeros((2, NPC, NST), F32)
            y, er, ei, _, _ = _s5_scan(*args, jnp.concatenate([z, ir], axis=1),
                                       jnp.concatenate([z, ii], axis=1), ntr=2, col_off=0, emit_y=True)
            st_re.append(er[:, :nb])
            st_im.append(ei[:, :nb])
            x = _glu(x, y.transpose(0, 2, 1, 3), mods, i, g, ssm_d[j][None, :], w_glu, b_glu[j][None, :])
        else:
            g = g_mix[i][None, :]
            bf = b_fourier[j][None, :]
            x = _fourier(x, mods, i, g, cc, cs, cl1, sl1, wf_bf, bf,
                         nseq=NPC, nseg=1, col_off=0, cond_off=0, cond_stride=0)
            x = _fourier(x, mods, i, g, cc, cs, cl8, sl8, wf_bf, bf,
                         nseq=2, nseg=SEG, col_off=NPC, cond_off=1, cond_stride=1)
        x = _ffn(x, mods, i, g_ffn[i][None, :], wup_bf, conv_w, conv_b.reshape(DEPTH, 1, 2 * DFF),
                 wdn_bf, g_final[None, :], final=(i == DEPTH - 1))

    y_prompt = x[:NPC * TILE].reshape(NPC, TILE, D)
    y_sample = x[NPC * TILE:].reshape(2, SEG * TILE, D)
    new_re = jnp.stack(st_re, axis=0).transpose(2, 0, 1, 3).reshape(nb, 2, 2, GS, PS)
    new_im = jnp.stack(st_im, axis=0).transpose(2, 0, 1, 3).reshape(nb, 2, 2, GS, PS)
    return (y_prompt, y_sample, new_re, new_im)
```

```python
import functools
import math

import jax
import jax.numpy as jnp
import numpy as np
from jax import lax
from jax.experimental import pallas as pl
from jax.experimental.pallas import tpu as pltpu

F32 = jnp.float32
BF16 = jnp.bfloat16

D = 1024
TILE = 256
NCOL = 32
NPC = 16
SEG = 8
DEPTH = 4
GS = 64
GC = 16
PS = 64
NST = GS * PS
FG = 4
FGC = 256
DFF = 2816
EPS = 1e-6
GRID_W = 64

TT = 32
NCH = TILE // TT
SW = 512
FC = 256

VMEM_LIMIT = 56 * 1024 * 1024


def _cparams(n_axes):
    return pltpu.CompilerParams(dimension_semantics=("arbitrary",) * n_axes,
                                vmem_limit_bytes=VMEM_LIMIT)


def _normmod(x, g, sc, sh):
    ms = jnp.mean(x * x, axis=-1, keepdims=True)
    return x * lax.rsqrt(ms + EPS) * g * (1.0 + sc) + sh


def _sigmoid(x):
    return 1.0 / (1.0 + jnp.exp(-x))


def _cond_of_col(i):
    return jnp.where(i < NPC, 0, 1 + (i - NPC) // SEG)


def _ada_kernel(c_ref, w_ref, b_ref, o_ref):
    c = c_ref[...]
    s = (c * _sigmoid(c)).astype(BF16)
    o_ref[...] = jnp.dot(s, w_ref[...].astype(BF16), preferred_element_type=F32) + b_ref[...]


def _ada(cond8, w_ada, b_ada):
    tn = 1536
    return pl.pallas_call(
        _ada_kernel,
        grid=(DEPTH, 6 * D // tn),
        in_specs=[pl.BlockSpec((8, D), lambda l, n: (0, 0)),
                  pl.BlockSpec((None, D, tn), lambda l, n: (l, 0, n)),
                  pl.BlockSpec((None, 1, tn), lambda l, n: (l, 0, n))],
        out_specs=pl.BlockSpec((None, 8, tn), lambda l, n: (l, 0, n)),
        out_shape=jax.ShapeDtypeStruct((DEPTH, 8, 6 * D), F32),
        compiler_params=_cparams(2),
        name="ada",
    )(cond8, w_ada, b_ada.reshape(DEPTH, 1, 6 * D))


def _cast_kernel(w_ref, o_ref):
    o_ref[...] = w_ref[...].astype(BF16)


def _cast_bf16(w2d, block_rows):
    r, c = w2d.shape
    return pl.pallas_call(
        _cast_kernel,
        grid=(r // block_rows,),
        in_specs=[pl.BlockSpec((block_rows, c), lambda i: (i, 0))],
        out_specs=pl.BlockSpec((block_rows, c), lambda i: (i, 0)),
        out_shape=jax.ShapeDtypeStruct((r, c), BF16),
        compiler_params=_cparams(1),
        name="cast",
    )(w2d)


def _abar(lr, li, logdt):
    dt = jnp.exp(logdt)
    mag = jnp.exp(lr * dt)
    return mag * jnp.cos(li * dt), mag * jnp.sin(li * dt)


def _s5_kernel(x_ref, sc_ref, sh_ref, g_ref, lr_ref, li_ref, dt_ref, btr_ref, bti_ref,
               cwr_ref, cwi_ref, h0r_ref, h0i_ref, *rest, emit_y):
    if emit_y:
        y_ref, er_ref, ei_ref, fr_ref, fi_ref = rest[:5]
        scratch = rest[5:]
    else:
        y_ref = None
        er_ref, ei_ref, fr_ref, fi_ref = rest[:4]
        scratch = rest[4:]
    bu_re, bu_im, bt_s, ct_s, ar_s, ai_s, hs_re, hs_im = scratch

    d = pl.program_id(1)
    i = pl.program_id(2)
    rev = d == 1

    @pl.when(i == 0)
    def _prep():
        lr = lr_ref[...]
        li = li_ref[...]
        ar, ai = _abar(lr, li, dt_ref[...])
        ar_s[...] = jnp.broadcast_to(ar, (NPC, NST))
        ai_s[...] = jnp.broadcast_to(ai, (NPC, NST))
        xr = ar - 1.0
        den = lr * lr + li * li
        fr = (xr * lr + ai * li) / den
        fi = (ai * lr - xr * li) / den
        btr = btr_ref[...]
        bti = bti_ref[...]
        bbr = fr * btr - fi * bti
        bbi = fr * bti + fi * btr
        r = lax.broadcasted_iota(jnp.int32, (128, SW), 0)
        c = lax.broadcasted_iota(jnp.int32, (128, SW), 1)
        bmask = ((r >> 4) == (c >> 6)).astype(F32)
        for m in range(8):
            sl = slice(m * SW, (m + 1) * SW)
            tr = jnp.concatenate([bbr[:, sl]] * 8, axis=0) * bmask
            ti = jnp.concatenate([bbi[:, sl]] * 8, axis=0) * bmask
            bt_s[m, :, 0:SW] = tr.astype(BF16)
            bt_s[m, :, SW:2 * SW] = ti.astype(BF16)
        r2 = lax.broadcasted_iota(jnp.int32, (SW, 128), 0)
        c2 = lax.broadcasted_iota(jnp.int32, (SW, 128), 1)
        cmask = ((r2 >> 6) == (c2 >> 4)).astype(F32)
        for n in range(8):
            sl = slice(n * SW, (n + 1) * SW)
            ct_s[n, 0:SW, :] = (cwr_ref[sl, :] * cmask).astype(BF16)
            ct_s[n, SW:2 * SW, :] = (-(cwi_ref[sl, :] * cmask)).astype(BF16)
        hs_re[...] = h0r_ref[...]
        hs_im[...] = h0i_ref[...]

    x3 = x_ref[...]
    ms = jnp.mean(x3 * x3, axis=-1, keepdims=True)
    u3 = x3 * lax.rsqrt(ms + EPS) * g_ref[...][None] * (1.0 + sc_ref[...][None]) + sh_ref[...][None]
    u = u3.reshape(TT * NPC, D).astype(BF16)
    for m in range(8):
        res = jnp.dot(u[:, m * 128:(m + 1) * 128], bt_s[m], preferred_element_type=F32)
        bu_re[:, m * SW:(m + 1) * SW] = res[:, 0:SW]
        bu_im[:, m * SW:(m + 1) * SW] = res[:, SW:2 * SW]

    for s in range(NST // SW):
        lanes = slice(s * SW, (s + 1) * SW)
        ar = ar_s[:, lanes]
        ai = ai_s[:, lanes]

        def body(t, carry, lanes=lanes, ar=ar, ai=ai):
            hr, hi = carry
            tt = jnp.where(rev, TT - 1 - t, t)
            r0 = pl.multiple_of(tt * NPC, NPC)
            nr = ar * hr - ai * hi + bu_re[pl.ds(r0, NPC), lanes]
            ni = ar * hi + ai * hr + bu_im[pl.ds(r0, NPC), lanes]
            bu_re[pl.ds(r0, NPC), lanes] = nr
            bu_im[pl.ds(r0, NPC), lanes] = ni
            return nr, ni

        hr, hi = lax.fori_loop(0, TT, body, (hs_re[:, lanes], hs_im[:, lanes]))
        hs_re[:, lanes] = hr
        hs_im[:, lanes] = hi

    @pl.when(i == 0)
    def _edge():
        e0 = pl.multiple_of(jnp.where(rev, TT - 1, 0) * NPC, NPC)
        er_ref[...] = bu_re[pl.ds(e0, NPC), :]
        ei_ref[...] = bu_im[pl.ds(e0, NPC), :]

    @pl.when(i == NCH - 1)
    def _fin():
        fr_ref[...] = hs_re[...]
        fi_ref[...] = hs_im[...]

    if emit_y:
        for n in range(8):
            sl = slice(n * SW, (n + 1) * SW)
            yn = (jnp.dot(bu_re[:, sl].astype(BF16), ct_s[n, 0:SW, :], preferred_element_type=F32)
                  + jnp.dot(bu_im[:, sl].astype(BF16), ct_s[n, SW:2 * SW, :], preferred_element_type=F32))
            y_ref[:, :, n * 128:(n + 1) * 128] = yn.astype(BF16).reshape(TT, NPC, 128)


def _s5_scan(x, sc, sh, g, lr, li, dt, btr, bti, cwr, cwi, h0r, h0i, *, ntr, col_off, emit_y):
    def chunk(d, i):
        return jnp.where(d == 0, i, NCH - 1 - i)

    par = lambda k, d, i: (d, 0, 0)
    st = lambda k, d, i: (d, k, 0)
    in_specs = [
        pl.BlockSpec((TT, NPC, D), lambda k, d, i: (chunk(d, i), k + col_off, 0)),
        pl.BlockSpec((NPC, D), lambda k, d, i: (k + col_off, 0)),
        pl.BlockSpec((NPC, D), lambda k, d, i: (k + col_off, 0)),
        pl.BlockSpec((1, D), lambda k, d, i: (0, 0)),
        pl.BlockSpec((None, 1, NST), par),
        pl.BlockSpec((None, 1, NST), par),
        pl.BlockSpec((None, 1, NST), par),
        pl.BlockSpec((None, GC, NST), par),
        pl.BlockSpec((None, GC, NST), par),
        pl.BlockSpec((None, NST, 128), par),
        pl.BlockSpec((None, NST, 128), par),
        pl.BlockSpec((None, NPC, NST), st),
        pl.BlockSpec((None, NPC, NST), st),
    ]
    st_shape = jax.ShapeDtypeStruct((2, NPC * ntr, NST), F32)
    out_specs = [pl.BlockSpec((None, NPC, NST), st)] * 4
    out_shape = [st_shape] * 4
    if emit_y:
        out_specs = [pl.BlockSpec((None, TT, NPC, D), lambda k, d, i: (d, chunk(d, i), k, 0))] + out_specs
        out_shape = [jax.ShapeDtypeStruct((2, TILE, NPC * ntr, D), BF16)] + out_shape
    return pl.pallas_call(
        functools.partial(_s5_kernel, emit_y=emit_y),
        grid=(ntr, 2, NCH),
        in_specs=in_specs,
        out_specs=out_specs,
        out_shape=out_shape,
        scratch_shapes=[pltpu.VMEM((TT * NPC, NST), F32), pltpu.VMEM((TT * NPC, NST), F32),
                        pltpu.VMEM((8, 128, 2 * SW), BF16), pltpu.VMEM((8, 2 * SW, 128), BF16),
                        pltpu.VMEM((NPC, NST), F32), pltpu.VMEM((NPC, NST), F32),
                        pltpu.VMEM((NPC, NST), F32), pltpu.VMEM((NPC, NST), F32)],
        compiler_params=_cparams(3),
        name="s5_scan" if emit_y else "s5_states",
    )(x, sc, sh, g, lr, li, dt, btr, bti, cwr, cwi, h0r, h0i)


def _carry_kernel(fr_ref, fi_ref, h0r_ref, h0i_ref, lr_ref, li_ref, dt_ref, or_ref, oi_ref):
    for d in range(2):
        ar, ai = _abar(lr_ref[d], li_ref[d], dt_ref[d])
        for _ in range(8):
            ar, ai = ar * ar - ai * ai, 2.0 * ar * ai
        for b in range(2):
            order = list(range(SEG)) if d == 0 else list(range(SEG - 1, -1, -1))
            r = b * SEG + order[0]
            or_ref[d, r:r + 1, :] = h0r_ref[d, r:r + 1, :]
            oi_ref[d, r:r + 1, :] = h0i_ref[d, r:r + 1, :]
            tr = fr_ref[d, r:r + 1, :]
            ti = fi_ref[d, r:r + 1, :]
            for s in order[1:]:
                r = b * SEG + s
                or_ref[d, r:r + 1, :] = tr
                oi_ref[d, r:r + 1, :] = ti
                tr, ti = (fr_ref[d, r:r + 1, :] + ar * tr - ai * ti,
                          fi_ref[d, r:r + 1, :] + ar * ti + ai * tr)


def _carry(fr, fi, h0r, h0i, lr, li, dt):
    shp = jax.ShapeDtypeStruct((2, NPC, NST), F32)
    return pl.pallas_call(_carry_kernel, out_shape=[shp, shp], name="s5_carry")(fr, fi, h0r, h0i, lr, li, dt)


def _glu_kernel(x_ref, y_ref, mod_ref, g_ref, d_ref, w_ref, b_ref, o_ref, wbf):
    i = pl.program_id(0)

    @pl.when(i == 0)
    def _():
        wbf[...] = w_ref[...].astype(BF16)

    x = x_ref[...]
    mod = mod_ref[...]
    h = _normmod(x, g_ref[...], mod[1:2], mod[0:1])
    y = y_ref[0].astype(F32) + y_ref[1].astype(F32) + d_ref[...] * h
    ge = 0.5 * y * (1.0 + jnp.tanh(math.sqrt(2.0 / math.pi) * (y + 0.044715 * (y * y * y))))
    z = jnp.dot(ge.astype(BF16), wbf[...], preferred_element_type=F32) + b_ref[...]
    out = z[:, 0:D] * _sigmoid(z[:, D:2 * D])
    o_ref[...] = x + mod[2:3] * out


def _glu(x, y, mods, layer, g, dvec, w, b):
    return pl.pallas_call(
        _glu_kernel,
        grid=(NCOL,),
        in_specs=[pl.BlockSpec((TILE, D), lambda i: (i, 0)),
                  pl.BlockSpec((2, None, TILE, D), lambda i: (0, i, 0, 0)),
                  pl.BlockSpec((None, None, 6, D), lambda i: (layer, _cond_of_col(i), 0, 0)),
                  pl.BlockSpec((1, D), lambda i: (0, 0)),
                  pl.BlockSpec((1, D), lambda i: (0, 0)),
                  pl.BlockSpec((None, D, 2 * D), lambda i: (layer // 2, 0, 0)),
                  pl.BlockSpec((1, 2 * D), lambda i: (0, 0))],
        out_specs=pl.BlockSpec((TILE, D), lambda i: (i, 0)),
        out_shape=jax.ShapeDtypeStruct((NCOL * TILE, D), F32),
        scratch_shapes=[pltpu.VMEM((D, 2 * D), BF16)],
        compiler_params=_cparams(1),
        name="s5_glu",
    )(x, y, mods, g, dvec, w, b)


def _fourier_kernel(x_ref, mod_ref, g_ref, cc_ref, cs_ref, cl_ref, sl_ref, w_ref, b_ref,
                    o_ref, xc_s, xs_s, *, scale):
    ph = pl.program_id(1)
    s = pl.program_id(2)
    mod = mod_ref[...]

    @pl.when(ph == 0)
    def _channel_dft():
        h = _normmod(x_ref[...], g_ref[...], mod[1:2], mod[0:1]).astype(BF16)
        r0 = pl.multiple_of(s * TILE, TILE)
        for q in range(FG):
            hq = h[:, q * FGC:(q + 1) * FGC]
            xc_s[pl.ds(r0, TILE), q * FGC:(q + 1) * FGC] = jnp.dot(
                hq, cc_ref[...], preferred_element_type=F32).astype(BF16)
            xs_s[pl.ds(r0, TILE), q * FGC:(q + 1) * FGC] = jnp.dot(
                hq, cs_ref[...], preferred_element_type=F32).astype(BF16)

    @pl.when(ph == 1)
    def _position_dft():
        f = (jnp.dot(cl_ref[...], xc_s[...], preferred_element_type=F32)
             - jnp.dot(sl_ref[...], xs_s[...], preferred_element_type=F32)) * scale
        o = jnp.dot(f.astype(BF16), w_ref[...], preferred_element_type=F32) + b_ref[...]
        o_ref[...] = x_ref[...] + mod[2:3] * o


def _fourier(x, mods, layer, g, cc, cs, cl, sl, w, b, *, nseq, nseg, col_off, cond_off, cond_stride):
    ln = nseg * TILE
    col = lambda q, ph, s: (col_off + q * nseg + s, 0)
    return pl.pallas_call(
        functools.partial(_fourier_kernel, scale=1.0 / math.sqrt(ln * FGC)),
        grid=(nseq, 2, nseg),
        in_specs=[pl.BlockSpec((TILE, D), col),
                  pl.BlockSpec((None, None, 6, D), lambda q, ph, s: (layer, cond_off + q * cond_stride, 0, 0)),
                  pl.BlockSpec((1, D), lambda q, ph, s: (0, 0)),
                  pl.BlockSpec((FGC, FGC), lambda q, ph, s: (0, 0)),
                  pl.BlockSpec((FGC, FGC), lambda q, ph, s: (0, 0)),
                  pl.BlockSpec((TILE, ln), lambda q, ph, s: (s * ph, 0)),
                  pl.BlockSpec((TILE, ln), lambda q, ph, s: (s * ph, 0)),
                  pl.BlockSpec((None, D, D), lambda q, ph, s: (layer // 2, 0, 0)),
                  pl.BlockSpec((1, D), lambda q, ph, s: (0, 0))],
        out_specs=pl.BlockSpec((TILE, D), lambda q, ph, s: (col_off + q * nseg + s * ph, 0)),
        out_shape=jax.ShapeDtypeStruct((NCOL * TILE, D), F32),
        scratch_shapes=[pltpu.VMEM((ln, D), BF16), pltpu.VMEM((ln, D), BF16)],
        input_output_aliases={0: 0},
        compiler_params=_cparams(3),
        name="fourier%d" % nseg,
    )(x, mods, g, cc, cs, cl, sl, w, b)


@functools.lru_cache(maxsize=None)
def _dft_mats(n):
    k = np.arange(n, dtype=np.int64)
    ang = ((k[:, None] * k[None, :]) % n).astype(np.float64) * (2.0 * math.pi / n)
    return np.cos(ang).astype(np.float32), np.sin(ang).astype(np.float32)


def _ffn_kernel(x_ref, mod_ref, g_ref, wup_ref, cw_ref, cb_ref, wdn_ref, gf_ref, o_ref, *, final):
    i = pl.program_id(0)
    x = x_ref[...]
    mod = mod_ref[...]
    h = _normmod(x, g_ref[...], mod[4:5], mod[3:4]).astype(BF16)
    sub = lax.broadcasted_iota(jnp.int32, (8, FC), 0)
    one = jnp.ones((8, FC), F32)
    first0 = (sub != 0).astype(F32)
    last0 = (sub != 7).astype(F32)
    inner_first = jnp.where(i < NPC, one, first0)
    inner_last = jnp.where(i < NPC, one, last0)
    prev_slabs = [(0, first0)] + [(r, inner_first) for r in range(GRID_W, TILE, GRID_W)]
    next_slabs = [(r - 8, inner_last) for r in range(GRID_W, TILE, GRID_W)] + [(TILE - 8, last0)]

    def mask_rows(a, slabs):
        parts, last = [], 0
        for r0, m in slabs:
            if r0 > last:
                parts.append(a[last:r0])
            parts.append(a[r0:r0 + 8] * m)
            last = r0 + 8
        if last < TILE:
            parts.append(a[last:])
        return jnp.concatenate(parts, axis=0)

    def conv(up, off):
        w = cw_ref[:, off:off + FC]
        prev = mask_rows(pltpu.roll(up, 1, 0), prev_slabs)
        nxt = mask_rows(pltpu.roll(up, TILE - 1, 0), next_slabs)
        return w[0:1] * prev + w[1:2] * up + w[2:3] * nxt + cb_ref[:, off:off + FC]

    def up_chunk(c):
        og = c * FC
        ov = DFF + c * FC
        return (jnp.dot(h, wup_ref[:, og:og + FC], preferred_element_type=F32),
                jnp.dot(h, wup_ref[:, ov:ov + FC], preferred_element_type=F32))

    def down_chunk(act, c):
        return jnp.dot(act, wdn_ref[c * FC:(c + 1) * FC, :], preferred_element_type=F32)

    nchunk = DFF // FC
    acc = jnp.zeros((TILE, D), F32)
    nxt_up = up_chunk(0)
    act = None
    for c in range(nchunk):
        cur_up = nxt_up
        if c + 1 < nchunk:
            nxt_up = up_chunk(c + 1)
        if act is not None:
            acc = acc + down_chunk(act, c - 1)
        gate = conv(cur_up[0], c * FC)
        val = conv(cur_up[1], DFF + c * FC)
        hg = 0.5 * gate
        act = ((hg + hg * jnp.tanh(hg)) * val).astype(BF16)
    acc = acc + down_chunk(act, nchunk - 1)
    y = x + mod[5:6] * acc
    if final:
        ms = jnp.mean(y * y, axis=-1, keepdims=True)
        y = y * lax.rsqrt(ms + EPS) * gf_ref[...]
    o_ref[...] = y


def _ffn(x, mods, layer, g, wup, cw, cb, wdn, gf, *, final):
    out_spec = pl.BlockSpec((TILE, D), lambda i: (i, 0))
    out_shape = jax.ShapeDtypeStruct((NCOL * TILE, D), F32)
    return pl.pallas_call(
        functools.partial(_ffn_kernel, final=final),
        grid=(NCOL,),
        in_specs=[pl.BlockSpec((TILE, D), lambda i: (i, 0)),
                  pl.BlockSpec((None, None, 6, D), lambda i: (layer, _cond_of_col(i), 0, 0)),
                  pl.BlockSpec((1, D), lambda i: (0, 0)),
                  pl.BlockSpec((None, D, 2 * DFF), lambda i: (layer, 0, 0)),
                  pl.BlockSpec((None, 3, 2 * DFF), lambda i: (layer, 0, 0)),
                  pl.BlockSpec((None, 1, 2 * DFF), lambda i: (layer, 0, 0)),
                  pl.BlockSpec((None, DFF, D), lambda i: (layer, 0, 0)),
                  pl.BlockSpec((1, D), lambda i: (0, 0))],
        out_specs=out_spec,
        out_shape=out_shape,
        compiler_params=_cparams(1),
        name="ffn",
    )(x, mods, g, wup, cw, cb, wdn, gf)


def kernel(x_prompt, x_sample, state_ssm_re, state_ssm_im, c, c_ctx, w_ada, b_ada, g_mix, g_ffn,
           ssm_lam_re, ssm_lam_im, ssm_log_dt, ssm_b_re, ssm_b_im, ssm_c_re, ssm_c_im, ssm_d,
           w_glu, b_glu, w_fourier, b_fourier, w_up, conv_w, conv_b, w_down, g_final):
    nb = x_prompt.shape[0]
    x = jnp.concatenate([x_prompt.reshape(NPC * TILE, D), x_sample.reshape(NPC * TILE, D)], axis=0)

    cond8 = jnp.concatenate([c_ctx[None, :], c, jnp.zeros((5, D), F32)], axis=0)
    mods = _ada(cond8, w_ada, b_ada).reshape(DEPTH, 8, 6, D)
    col_cond = jnp.asarray([0] * NPC + [1] * SEG + [2] * SEG, jnp.int32)

    wup_bf = _cast_bf16(w_up.reshape(DEPTH * D, 2 * DFF), 256).reshape(DEPTH, D, 2 * DFF)
    wdn_bf = _cast_bf16(w_down.reshape(DEPTH * DFF, D), DFF).reshape(DEPTH, DFF, D)
    wf_bf = _cast_bf16(w_fourier.reshape(2 * D, D), D).reshape(2, D, D)

    cc, cs = (jnp.asarray(m).astype(BF16) for m in _dft_mats(FGC))
    cl1, sl1 = cc, cs
    cl8, sl8 = (jnp.asarray(m).astype(BF16) for m in _dft_mats(SEG * TILE))

    st_re, st_im = [], []
    for i in range(DEPTH):
        j = i // 2
        if i % 2 == 0:
            sh1 = mods[i, :, 0, :][col_cond]
            sc1 = mods[i, :, 1, :][col_cond]
            g = g_mix[i][None, :]
            lr = ssm_lam_re[j].reshape(2, 1, NST)
            li = ssm_lam_im[j].reshape(2, 1, NST)
            dt = jnp.repeat(ssm_log_dt[j], PS, axis=-1).reshape(2, 1, NST)
            btr = ssm_b_re[j].transpose(0, 3, 1, 2).reshape(2, GC, NST)
            bti = ssm_b_im[j].transpose(0, 3, 1, 2).reshape(2, GC, NST)
            cwr = jnp.tile(ssm_c_re[j].transpose(0, 1, 3, 2).reshape(2, NST, GC), (1, 1, 8))
            cwi = jnp.tile(ssm_c_im[j].transpose(0, 1, 3, 2).reshape(2, NST, GC), (1, 1, 8))
            sre = state_ssm_re[:, j].reshape(2, 2, NST)
            sim = state_ssm_im[:, j].reshape(2, 2, NST)
            h0r = jnp.zeros((2, 2, SEG, NST), F32)
            h0i = jnp.zeros((2, 2, SEG, NST), F32)
            h0r = h0r.at[0, :, 0].set(sre[:, 0]).at[1, :, SEG - 1].set(sre[:, 1]).reshape(2, NPC, NST)
            h0i = h0i.at[0, :, 0].set(sim[:, 0]).at[1, :, SEG - 1].set(sim[:, 1]).reshape(2, NPC, NST)
            x_tm = x.reshape(NCOL, TILE, D).transpose(1, 0, 2)
            args = (x_tm, sc1, sh1, g, lr, li, dt, btr, bti, cwr, cwi)
            _, _, fr, fi = _s5_scan(*args, h0r, h0i, ntr=1, col_off=1, emit_y=False)
            ir, ii = _carry(fr, fi, h0r, h0i, lr, li, dt)
            z = jnp.zeros((2, NPC, NST), F32)
            y, er, ei, _, _ = _s5_scan(*args, jnp.concatenate([z, ir], axis=1),
                                       jnp.concatenate([z, ii], axis=1), ntr=2, col_off=0, emit_y=True)
            st_re.append(er[:, :nb])
            st_im.append(ei[:, :nb])
            x = _glu(x, y.transpose(0, 2, 1, 3), mods, i, g, ssm_d[j][None, :], w_glu, b_glu[j][None, :])
        else:
            g = g_mix[i][None, :]
            bf = b_fourier[j][None, :]
            x = _fourier(x, mods, i, g, cc, cs, cl1, sl1, wf_bf, bf,
                         nseq=NPC, nseg=1, col_off=0, cond_off=0, cond_stride=0)
            x = _fourier(x, mods, i, g, cc, cs, cl8, sl8, wf_bf, bf,
                         nseq=2, nseg=SEG, col_off=NPC, cond_off=1, cond_stride=1)
        x = _ffn(x, mods, i, g_ffn[i][None, :], wup_bf, conv_w, conv_b.reshape(DEPTH, 1, 2 * DFF),
                 wdn_bf, g_final[None, :], final=(i == DEPTH - 1))

    y_prompt = x[:NPC * TILE].reshape(NPC, TILE, D)
    y_sample = x[NPC * TILE:].reshape(2, SEG * TILE, D)
    new_re = jnp.stack(st_re, axis=0).transpose(2, 0, 1, 3).reshape(nb, 2, 2, GS, PS)
    new_im = jnp.stack(st_im, axis=0).transpose(2, 0, 1, 3).reshape(nb, 2, 2, GS, PS)
    return (y_prompt, y_sample, new_re, new_im)
```

```python
import functools
import math

import jax
import jax.numpy as jnp
import numpy as np
from jax import lax
from jax.experimental import pallas as pl
from jax.experimental.pallas import tpu as pltpu

F32 = jnp.float32
BF16 = jnp.bfloat16

D = 1024
TILE = 256
NCOL = 32
NPC = 16
SEG = 8
DEPTH = 4
GS = 64
GC = 16
PS = 64
NST = GS * PS
FG = 4
FGC = 256
DFF = 2816
EPS = 1e-6
GRID_W = 64

TT = 32
NCH = TILE // TT
SW = 512
NSLAB = NST // SW
FC = 256

VMEM_LIMIT = 56 * 1024 * 1024


def _cparams(n_axes):
    return pltpu.CompilerParams(dimension_semantics=("arbitrary",) * n_axes,
                                vmem_limit_bytes=VMEM_LIMIT)


def _normmod(x, g, sc, sh):
    ms = jnp.mean(x * x, axis=-1, keepdims=True)
    return x * lax.rsqrt(ms + EPS) * g * (1.0 + sc) + sh


def _sigmoid(x):
    return 1.0 / (1.0 + jnp.exp(-x))


def _cond_of_col(i):
    return jnp.where(i < NPC, 0, 1 + (i - NPC) // SEG)


def _ada_kernel(c_ref, w_ref, b_ref, o_ref):
    c = c_ref[...]
    s = (c * _sigmoid(c)).astype(BF16)
    o_ref[...] = jnp.dot(s, w_ref[...].astype(BF16), preferred_element_type=F32) + b_ref[...]


def _ada(cond8, w_ada, b_ada):
    tn = 1536
    return pl.pallas_call(
        _ada_kernel,
        grid=(DEPTH, 6 * D // tn),
        in_specs=[pl.BlockSpec((8, D), lambda l, n: (0, 0)),
                  pl.BlockSpec((None, D, tn), lambda l, n: (l, 0, n)),
                  pl.BlockSpec((None, 1, tn), lambda l, n: (l, 0, n))],
        out_specs=pl.BlockSpec((None, 8, tn), lambda l, n: (l, 0, n)),
        out_shape=jax.ShapeDtypeStruct((DEPTH, 8, 6 * D), F32),
        compiler_params=_cparams(2),
        name="ada",
    )(cond8, w_ada, b_ada.reshape(DEPTH, 1, 6 * D))


def _cast_kernel(w_ref, o_ref):
    o_ref[...] = w_ref[...].astype(BF16)


def _cast_bf16(w2d, block_rows):
    r, c = w2d.shape
    return pl.pallas_call(
        _cast_kernel,
        grid=(r // block_rows,),
        in_specs=[pl.BlockSpec((block_rows, c), lambda i: (i, 0))],
        out_specs=pl.BlockSpec((block_rows, c), lambda i: (i, 0)),
        out_shape=jax.ShapeDtypeStruct((r, c), BF16),
        compiler_params=_cparams(1),
        name="cast",
    )(w2d)


def _abar(lr, li, logdt):
    dt = jnp.exp(logdt)
    mag = jnp.exp(lr * dt)
    return mag * jnp.cos(li * dt), mag * jnp.sin(li * dt)


def _s5_kernel(x_ref, sc_ref, sh_ref, g_ref, lr_ref, li_ref, dt_ref, btr_ref, bti_ref,
               cwr_ref, cwi_ref, h0r_ref, h0i_ref, *rest, emit_y, rev):
    if emit_y:
        y_ref, er_ref, ei_ref, fr_ref, fi_ref = rest[:5]
        scratch = rest[5:]
    else:
        y_ref = None
        er_ref, ei_ref, fr_ref, fi_ref = rest[:4]
        scratch = rest[4:]
    bu_re = scratch[0:NSLAB]
    bu_im = scratch[NSLAB:2 * NSLAB]
    bt_s, ct_s, ar_s, ai_s, hs_re, hs_im = scratch[2 * NSLAB:]

    i = pl.program_id(1)

    @pl.when(i == 0)
    def _prep():
        lr = lr_ref[...]
        li = li_ref[...]
        ar, ai = _abar(lr, li, dt_ref[...])
        ar_s[...] = jnp.broadcast_to(ar, (8, NST))
        ai_s[...] = jnp.broadcast_to(ai, (8, NST))
        xr = ar - 1.0
        den = lr * lr + li * li
        fr = (xr * lr + ai * li) / den
        fi = (ai * lr - xr * li) / den
        btr = btr_ref[...]
        bti = bti_ref[...]
        bbr = fr * btr - fi * bti
        bbi = fr * bti + fi * btr
        r = lax.broadcasted_iota(jnp.int32, (128, SW), 0)
        c = lax.broadcasted_iota(jnp.int32, (128, SW), 1)
        bmask = ((r >> 4) == (c >> 6)).astype(F32)
        for m in range(NSLAB):
            sl = slice(m * SW, (m + 1) * SW)
            tr = jnp.concatenate([bbr[:, sl]] * 8, axis=0) * bmask
            ti = jnp.concatenate([bbi[:, sl]] * 8, axis=0) * bmask
            bt_s[m, :, 0:SW] = tr.astype(BF16)
            bt_s[m, :, SW:2 * SW] = ti.astype(BF16)
        r2 = lax.broadcasted_iota(jnp.int32, (SW, 128), 0)
        c2 = lax.broadcasted_iota(jnp.int32, (SW, 128), 1)
        cmask = ((r2 >> 6) == (c2 >> 4)).astype(F32)
        for n in range(NSLAB):
            sl = slice(n * SW, (n + 1) * SW)
            ct_s[n, 0:SW, :] = (cwr_ref[sl, :] * cmask).astype(BF16)
            ct_s[n, SW:2 * SW, :] = (-(cwi_ref[sl, :] * cmask)).astype(BF16)
        hs_re[...] = h0r_ref[...]
        hs_im[...] = h0i_ref[...]

    x3 = x_ref[...]
    ms = jnp.mean(x3 * x3, axis=-1, keepdims=True)
    u3 = x3 * lax.rsqrt(ms + EPS) * g_ref[...][None] * (1.0 + sc_ref[...][None]) + sh_ref[...][None]
    u = u3.reshape(TT * NPC, D).astype(BF16)

    def expand(m):
        res = jnp.dot(u[:, m * 128:(m + 1) * 128], bt_s[m], preferred_element_type=F32)
        bu_re[m][...] = res[:, 0:SW]
        bu_im[m][...] = res[:, SW:2 * SW]

    def recur(s):
        lanes = slice(s * SW, (s + 1) * SW)
        ar = ar_s[:, lanes]
        ai = ai_s[:, lanes]
        h = [hs_re[0:8, lanes], hs_im[0:8, lanes], hs_re[8:16, lanes], hs_im[8:16, lanes]]
        order = range(TT - 1, -1, -1) if rev else range(TT)
        for t in order:
            for q in range(2):
                rows = slice(t * NPC + 8 * q, t * NPC + 8 * q + 8)
                hr, hi = h[2 * q], h[2 * q + 1]
                nr = ar * hr - ai * hi + bu_re[s][rows, :]
                ni = ar * hi + ai * hr + bu_im[s][rows, :]
                bu_re[s][rows, :] = nr
                bu_im[s][rows, :] = ni
                h[2 * q], h[2 * q + 1] = nr, ni
        hs_re[0:8, lanes] = h[0]
        hs_im[0:8, lanes] = h[1]
        hs_re[8:16, lanes] = h[2]
        hs_im[8:16, lanes] = h[3]

    def contract(n):
        yn = (jnp.dot(bu_re[n][...].astype(BF16), ct_s[n, 0:SW, :], preferred_element_type=F32)
              + jnp.dot(bu_im[n][...].astype(BF16), ct_s[n, SW:2 * SW, :], preferred_element_type=F32))
        y_ref[:, :, n * 128:(n + 1) * 128] = yn.astype(BF16).reshape(TT, NPC, 128)

    expand(0)
    for s in range(NSLAB):
        if s + 1 < NSLAB:
            expand(s + 1)
        if emit_y and s >= 1:
            contract(s - 1)
        recur(s)
    if emit_y:
        contract(NSLAB - 1)

    @pl.when(i == 0)
    def _edge():
        e0 = (TT - 1) * NPC if rev else 0
        for s in range(NSLAB):
            er_ref[:, s * SW:(s + 1) * SW] = bu_re[s][e0:e0 + NPC, :]
            ei_ref[:, s * SW:(s + 1) * SW] = bu_im[s][e0:e0 + NPC, :]

    @pl.when(i == NCH - 1)
    def _fin():
        fr_ref[...] = hs_re[...]
        fi_ref[...] = hs_im[...]


def _s5_scan(x, sc, sh, g, lr, li, dt, btr, bti, cwr, cwi, h0r, h0i, *, d, ntr, col_off, emit_y):
    rev = d == 1

    def chunk(i):
        return NCH - 1 - i if rev else i

    par = lambda k, i: (d, 0, 0)
    st_in = lambda k, i: (d, k, 0)
    st = lambda k, i: (k, 0)
    in_specs = [
        pl.BlockSpec((TT, NPC, D), lambda k, i: (chunk(i), k + col_off, 0)),
        pl.BlockSpec((NPC, D), lambda k, i: (k + col_off, 0)),
        pl.BlockSpec((NPC, D), lambda k, i: (k + col_off, 0)),
        pl.BlockSpec((1, D), lambda k, i: (0, 0)),
        pl.BlockSpec((None, 1, NST), par),
        pl.BlockSpec((None, 1, NST), par),
        pl.BlockSpec((None, 1, NST), par),
        pl.BlockSpec((None, GC, NST), par),
        pl.BlockSpec((None, GC, NST), par),
        pl.BlockSpec((None, NST, 128), par),
        pl.BlockSpec((None, NST, 128), par),
        pl.BlockSpec((None, NPC, NST), st_in),
        pl.BlockSpec((None, NPC, NST), st_in),
    ]
    st_shape = jax.ShapeDtypeStruct((NPC * ntr, NST), F32)
    out_specs = [pl.BlockSpec((NPC, NST), st)] * 4
    out_shape = [st_shape] * 4
    if emit_y:
        out_specs = [pl.BlockSpec((TT, NPC, D), lambda k, i: (chunk(i), k, 0))] + out_specs
        out_shape = [jax.ShapeDtypeStruct((TILE, NPC * ntr, D), BF16)] + out_shape
    slab = pltpu.VMEM((TT * NPC, SW), F32)
    return pl.pallas_call(
        functools.partial(_s5_kernel, emit_y=emit_y, rev=rev),
        grid=(ntr, NCH),
        in_specs=in_specs,
        out_specs=out_specs,
        out_shape=out_shape,
        scratch_shapes=[slab] * (2 * NSLAB) + [
            pltpu.VMEM((NSLAB, 128, 2 * SW), BF16), pltpu.VMEM((NSLAB, 2 * SW, 128), BF16),
            pltpu.VMEM((8, NST), F32), pltpu.VMEM((8, NST), F32),
            pltpu.VMEM((NPC, NST), F32), pltpu.VMEM((NPC, NST), F32)],
        compiler_params=_cparams(2),
        name=("s5_scan" if emit_y else "s5_states") + ("_bwd" if rev else "_fwd"),
    )(x, sc, sh, g, lr, li, dt, btr, bti, cwr, cwi, h0r, h0i)


def _carry_kernel(fr_ref, fi_ref, h0r_ref, h0i_ref, lr_ref, li_ref, dt_ref, or_ref, oi_ref):
    for d in range(2):
        ar, ai = _abar(lr_ref[d], li_ref[d], dt_ref[d])
        for _ in range(8):
            ar, ai = ar * ar - ai * ai, 2.0 * ar * ai
        for b in range(2):
            order = list(range(SEG)) if d == 0 else list(range(SEG - 1, -1, -1))
            r = b * SEG + order[0]
            or_ref[d, r:r + 1, :] = h0r_ref[d, r:r + 1, :]
            oi_ref[d, r:r + 1, :] = h0i_ref[d, r:r + 1, :]
            tr = fr_ref[d, r:r + 1, :]
            ti = fi_ref[d, r:r + 1, :]
            for s in order[1:]:
                r = b * SEG + s
                or_ref[d, r:r + 1, :] = tr
                oi_ref[d, r:r + 1, :] = ti
                tr, ti = (fr_ref[d, r:r + 1, :] + ar * tr - ai * ti,
                          fi_ref[d, r:r + 1, :] + ar * ti + ai * tr)


def _carry(fr, fi, h0r, h0i, lr, li, dt):
    shp = jax.ShapeDtypeStruct((2, NPC, NST), F32)
    return pl.pallas_call(_carry_kernel, out_shape=[shp, shp], name="s5_carry")(fr, fi, h0r, h0i, lr, li, dt)


def _glu_kernel(xa_ref, xb_ref, yf_ref, yb_ref, mod_ref, g_ref, d_ref, w_ref, b_ref, o_ref, wbf):
    i = pl.program_id(0)

    @pl.when(i == 0)
    def _():
        wbf[...] = w_ref[...].astype(BF16)

    x = jnp.where(i < NPC, xa_ref[...], xb_ref[...])
    mod = mod_ref[...]
    h = _normmod(x, g_ref[...], mod[1:2], mod[0:1])
    y = yf_ref[...].astype(F32) + yb_ref[...].astype(F32) + d_ref[...] * h
    ge = 0.5 * y * (1.0 + jnp.tanh(math.sqrt(2.0 / math.pi) * (y + 0.044715 * (y * y * y))))
    z = jnp.dot(ge.astype(BF16), wbf[...], preferred_element_type=F32) + b_ref[...]
    out = z[:, 0:D] * _sigmoid(z[:, D:2 * D])
    o_ref[...] = x + mod[2:3] * out


def _glu(xa, xb, off_b, yf, yb, mods, layer, g, dvec, w, b):
    return pl.pallas_call(
        _glu_kernel,
        grid=(NCOL,),
        in_specs=[pl.BlockSpec((TILE, D), lambda i: (jnp.minimum(i, NPC - 1), 0)),
                  pl.BlockSpec((TILE, D), lambda i: (off_b + jnp.maximum(i - NPC, 0), 0)),
                  pl.BlockSpec((None, TILE, D), lambda i: (i, 0, 0)),
                  pl.BlockSpec((None, TILE, D), lambda i: (i, 0, 0)),
                  pl.BlockSpec((None, None, 6, D), lambda i: (layer, _cond_of_col(i), 0, 0)),
                  pl.BlockSpec((1, D), lambda i: (0, 0)),
                  pl.BlockSpec((1, D), lambda i: (0, 0)),
                  pl.BlockSpec((None, D, 2 * D), lambda i: (layer // 2, 0, 0)),
                  pl.BlockSpec((1, 2 * D), lambda i: (0, 0))],
        out_specs=pl.BlockSpec((TILE, D), lambda i: (i, 0)),
        out_shape=jax.ShapeDtypeStruct((NCOL * TILE, D), F32),
        scratch_shapes=[pltpu.VMEM((D, 2 * D), BF16)],
        compiler_params=_cparams(1),
        name="s5_glu",
    )(xa, xb, yf, yb, mods, g, dvec, w, b)


def _fourier_kernel(x_ref, mod_ref, g_ref, cc_ref, cs_ref, cl_ref, sl_ref, w_ref, b_ref,
                    o_ref, xc_s, xs_s, *, scale):
    ph = pl.program_id(1)
    s = pl.program_id(2)
    mod = mod_ref[...]

    @pl.when(ph == 0)
    def _channel_dft():
        h = _normmod(x_ref[...], g_ref[...], mod[1:2], mod[0:1]).astype(BF16)
        r0 = pl.multiple_of(s * TILE, TILE)
        for q in range(FG):
            hq = h[:, q * FGC:(q + 1) * FGC]
            xc_s[pl.ds(r0, TILE), q * FGC:(q + 1) * FGC] = jnp.dot(
                hq, cc_ref[...], preferred_element_type=F32).astype(BF16)
            xs_s[pl.ds(r0, TILE), q * FGC:(q + 1) * FGC] = jnp.dot(
                hq, cs_ref[...], preferred_element_type=F32).astype(BF16)

    @pl.when(ph == 1)
    def _position_dft():
        f = (jnp.dot(cl_ref[...], xc_s[...], preferred_element_type=F32)
             - jnp.dot(sl_ref[...], xs_s[...], preferred_element_type=F32)) * scale
        o = jnp.dot(f.astype(BF16), w_ref[...], preferred_element_type=F32) + b_ref[...]
        o_ref[...] = x_ref[...] + mod[2:3] * o


def _fourier(x, mods, layer, g, cc, cs, cl, sl, w, b, *, nseq, nseg, col_off, cond_off, cond_stride):
    ln = nseg * TILE
    col = lambda q, ph, s: (col_off + q * nseg + s, 0)
    return pl.pallas_call(
        functools.partial(_fourier_kernel, scale=1.0 / math.sqrt(ln * FGC)),
        grid=(nseq, 2, nseg),
        in_specs=[pl.BlockSpec((TILE, D), col),
                  pl.BlockSpec((None, None, 6, D), lambda q, ph, s: (layer, cond_off + q * cond_stride, 0, 0)),
                  pl.BlockSpec((1, D), lambda q, ph, s: (0, 0)),
                  pl.BlockSpec((FGC, FGC), lambda q, ph, s: (0, 0)),
                  pl.BlockSpec((FGC, FGC), lambda q, ph, s: (0, 0)),
                  pl.BlockSpec((TILE, ln), lambda q, ph, s: (s * ph, 0)),
                  pl.BlockSpec((TILE, ln), lambda q, ph, s: (s * ph, 0)),
                  pl.BlockSpec((None, D, D), lambda q, ph, s: (layer // 2, 0, 0)),
                  pl.BlockSpec((1, D), lambda q, ph, s: (0, 0))],
        out_specs=pl.BlockSpec((TILE, D), lambda q, ph, s: (col_off + q * nseg + s * ph, 0)),
        out_shape=jax.ShapeDtypeStruct((NCOL * TILE, D), F32),
        scratch_shapes=[pltpu.VMEM((ln, D), BF16), pltpu.VMEM((ln, D), BF16)],
        input_output_aliases={0: 0},
        compiler_params=_cparams(3),
        name="fourier%d" % nseg,
    )(x, mods, g, cc, cs, cl, sl, w, b)


@functools.lru_cache(maxsize=None)
def _dft_mats(n):
    k = np.arange(n, dtype=np.int64)
    ang = ((k[:, None] * k[None, :]) % n).astype(np.float64) * (2.0 * math.pi / n)
    return np.cos(ang).astype(np.float32), np.sin(ang).astype(np.float32)


def _ffn_kernel(x_ref, mod_ref, g_ref, wup_ref, cw_ref, cb_ref, wdn_ref, gf_ref, *o_refs, final):
    i = pl.program_id(0)
    x = x_ref[...]
    mod = mod_ref[...]
    h = _normmod(x, g_ref[...], mod[4:5], mod[3:4]).astype(BF16)
    sub = lax.broadcasted_iota(jnp.int32, (8, FC), 0)
    one = jnp.ones((8, FC), F32)
    first0 = (sub != 0).astype(F32)
    last0 = (sub != 7).astype(F32)
    inner_first = jnp.where(i < NPC, one, first0)
    inner_last = jnp.where(i < NPC, one, last0)
    prev_slabs = [(0, first0)] + [(r, inner_first) for r in range(GRID_W, TILE, GRID_W)]
    next_slabs = [(r - 8, inner_last) for r in range(GRID_W, TILE, GRID_W)] + [(TILE - 8, last0)]

    def mask_rows(a, slabs):
        parts, last = [], 0
        for r0, m in slabs:
            if r0 > last:
                parts.append(a[last:r0])
            parts.append(a[r0:r0 + 8] * m)
            last = r0 + 8
        if last < TILE:
            parts.append(a[last:])
        return jnp.concatenate(parts, axis=0)

    def conv(up, off):
        w = cw_ref[:, off:off + FC]
        prev = mask_rows(pltpu.roll(up, 1, 0), prev_slabs)
        nxt = mask_rows(pltpu.roll(up, TILE - 1, 0), next_slabs)
        return w[0:1] * prev + w[1:2] * up + w[2:3] * nxt + cb_ref[:, off:off + FC]

    def up_chunk(c):
        og = c * FC
        ov = DFF + c * FC
        return (jnp.dot(h, wup_ref[:, og:og + FC], preferred_element_type=F32),
                jnp.dot(h, wup_ref[:, ov:ov + FC], preferred_element_type=F32))

    def down_chunk(act, c):
        return jnp.dot(act, wdn_ref[c * FC:(c + 1) * FC, :], preferred_element_type=F32)

    nchunk = DFF // FC
    acc = jnp.zeros((TILE, D), F32)
    nxt_up = up_chunk(0)
    act = None
    for c in range(nchunk):
        cur_up = nxt_up
        if c + 1 < nchunk:
            nxt_up = up_chunk(c + 1)
        if act is not None:
            acc = acc + down_chunk(act, c - 1)
        gate = conv(cur_up[0], c * FC)
        val = conv(cur_up[1], DFF + c * FC)
        hg = 0.5 * gate
        act = ((hg + hg * jnp.tanh(hg)) * val).astype(BF16)
    acc = acc + down_chunk(act, nchunk - 1)
    y = x + mod[5:6] * acc
    if not final:
        o_refs[0][...] = y
    else:
        ms = jnp.mean(y * y, axis=-1, keepdims=True)
        y = y * lax.rsqrt(ms + EPS) * gf_ref[...]

        @pl.when(i < NPC)
        def _():
            o_refs[0][...] = y

        @pl.when(i >= NPC)
        def _():
            o_refs[1][...] = y


def _ffn(x, mods, layer, g, wup, cw, cb, wdn, gf, *, final):
    if final:
        out_specs = [pl.BlockSpec((TILE, D), lambda i: (jnp.minimum(i, NPC - 1), 0)),
                     pl.BlockSpec((TILE, D), lambda i: (jnp.maximum(i - NPC, 0), 0))]
        out_shape = [jax.ShapeDtypeStruct((NPC * TILE, D), F32)] * 2
    else:
        out_specs = pl.BlockSpec((TILE, D), lambda i: (i, 0))
        out_shape = jax.ShapeDtypeStruct((NCOL * TILE, D), F32)
    return pl.pallas_call(
        functools.partial(_ffn_kernel, final=final),
        grid=(NCOL,),
        in_specs=[pl.BlockSpec((TILE, D), lambda i: (i, 0)),
                  pl.BlockSpec((None, None, 6, D), lambda i: (layer, _cond_of_col(i), 0, 0)),
                  pl.BlockSpec((1, D), lambda i: (0, 0)),
                  pl.BlockSpec((None, D, 2 * DFF), lambda i: (layer, 0, 0)),
                  pl.BlockSpec((None, 3, 2 * DFF), lambda i: (layer, 0, 0)),
                  pl.BlockSpec((None, 1, 2 * DFF), lambda i: (layer, 0, 0)),
                  pl.BlockSpec((None, DFF, D), lambda i: (layer, 0, 0)),
                  pl.BlockSpec((1, D), lambda i: (0, 0))],
        out_specs=out_specs,
        out_shape=out_shape,
        compiler_params=_cparams(1),
        name="ffn",
    )(x, mods, g, wup, cw, cb, wdn, gf)


def kernel(x_prompt, x_sample, state_ssm_re, state_ssm_im, c, c_ctx, w_ada, b_ada, g_mix, g_ffn,
           ssm_lam_re, ssm_lam_im, ssm_log_dt, ssm_b_re, ssm_b_im, ssm_c_re, ssm_c_im, ssm_d,
           w_glu, b_glu, w_fourier, b_fourier, w_up, conv_w, conv_b, w_down, g_final):
    nb = x_prompt.shape[0]
    xp = x_prompt.reshape(NPC * TILE, D)
    xs = x_sample.reshape(NPC * TILE, D)
    x = None

    cond8 = jnp.concatenate([c_ctx[None, :], c, jnp.zeros((5, D), F32)], axis=0)
    mods = _ada(cond8, w_ada, b_ada).reshape(DEPTH, 8, 6, D)
    col_cond = jnp.asarray([0] * NPC + [1] * SEG + [2] * SEG, jnp.int32)

    wup_bf = _cast_bf16(w_up.reshape(DEPTH * D, 2 * DFF), 256).reshape(DEPTH, D, 2 * DFF)
    wdn_bf = _cast_bf16(w_down.reshape(DEPTH * DFF, D), DFF).reshape(DEPTH, DFF, D)
    wf_bf = _cast_bf16(w_fourier.reshape(2 * D, D), D).reshape(2, D, D)

    cc, cs = (jnp.asarray(m).astype(BF16) for m in _dft_mats(FGC))
    cl1, sl1 = cc, cs
    cl8, sl8 = (jnp.asarray(m).astype(BF16) for m in _dft_mats(SEG * TILE))

    st_re, st_im = [], []
    for i in range(DEPTH):
        j = i // 2
        if i % 2 == 0:
            sh1 = mods[i, :, 0, :][col_cond]
            sc1 = mods[i, :, 1, :][col_cond]
            g = g_mix[i][None, :]
            lr = ssm_lam_re[j].reshape(2, 1, NST)
            li = ssm_lam_im[j].reshape(2, 1, NST)
            dt = jnp.repeat(ssm_log_dt[j], PS, axis=-1).reshape(2, 1, NST)
            btr = ssm_b_re[j].transpose(0, 3, 1, 2).reshape(2, GC, NST)
            bti = ssm_b_im[j].transpose(0, 3, 1, 2).reshape(2, GC, NST)
            cwr = jnp.tile(ssm_c_re[j].transpose(0, 1, 3, 2).reshape(2, NST, GC), (1, 1, 8))
            cwi = jnp.tile(ssm_c_im[j].transpose(0, 1, 3, 2).reshape(2, NST, GC), (1, 1, 8))
            sre = state_ssm_re[:, j].reshape(2, 2, NST)
            sim = state_ssm_im[:, j].reshape(2, 2, NST)
            h0r = jnp.zeros((2, 2, SEG, NST), F32)
            h0i = jnp.zeros((2, 2, SEG, NST), F32)
            h0r = h0r.at[0, :, 0].set(sre[:, 0]).at[1, :, SEG - 1].set(sre[:, 1]).reshape(2, NPC, NST)
            h0i = h0i.at[0, :, 0].set(sim[:, 0]).at[1, :, SEG - 1].set(sim[:, 1]).reshape(2, NPC, NST)
            if x is None:
                x_tm = jnp.concatenate([xp.reshape(NPC, TILE, D).transpose(1, 0, 2),
                                        xs.reshape(NPC, TILE, D).transpose(1, 0, 2)], axis=1)
            else:
                x_tm = x.reshape(NCOL, TILE, D).transpose(1, 0, 2)
            args = (x_tm, sc1, sh1, g, lr, li, dt, btr, bti, cwr, cwi)
            fin = [_s5_scan(*args, h0r, h0i, d=d, ntr=1, col_off=1, emit_y=False)[2:] for d in range(2)]
            fr = jnp.stack([fin[0][0], fin[1][0]], axis=0)
            fi = jnp.stack([fin[0][1], fin[1][1]], axis=0)
            ir, ii = _carry(fr, fi, h0r, h0i, lr, li, dt)
            z = jnp.zeros((2, NPC, NST), F32)
            h0r2 = jnp.concatenate([z, ir], axis=1)
            h0i2 = jnp.concatenate([z, ii], axis=1)
            outs = [_s5_scan(*args, h0r2, h0i2, d=d, ntr=2, col_off=0, emit_y=True) for d in range(2)]
            st_re.append(jnp.stack([outs[0][1][:nb], outs[1][1][:nb]], axis=0))
            st_im.append(jnp.stack([outs[0][2][:nb], outs[1][2][:nb]], axis=0))
            yf = outs[0][0].transpose(1, 0, 2)
            yb = outs[1][0].transpose(1, 0, 2)
            xa, xb, off_b = (xp, xs, 0) if x is None else (x, x, NPC)
            x = _glu(xa, xb, off_b, yf, yb, mods, i, g, ssm_d[j][None, :], w_glu, b_glu[j][None, :])
        else:
            g = g_mix[i][None, :]
            bf = b_fourier[j][None, :]
            x = _fourier(x, mods, i, g, cc, cs, cl1, sl1, wf_bf, bf,
                         nseq=NPC, nseg=1, col_off=0, cond_off=0, cond_stride=0)
            x = _fourier(x, mods, i, g, cc, cs, cl8, sl8, wf_bf, bf,
                         nseq=2, nseg=SEG, col_off=NPC, cond_off=1, cond_stride=1)
        x = _ffn(x, mods, i, g_ffn[i][None, :], wup_bf, conv_w, conv_b.reshape(DEPTH, 1, 2 * DFF),
                 wdn_bf, g_final[None, :], final=(i == DEPTH - 1))

    y_prompt = x[0].reshape(NPC, TILE, D)
    y_sample = x[1].reshape(2, SEG * TILE, D)
    new_re = jnp.stack(st_re, axis=0).transpose(2, 0, 1, 3).reshape(nb, 2, 2, GS, PS)
    new_im = jnp.stack(st_im, axis=0).transpose(2, 0, 1, 3).reshape(nb, 2, 2, GS, PS)
    return (y_prompt, y_sample, new_re, new_im)
```

```python
import functools
import math

import jax
import jax.numpy as jnp
import numpy as np
from jax import lax
from jax.experimental import pallas as pl
from jax.experimental.pallas import tpu as pltpu

F32 = jnp.float32
BF16 = jnp.bfloat16

D = 1024
TILE = 256
NCOL = 32
NPC = 16
SEG = 8
DEPTH = 4
GS = 64
GC = 16
PS = 64
NST = GS * PS
FG = 4
FGC = 256
DFF = 2816
EPS = 1e-6
GRID_W = 64

TT = 32
NCH = TILE // TT
SW = 512
NSLAB = NST // SW
FC = 256

VMEM_LIMIT = 56 * 1024 * 1024


def _cparams(n_axes):
    return pltpu.CompilerParams(dimension_semantics=("arbitrary",) * n_axes,
                                vmem_limit_bytes=VMEM_LIMIT)


def _normmod(x, g, sc, sh):
    ms = jnp.mean(x * x, axis=-1, keepdims=True)
    return x * lax.rsqrt(ms + EPS) * g * (1.0 + sc) + sh


def _sigmoid(x):
    return 1.0 / (1.0 + jnp.exp(-x))


def _cond_of_col(i):
    return jnp.where(i < NPC, 0, 1 + (i - NPC) // SEG)


def _ada_kernel(c_ref, w_ref, b_ref, o_ref):
    c = c_ref[...]
    s = (c * _sigmoid(c)).astype(BF16)
    o_ref[...] = jnp.dot(s, w_ref[...].astype(BF16), preferred_element_type=F32) + b_ref[...]


def _ada(cond8, w_ada, b_ada):
    tn = 1536
    return pl.pallas_call(
        _ada_kernel,
        grid=(DEPTH, 6 * D // tn),
        in_specs=[pl.BlockSpec((8, D), lambda l, n: (0, 0)),
                  pl.BlockSpec((None, D, tn), lambda l, n: (l, 0, n)),
                  pl.BlockSpec((None, 1, tn), lambda l, n: (l, 0, n))],
        out_specs=pl.BlockSpec((None, 8, tn), lambda l, n: (l, 0, n)),
        out_shape=jax.ShapeDtypeStruct((DEPTH, 8, 6 * D), F32),
        compiler_params=_cparams(2),
        name="ada",
    )(cond8, w_ada, b_ada.reshape(DEPTH, 1, 6 * D))


def _cast_kernel(w_ref, o_ref):
    o_ref[...] = w_ref[...].astype(BF16)


def _cast_bf16(w2d, block_rows):
    r, c = w2d.shape
    return pl.pallas_call(
        _cast_kernel,
        grid=(r // block_rows,),
        in_specs=[pl.BlockSpec((block_rows, c), lambda i: (i, 0))],
        out_specs=pl.BlockSpec((block_rows, c), lambda i: (i, 0)),
        out_shape=jax.ShapeDtypeStruct((r, c), BF16),
        compiler_params=_cparams(1),
        name="cast",
    )(w2d)


def _abar(lr, li, logdt):
    dt = jnp.exp(logdt)
    mag = jnp.exp(lr * dt)
    return mag * jnp.cos(li * dt), mag * jnp.sin(li * dt)


def _s5_kernel(xa_hbm, xb_hbm, sc_ref, sh_ref, g_ref, lr_ref, li_ref, dt_ref, btr_ref, bti_ref,
               cwr_ref, cwi_ref, h0r_ref, h0i_ref, *rest, emit_y, rev, ntr, col_off, offa, offb):
    if emit_y:
        y_hbm, er_ref, ei_ref, fr_ref, fi_ref = rest[:5]
        scratch = rest[5:]
    else:
        y_hbm = None
        er_ref, ei_ref, fr_ref, fi_ref = rest[:4]
        scratch = rest[4:]
    bu_re = scratch[0:NSLAB]
    bu_im = scratch[NSLAB:2 * NSLAB]
    bt_s, ct_s, ar_s, ai_s, hs_re, hs_im, xbuf, sem_in = scratch[2 * NSLAB:2 * NSLAB + 8]
    if emit_y:
        ybuf, sem_out = scratch[2 * NSLAB + 8:]

    k = pl.program_id(0)
    i = pl.program_id(1)
    n = k * NCH + i
    nsteps = ntr * NCH
    slot = lax.rem(n, 2)

    def t_start(i_):
        return pl.multiple_of((NCH - 1 - i_ if rev else i_) * TT, TT)

    def x_copies(src, base, i_, slot_):
        return [pltpu.make_async_copy(src.at[base + c, pl.ds(t_start(i_), TT), :],
                                      xbuf.at[slot_, :, c, :], sem_in.at[slot_]) for c in range(NPC)]

    def start_x(k_, i_, slot_):
        if ntr == 1:
            src, base = (xa_hbm, offa) if col_off == 0 else (xb_hbm, offb)
            for cp in x_copies(src, base, i_, slot_):
                cp.start()
        else:
            @pl.when(k_ == 0)
            def _():
                for cp in x_copies(xa_hbm, offa, i_, slot_):
                    cp.start()

            @pl.when(k_ == 1)
            def _():
                for cp in x_copies(xb_hbm, offb, i_, slot_):
                    cp.start()

    def y_copies(k_, i_, slot_):
        return [pltpu.make_async_copy(ybuf.at[slot_, :, c, :],
                                      y_hbm.at[NPC * k_ + c, pl.ds(t_start(i_), TT), :],
                                      sem_out.at[slot_]) for c in range(NPC)]

    @pl.when(n == 0)
    def _first_fetch():
        start_x(k, i, slot)

    @pl.when(n + 1 < nsteps)
    def _prefetch():
        last = i == NCH - 1
        start_x(jnp.where(last, k + 1, k), jnp.where(last, 0, i + 1), 1 - slot)

    if emit_y:
        @pl.when(n >= 2)
        def _ybuf_free():
            for cp in y_copies(k, i, slot):
                cp.wait()

    @pl.when(i == 0)
    def _prep():
        lr = lr_ref[...]
        li = li_ref[...]
        ar, ai = _abar(lr, li, dt_ref[...])
        ar_s[...] = jnp.broadcast_to(ar, (8, NST))
        ai_s[...] = jnp.broadcast_to(ai, (8, NST))
        xr = ar - 1.0
        den = lr * lr + li * li
        fr = (xr * lr + ai * li) / den
        fi = (ai * lr - xr * li) / den
        btr = btr_ref[...]
        bti = bti_ref[...]
        bbr = fr * btr - fi * bti
        bbi = fr * bti + fi * btr
        r = lax.broadcasted_iota(jnp.int32, (128, SW), 0)
        c = lax.broadcasted_iota(jnp.int32, (128, SW), 1)
        bmask = ((r >> 4) == (c >> 6)).astype(F32)
        for m in range(NSLAB):
            sl = slice(m * SW, (m + 1) * SW)
            tr = jnp.concatenate([bbr[:, sl]] * 8, axis=0) * bmask
            ti = jnp.concatenate([bbi[:, sl]] * 8, axis=0) * bmask
            bt_s[m, :, 0:SW] = tr.astype(BF16)
            bt_s[m, :, SW:2 * SW] = ti.astype(BF16)
        r2 = lax.broadcasted_iota(jnp.int32, (SW, 128), 0)
        c2 = lax.broadcasted_iota(jnp.int32, (SW, 128), 1)
        cmask = ((r2 >> 6) == (c2 >> 4)).astype(F32)
        for n in range(NSLAB):
            sl = slice(n * SW, (n + 1) * SW)
            ct_s[n, 0:SW, :] = (cwr_ref[sl, :] * cmask).astype(BF16)
            ct_s[n, SW:2 * SW, :] = (-(cwi_ref[sl, :] * cmask)).astype(BF16)
        hs_re[...] = h0r_ref[...]
        hs_im[...] = h0i_ref[...]

    for cp in x_copies(xa_hbm, offa, i, slot):
        cp.wait()
    x3 = xbuf[slot]
    ms = jnp.mean(x3 * x3, axis=-1, keepdims=True)
    u3 = x3 * lax.rsqrt(ms + EPS) * g_ref[...][None] * (1.0 + sc_ref[...][None]) + sh_ref[...][None]
    u = u3.reshape(TT * NPC, D).astype(BF16)

    def expand(m):
        res = jnp.dot(u[:, m * 128:(m + 1) * 128], bt_s[m], preferred_element_type=F32)
        bu_re[m][...] = res[:, 0:SW]
        bu_im[m][...] = res[:, SW:2 * SW]

    def recur(s):
        lanes = slice(s * SW, (s + 1) * SW)
        ar = ar_s[:, lanes]
        ai = ai_s[:, lanes]
        h = [hs_re[0:8, lanes], hs_im[0:8, lanes], hs_re[8:16, lanes], hs_im[8:16, lanes]]
        order = range(TT - 1, -1, -1) if rev else range(TT)
        for t in order:
            for q in range(2):
                rows = slice(t * NPC + 8 * q, t * NPC + 8 * q + 8)
                hr, hi = h[2 * q], h[2 * q + 1]
                nr = ar * hr - ai * hi + bu_re[s][rows, :]
                ni = ar * hi + ai * hr + bu_im[s][rows, :]
                bu_re[s][rows, :] = nr
                bu_im[s][rows, :] = ni
                h[2 * q], h[2 * q + 1] = nr, ni
        hs_re[0:8, lanes] = h[0]
        hs_im[0:8, lanes] = h[1]
        hs_re[8:16, lanes] = h[2]
        hs_im[8:16, lanes] = h[3]

    def contract(q):
        yq = (jnp.dot(bu_re[q][...].astype(BF16), ct_s[q, 0:SW, :], preferred_element_type=F32)
              + jnp.dot(bu_im[q][...].astype(BF16), ct_s[q, SW:2 * SW, :], preferred_element_type=F32))
        ybuf[slot, :, :, q * 128:(q + 1) * 128] = yq.reshape(TT, NPC, 128)

    expand(0)
    for s in range(NSLAB):
        if s + 1 < NSLAB:
            expand(s + 1)
        if emit_y and s >= 1:
            contract(s - 1)
        recur(s)
    if emit_y:
        contract(NSLAB - 1)
        for cp in y_copies(k, i, slot):
            cp.start()

        @pl.when(n == nsteps - 1)
        def _drain():
            for cp in y_copies(k, i, slot) + y_copies(k, i, 1 - slot):
                cp.wait()

    @pl.when(i == 0)
    def _edge():
        e0 = (TT - 1) * NPC if rev else 0
        for s in range(NSLAB):
            er_ref[:, s * SW:(s + 1) * SW] = bu_re[s][e0:e0 + NPC, :]
            ei_ref[:, s * SW:(s + 1) * SW] = bu_im[s][e0:e0 + NPC, :]

    @pl.when(i == NCH - 1)
    def _fin():
        fr_ref[...] = hs_re[...]
        fi_ref[...] = hs_im[...]


def _s5_scan(xa, xb, offa, offb, sc, sh, g, lr, li, dt, btr, bti, cwr, cwi, h0r, h0i, *, d, ntr, col_off, emit_y):
    rev = d == 1
    par = lambda k, i: (d, 0, 0)
    st_in = lambda k, i: (d, k, 0)
    st = lambda k, i: (k, 0)
    in_specs = [
        pl.BlockSpec(memory_space=pl.ANY),
        pl.BlockSpec(memory_space=pl.ANY),
        pl.BlockSpec((NPC, D), lambda k, i: (k + col_off, 0)),
        pl.BlockSpec((NPC, D), lambda k, i: (k + col_off, 0)),
        pl.BlockSpec((1, D), lambda k, i: (0, 0)),
        pl.BlockSpec((None, 1, NST), par),
        pl.BlockSpec((None, 1, NST), par),
        pl.BlockSpec((None, 1, NST), par),
        pl.BlockSpec((None, GC, NST), par),
        pl.BlockSpec((None, GC, NST), par),
        pl.BlockSpec((None, NST, 128), par),
        pl.BlockSpec((None, NST, 128), par),
        pl.BlockSpec((None, NPC, NST), st_in),
        pl.BlockSpec((None, NPC, NST), st_in),
    ]
    st_shape = jax.ShapeDtypeStruct((NPC * ntr, NST), F32)
    out_specs = [pl.BlockSpec((NPC, NST), st)] * 4
    out_shape = [st_shape] * 4
    slab = pltpu.VMEM((TT * NPC, SW), F32)
    scratch = [slab] * (2 * NSLAB) + [
        pltpu.VMEM((NSLAB, 128, 2 * SW), BF16), pltpu.VMEM((NSLAB, 2 * SW, 128), BF16),
        pltpu.VMEM((8, NST), F32), pltpu.VMEM((8, NST), F32),
        pltpu.VMEM((NPC, NST), F32), pltpu.VMEM((NPC, NST), F32),
        pltpu.VMEM((2, TT, NPC, D), F32), pltpu.SemaphoreType.DMA((2,))]
    if emit_y:
        out_specs = [pl.BlockSpec(memory_space=pl.ANY)] + out_specs
        out_shape = [jax.ShapeDtypeStruct((NPC * ntr, TILE, D), F32)] + out_shape
        scratch += [pltpu.VMEM((2, TT, NPC, D), F32), pltpu.SemaphoreType.DMA((2,))]
    return pl.pallas_call(
        functools.partial(_s5_kernel, emit_y=emit_y, rev=rev, ntr=ntr, col_off=col_off, offa=offa, offb=offb),
        grid=(ntr, NCH),
        in_specs=in_specs,
        out_specs=out_specs,
        out_shape=out_shape,
        scratch_shapes=scratch,
        compiler_params=_cparams(2),
        name=("s5_scan" if emit_y else "s5_states") + ("_bwd" if rev else "_fwd"),
    )(xa, xb, sc, sh, g, lr, li, dt, btr, bti, cwr, cwi, h0r, h0i)


def _carry_kernel(fr_ref, fi_ref, h0r_ref, h0i_ref, lr_ref, li_ref, dt_ref, or_ref, oi_ref):
    for d in range(2):
        ar, ai = _abar(lr_ref[d], li_ref[d], dt_ref[d])
        for _ in range(8):
            ar, ai = ar * ar - ai * ai, 2.0 * ar * ai
        for b in range(2):
            order = list(range(SEG)) if d == 0 else list(range(SEG - 1, -1, -1))
            r = b * SEG + order[0]
            or_ref[d, r:r + 1, :] = h0r_ref[d, r:r + 1, :]
            oi_ref[d, r:r + 1, :] = h0i_ref[d, r:r + 1, :]
            tr = fr_ref[d, r:r + 1, :]
            ti = fi_ref[d, r:r + 1, :]
            for s in order[1:]:
                r = b * SEG + s
                or_ref[d, r:r + 1, :] = tr
                oi_ref[d, r:r + 1, :] = ti
                tr, ti = (fr_ref[d, r:r + 1, :] + ar * tr - ai * ti,
                          fi_ref[d, r:r + 1, :] + ar * ti + ai * tr)


def _carry(fr, fi, h0r, h0i, lr, li, dt):
    shp = jax.ShapeDtypeStruct((2, NPC, NST), F32)
    return pl.pallas_call(_carry_kernel, out_shape=[shp, shp], name="s5_carry")(fr, fi, h0r, h0i, lr, li, dt)


def _glu_kernel(xa_ref, xb_ref, yf_ref, yb_ref, mod_ref, g_ref, d_ref, w_ref, b_ref, o_ref, wbf):
    i = pl.program_id(0)

    @pl.when(i == 0)
    def _():
        wbf[...] = w_ref[...].astype(BF16)

    x = jnp.where(i < NPC, xa_ref[...], xb_ref[...])
    mod = mod_ref[...]
    h = _normmod(x, g_ref[...], mod[1:2], mod[0:1])
    y = yf_ref[...].astype(F32) + yb_ref[...].astype(F32) + d_ref[...] * h
    ge = 0.5 * y * (1.0 + jnp.tanh(math.sqrt(2.0 / math.pi) * (y + 0.044715 * (y * y * y))))
    z = jnp.dot(ge.astype(BF16), wbf[...], preferred_element_type=F32) + b_ref[...]
    out = z[:, 0:D] * _sigmoid(z[:, D:2 * D])
    o_ref[...] = x + mod[2:3] * out


def _glu(xa, xb, off_b, yf, yb, mods, layer, g, dvec, w, b):
    return pl.pallas_call(
        _glu_kernel,
        grid=(NCOL,),
        in_specs=[pl.BlockSpec((TILE, D), lambda i: (jnp.minimum(i, NPC - 1), 0)),
                  pl.BlockSpec((TILE, D), lambda i: (off_b + jnp.maximum(i - NPC, 0), 0)),
                  pl.BlockSpec((None, TILE, D), lambda i: (i, 0, 0)),
                  pl.BlockSpec((None, TILE, D), lambda i: (i, 0, 0)),
                  pl.BlockSpec((None, None, 6, D), lambda i: (layer, _cond_of_col(i), 0, 0)),
                  pl.BlockSpec((1, D), lambda i: (0, 0)),
                  pl.BlockSpec((1, D), lambda i: (0, 0)),
                  pl.BlockSpec((None, D, 2 * D), lambda i: (layer // 2, 0, 0)),
                  pl.BlockSpec((1, 2 * D), lambda i: (0, 0))],
        out_specs=pl.BlockSpec((TILE, D), lambda i: (i, 0)),
        out_shape=jax.ShapeDtypeStruct((NCOL * TILE, D), F32),
        scratch_shapes=[pltpu.VMEM((D, 2 * D), BF16)],
        compiler_params=_cparams(1),
        name="s5_glu",
    )(xa, xb, yf, yb, mods, g, dvec, w, b)


def _fourier_kernel(x_ref, mod_ref, g_ref, cc_ref, cs_ref, cl_ref, sl_ref, w_ref, b_ref,
                    o_ref, xc_s, xs_s, *, scale):
    ph = pl.program_id(1)
    s = pl.program_id(2)
    mod = mod_ref[...]

    @pl.when(ph == 0)
    def _channel_dft():
        h = _normmod(x_ref[...], g_ref[...], mod[1:2], mod[0:1]).astype(BF16)
        r0 = pl.multiple_of(s * TILE, TILE)
        for q in range(FG):
            hq = h[:, q * FGC:(q + 1) * FGC]
            xc_s[pl.ds(r0, TILE), q * FGC:(q + 1) * FGC] = jnp.dot(
                hq, cc_ref[...], preferred_element_type=F32).astype(BF16)
            xs_s[pl.ds(r0, TILE), q * FGC:(q + 1) * FGC] = jnp.dot(
                hq, cs_ref[...], preferred_element_type=F32).astype(BF16)

    @pl.when(ph == 1)
    def _position_dft():
        f = (jnp.dot(cl_ref[...], xc_s[...], preferred_element_type=F32)
             - jnp.dot(sl_ref[...], xs_s[...], preferred_element_type=F32)) * scale
        o = jnp.dot(f.astype(BF16), w_ref[...], preferred_element_type=F32) + b_ref[...]
        o_ref[...] = x_ref[...] + mod[2:3] * o


def _fourier(x, mods, layer, g, cc, cs, cl, sl, w, b, *, nseq, nseg, col_off, cond_off, cond_stride):
    ln = nseg * TILE
    col = lambda q, ph, s: (col_off + q * nseg + s, 0)
    return pl.pallas_call(
        functools.partial(_fourier_kernel, scale=1.0 / math.sqrt(ln * FGC)),
        grid=(nseq, 2, nseg),
        in_specs=[pl.BlockSpec((TILE, D), col),
                  pl.BlockSpec((None, None, 6, D), lambda q, ph, s: (layer, cond_off + q * cond_stride, 0, 0)),
                  pl.BlockSpec((1, D), lambda q, ph, s: (0, 0)),
                  pl.BlockSpec((FGC, FGC), lambda q, ph, s: (0, 0)),
                  pl.BlockSpec((FGC, FGC), lambda q, ph, s: (0, 0)),
                  pl.BlockSpec((TILE, ln), lambda q, ph, s: (s * ph, 0)),
                  pl.BlockSpec((TILE, ln), lambda q, ph, s: (s * ph, 0)),
                  pl.BlockSpec((None, D, D), lambda q, ph, s: (layer // 2, 0, 0)),
                  pl.BlockSpec((1, D), lambda q, ph, s: (0, 0))],
        out_specs=pl.BlockSpec((TILE, D), lambda q, ph, s: (col_off + q * nseg + s * ph, 0)),
        out_shape=jax.ShapeDtypeStruct((NCOL * TILE, D), F32),
        scratch_shapes=[pltpu.VMEM((ln, D), BF16), pltpu.VMEM((ln, D), BF16)],
        input_output_aliases={0: 0},
        compiler_params=_cparams(3),
        name="fourier%d" % nseg,
    )(x, mods, g, cc, cs, cl, sl, w, b)


@functools.lru_cache(maxsize=None)
def _dft_mats(n):
    k = np.arange(n, dtype=np.int64)
    ang = ((k[:, None] * k[None, :]) % n).astype(np.float64) * (2.0 * math.pi / n)
    return np.cos(ang).astype(np.float32), np.sin(ang).astype(np.float32)


def _ffn_kernel(x_ref, mod_ref, g_ref, wup_ref, cw_ref, cb_ref, wdn_ref, gf_ref, *o_refs, final):
    i = pl.program_id(0)
    x = x_ref[...]
    mod = mod_ref[...]
    h = _normmod(x, g_ref[...], mod[4:5], mod[3:4]).astype(BF16)
    sub = lax.broadcasted_iota(jnp.int32, (8, FC), 0)
    one = jnp.ones((8, FC), F32)
    first0 = (sub != 0).astype(F32)
    last0 = (sub != 7).astype(F32)
    inner_first = jnp.where(i < NPC, one, first0)
    inner_last = jnp.where(i < NPC, one, last0)
    prev_slabs = [(0, first0)] + [(r, inner_first) for r in range(GRID_W, TILE, GRID_W)]
    next_slabs = [(r - 8, inner_last) for r in range(GRID_W, TILE, GRID_W)] + [(TILE - 8, last0)]

    def mask_rows(a, slabs):
        parts, last = [], 0
        for r0, m in slabs:
            if r0 > last:
                parts.append(a[last:r0])
            parts.append(a[r0:r0 + 8] * m)
            last = r0 + 8
        if last < TILE:
            parts.append(a[last:])
        return jnp.concatenate(parts, axis=0)

    def conv(up, off):
        w = cw_ref[:, off:off + FC]
        prev = mask_rows(pltpu.roll(up, 1, 0), prev_slabs)
        nxt = mask_rows(pltpu.roll(up, TILE - 1, 0), next_slabs)
        return w[0:1] * prev + w[1:2] * up + w[2:3] * nxt + cb_ref[:, off:off + FC]

    def up_chunk(c):
        og = c * FC
        ov = DFF + c * FC
        return (jnp.dot(h, wup_ref[:, og:og + FC], preferred_element_type=F32),
                jnp.dot(h, wup_ref[:, ov:ov + FC], preferred_element_type=F32))

    def down_chunk(act, c):
        return jnp.dot(act, wdn_ref[c * FC:(c + 1) * FC, :], preferred_element_type=F32)

    nchunk = DFF // FC
    acc = jnp.zeros((TILE, D), F32)
    nxt_up = up_chunk(0)
    act = None
    for c in range(nchunk):
        cur_up = nxt_up
        if c + 1 < nchunk:
            nxt_up = up_chunk(c + 1)
        if act is not None:
            acc = acc + down_chunk(act, c - 1)
        gate = conv(cur_up[0], c * FC)
        val = conv(cur_up[1], DFF + c * FC)
        hg = 0.5 * gate
        act = ((hg + hg * jnp.tanh(hg)) * val).astype(BF16)
    acc = acc + down_chunk(act, nchunk - 1)
    y = x + mod[5:6] * acc
    if not final:
        o_refs[0][...] = y
    else:
        ms = jnp.mean(y * y, axis=-1, keepdims=True)
        y = y * lax.rsqrt(ms + EPS) * gf_ref[...]

        @pl.when(i < NPC)
        def _():
            o_refs[0][...] = y

        @pl.when(i >= NPC)
        def _():
            o_refs[1][...] = y


def _ffn(x, mods, layer, g, wup, cw, cb, wdn, gf, *, final):
    if final:
        out_specs = [pl.BlockSpec((TILE, D), lambda i: (jnp.minimum(i, NPC - 1), 0)),
                     pl.BlockSpec((TILE, D), lambda i: (jnp.maximum(i - NPC, 0), 0))]
        out_shape = [jax.ShapeDtypeStruct((NPC * TILE, D), F32)] * 2
    else:
        out_specs = pl.BlockSpec((TILE, D), lambda i: (i, 0))
        out_shape = jax.ShapeDtypeStruct((NCOL * TILE, D), F32)
    return pl.pallas_call(
        functools.partial(_ffn_kernel, final=final),
        grid=(NCOL,),
        in_specs=[pl.BlockSpec((TILE, D), lambda i: (i, 0)),
                  pl.BlockSpec((None, None, 6, D), lambda i: (layer, _cond_of_col(i), 0, 0)),
                  pl.BlockSpec((1, D), lambda i: (0, 0)),
                  pl.BlockSpec((None, D, 2 * DFF), lambda i: (layer, 0, 0)),
                  pl.BlockSpec((None, 3, 2 * DFF), lambda i: (layer, 0, 0)),
                  pl.BlockSpec((None, 1, 2 * DFF), lambda i: (layer, 0, 0)),
                  pl.BlockSpec((None, DFF, D), lambda i: (layer, 0, 0)),
                  pl.BlockSpec((1, D), lambda i: (0, 0))],
        out_specs=out_specs,
        out_shape=out_shape,
        compiler_params=_cparams(1),
        name="ffn",
    )(x, mods, g, wup, cw, cb, wdn, gf)


def kernel(x_prompt, x_sample, state_ssm_re, state_ssm_im, c, c_ctx, w_ada, b_ada, g_mix, g_ffn,
           ssm_lam_re, ssm_lam_im, ssm_log_dt, ssm_b_re, ssm_b_im, ssm_c_re, ssm_c_im, ssm_d,
           w_glu, b_glu, w_fourier, b_fourier, w_up, conv_w, conv_b, w_down, g_final):
    nb = x_prompt.shape[0]
    xp = x_prompt.reshape(NPC * TILE, D)
    xs = x_sample.reshape(NPC * TILE, D)
    x = None

    cond8 = jnp.concatenate([c_ctx[None, :], c, jnp.zeros((5, D), F32)], axis=0)
    mods = _ada(cond8, w_ada, b_ada).reshape(DEPTH, 8, 6, D)
    col_cond = jnp.asarray([0] * NPC + [1] * SEG + [2] * SEG, jnp.int32)

    wup_bf = _cast_bf16(w_up.reshape(DEPTH * D, 2 * DFF), 256).reshape(DEPTH, D, 2 * DFF)
    wdn_bf = _cast_bf16(w_down.reshape(DEPTH * DFF, D), DFF).reshape(DEPTH, DFF, D)
    wf_bf = _cast_bf16(w_fourier.reshape(2 * D, D), D).reshape(2, D, D)

    cc, cs = (jnp.asarray(m).astype(BF16) for m in _dft_mats(FGC))
    cl1, sl1 = cc, cs
    cl8, sl8 = (jnp.asarray(m).astype(BF16) for m in _dft_mats(SEG * TILE))

    st_re, st_im = [], []
    for i in range(DEPTH):
        j = i // 2
        if i % 2 == 0:
            sh1 = mods[i, :, 0, :][col_cond]
            sc1 = mods[i, :, 1, :][col_cond]
            g = g_mix[i][None, :]
            lr = ssm_lam_re[j].reshape(2, 1, NST)
            li = ssm_lam_im[j].reshape(2, 1, NST)
            dt = jnp.repeat(ssm_log_dt[j], PS, axis=-1).reshape(2, 1, NST)
            btr = ssm_b_re[j].transpose(0, 3, 1, 2).reshape(2, GC, NST)
            bti = ssm_b_im[j].transpose(0, 3, 1, 2).reshape(2, GC, NST)
            cwr = jnp.tile(ssm_c_re[j].transpose(0, 1, 3, 2).reshape(2, NST, GC), (1, 1, 8))
            cwi = jnp.tile(ssm_c_im[j].transpose(0, 1, 3, 2).reshape(2, NST, GC), (1, 1, 8))
            sre = state_ssm_re[:, j].reshape(2, 2, NST)
            sim = state_ssm_im[:, j].reshape(2, 2, NST)
            h0r = jnp.zeros((2, 2, SEG, NST), F32)
            h0i = jnp.zeros((2, 2, SEG, NST), F32)
            h0r = h0r.at[0, :, 0].set(sre[:, 0]).at[1, :, SEG - 1].set(sre[:, 1]).reshape(2, NPC, NST)
            h0i = h0i.at[0, :, 0].set(sim[:, 0]).at[1, :, SEG - 1].set(sim[:, 1]).reshape(2, NPC, NST)
            xa, xb, off_b = (xp, xs, 0) if x is None else (x, x, NPC)
            xa3 = xa.reshape(-1, TILE, D)
            xb3 = xb.reshape(-1, TILE, D)
            args = (xa3, xb3, 0, off_b, sc1, sh1, g, lr, li, dt, btr, bti, cwr, cwi)
            fin = [_s5_scan(*args, h0r, h0i, d=d, ntr=1, col_off=1, emit_y=False)[2:] for d in range(2)]
            fr = jnp.stack([fin[0][0], fin[1][0]], axis=0)
            fi = jnp.stack([fin[0][1], fin[1][1]], axis=0)
            ir, ii = _carry(fr, fi, h0r, h0i, lr, li, dt)
            z = jnp.zeros((2, NPC, NST), F32)
            h0r2 = jnp.concatenate([z, ir], axis=1)
            h0i2 = jnp.concatenate([z, ii], axis=1)
            outs = [_s5_scan(*args, h0r2, h0i2, d=d, ntr=2, col_off=0, emit_y=True) for d in range(2)]
            st_re.append(jnp.stack([outs[0][1][:nb], outs[1][1][:nb]], axis=0))
            st_im.append(jnp.stack([outs[0][2][:nb], outs[1][2][:nb]], axis=0))
            yf, yb = outs[0][0], outs[1][0]
            x = _glu(xa, xb, off_b, yf, yb, mods, i, g, ssm_d[j][None, :], w_glu, b_glu[j][None, :])
        else:
            g = g_mix[i][None, :]
            bf = b_fourier[j][None, :]
            x = _fourier(x, mods, i, g, cc, cs, cl1, sl1, wf_bf, bf,
                         nseq=NPC, nseg=1, col_off=0, cond_off=0, cond_stride=0)
            x = _fourier(x, mods, i, g, cc, cs, cl8, sl8, wf_bf, bf,
                         nseq=2, nseg=SEG, col_off=NPC, cond_off=1, cond_stride=1)
        x = _ffn(x, mods, i, g_ffn[i][None, :], wup_bf, conv_w, conv_b.reshape(DEPTH, 1, 2 * DFF),
                 wdn_bf, g_final[None, :], final=(i == DEPTH - 1))

    y_prompt = x[0].reshape(NPC, TILE, D)
    y_sample = x[1].reshape(2, SEG * TILE, D)
    new_re = jnp.stack(st_re, axis=0).transpose(2, 0, 1, 3).reshape(nb, 2, 2, GS, PS)
    new_im = jnp.stack(st_im, axis=0).transpose(2, 0, 1, 3).reshape(nb, 2, 2, GS, PS)
    return (y_prompt, y_sample, new_re, new_im)
```

```python
import functools
import math

import jax
import jax.numpy as jnp
import numpy as np
from jax import lax
from jax.experimental import pallas as pl
from jax.experimental.pallas import tpu as pltpu

F32 = jnp.float32
BF16 = jnp.bfloat16

D = 1024
TILE = 256
NCOL = 32
NPC = 16
SEG = 8
DEPTH = 4
GS = 64
GC = 16
PS = 64
NST = GS * PS
FG = 4
FGC = 256
DFF = 2816
EPS = 1e-6
GRID_W = 64

TT = 32
NCH = TILE // TT
SW = 512
NSLAB = NST // SW
FC = 256

VMEM_LIMIT = 56 * 1024 * 1024


def _cparams(n_axes):
    return pltpu.CompilerParams(dimension_semantics=("arbitrary",) * n_axes,
                                vmem_limit_bytes=VMEM_LIMIT)


def _normmod(x, g, sc, sh):
    ms = jnp.mean(x * x, axis=-1, keepdims=True)
    return x * lax.rsqrt(ms + EPS) * g * (1.0 + sc) + sh


def _sigmoid(x):
    return 1.0 / (1.0 + jnp.exp(-x))


def _cond_of_col(i):
    return jnp.where(i < NPC, 0, 1 + (i - NPC) // SEG)


def _ada_kernel(c_ref, w_ref, b_ref, o_ref):
    c = c_ref[...]
    s = (c * _sigmoid(c)).astype(BF16)
    o_ref[...] = jnp.dot(s, w_ref[...].astype(BF16), preferred_element_type=F32) + b_ref[...]


def _ada(cond8, w_ada, b_ada):
    tn = 1536
    return pl.pallas_call(
        _ada_kernel,
        grid=(DEPTH, 6 * D // tn),
        in_specs=[pl.BlockSpec((8, D), lambda l, n: (0, 0)),
                  pl.BlockSpec((None, D, tn), lambda l, n: (l, 0, n)),
                  pl.BlockSpec((None, 1, tn), lambda l, n: (l, 0, n))],
        out_specs=pl.BlockSpec((None, 8, tn), lambda l, n: (l, 0, n)),
        out_shape=jax.ShapeDtypeStruct((DEPTH, 8, 6 * D), F32),
        compiler_params=_cparams(2),
        name="ada",
    )(cond8, w_ada, b_ada.reshape(DEPTH, 1, 6 * D))


def _cast_kernel(w_ref, o_ref):
    o_ref[...] = w_ref[...].astype(BF16)


def _cast_bf16(w2d, block_rows):
    r, c = w2d.shape
    return pl.pallas_call(
        _cast_kernel,
        grid=(r // block_rows,),
        in_specs=[pl.BlockSpec((block_rows, c), lambda i: (i, 0))],
        out_specs=pl.BlockSpec((block_rows, c), lambda i: (i, 0)),
        out_shape=jax.ShapeDtypeStruct((r, c), BF16),
        compiler_params=_cparams(1),
        name="cast",
    )(w2d)


def _abar(lr, li, logdt):
    dt = jnp.exp(logdt)
    mag = jnp.exp(lr * dt)
    return mag * jnp.cos(li * dt), mag * jnp.sin(li * dt)


def _s5_kernel(*refs, rev, col_base, init, emit_y, emit_edge, emit_fin):
    x_hbm, sc_ref, sh_ref, g_ref, lr_ref, li_ref, dt_ref, btr_ref, bti_ref, cwr_ref, cwi_ref = refs[:11]
    pos = 11
    if init != "zero":
        h0r_ref, h0i_ref = refs[pos:pos + 2]
        pos += 2
    if emit_y:
        y_hbm = refs[pos]
        pos += 1
    if emit_edge:
        er_ref, ei_ref = refs[pos:pos + 2]
        pos += 2
    if emit_fin:
        fr_ref, fi_ref = refs[pos:pos + 2]
        pos += 2
    scratch = refs[pos:]
    bu_re = scratch[0:NSLAB]
    bu_im = scratch[NSLAB:2 * NSLAB]
    bt_s, ct_s, ar_s, ai_s, hs_re, hs_im, xbuf, sem_in = scratch[2 * NSLAB:2 * NSLAB + 8]
    if emit_y:
        ybuf, sem_out = scratch[2 * NSLAB + 8:]

    i = pl.program_id(0)
    slot = lax.rem(i, 2)

    def t_start(i_):
        return pl.multiple_of((NCH - 1 - i_ if rev else i_) * TT, TT)

    def x_copies(i_, slot_):
        return [pltpu.make_async_copy(x_hbm.at[col_base + c, pl.ds(t_start(i_), TT), :],
                                      xbuf.at[slot_, :, c, :], sem_in.at[slot_]) for c in range(NPC)]

    def y_copies(i_, slot_):
        return [pltpu.make_async_copy(ybuf.at[slot_, :, c, :],
                                      y_hbm.at[c, pl.ds(t_start(i_), TT), :],
                                      sem_out.at[slot_]) for c in range(NPC)]

    @pl.when(i == 0)
    def _first_fetch():
        for cp in x_copies(i, slot):
            cp.start()

    if emit_y:
        @pl.when(i >= 2)
        def _ybuf_free():
            for cp in y_copies(i, slot):
                cp.wait()

    @pl.when(i == 0)
    def _prep():
        lr = lr_ref[...]
        li = li_ref[...]
        ar, ai = _abar(lr, li, dt_ref[...])
        ar_s[...] = jnp.broadcast_to(ar, (8, NST))
        ai_s[...] = jnp.broadcast_to(ai, (8, NST))
        xr = ar - 1.0
        den = lr * lr + li * li
        fr = (xr * lr + ai * li) / den
        fi = (ai * lr - xr * li) / den
        btr = btr_ref[...]
        bti = bti_ref[...]
        bbr = fr * btr - fi * bti
        bbi = fr * bti + fi * btr
        r = lax.broadcasted_iota(jnp.int32, (128, SW), 0)
        c = lax.broadcasted_iota(jnp.int32, (128, SW), 1)
        bmask = ((r >> 4) == (c >> 6)).astype(F32)
        for m in range(NSLAB):
            sl = slice(m * SW, (m + 1) * SW)
            tr = jnp.concatenate([bbr[:, sl]] * 8, axis=0) * bmask
            ti = jnp.concatenate([bbi[:, sl]] * 8, axis=0) * bmask
            bt_s[m, :, 0:SW] = tr.astype(BF16)
            bt_s[m, :, SW:2 * SW] = ti.astype(BF16)
        r2 = lax.broadcasted_iota(jnp.int32, (SW, 128), 0)
        c2 = lax.broadcasted_iota(jnp.int32, (SW, 128), 1)
        cmask = ((r2 >> 6) == (c2 >> 4)).astype(F32)
        for q in range(NSLAB):
            sl = slice(q * SW, (q + 1) * SW)
            ct_s[q, 0:SW, :] = (cwr_ref[sl, :] * cmask).astype(BF16)
            ct_s[q, SW:2 * SW, :] = (-(cwi_ref[sl, :] * cmask)).astype(BF16)
        if init == "carry":
            hs_re[...] = h0r_ref[...]
            hs_im[...] = h0i_ref[...]
        else:
            hs_re[...] = jnp.zeros((NPC, NST), F32)
            hs_im[...] = jnp.zeros((NPC, NST), F32)
            if init == "state":
                seg0 = SEG - 1 if rev else 0
                for b in range(2):
                    row = b * SEG + seg0
                    hs_re[row:row + 1, :] = h0r_ref[b]
                    hs_im[row:row + 1, :] = h0i_ref[b]

    for cp in x_copies(i, slot):
        cp.wait()
    for cp in x_copies(jnp.minimum(i + 1, NCH - 1), 1 - slot):
        cp.start()

    x3 = xbuf[slot]
    ms = jnp.mean(x3 * x3, axis=-1, keepdims=True)
    u3 = x3 * lax.rsqrt(ms + EPS) * g_ref[...][None] * (1.0 + sc_ref[...][None]) + sh_ref[...][None]
    u = u3.reshape(TT * NPC, D).astype(BF16)

    def expand(m):
        res = jnp.dot(u[:, m * 128:(m + 1) * 128], bt_s[m], preferred_element_type=F32)
        bu_re[m][...] = res[:, 0:SW]
        bu_im[m][...] = res[:, SW:2 * SW]

    def recur(s):
        lanes = slice(s * SW, (s + 1) * SW)
        ar = ar_s[:, lanes]
        ai = ai_s[:, lanes]
        h = [hs_re[0:8, lanes], hs_im[0:8, lanes], hs_re[8:16, lanes], hs_im[8:16, lanes]]
        order = range(TT - 1, -1, -1) if rev else range(TT)
        for t in order:
            for q in range(2):
                rows = slice(t * NPC + 8 * q, t * NPC + 8 * q + 8)
                hr, hi = h[2 * q], h[2 * q + 1]
                nr = ar * hr - ai * hi + bu_re[s][rows, :]
                ni = ar * hi + ai * hr + bu_im[s][rows, :]
                bu_re[s][rows, :] = nr
                bu_im[s][rows, :] = ni
                h[2 * q], h[2 * q + 1] = nr, ni
        hs_re[0:8, lanes] = h[0]
        hs_im[0:8, lanes] = h[1]
        hs_re[8:16, lanes] = h[2]
        hs_im[8:16, lanes] = h[3]

    def contract(q):
        yq = (jnp.dot(bu_re[q][...].astype(BF16), ct_s[q, 0:SW, :], preferred_element_type=F32)
              + jnp.dot(bu_im[q][...].astype(BF16), ct_s[q, SW:2 * SW, :], preferred_element_type=F32))
        ybuf[slot, :, :, q * 128:(q + 1) * 128] = yq.reshape(TT, NPC, 128)

    expand(0)
    for s in range(NSLAB):
        if s + 1 < NSLAB:
            expand(s + 1)
        if emit_y and s >= 1:
            contract(s - 1)
        recur(s)
    if emit_y:
        contract(NSLAB - 1)
        for cp in y_copies(i, slot):
            cp.start()

    if emit_edge:
        @pl.when(i == 0)
        def _edge():
            e0 = (TT - 1) * NPC if rev else 0
            for s in range(NSLAB):
                er_ref[:, s * SW:(s + 1) * SW] = bu_re[s][e0:e0 + NPC, :]
                ei_ref[:, s * SW:(s + 1) * SW] = bu_im[s][e0:e0 + NPC, :]

    @pl.when(i == NCH - 1)
    def _last():
        if emit_fin:
            fr_ref[...] = hs_re[...]
            fi_ref[...] = hs_im[...]
        for cp in x_copies(i, 1 - slot):
            cp.wait()
        if emit_y:
            for cp in y_copies(i, slot) + y_copies(i, 1 - slot):
                cp.wait()


def _s5_scan(x3, col_base, mods_t, layer, g, lam_re, lam_im, dts, bt_re, bt_im, cw_re, cw_im, *,
             jd, rev, trunk, init, h0=None, emit_y, emit_edge, emit_fin):
    par = lambda i: (jd, 0, 0)
    in_specs = [
        pl.BlockSpec(memory_space=pl.ANY),
        pl.BlockSpec((None, None, NPC, D), lambda i: (layer, 1, trunk, 0)),
        pl.BlockSpec((None, None, NPC, D), lambda i: (layer, 0, trunk, 0)),
        pl.BlockSpec((1, D), lambda i: (0, 0)),
        pl.BlockSpec((None, 1, NST), par),
        pl.BlockSpec((None, 1, NST), par),
        pl.BlockSpec((None, 1, NST), par),
        pl.BlockSpec((None, GC, NST), par),
        pl.BlockSpec((None, GC, NST), par),
        pl.BlockSpec((None, NST, 128), par),
        pl.BlockSpec((None, NST, 128), par),
    ]
    args = [x3, mods_t, mods_t, g, lam_re, lam_im, dts, bt_re, bt_im, cw_re, cw_im]
    if init == "state":
        in_specs += [pl.BlockSpec((2, None, 1, NST), lambda i: (0, jd, 0, 0))] * 2
        args += list(h0)
    elif init == "carry":
        in_specs += [pl.BlockSpec((None, NPC, NST), lambda i: (1 if rev else 0, 0, 0))] * 2
        args += list(h0)
    out_specs, out_shape = [], []
    slab = pltpu.VMEM((TT * NPC, SW), F32)
    scratch = [slab] * (2 * NSLAB) + [
        pltpu.VMEM((NSLAB, 128, 2 * SW), BF16), pltpu.VMEM((NSLAB, 2 * SW, 128), BF16),
        pltpu.VMEM((8, NST), F32), pltpu.VMEM((8, NST), F32),
        pltpu.VMEM((NPC, NST), F32), pltpu.VMEM((NPC, NST), F32),
        pltpu.VMEM((2, TT, NPC, D), F32), pltpu.SemaphoreType.DMA((2,))]
    if emit_y:
        out_specs.append(pl.BlockSpec(memory_space=pl.ANY))
        out_shape.append(jax.ShapeDtypeStruct((NPC, TILE, D), F32))
        scratch += [pltpu.VMEM((2, TT, NPC, D), F32), pltpu.SemaphoreType.DMA((2,))]
    n_state_outs = 2 * (int(emit_edge) + int(emit_fin))
    out_specs += [pl.BlockSpec((NPC, NST), lambda i: (0, 0))] * n_state_outs
    out_shape += [jax.ShapeDtypeStruct((NPC, NST), F32)] * n_state_outs
    return pl.pallas_call(
        functools.partial(_s5_kernel, rev=rev, col_base=col_base, init=init,
                          emit_y=emit_y, emit_edge=emit_edge, emit_fin=emit_fin),
        grid=(NCH,),
        in_specs=in_specs,
        out_specs=out_specs,
        out_shape=out_shape,
        scratch_shapes=scratch,
        compiler_params=_cparams(1),
        name=("s5_scan" if emit_y else "s5_states") + ("_bwd" if rev else "_fwd") + str(trunk),
    )(*args)


def _carry_kernel(frf_ref, fif_ref, frb_ref, fib_ref, sr_ref, si_ref, lr_ref, li_ref, dt_ref,
                  or_ref, oi_ref, *, j):
    for d, (fr_ref, fi_ref) in enumerate(((frf_ref, fif_ref), (frb_ref, fib_ref))):
        jd = 2 * j + d
        ar, ai = _abar(lr_ref[jd], li_ref[jd], dt_ref[jd])
        for _ in range(8):
            ar, ai = ar * ar - ai * ai, 2.0 * ar * ai
        for b in range(2):
            order = list(range(SEG)) if d == 0 else list(range(SEG - 1, -1, -1))
            r = b * SEG + order[0]
            or_ref[d, r:r + 1, :] = sr_ref[b, jd]
            oi_ref[d, r:r + 1, :] = si_ref[b, jd]
            tr = fr_ref[r:r + 1, :]
            ti = fi_ref[r:r + 1, :]
            for s in order[1:]:
                r = b * SEG + s
                or_ref[d, r:r + 1, :] = tr
                oi_ref[d, r:r + 1, :] = ti
                tr, ti = (fr_ref[r:r + 1, :] + ar * tr - ai * ti,
                          fi_ref[r:r + 1, :] + ar * ti + ai * tr)


def _carry(frf, fif, frb, fib, st_re4, st_im4, lam_re, lam_im, dts, *, j):
    shp = jax.ShapeDtypeStruct((2, NPC, NST), F32)
    return pl.pallas_call(functools.partial(_carry_kernel, j=j), out_shape=[shp, shp], name="s5_carry")(
        frf, fif, frb, fib, st_re4, st_im4, lam_re, lam_im, dts)


GT = 2 * TILE


def _glu_kernel(xa_ref, xb_ref, yfa_ref, yba_ref, yfb_ref, ybb_ref, mod_ref, g_ref, d_ref, w_ref, b_ref,
                o_ref, wbf):
    i = pl.program_id(0)

    @pl.when(i == 0)
    def _():
        wbf[...] = w_ref[...].astype(BF16)

    mod = mod_ref[...]
    first_prompt = i < NPC // 2

    def pre(q):
        rows = slice(q * TILE, (q + 1) * TILE)
        x = jnp.where(first_prompt, xa_ref[rows, :], xb_ref[rows, :])
        h = _normmod(x, g_ref[...], mod[1:2], mod[0:1])
        y_ssm = jnp.where(first_prompt, yfa_ref[rows, :] + yba_ref[rows, :], yfb_ref[rows, :] + ybb_ref[rows, :])
        y = y_ssm + d_ref[...] * h
        ge = 0.5 * y * (1.0 + jnp.tanh(math.sqrt(2.0 / math.pi) * (y + 0.044715 * (y * y * y))))
        return x, ge.astype(BF16)

    def gate(ge):
        return jnp.dot(ge, wbf[...], preferred_element_type=F32) + b_ref[...]

    def post(q, x, z):
        out = z[:, 0:D] * _sigmoid(z[:, D:2 * D])
        o_ref[q * TILE:(q + 1) * TILE, :] = x + mod[2:3] * out

    x0, ge0 = pre(0)
    z0 = gate(ge0)
    x1, ge1 = pre(1)
    z1 = gate(ge1)
    post(0, x0, z0)
    post(1, x1, z1)


def _glu(xa, xb, off_b, ys, mods, layer, g, dvec, w, b):
    nblk = NCOL // 2
    half = NPC // 2
    pa = lambda i: (jnp.minimum(i, half - 1), 0)
    pb = lambda i: (jnp.maximum(i - half, 0), 0)
    return pl.pallas_call(
        _glu_kernel,
        grid=(nblk,),
        in_specs=[pl.BlockSpec((GT, D), pa),
                  pl.BlockSpec((GT, D), lambda i: (off_b // 2 + jnp.maximum(i - half, 0), 0)),
                  pl.BlockSpec((GT, D), pa),
                  pl.BlockSpec((GT, D), pa),
                  pl.BlockSpec((GT, D), pb),
                  pl.BlockSpec((GT, D), pb),
                  pl.BlockSpec((None, None, 6, D), lambda i: (layer, _cond_of_col(2 * i), 0, 0)),
                  pl.BlockSpec((1, D), lambda i: (0, 0)),
                  pl.BlockSpec((1, D), lambda i: (0, 0)),
                  pl.BlockSpec((None, D, 2 * D), lambda i: (layer // 2, 0, 0)),
                  pl.BlockSpec((1, 2 * D), lambda i: (0, 0))],
        out_specs=pl.BlockSpec((GT, D), lambda i: (i, 0)),
        out_shape=jax.ShapeDtypeStruct((NCOL * TILE, D), F32),
        scratch_shapes=[pltpu.VMEM((D, 2 * D), BF16)],
        compiler_params=_cparams(1),
        name="s5_glu",
    )(xa, xb, *ys, mods, g, dvec, w, b)


def _fourier_kernel(x_ref, mod_ref, g_ref, cc_ref, cs_ref, cl_ref, sl_ref, w_ref, b_ref,
                    o_ref, *scratch, scale, nseg):
    mod = mod_ref[...]

    def channel_dft():
        h = _normmod(x_ref[...], g_ref[...], mod[1:2], mod[0:1]).astype(BF16)
        xc, xs = [], []
        for q in range(FG):
            hq = h[:, q * FGC:(q + 1) * FGC]
            xc.append(jnp.dot(hq, cc_ref[...], preferred_element_type=F32).astype(BF16))
            xs.append(jnp.dot(hq, cs_ref[...], preferred_element_type=F32).astype(BF16))
        return jnp.concatenate(xc, axis=1), jnp.concatenate(xs, axis=1)

    def position_dft(xc, xs):
        f = (jnp.dot(cl_ref[...], xc, preferred_element_type=F32)
             - jnp.dot(sl_ref[...], xs, preferred_element_type=F32)) * scale
        o = jnp.dot(f.astype(BF16), w_ref[...], preferred_element_type=F32) + b_ref[...]
        o_ref[...] = x_ref[...] + mod[2:3] * o

    if nseg == 1:
        position_dft(*channel_dft())
        return

    xc_s, xs_s = scratch
    ph = pl.program_id(1)
    s = pl.program_id(2)

    @pl.when(ph == 0)
    def _():
        r0 = pl.multiple_of(s * TILE, TILE)
        xc, xs = channel_dft()
        xc_s[pl.ds(r0, TILE), :] = xc
        xs_s[pl.ds(r0, TILE), :] = xs

    @pl.when(ph == 1)
    def _():
        position_dft(xc_s[...], xs_s[...])


def _fourier(x, mods, layer, g, cc, cs, cl, sl, w, b, *, nseq, nseg, col_off, cond_off, cond_stride):
    ln = nseg * TILE
    if nseg == 1:
        grid = (nseq,)
        idx = lambda f: (lambda q: f(q, 1, 0))
        scratch = []
    else:
        grid = (nseq, 2, nseg)
        idx = lambda f: f
        scratch = [pltpu.VMEM((ln, D), BF16), pltpu.VMEM((ln, D), BF16)]
    return pl.pallas_call(
        functools.partial(_fourier_kernel, scale=1.0 / math.sqrt(ln * FGC), nseg=nseg),
        grid=grid,
        in_specs=[pl.BlockSpec((TILE, D), idx(lambda q, ph, s: (col_off + q * nseg + s, 0))),
                  pl.BlockSpec((None, None, 6, D), idx(lambda q, ph, s: (layer, cond_off + q * cond_stride, 0, 0))),
                  pl.BlockSpec((1, D), idx(lambda q, ph, s: (0, 0))),
                  pl.BlockSpec((FGC, FGC), idx(lambda q, ph, s: (0, 0))),
                  pl.BlockSpec((FGC, FGC), idx(lambda q, ph, s: (0, 0))),
                  pl.BlockSpec((TILE, ln), idx(lambda q, ph, s: (s * ph, 0))),
                  pl.BlockSpec((TILE, ln), idx(lambda q, ph, s: (s * ph, 0))),
                  pl.BlockSpec((None, D, D), idx(lambda q, ph, s: (layer // 2, 0, 0))),
                  pl.BlockSpec((1, D), idx(lambda q, ph, s: (0, 0)))],
        out_specs=pl.BlockSpec((TILE, D), idx(lambda q, ph, s: (col_off + q * nseg + s * ph, 0))),
        out_shape=jax.ShapeDtypeStruct((NCOL * TILE, D), F32),
        scratch_shapes=scratch,
        input_output_aliases={0: 0},
        compiler_params=_cparams(len(grid)),
        name="fourier%d" % nseg,
    )(x, mods, g, cc, cs, cl, sl, w, b)


@functools.lru_cache(maxsize=None)
def _dft_mats(n):
    k = np.arange(n, dtype=np.int64)
    ang = ((k[:, None] * k[None, :]) % n).astype(np.float64) * (2.0 * math.pi / n)
    return np.cos(ang).astype(np.float32), np.sin(ang).astype(np.float32)


def _ffn_kernel(x_ref, mod_ref, g_ref, wup_ref, cw_ref, cb_ref, wdn_ref, gf_ref, *o_refs, final):
    i = pl.program_id(0)
    x = x_ref[...]
    mod = mod_ref[...]
    h = _normmod(x, g_ref[...], mod[4:5], mod[3:4]).astype(BF16)
    sub = lax.broadcasted_iota(jnp.int32, (8, FC), 0)
    one = jnp.ones((8, FC), F32)
    first0 = (sub != 0).astype(F32)
    last0 = (sub != 7).astype(F32)
    inner_first = jnp.where(i < NPC, one, first0)
    inner_last = jnp.where(i < NPC, one, last0)
    prev_slabs = [(0, first0)] + [(r, inner_first) for r in range(GRID_W, TILE, GRID_W)]
    next_slabs = [(r - 8, inner_last) for r in range(GRID_W, TILE, GRID_W)] + [(TILE - 8, last0)]

    def mask_rows(a, slabs):
        parts, last = [], 0
        for r0, m in slabs:
            if r0 > last:
                parts.append(a[last:r0])
            parts.append(a[r0:r0 + 8] * m)
            last = r0 + 8
        if last < TILE:
            parts.append(a[last:])
        return jnp.concatenate(parts, axis=0)

    def conv(up, off):
        w = cw_ref[:, off:off + FC]
        prev = mask_rows(pltpu.roll(up, 1, 0), prev_slabs)
        nxt = mask_rows(pltpu.roll(up, TILE - 1, 0), next_slabs)
        return w[0:1] * prev + w[1:2] * up + w[2:3] * nxt + cb_ref[:, off:off + FC]

    def up_chunk(c):
        og = c * FC
        ov = DFF + c * FC
        return (jnp.dot(h, wup_ref[:, og:og + FC], preferred_element_type=F32),
                jnp.dot(h, wup_ref[:, ov:ov + FC], preferred_element_type=F32))

    def down_chunk(act, c):
        return jnp.dot(act, wdn_ref[c * FC:(c + 1) * FC, :], preferred_element_type=F32)

    nchunk = DFF // FC
    acc = jnp.zeros((TILE, D), F32)
    nxt_up = up_chunk(0)
    act = None
    for c in range(nchunk):
        cur_up = nxt_up
        if c + 1 < nchunk:
            nxt_up = up_chunk(c + 1)
        if act is not None:
            acc = acc + down_chunk(act, c - 1)
        gate = conv(cur_up[0], c * FC)
        val = conv(cur_up[1], DFF + c * FC)
        hg = 0.5 * gate
        act = ((hg + hg * jnp.tanh(hg)) * val).astype(BF16)
    acc = acc + down_chunk(act, nchunk - 1)
    y = x + mod[5:6] * acc
    if not final:
        o_refs[0][...] = y
    else:
        ms = jnp.mean(y * y, axis=-1, keepdims=True)
        y = y * lax.rsqrt(ms + EPS) * gf_ref[...]

        @pl.when(i < NPC)
        def _():
            o_refs[0][...] = y

        @pl.when(i >= NPC)
        def _():
            o_refs[1][...] = y


def _ffn(x, mods, layer, g, wup, cw, cb, wdn, gf, *, final):
    if final:
        out_specs = [pl.BlockSpec((TILE, D), lambda i: (jnp.minimum(i, NPC - 1), 0)),
                     pl.BlockSpec((TILE, D), lambda i: (jnp.maximum(i - NPC, 0), 0))]
        out_shape = [jax.ShapeDtypeStruct((NPC * TILE, D), F32)] * 2
    else:
        out_specs = pl.BlockSpec((TILE, D), lambda i: (i, 0))
        out_shape = jax.ShapeDtypeStruct((NCOL * TILE, D), F32)
    return pl.pallas_call(
        functools.partial(_ffn_kernel, final=final),
        grid=(NCOL,),
        in_specs=[pl.BlockSpec((TILE, D), lambda i: (i, 0)),
                  pl.BlockSpec((None, None, 6, D), lambda i: (layer, _cond_of_col(i), 0, 0)),
                  pl.BlockSpec((1, D), lambda i: (0, 0)),
                  pl.BlockSpec((None, D, 2 * DFF), lambda i: (layer, 0, 0)),
                  pl.BlockSpec((None, 3, 2 * DFF), lambda i: (layer, 0, 0)),
                  pl.BlockSpec((None, 1, 2 * DFF), lambda i: (layer, 0, 0)),
                  pl.BlockSpec((None, DFF, D), lambda i: (layer, 0, 0)),
                  pl.BlockSpec((1, D), lambda i: (0, 0))],
        out_specs=out_specs,
        out_shape=out_shape,
        compiler_params=_cparams(1),
        name="ffn",
    )(x, mods, g, wup, cw, cb, wdn, gf)


def kernel(x_prompt, x_sample, state_ssm_re, state_ssm_im, c, c_ctx, w_ada, b_ada, g_mix, g_ffn,
           ssm_lam_re, ssm_lam_im, ssm_log_dt, ssm_b_re, ssm_b_im, ssm_c_re, ssm_c_im, ssm_d,
           w_glu, b_glu, w_fourier, b_fourier, w_up, conv_w, conv_b, w_down, g_final):
    nb = x_prompt.shape[0]
    xp = x_prompt.reshape(NPC * TILE, D)
    xs = x_sample.reshape(NPC * TILE, D)
    x = None

    cond8 = jnp.concatenate([c_ctx[None, :], c, jnp.zeros((5, D), F32)], axis=0)
    mods = _ada(cond8, w_ada, b_ada).reshape(DEPTH, 8, 6, D)
    col_cond = jnp.asarray([0] * NPC + [1] * SEG + [2] * SEG, jnp.int32)
    mods_t = mods.transpose(0, 2, 1, 3)[:, :, col_cond]

    wup_bf = _cast_bf16(w_up.reshape(DEPTH * D, 2 * DFF), 256).reshape(DEPTH, D, 2 * DFF)
    wdn_bf = _cast_bf16(w_down.reshape(DEPTH * DFF, D), DFF).reshape(DEPTH, DFF, D)
    wf_bf = _cast_bf16(w_fourier.reshape(2 * D, D), D).reshape(2, D, D)

    cc, cs = (jnp.asarray(m).astype(BF16) for m in _dft_mats(FGC))
    cl1, sl1 = cc, cs
    cl8, sl8 = (jnp.asarray(m).astype(BF16) for m in _dft_mats(SEG * TILE))

    njd = ssm_lam_re.shape[0] * 2
    lam_re = ssm_lam_re.reshape(njd, 1, NST)
    lam_im = ssm_lam_im.reshape(njd, 1, NST)
    dts = jnp.repeat(ssm_log_dt.reshape(njd, GS), PS, axis=-1).reshape(njd, 1, NST)
    bt_re = ssm_b_re.transpose(0, 1, 4, 2, 3).reshape(njd, GC, NST)
    bt_im = ssm_b_im.transpose(0, 1, 4, 2, 3).reshape(njd, GC, NST)
    cw_re = jnp.tile(ssm_c_re.transpose(0, 1, 2, 4, 3).reshape(njd, NST, GC), (1, 1, 8))
    cw_im = jnp.tile(ssm_c_im.transpose(0, 1, 2, 4, 3).reshape(njd, NST, GC), (1, 1, 8))
    st_re4 = state_ssm_re.reshape(2, njd, 1, NST)
    st_im4 = state_ssm_im.reshape(2, njd, 1, NST)

    edges = []
    for i in range(DEPTH):
        j = i // 2
        if i % 2 == 0:
            g = g_mix[i][None, :]
            xa, xb, off_b = (xp, xs, 0) if x is None else (x, x, NPC)
            xa3 = xa.reshape(-1, TILE, D)
            xb3 = xb.reshape(-1, TILE, D)
            par = (mods_t, i, g, lam_re, lam_im, dts, bt_re, bt_im, cw_re, cw_im)
            fin = [_s5_scan(xb3, off_b, *par, jd=2 * j + d, rev=d == 1, trunk=1, init="state",
                            h0=(st_re4, st_im4), emit_y=False, emit_edge=False, emit_fin=True)
                   for d in range(2)]
            carry = _carry(*fin[0], *fin[1], st_re4, st_im4, lam_re, lam_im, dts, j=j)
            ya, yb = [], []
            for d in range(2):
                y, er, ei = _s5_scan(xa3, 0, *par, jd=2 * j + d, rev=d == 1, trunk=0, init="zero",
                                     emit_y=True, emit_edge=True, emit_fin=False)
                ya.append(y.reshape(NPC * TILE, D))
                edges += [er, ei]
                (y,) = _s5_scan(xb3, off_b, *par, jd=2 * j + d, rev=d == 1, trunk=1, init="carry",
                                h0=carry, emit_y=True, emit_edge=False, emit_fin=False)
                yb.append(y.reshape(NPC * TILE, D))
            x = _glu(xa, xb, off_b, ya + yb, mods, i, g, ssm_d[j][None, :], w_glu, b_glu[j][None, :])
        else:
            g = g_mix[i][None, :]
            bf = b_fourier[j][None, :]
            x = _fourier(x, mods, i, g, cc, cs, cl1, sl1, wf_bf, bf,
                         nseq=NPC, nseg=1, col_off=0, cond_off=0, cond_stride=0)
            x = _fourier(x, mods, i, g, cc, cs, cl8, sl8, wf_bf, bf,
                         nseq=2, nseg=SEG, col_off=NPC, cond_off=1, cond_stride=1)
        x = _ffn(x, mods, i, g_ffn[i][None, :], wup_bf, conv_w, conv_b.reshape(DEPTH, 1, 2 * DFF),
                 wdn_bf, g_final[None, :], final=(i == DEPTH - 1))

    y_prompt = x[0].reshape(NPC, TILE, D)
    y_sample = x[1].reshape(2, SEG * TILE, D)
    ed = jnp.stack(edges, axis=0).reshape(DEPTH // 2, 2, 2, NPC, NST)[:, :, :, :nb]
    new_re = ed[:, :, 0].transpose(2, 0, 1, 3).reshape(nb, DEPTH // 2, 2, GS, PS)
    new_im = ed[:, :, 1].transpose(2, 0, 1, 3).reshape(nb, DEPTH // 2, 2, GS, PS)
    return (y_prompt, y_sample, new_re, new_im)
```

```python
import functools
import math

import jax
import jax.numpy as jnp
import numpy as np
from jax import lax
from jax.experimental import pallas as pl
from jax.experimental.pallas import tpu as pltpu

F32 = jnp.float32
BF16 = jnp.bfloat16

D = 1024
TILE = 256
NCOL = 32
NPC = 16
SEG = 8
DEPTH = 4
GS = 64
GC = 16
PS = 64
NST = GS * PS
FG = 4
FGC = 256
DFF = 2816
EPS = 1e-6
GRID_W = 64

TT = 32
NCH = TILE // TT
SW = 512
NSLAB = NST // SW
FC = 256

VMEM_LIMIT = 56 * 1024 * 1024


def _cparams(n_axes):
    return pltpu.CompilerParams(dimension_semantics=("arbitrary",) * n_axes,
                                vmem_limit_bytes=VMEM_LIMIT)


def _normmod(x, g, sc, sh):
    ms = jnp.mean(x * x, axis=-1, keepdims=True)
    return x * lax.rsqrt(ms + EPS) * (g * (1.0 + sc)) + sh


def _sigmoid(x):
    return 1.0 / (1.0 + jnp.exp(-x))


def _cond_of_col(i):
    return jnp.where(i < NPC, 0, 1 + (i - NPC) // SEG)


def _ada_kernel(c_ref, w_ref, b_ref, o_ref):
    c = c_ref[...]
    s = (c * _sigmoid(c)).astype(BF16)
    o_ref[...] = jnp.dot(s, w_ref[...].astype(BF16), preferred_element_type=F32) + b_ref[...]


def _ada(cond8, w_ada, b_ada):
    tn = 1536
    return pl.pallas_call(
        _ada_kernel,
        grid=(DEPTH, 6 * D // tn),
        in_specs=[pl.BlockSpec((8, D), lambda l, n: (0, 0)),
                  pl.BlockSpec((None, D, tn), lambda l, n: (l, 0, n)),
                  pl.BlockSpec((None, 1, tn), lambda l, n: (l, 0, n))],
        out_specs=pl.BlockSpec((None, 8, tn), lambda l, n: (l, 0, n)),
        out_shape=jax.ShapeDtypeStruct((DEPTH, 8, 6 * D), F32),
        compiler_params=_cparams(2),
        name="ada",
    )(cond8, w_ada, b_ada.reshape(DEPTH, 1, 6 * D))


def _cast_kernel(w_ref, o_ref):
    o_ref[...] = w_ref[...].astype(BF16)


def _cast_bf16(w2d, block_rows):
    r, c = w2d.shape
    return pl.pallas_call(
        _cast_kernel,
        grid=(r // block_rows,),
        in_specs=[pl.BlockSpec((block_rows, c), lambda i: (i, 0))],
        out_specs=pl.BlockSpec((block_rows, c), lambda i: (i, 0)),
        out_shape=jax.ShapeDtypeStruct((r, c), BF16),
        compiler_params=_cparams(1),
        name="cast",
    )(w2d)


def _abar(lr, li, logdt):
    dt = jnp.exp(logdt)
    mag = jnp.exp(lr * dt)
    return mag * jnp.cos(li * dt), mag * jnp.sin(li * dt)


def _s5_kernel(*refs, rev, col_base, init, emit_y, emit_edge, emit_fin):
    x_hbm, sc_ref, sh_ref, g_ref, lr_ref, li_ref, dt_ref, btr_ref, bti_ref, cwr_ref, cwi_ref = refs[:11]
    pos = 11
    if init != "zero":
        h0r_ref, h0i_ref = refs[pos:pos + 2]
        pos += 2
    if emit_y:
        y_hbm = refs[pos]
        pos += 1
    if emit_edge:
        er_ref, ei_ref = refs[pos:pos + 2]
        pos += 2
    if emit_fin:
        fr_ref, fi_ref = refs[pos:pos + 2]
        pos += 2
    scratch = refs[pos:]
    bu_re = scratch[0:NSLAB]
    bu_im = scratch[NSLAB:2 * NSLAB]
    bt_s, ct_s, ar_s, ai_s, hs_re, hs_im, xbuf, sem_in = scratch[2 * NSLAB:2 * NSLAB + 8]
    if emit_y:
        ybuf, sem_out = scratch[2 * NSLAB + 8:]

    i = pl.program_id(0)
    slot = lax.rem(i, 2)

    def t_start(i_):
        return pl.multiple_of((NCH - 1 - i_ if rev else i_) * TT, TT)

    def x_copies(i_, slot_):
        return [pltpu.make_async_copy(x_hbm.at[col_base + c, pl.ds(t_start(i_), TT), :],
                                      xbuf.at[slot_, :, c, :], sem_in.at[slot_]) for c in range(NPC)]

    def y_copies(i_, slot_):
        return [pltpu.make_async_copy(ybuf.at[slot_, :, c, :],
                                      y_hbm.at[c, pl.ds(t_start(i_), TT), :],
                                      sem_out.at[slot_]) for c in range(NPC)]

    @pl.when(i == 0)
    def _first_fetch():
        for cp in x_copies(i, slot):
            cp.start()

    if emit_y:
        @pl.when(i >= 2)
        def _ybuf_free():
            for cp in y_copies(i, slot):
                cp.wait()

    @pl.when(i == 0)
    def _prep():
        lr = lr_ref[...]
        li = li_ref[...]
        ar, ai = _abar(lr, li, dt_ref[...])
        ar_s[...] = jnp.broadcast_to(ar, (8, NST))
        ai_s[...] = jnp.broadcast_to(ai, (8, NST))
        xr = ar - 1.0
        den = lr * lr + li * li
        fr = (xr * lr + ai * li) / den
        fi = (ai * lr - xr * li) / den
        btr = btr_ref[...]
        bti = bti_ref[...]
        bbr = fr * btr - fi * bti
        bbi = fr * bti + fi * btr
        r = lax.broadcasted_iota(jnp.int32, (128, SW), 0)
        c = lax.broadcasted_iota(jnp.int32, (128, SW), 1)
        bmask = ((r >> 4) == (c >> 6)).astype(F32)
        for m in range(NSLAB):
            sl = slice(m * SW, (m + 1) * SW)
            tr = jnp.concatenate([bbr[:, sl]] * 8, axis=0) * bmask
            ti = jnp.concatenate([bbi[:, sl]] * 8, axis=0) * bmask
            bt_s[m, :, 0:SW] = tr.astype(BF16)
            bt_s[m, :, SW:2 * SW] = ti.astype(BF16)
        r2 = lax.broadcasted_iota(jnp.int32, (SW, 128), 0)
        c2 = lax.broadcasted_iota(jnp.int32, (SW, 128), 1)
        cmask = ((r2 >> 6) == (c2 >> 4)).astype(F32)
        for q in range(NSLAB):
            sl = slice(q * SW, (q + 1) * SW)
            ct_s[q, 0:SW, :] = (cwr_ref[sl, :] * cmask).astype(BF16)
            ct_s[q, SW:2 * SW, :] = (-(cwi_ref[sl, :] * cmask)).astype(BF16)
        if init == "carry":
            hs_re[...] = h0r_ref[...]
            hs_im[...] = h0i_ref[...]
        else:
            hs_re[...] = jnp.zeros((NPC, NST), F32)
            hs_im[...] = jnp.zeros((NPC, NST), F32)
            if init == "state":
                seg0 = SEG - 1 if rev else 0
                for b in range(2):
                    row = b * SEG + seg0
                    hs_re[row:row + 1, :] = h0r_ref[b]
                    hs_im[row:row + 1, :] = h0i_ref[b]

    for cp in x_copies(i, slot):
        cp.wait()
    for cp in x_copies(jnp.minimum(i + 1, NCH - 1), 1 - slot):
        cp.start()

    x3 = xbuf[slot]
    ms = jnp.mean(x3 * x3, axis=-1, keepdims=True)
    gain = g_ref[...] * (1.0 + sc_ref[...])
    u3 = x3 * lax.rsqrt(ms + EPS) * gain[None] + sh_ref[...][None]
    u = u3.reshape(TT * NPC, D).astype(BF16)

    def flip_time(a):
        if not rev:
            return a
        return jnp.concatenate([a[(TT - 1 - t) * NPC:(TT - t) * NPC] for t in range(TT)], axis=0)

    u = flip_time(u)

    def expand(m):
        res = jnp.dot(u[:, m * 128:(m + 1) * 128], bt_s[m], preferred_element_type=F32)
        bu_re[m][...] = res[:, 0:SW]
        bu_im[m][...] = res[:, SW:2 * SW]

    def recur(s):
        lanes = slice(s * SW, (s + 1) * SW)
        ar = ar_s[:, lanes]
        ai = ai_s[:, lanes]
        h = [hs_re[0:8, lanes], hs_im[0:8, lanes], hs_re[8:16, lanes], hs_im[8:16, lanes]]
        for t in range(TT):
            for q in range(2):
                rows = slice(t * NPC + 8 * q, t * NPC + 8 * q + 8)
                hr, hi = h[2 * q], h[2 * q + 1]
                nr = ar * hr - ai * hi + bu_re[s][rows, :]
                ni = ar * hi + ai * hr + bu_im[s][rows, :]
                bu_re[s][rows, :] = nr
                bu_im[s][rows, :] = ni
                h[2 * q], h[2 * q + 1] = nr, ni
        hs_re[0:8, lanes] = h[0]
        hs_im[0:8, lanes] = h[1]
        hs_re[8:16, lanes] = h[2]
        hs_im[8:16, lanes] = h[3]

    def contract(q):
        yq = (jnp.dot(bu_re[q][...].astype(BF16), ct_s[q, 0:SW, :], preferred_element_type=F32)
              + jnp.dot(bu_im[q][...].astype(BF16), ct_s[q, SW:2 * SW, :], preferred_element_type=F32))
        ybuf[slot, :, :, q * 128:(q + 1) * 128] = flip_time(yq).reshape(TT, NPC, 128)

    expand(0)
    for s in range(NSLAB):
        if s + 1 < NSLAB:
            expand(s + 1)
        if emit_y and s >= 1:
            contract(s - 1)
        recur(s)
    if emit_y:
        contract(NSLAB - 1)
        for cp in y_copies(i, slot):
            cp.start()

    if emit_edge:
        @pl.when(i == 0)
        def _edge():
            for s in range(NSLAB):
                er_ref[:, s * SW:(s + 1) * SW] = bu_re[s][0:NPC, :]
                ei_ref[:, s * SW:(s + 1) * SW] = bu_im[s][0:NPC, :]

    @pl.when(i == NCH - 1)
    def _last():
        if emit_fin:
            fr_ref[...] = hs_re[...]
            fi_ref[...] = hs_im[...]
        for cp in x_copies(i, 1 - slot):
            cp.wait()
        if emit_y:
            for cp in y_copies(i, slot) + y_copies(i, 1 - slot):
                cp.wait()


def _s5_scan(x3, col_base, mods_t, layer, g, lam_re, lam_im, dts, bt_re, bt_im, cw_re, cw_im, *,
             jd, rev, trunk, init, h0=None, emit_y, emit_edge, emit_fin):
    par = lambda i: (jd, 0, 0)
    in_specs = [
        pl.BlockSpec(memory_space=pl.ANY),
        pl.BlockSpec((None, None, NPC, D), lambda i: (layer, 1, trunk, 0)),
        pl.BlockSpec((None, None, NPC, D), lambda i: (layer, 0, trunk, 0)),
        pl.BlockSpec((1, D), lambda i: (0, 0)),
        pl.BlockSpec((None, 1, NST), par),
        pl.BlockSpec((None, 1, NST), par),
        pl.BlockSpec((None, 1, NST), par),
        pl.BlockSpec((None, GC, NST), par),
        pl.BlockSpec((None, GC, NST), par),
        pl.BlockSpec((None, NST, 128), par),
        pl.BlockSpec((None, NST, 128), par),
    ]
    args = [x3, mods_t, mods_t, g, lam_re, lam_im, dts, bt_re, bt_im, cw_re, cw_im]
    if init == "state":
        in_specs += [pl.BlockSpec((2, None, 1, NST), lambda i: (0, jd, 0, 0))] * 2
        args += list(h0)
    elif init == "carry":
        in_specs += [pl.BlockSpec((None, NPC, NST), lambda i: (1 if rev else 0, 0, 0))] * 2
        args += list(h0)
    out_specs, out_shape = [], []
    slab = pltpu.VMEM((TT * NPC, SW), F32)
    scratch = [slab] * (2 * NSLAB) + [
        pltpu.VMEM((NSLAB, 128, 2 * SW), BF16), pltpu.VMEM((NSLAB, 2 * SW, 128), BF16),
        pltpu.VMEM((8, NST), F32), pltpu.VMEM((8, NST), F32),
        pltpu.VMEM((NPC, NST), F32), pltpu.VMEM((NPC, NST), F32),
        pltpu.VMEM((2, TT, NPC, D), F32), pltpu.SemaphoreType.DMA((2,))]
    if emit_y:
        out_specs.append(pl.BlockSpec(memory_space=pl.ANY))
        out_shape.append(jax.ShapeDtypeStruct((NPC, TILE, D), F32))
        scratch += [pltpu.VMEM((2, TT, NPC, D), F32), pltpu.SemaphoreType.DMA((2,))]
    n_state_outs = 2 * (int(emit_edge) + int(emit_fin))
    out_specs += [pl.BlockSpec((NPC, NST), lambda i: (0, 0))] * n_state_outs
    out_shape += [jax.ShapeDtypeStruct((NPC, NST), F32)] * n_state_outs
    return pl.pallas_call(
        functools.partial(_s5_kernel, rev=rev, col_base=col_base, init=init,
                          emit_y=emit_y, emit_edge=emit_edge, emit_fin=emit_fin),
        grid=(NCH,),
        in_specs=in_specs,
        out_specs=out_specs,
        out_shape=out_shape,
        scratch_shapes=scratch,
        compiler_params=_cparams(1),
        name=("s5_scan" if emit_y else "s5_states") + ("_bwd" if rev else "_fwd") + str(trunk),
    )(*args)


def _carry_kernel(frf_ref, fif_ref, frb_ref, fib_ref, sr_ref, si_ref, lr_ref, li_ref, dt_ref,
                  or_ref, oi_ref, *, j):
    for d, (fr_ref, fi_ref) in enumerate(((frf_ref, fif_ref), (frb_ref, fib_ref))):
        jd = 2 * j + d
        ar, ai = _abar(lr_ref[jd], li_ref[jd], dt_ref[jd])
        for _ in range(8):
            ar, ai = ar * ar - ai * ai, 2.0 * ar * ai
        for b in range(2):
            order = list(range(SEG)) if d == 0 else list(range(SEG - 1, -1, -1))
            r = b * SEG + order[0]
            or_ref[d, r:r + 1, :] = sr_ref[b, jd]
            oi_ref[d, r:r + 1, :] = si_ref[b, jd]
            tr = fr_ref[r:r + 1, :]
            ti = fi_ref[r:r + 1, :]
            for s in order[1:]:
                r = b * SEG + s
                or_ref[d, r:r + 1, :] = tr
                oi_ref[d, r:r + 1, :] = ti
                tr, ti = (fr_ref[r:r + 1, :] + ar * tr - ai * ti,
                          fi_ref[r:r + 1, :] + ar * ti + ai * tr)


def _carry(frf, fif, frb, fib, st_re4, st_im4, lam_re, lam_im, dts, *, j):
    shp = jax.ShapeDtypeStruct((2, NPC, NST), F32)
    return pl.pallas_call(functools.partial(_carry_kernel, j=j), out_shape=[shp, shp], name="s5_carry")(
        frf, fif, frb, fib, st_re4, st_im4, lam_re, lam_im, dts)


GT = 2 * TILE


def _glu_kernel(xa_ref, xb_ref, yfa_ref, yba_ref, yfb_ref, ybb_ref, mod_ref, g_ref, d_ref, w_ref, b_ref,
                o_ref, wbf, ge_s, z_s):
    i = pl.program_id(0)

    @pl.when(i == 0)
    def _():
        wbf[...] = w_ref[...].astype(BF16)

    mod = mod_ref[...]
    first_prompt = i < NPC // 2

    nq = 4
    cw = D // nq
    rw = TILE // nq

    def x_rows(r0, r1, c0=0, c1=D):
        return jnp.where(first_prompt, xa_ref[r0:r1, c0:c1], xb_ref[r0:r1, c0:c1])

    def pre(r0, r1):
        h = _normmod(x_rows(r0, r1), g_ref[...], mod[1:2], mod[0:1])
        y_ssm = jnp.where(first_prompt, yfa_ref[r0:r1, :] + yba_ref[r0:r1, :], yfb_ref[r0:r1, :] + ybb_ref[r0:r1, :])
        y = y_ssm + d_ref[...] * h
        k1 = -2.0 * math.sqrt(2.0 / math.pi)
        ge = y / (1.0 + jnp.exp(y * (k1 + (k1 * 0.044715) * (y * y))))
        ge_s[r0:r1, :] = ge.astype(BF16)

    def gate(q, j):
        rows = slice(q * TILE, (q + 1) * TILE)
        for c0 in (j * cw, D + j * cw):
            z_s[rows, c0:c0 + cw] = (jnp.dot(ge_s[rows, :], wbf[:, c0:c0 + cw], preferred_element_type=F32)
                                     + b_ref[:, c0:c0 + cw])

    def post(q, j):
        rows = slice(q * TILE, (q + 1) * TILE)
        c0, c1 = j * cw, (j + 1) * cw
        out = z_s[rows, c0:c1] * _sigmoid(z_s[rows, D + c0:D + c1])
        o_ref[rows, c0:c1] = x_rows(q * TILE, (q + 1) * TILE, c0, c1) + mod[2:3, c0:c1] * out

    for j in range(nq):
        pre(j * rw, (j + 1) * rw)
    for j in range(nq):
        gate(0, j)
        pre(TILE + j * rw, TILE + (j + 1) * rw)
    for j in range(nq):
        gate(1, j)
        post(0, j)
    for j in range(nq):
        post(1, j)


def _glu(xa, xb, off_b, ys, mods, layer, g, dvec, w, b):
    nblk = NCOL // 2
    half = NPC // 2
    pa = lambda i: (jnp.minimum(i, half - 1), 0)
    pb = lambda i: (jnp.maximum(i - half, 0), 0)
    return pl.pallas_call(
        _glu_kernel,
        grid=(nblk,),
        in_specs=[pl.BlockSpec((GT, D), pa),
                  pl.BlockSpec((GT, D), lambda i: (off_b // 2 + jnp.maximum(i - half, 0), 0)),
                  pl.BlockSpec((GT, D), pa),
                  pl.BlockSpec((GT, D), pa),
                  pl.BlockSpec((GT, D), pb),
                  pl.BlockSpec((GT, D), pb),
                  pl.BlockSpec((None, None, 6, D), lambda i: (layer, _cond_of_col(2 * i), 0, 0)),
                  pl.BlockSpec((1, D), lambda i: (0, 0)),
                  pl.BlockSpec((1, D), lambda i: (0, 0)),
                  pl.BlockSpec((None, D, 2 * D), lambda i: (layer // 2, 0, 0)),
                  pl.BlockSpec((1, 2 * D), lambda i: (0, 0))],
        out_specs=pl.BlockSpec((GT, D), lambda i: (i, 0)),
        out_shape=jax.ShapeDtypeStruct((NCOL * TILE, D), F32),
        scratch_shapes=[pltpu.VMEM((D, 2 * D), BF16), pltpu.VMEM((GT, D), BF16), pltpu.VMEM((GT, 2 * D), F32)],
        compiler_params=_cparams(1),
        name="s5_glu",
    )(xa, xb, *ys, mods, g, dvec, w, b)


def _fourier_kernel(x_ref, mod_ref, g_ref, cc_ref, cs_ref, cl_ref, sl_ref, w_ref, b_ref,
                    o_ref, *scratch, scale, nseg):
    mod = mod_ref[...]

    def channel_dft():
        h = _normmod(x_ref[...], g_ref[...], mod[1:2], mod[0:1]).astype(BF16)
        xc, xs = [], []
        for q in range(FG):
            hq = h[:, q * FGC:(q + 1) * FGC]
            xc.append(jnp.dot(hq, cc_ref[...], preferred_element_type=F32).astype(BF16))
            xs.append(jnp.dot(hq, cs_ref[...], preferred_element_type=F32).astype(BF16))
        return jnp.concatenate(xc, axis=1), jnp.concatenate(xs, axis=1)

    def position_dft(xc, xs):
        f = (jnp.dot(cl_ref[...], xc, preferred_element_type=F32)
             - jnp.dot(sl_ref[...], xs, preferred_element_type=F32)) * scale
        o = jnp.dot(f.astype(BF16), w_ref[...], preferred_element_type=F32) + b_ref[...]
        o_ref[...] = x_ref[...] + mod[2:3] * o

    if nseg == 1:
        position_dft(*channel_dft())
        return

    xc_s, xs_s = scratch
    ph = pl.program_id(1)
    s = pl.program_id(2)

    @pl.when(ph == 0)
    def _():
        r0 = pl.multiple_of(s * TILE, TILE)
        xc, xs = channel_dft()
        xc_s[pl.ds(r0, TILE), :] = xc
        xs_s[pl.ds(r0, TILE), :] = xs

    @pl.when(ph == 1)
    def _():
        position_dft(xc_s[...], xs_s[...])


def _fourier(x, mods, layer, g, cc, cs, cl, sl, w, b, *, nseq, nseg, col_off, cond_off, cond_stride):
    ln = nseg * TILE
    if nseg == 1:
        grid = (nseq,)
        idx = lambda f: (lambda q: f(q, 1, 0))
        scratch = []
    else:
        grid = (nseq, 2, nseg)
        idx = lambda f: f
        scratch = [pltpu.VMEM((ln, D), BF16), pltpu.VMEM((ln, D), BF16)]
    return pl.pallas_call(
        functools.partial(_fourier_kernel, scale=1.0 / math.sqrt(ln * FGC), nseg=nseg),
        grid=grid,
        in_specs=[pl.BlockSpec((TILE, D), idx(lambda q, ph, s: (col_off + q * nseg + s, 0))),
                  pl.BlockSpec((None, None, 6, D), idx(lambda q, ph, s: (layer, cond_off + q * cond_stride, 0, 0))),
                  pl.BlockSpec((1, D), idx(lambda q, ph, s: (0, 0))),
                  pl.BlockSpec((FGC, FGC), idx(lambda q, ph, s: (0, 0))),
                  pl.BlockSpec((FGC, FGC), idx(lambda q, ph, s: (0, 0))),
                  pl.BlockSpec((TILE, ln), idx(lambda q, ph, s: (s * ph, 0))),
                  pl.BlockSpec((TILE, ln), idx(lambda q, ph, s: (s * ph, 0))),
                  pl.BlockSpec((None, D, D), idx(lambda q, ph, s: (layer // 2, 0, 0))),
                  pl.BlockSpec((1, D), idx(lambda q, ph, s: (0, 0)))],
        out_specs=pl.BlockSpec((TILE, D), idx(lambda q, ph, s: (col_off + q * nseg + s * ph, 0))),
        out_shape=jax.ShapeDtypeStruct((NCOL * TILE, D), F32),
        scratch_shapes=scratch,
        input_output_aliases={0: 0},
        compiler_params=_cparams(len(grid)),
        name="fourier%d" % nseg,
    )(x, mods, g, cc, cs, cl, sl, w, b)


@functools.lru_cache(maxsize=None)
def _dft_mats(n):
    k = np.arange(n, dtype=np.int64)
    ang = ((k[:, None] * k[None, :]) % n).astype(np.float64) * (2.0 * math.pi / n)
    return np.cos(ang).astype(np.float32), np.sin(ang).astype(np.float32)


def _ffn_kernel(x_ref, mod_ref, g_ref, wup_ref, cw_ref, cb_ref, wdn_ref, gf_ref, *o_refs, final):
    i = pl.program_id(0)
    x = x_ref[...]
    mod = mod_ref[...]
    h = _normmod(x, g_ref[...], mod[4:5], mod[3:4]).astype(BF16)
    sub = lax.broadcasted_iota(jnp.int32, (8, FC), 0)
    one = jnp.ones((8, FC), F32)
    first0 = (sub != 0).astype(F32)
    last0 = (sub != 7).astype(F32)
    inner_first = jnp.where(i < NPC, one, first0)
    inner_last = jnp.where(i < NPC, one, last0)
    prev_slabs = [(0, first0)] + [(r, inner_first) for r in range(GRID_W, TILE, GRID_W)]
    next_slabs = [(r - 8, inner_last) for r in range(GRID_W, TILE, GRID_W)] + [(TILE - 8, last0)]

    def mask_rows(a, slabs):
        parts, last = [], 0
        for r0, m in slabs:
            if r0 > last:
                parts.append(a[last:r0])
            parts.append(a[r0:r0 + 8] * m)
            last = r0 + 8
        if last < TILE:
            parts.append(a[last:])
        return jnp.concatenate(parts, axis=0)

    def conv(up, off):
        w = cw_ref[:, off:off + FC]
        prev = mask_rows(pltpu.roll(up, 1, 0), prev_slabs)
        nxt = mask_rows(pltpu.roll(up, TILE - 1, 0), next_slabs)
        return w[0:1] * prev + w[1:2] * up + w[2:3] * nxt + cb_ref[:, off:off + FC]

    def up_chunk(c):
        og = c * FC
        ov = DFF + c * FC
        return (jnp.dot(h, wup_ref[:, og:og + FC], preferred_element_type=F32),
                jnp.dot(h, wup_ref[:, ov:ov + FC], preferred_element_type=F32))

    def down_chunk(act, c):
        return jnp.dot(act, wdn_ref[c * FC:(c + 1) * FC, :], preferred_element_type=F32)

    nchunk = DFF // FC
    acc = jnp.zeros((TILE, D), F32)
    nxt_up = up_chunk(0)
    act = None
    for c in range(nchunk):
        cur_up = nxt_up
        if c + 1 < nchunk:
            nxt_up = up_chunk(c + 1)
        if act is not None:
            acc = acc + down_chunk(act, c - 1)
        gate = conv(cur_up[0], c * FC)
        val = conv(cur_up[1], DFF + c * FC)
        hg = 0.5 * gate
        act = ((hg + hg * jnp.tanh(hg)) * val).astype(BF16)
    acc = acc + down_chunk(act, nchunk - 1)
    y = x + mod[5:6] * acc
    if not final:
        o_refs[0][...] = y
    else:
        ms = jnp.mean(y * y, axis=-1, keepdims=True)
        y = y * lax.rsqrt(ms + EPS) * gf_ref[...]

        @pl.when(i < NPC)
        def _():
            o_refs[0][...] = y

        @pl.when(i >= NPC)
        def _():
            o_refs[1][...] = y


def _ffn(x, mods, layer, g, wup, cw, cb, wdn, gf, *, final):
    if final:
        out_specs = [pl.BlockSpec((TILE, D), lambda i: (jnp.minimum(i, NPC - 1), 0)),
                     pl.BlockSpec((TILE, D), lambda i: (jnp.maximum(i - NPC, 0), 0))]
        out_shape = [jax.ShapeDtypeStruct((NPC * TILE, D), F32)] * 2
    else:
        out_specs = pl.BlockSpec((TILE, D), lambda i: (i, 0))
        out_shape = jax.ShapeDtypeStruct((NCOL * TILE, D), F32)
    return pl.pallas_call(
        functools.partial(_ffn_kernel, final=final),
        grid=(NCOL,),
        in_specs=[pl.BlockSpec((TILE, D), lambda i: (i, 0)),
                  pl.BlockSpec((None, None, 6, D), lambda i: (layer, _cond_of_col(i), 0, 0)),
                  pl.BlockSpec((1, D), lambda i: (0, 0)),
                  pl.BlockSpec((None, D, 2 * DFF), lambda i: (layer, 0, 0)),
                  pl.BlockSpec((None, 3, 2 * DFF), lambda i: (layer, 0, 0)),
                  pl.BlockSpec((None, 1, 2 * DFF), lambda i: (layer, 0, 0)),
                  pl.BlockSpec((None, DFF, D), lambda i: (layer, 0, 0)),
                  pl.BlockSpec((1, D), lambda i: (0, 0))],
        out_specs=out_specs,
        out_shape=out_shape,
        compiler_params=_cparams(1),
        name="ffn",
    )(x, mods, g, wup, cw, cb, wdn, gf)


def kernel(x_prompt, x_sample, state_ssm_re, state_ssm_im, c, c_ctx, w_ada, b_ada, g_mix, g_ffn,
           ssm_lam_re, ssm_lam_im, ssm_log_dt, ssm_b_re, ssm_b_im, ssm_c_re, ssm_c_im, ssm_d,
           w_glu, b_glu, w_fourier, b_fourier, w_up, conv_w, conv_b, w_down, g_final):
    nb = x_prompt.shape[0]
    xp = x_prompt.reshape(NPC * TILE, D)
    xs = x_sample.reshape(NPC * TILE, D)
    x = None

    cond8 = jnp.concatenate([c_ctx[None, :], c, jnp.zeros((5, D), F32)], axis=0)
    mods = _ada(cond8, w_ada, b_ada).reshape(DEPTH, 8, 6, D)
    col_cond = jnp.asarray([0] * NPC + [1] * SEG + [2] * SEG, jnp.int32)
    mods_t = mods.transpose(0, 2, 1, 3)[:, :, col_cond]

    wup_bf = _cast_bf16(w_up.reshape(DEPTH * D, 2 * DFF), 256).reshape(DEPTH, D, 2 * DFF)
    wdn_bf = _cast_bf16(w_down.reshape(DEPTH * DFF, D), DFF).reshape(DEPTH, DFF, D)
    wf_bf = _cast_bf16(w_fourier.reshape(2 * D, D), D).reshape(2, D, D)

    cc, cs = (jnp.asarray(m).astype(BF16) for m in _dft_mats(FGC))
    cl1, sl1 = cc, cs
    cl8, sl8 = (jnp.asarray(m).astype(BF16) for m in _dft_mats(SEG * TILE))

    njd = ssm_lam_re.shape[0] * 2
    lam_re = ssm_lam_re.reshape(njd, 1, NST)
    lam_im = ssm_lam_im.reshape(njd, 1, NST)
    dts = jnp.repeat(ssm_log_dt.reshape(njd, GS), PS, axis=-1).reshape(njd, 1, NST)
    bt_re = ssm_b_re.transpose(0, 1, 4, 2, 3).reshape(njd, GC, NST)
    bt_im = ssm_b_im.transpose(0, 1, 4, 2, 3).reshape(njd, GC, NST)
    cw_re = jnp.tile(ssm_c_re.transpose(0, 1, 2, 4, 3).reshape(njd, NST, GC), (1, 1, 8))
    cw_im = jnp.tile(ssm_c_im.transpose(0, 1, 2, 4, 3).reshape(njd, NST, GC), (1, 1, 8))
    st_re4 = state_ssm_re.reshape(2, njd, 1, NST)
    st_im4 = state_ssm_im.reshape(2, njd, 1, NST)

    edges = []
    for i in range(DEPTH):
        j = i // 2
        if i % 2 == 0:
            g = g_mix[i][None, :]
            xa, xb, off_b = (xp, xs, 0) if x is None else (x, x, NPC)
            xa3 = xa.reshape(-1, TILE, D)
            xb3 = xb.reshape(-1, TILE, D)
            par = (mods_t, i, g, lam_re, lam_im, dts, bt_re, bt_im, cw_re, cw_im)
            fin = [_s5_scan(xb3, off_b, *par, jd=2 * j + d, rev=d == 1, trunk=1, init="state",
                            h0=(st_re4, st_im4), emit_y=False, emit_edge=False, emit_fin=True)
                   for d in range(2)]
            carry = _carry(*fin[0], *fin[1], st_re4, st_im4, lam_re, lam_im, dts, j=j)
            ya, yb = [], []
            for d in range(2):
                y, er, ei = _s5_scan(xa3, 0, *par, jd=2 * j + d, rev=d == 1, trunk=0, init="zero",
                                     emit_y=True, emit_edge=True, emit_fin=False)
                ya.append(y.reshape(NPC * TILE, D))
                edges += [er, ei]
                (y,) = _s5_scan(xb3, off_b, *par, jd=2 * j + d, rev=d == 1, trunk=1, init="carry",
                                h0=carry, emit_y=True, emit_edge=False, emit_fin=False)
                yb.append(y.reshape(NPC * TILE, D))
            x = _glu(xa, xb, off_b, ya + yb, mods, i, g, ssm_d[j][None, :], w_glu, b_glu[j][None, :])
        else:
            g = g_mix[i][None, :]
            bf = b_fourier[j][None, :]
            x = _fourier(x, mods, i, g, cc, cs, cl1, sl1, wf_bf, bf,
                         nseq=NPC, nseg=1, col_off=0, cond_off=0, cond_stride=0)
            x = _fourier(x, mods, i, g, cc, cs, cl8, sl8, wf_bf, bf,
                         nseq=2, nseg=SEG, col_off=NPC, cond_off=1, cond_stride=1)
        x = _ffn(x, mods, i, g_ffn[i][None, :], wup_bf, conv_w, conv_b.reshape(DEPTH, 1, 2 * DFF),
                 wdn_bf, g_final[None, :], final=(i == DEPTH - 1))

    y_prompt = x[0].reshape(NPC, TILE, D)
    y_sample = x[1].reshape(2, SEG * TILE, D)
    ed = jnp.stack(edges, axis=0).reshape(DEPTH // 2, 2, 2, NPC, NST)[:, :, :, :nb]
    new_re = ed[:, :, 0].transpose(2, 0, 1, 3).reshape(nb, DEPTH // 2, 2, GS, PS)
    new_im = ed[:, :, 1].transpose(2, 0, 1, 3).reshape(nb, DEPTH // 2, 2, GS, PS)
    return (y_prompt, y_sample, new_re, new_im)
```

```python
import functools
import math

import jax
import jax.numpy as jnp
import numpy as np
from jax import lax
from jax.experimental import pallas as pl
from jax.experimental.pallas import tpu as pltpu

F32 = jnp.float32
BF16 = jnp.bfloat16

D = 1024
TILE = 256
NCOL = 32
NPC = 16
SEG = 8
DEPTH = 4
GS = 64
GC = 16
PS = 64
NST = GS * PS
FG = 4
FGC = 256
DFF = 2816
EPS = 1e-6
GRID_W = 64

TT = 32
NCH = TILE // TT
SW = 512
NSLAB = NST // SW
FC = 256

VMEM_LIMIT = 56 * 1024 * 1024


def _cparams(n_axes):
    return pltpu.CompilerParams(dimension_semantics=("arbitrary",) * n_axes,
                                vmem_limit_bytes=VMEM_LIMIT)


def _normmod(x, g, sc, sh):
    ms = jnp.mean(x * x, axis=-1, keepdims=True)
    return x * lax.rsqrt(ms + EPS) * (g * (1.0 + sc)) + sh


def _sigmoid(x):
    return 1.0 / (1.0 + jnp.exp(-x))


def _cond_of_col(i):
    return jnp.where(i < NPC, 0, 1 + (i - NPC) // SEG)


def _ada_kernel(c_ref, w_ref, b_ref, o_ref):
    c = c_ref[...]
    s = (c * _sigmoid(c)).astype(BF16)
    o_ref[...] = jnp.dot(s, w_ref[...].astype(BF16), preferred_element_type=F32) + b_ref[...]


def _ada(cond8, w_ada, b_ada):
    tn = 1536
    return pl.pallas_call(
        _ada_kernel,
        grid=(DEPTH, 6 * D // tn),
        in_specs=[pl.BlockSpec((8, D), lambda l, n: (0, 0)),
                  pl.BlockSpec((None, D, tn), lambda l, n: (l, 0, n)),
                  pl.BlockSpec((None, 1, tn), lambda l, n: (l, 0, n))],
        out_specs=pl.BlockSpec((None, 8, tn), lambda l, n: (l, 0, n)),
        out_shape=jax.ShapeDtypeStruct((DEPTH, 8, 6 * D), F32),
        compiler_params=_cparams(2),
        name="ada",
    )(cond8, w_ada, b_ada.reshape(DEPTH, 1, 6 * D))


def _cast_kernel(w_ref, o_ref):
    o_ref[...] = w_ref[...].astype(BF16)


def _cast_bf16(w2d, block_rows):
    r, c = w2d.shape
    return pl.pallas_call(
        _cast_kernel,
        grid=(r // block_rows,),
        in_specs=[pl.BlockSpec((block_rows, c), lambda i: (i, 0))],
        out_specs=pl.BlockSpec((block_rows, c), lambda i: (i, 0)),
        out_shape=jax.ShapeDtypeStruct((r, c), BF16),
        compiler_params=_cparams(1),
        name="cast",
    )(w2d)


def _abar(lr, li, logdt):
    dt = jnp.exp(logdt)
    mag = jnp.exp(lr * dt)
    return mag * jnp.cos(li * dt), mag * jnp.sin(li * dt)


def _s5_kernel(*refs, rev, col_base, init, emit_y, emit_edge, emit_fin):
    x_hbm, sc_ref, sh_ref, g_ref, lr_ref, li_ref, dt_ref, btr_ref, bti_ref, cwr_ref, cwi_ref = refs[:11]
    pos = 11
    if init != "zero":
        h0r_ref, h0i_ref = refs[pos:pos + 2]
        pos += 2
    if emit_y:
        y_hbm = refs[pos]
        pos += 1
    if emit_edge:
        er_ref, ei_ref = refs[pos:pos + 2]
        pos += 2
    if emit_fin:
        fr_ref, fi_ref = refs[pos:pos + 2]
        pos += 2
    scratch = refs[pos:]
    bu_re = scratch[0:NSLAB]
    bu_im = scratch[NSLAB:2 * NSLAB]
    bt_s, ct_s, ar_s, ai_s, hs_re, hs_im, xbuf, sem_in = scratch[2 * NSLAB:2 * NSLAB + 8]
    pos = 2 * NSLAB + 8
    if emit_y:
        ybuf, sem_out = scratch[pos:pos + 2]
        hb = scratch[pos + 2:pos + 2 + NSLAB]
        pos += 2 + NSLAB
    if emit_edge:
        ed_re, ed_im = scratch[pos:pos + 2]

    i = pl.program_id(0)
    slot = lax.rem(i, 2)

    def t_start(i_):
        return pl.multiple_of((NCH - 1 - i_ if rev else i_) * TT, TT)

    def x_copies(i_, slot_):
        return [pltpu.make_async_copy(x_hbm.at[col_base + c, pl.ds(t_start(i_), TT), :],
                                      xbuf.at[slot_, :, c, :], sem_in.at[slot_]) for c in range(NPC)]

    def y_copies(i_, slot_):
        return [pltpu.make_async_copy(ybuf.at[slot_, :, c, :],
                                      y_hbm.at[c, pl.ds(t_start(i_), TT), :],
                                      sem_out.at[slot_]) for c in range(NPC)]

    @pl.when(i == 0)
    def _first_fetch():
        for cp in x_copies(i, slot):
            cp.start()

    if emit_y:
        @pl.when(i >= 2)
        def _ybuf_free():
            for cp in y_copies(i, slot):
                cp.wait()

    @pl.when(i == 0)
    def _prep():
        lr = lr_ref[...]
        li = li_ref[...]
        ar, ai = _abar(lr, li, dt_ref[...])
        ar_s[...] = jnp.broadcast_to(ar, (8, NST))
        ai_s[...] = jnp.broadcast_to(ai, (8, NST))
        xr = ar - 1.0
        den = lr * lr + li * li
        fr = (xr * lr + ai * li) / den
        fi = (ai * lr - xr * li) / den
        btr = btr_ref[...]
        bti = bti_ref[...]
        bbr = fr * btr - fi * bti
        bbi = fr * bti + fi * btr
        r = lax.broadcasted_iota(jnp.int32, (128, SW), 0)
        c = lax.broadcasted_iota(jnp.int32, (128, SW), 1)
        bmask = ((r >> 4) == (c >> 6)).astype(F32)
        for m in range(NSLAB):
            sl = slice(m * SW, (m + 1) * SW)
            tr = jnp.concatenate([bbr[:, sl]] * 8, axis=0) * bmask
            ti = jnp.concatenate([bbi[:, sl]] * 8, axis=0) * bmask
            bt_s[m, :, 0:SW] = tr.astype(BF16)
            bt_s[m, :, SW:2 * SW] = ti.astype(BF16)
        r2 = lax.broadcasted_iota(jnp.int32, (SW, 128), 0)
        c2 = lax.broadcasted_iota(jnp.int32, (SW, 128), 1)
        cmask = ((r2 >> 6) == (c2 >> 4)).astype(F32)
        for q in range(NSLAB):
            sl = slice(q * SW, (q + 1) * SW)
            ct_s[q, 0:SW, :] = (cwr_ref[sl, :] * cmask).astype(BF16)
            ct_s[q, SW:2 * SW, :] = (-(cwi_ref[sl, :] * cmask)).astype(BF16)
        if init == "carry":
            hs_re[...] = h0r_ref[...]
            hs_im[...] = h0i_ref[...]
        else:
            hs_re[...] = jnp.zeros((NPC, NST), F32)
            hs_im[...] = jnp.zeros((NPC, NST), F32)
            if init == "state":
                seg0 = SEG - 1 if rev else 0
                for b in range(2):
                    row = b * SEG + seg0
                    hs_re[row:row + 1, :] = h0r_ref[b]
                    hs_im[row:row + 1, :] = h0i_ref[b]

    for cp in x_copies(i, slot):
        cp.wait()
    for cp in x_copies(jnp.minimum(i + 1, NCH - 1), 1 - slot):
        cp.start()

    x3 = xbuf[slot]
    ms = jnp.mean(x3 * x3, axis=-1, keepdims=True)
    gain = g_ref[...] * (1.0 + sc_ref[...])
    u3 = x3 * lax.rsqrt(ms + EPS) * gain[None] + sh_ref[...][None]
    u = u3.reshape(TT * NPC, D).astype(BF16)

    def flip_time(a):
        if not rev:
            return a
        return jnp.concatenate([a[(TT - 1 - t) * NPC:(TT - t) * NPC] for t in range(TT)], axis=0)

    u = flip_time(u)

    def expand(m):
        res = jnp.dot(u[:, m * 128:(m + 1) * 128], bt_s[m], preferred_element_type=F32)
        bu_re[m][...] = res[:, 0:SW]
        bu_im[m][...] = res[:, SW:2 * SW]

    def recur(s):
        lanes = slice(s * SW, (s + 1) * SW)
        ar = ar_s[:, lanes]
        ai = ai_s[:, lanes]
        h = [hs_re[0:8, lanes], hs_im[0:8, lanes], hs_re[8:16, lanes], hs_im[8:16, lanes]]
        for t in range(TT):
            for q in range(2):
                rows = slice(t * NPC + 8 * q, t * NPC + 8 * q + 8)
                hr, hi = h[2 * q], h[2 * q + 1]
                nr = ar * hr - ai * hi + bu_re[s][rows, :]
                ni = ar * hi + ai * hr + bu_im[s][rows, :]
                h[2 * q], h[2 * q + 1] = nr, ni
            if emit_y:
                trow = slice(t * NPC, (t + 1) * NPC)
                hb[s][trow, 0:SW] = jnp.concatenate([h[0], h[2]], axis=0).astype(BF16)
                hb[s][trow, SW:2 * SW] = jnp.concatenate([h[1], h[3]], axis=0).astype(BF16)
            if emit_edge and t == 0:
                ed_re[0:8, lanes], ed_im[0:8, lanes] = h[0], h[1]
                ed_re[8:16, lanes], ed_im[8:16, lanes] = h[2], h[3]
        hs_re[0:8, lanes] = h[0]
        hs_im[0:8, lanes] = h[1]
        hs_re[8:16, lanes] = h[2]
        hs_im[8:16, lanes] = h[3]

    def contract(q):
        yq = jnp.dot(hb[q][...], ct_s[q], preferred_element_type=F32)
        ybuf[slot, :, :, q * 128:(q + 1) * 128] = flip_time(yq).reshape(TT, NPC, 128)

    expand(0)
    for s in range(NSLAB):
        if s + 1 < NSLAB:
            expand(s + 1)
        if emit_y and s >= 1:
            contract(s - 1)
        recur(s)
    if emit_y:
        contract(NSLAB - 1)
        for cp in y_copies(i, slot):
            cp.start()

    if emit_edge:
        @pl.when(i == 0)
        def _edge():
            er_ref[...] = ed_re[...]
            ei_ref[...] = ed_im[...]

    @pl.when(i == NCH - 1)
    def _last():
        if emit_fin:
            fr_ref[...] = hs_re[...]
            fi_ref[...] = hs_im[...]
        for cp in x_copies(i, 1 - slot):
            cp.wait()
        if emit_y:
            for cp in y_copies(i, slot) + y_copies(i, 1 - slot):
                cp.wait()


def _s5_scan(x3, col_base, mods_t, layer, g, lam_re, lam_im, dts, bt_re, bt_im, cw_re, cw_im, *,
             jd, rev, trunk, init, h0=None, emit_y, emit_edge, emit_fin):
    par = lambda i: (jd, 0, 0)
    in_specs = [
        pl.BlockSpec(memory_space=pl.ANY),
        pl.BlockSpec((None, None, NPC, D), lambda i: (layer, 1, trunk, 0)),
        pl.BlockSpec((None, None, NPC, D), lambda i: (layer, 0, trunk, 0)),
        pl.BlockSpec((1, D), lambda i: (0, 0)),
        pl.BlockSpec((None, 1, NST), par),
        pl.BlockSpec((None, 1, NST), par),
        pl.BlockSpec((None, 1, NST), par),
        pl.BlockSpec((None, GC, NST), par),
        pl.BlockSpec((None, GC, NST), par),
        pl.BlockSpec((None, NST, 128), par),
        pl.BlockSpec((None, NST, 128), par),
    ]
    args = [x3, mods_t, mods_t, g, lam_re, lam_im, dts, bt_re, bt_im, cw_re, cw_im]
    if init == "state":
        in_specs += [pl.BlockSpec((2, None, 1, NST), lambda i: (0, jd, 0, 0))] * 2
        args += list(h0)
    elif init == "carry":
        in_specs += [pl.BlockSpec((None, NPC, NST), lambda i: (1 if rev else 0, 0, 0))] * 2
        args += list(h0)
    out_specs, out_shape = [], []
    slab = pltpu.VMEM((TT * NPC, SW), F32)
    scratch = [slab] * (2 * NSLAB) + [
        pltpu.VMEM((NSLAB, 128, 2 * SW), BF16), pltpu.VMEM((NSLAB, 2 * SW, 128), BF16),
        pltpu.VMEM((8, NST), F32), pltpu.VMEM((8, NST), F32),
        pltpu.VMEM((NPC, NST), F32), pltpu.VMEM((NPC, NST), F32),
        pltpu.VMEM((2, TT, NPC, D), F32), pltpu.SemaphoreType.DMA((2,))]
    if emit_y:
        out_specs.append(pl.BlockSpec(memory_space=pl.ANY))
        out_shape.append(jax.ShapeDtypeStruct((NPC, TILE, D), F32))
        scratch += [pltpu.VMEM((2, TT, NPC, D), F32), pltpu.SemaphoreType.DMA((2,))]
        scratch += [pltpu.VMEM((TT * NPC, 2 * SW), BF16)] * NSLAB
    if emit_edge:
        scratch += [pltpu.VMEM((NPC, NST), F32)] * 2
    n_state_outs = 2 * (int(emit_edge) + int(emit_fin))
    out_specs += [pl.BlockSpec((NPC, NST), lambda i: (0, 0))] * n_state_outs
    out_shape += [jax.ShapeDtypeStruct((NPC, NST), F32)] * n_state_outs
    return pl.pallas_call(
        functools.partial(_s5_kernel, rev=rev, col_base=col_base, init=init,
                          emit_y=emit_y, emit_edge=emit_edge, emit_fin=emit_fin),
        grid=(NCH,),
        in_specs=in_specs,
        out_specs=out_specs,
        out_shape=out_shape,
        scratch_shapes=scratch,
        compiler_params=_cparams(1),
        name=("s5_scan" if emit_y else "s5_states") + ("_bwd" if rev else "_fwd") + str(trunk),
    )(*args)


def _carry_kernel(frf_ref, fif_ref, frb_ref, fib_ref, sr_ref, si_ref, lr_ref, li_ref, dt_ref,
                  or_ref, oi_ref, *, j):
    for d, (fr_ref, fi_ref) in enumerate(((frf_ref, fif_ref), (frb_ref, fib_ref))):
        jd = 2 * j + d
        ar, ai = _abar(lr_ref[jd], li_ref[jd], dt_ref[jd])
        for _ in range(8):
            ar, ai = ar * ar - ai * ai, 2.0 * ar * ai
        for b in range(2):
            order = list(range(SEG)) if d == 0 else list(range(SEG - 1, -1, -1))
            r = b * SEG + order[0]
            or_ref[d, r:r + 1, :] = sr_ref[b, jd]
            oi_ref[d, r:r + 1, :] = si_ref[b, jd]
            tr = fr_ref[r:r + 1, :]
            ti = fi_ref[r:r + 1, :]
            for s in order[1:]:
                r = b * SEG + s
                or_ref[d, r:r + 1, :] = tr
                oi_ref[d, r:r + 1, :] = ti
                tr, ti = (fr_ref[r:r + 1, :] + ar * tr - ai * ti,
                          fi_ref[r:r + 1, :] + ar * ti + ai * tr)


def _carry(frf, fif, frb, fib, st_re4, st_im4, lam_re, lam_im, dts, *, j):
    shp = jax.ShapeDtypeStruct((2, NPC, NST), F32)
    return pl.pallas_call(functools.partial(_carry_kernel, j=j), out_shape=[shp, shp], name="s5_carry")(
        frf, fif, frb, fib, st_re4, st_im4, lam_re, lam_im, dts)


GT = 2 * TILE


def _glu_kernel(xa_ref, xb_ref, yfa_ref, yba_ref, yfb_ref, ybb_ref, mod_ref, g_ref, d_ref, w_ref, b_ref,
                o_ref, wbf, ge_s, z_s):
    i = pl.program_id(0)

    @pl.when(i == 0)
    def _():
        wbf[...] = w_ref[...].astype(BF16)

    mod = mod_ref[...]
    first_prompt = i < NPC // 2

    nq = 4
    cw = D // nq
    rw = TILE // nq

    def x_rows(r0, r1, c0=0, c1=D):
        return jnp.where(first_prompt, xa_ref[r0:r1, c0:c1], xb_ref[r0:r1, c0:c1])

    def pre(r0, r1):
        h = _normmod(x_rows(r0, r1), g_ref[...], mod[1:2], mod[0:1])
        y_ssm = jnp.where(first_prompt, yfa_ref[r0:r1, :] + yba_ref[r0:r1, :], yfb_ref[r0:r1, :] + ybb_ref[r0:r1, :])
        y = y_ssm + d_ref[...] * h
        k1 = -2.0 * math.sqrt(2.0 / math.pi)
        ge = y / (1.0 + jnp.exp(y * (k1 + (k1 * 0.044715) * (y * y))))
        ge_s[r0:r1, :] = ge.astype(BF16)

    def gate(q, j):
        rows = slice(q * TILE, (q + 1) * TILE)
        for c0 in (j * cw, D + j * cw):
            z_s[rows, c0:c0 + cw] = (jnp.dot(ge_s[rows, :], wbf[:, c0:c0 + cw], preferred_element_type=F32)
                                     + b_ref[:, c0:c0 + cw])

    def post(q, j):
        rows = slice(q * TILE, (q + 1) * TILE)
        c0, c1 = j * cw, (j + 1) * cw
        out = z_s[rows, c0:c1] * _sigmoid(z_s[rows, D + c0:D + c1])
        o_ref[rows, c0:c1] = x_rows(q * TILE, (q + 1) * TILE, c0, c1) + mod[2:3, c0:c1] * out

    for j in range(nq):
        pre(j * rw, (j + 1) * rw)
    for j in range(nq):
        gate(0, j)
        pre(TILE + j * rw, TILE + (j + 1) * rw)
    for j in range(nq):
        gate(1, j)
        post(0, j)
    for j in range(nq):
        post(1, j)


def _glu(xa, xb, off_b, ys, mods, layer, g, dvec, w, b):
    nblk = NCOL // 2
    half = NPC // 2
    pa = lambda i: (jnp.minimum(i, half - 1), 0)
    pb = lambda i: (jnp.maximum(i - half, 0), 0)
    return pl.pallas_call(
        _glu_kernel,
        grid=(nblk,),
        in_specs=[pl.BlockSpec((GT, D), pa),
                  pl.BlockSpec((GT, D), lambda i: (off_b // 2 + jnp.maximum(i - half, 0), 0)),
                  pl.BlockSpec((GT, D), pa),
                  pl.BlockSpec((GT, D), pa),
                  pl.BlockSpec((GT, D), pb),
                  pl.BlockSpec((GT, D), pb),
                  pl.BlockSpec((None, None, 6, D), lambda i: (layer, _cond_of_col(2 * i), 0, 0)),
                  pl.BlockSpec((1, D), lambda i: (0, 0)),
                  pl.BlockSpec((1, D), lambda i: (0, 0)),
                  pl.BlockSpec((None, D, 2 * D), lambda i: (layer // 2, 0, 0)),
                  pl.BlockSpec((1, 2 * D), lambda i: (0, 0))],
        out_specs=pl.BlockSpec((GT, D), lambda i: (i, 0)),
        out_shape=jax.ShapeDtypeStruct((NCOL * TILE, D), F32),
        scratch_shapes=[pltpu.VMEM((D, 2 * D), BF16), pltpu.VMEM((GT, D), BF16), pltpu.VMEM((GT, 2 * D), F32)],
        compiler_params=_cparams(1),
        name="s5_glu",
    )(xa, xb, *ys, mods, g, dvec, w, b)


def _fourier_kernel(x_ref, mod_ref, g_ref, cc_ref, cs_ref, cl_ref, sl_ref, w_ref, b_ref,
                    o_ref, *scratch, scale, nseg):
    mod = mod_ref[...]

    def channel_dft():
        h = _normmod(x_ref[...], g_ref[...], mod[1:2], mod[0:1]).astype(BF16)
        xc, xs = [], []
        for q in range(FG):
            hq = h[:, q * FGC:(q + 1) * FGC]
            xc.append(jnp.dot(hq, cc_ref[...], preferred_element_type=F32).astype(BF16))
            xs.append(jnp.dot(hq, cs_ref[...], preferred_element_type=F32).astype(BF16))
        return jnp.concatenate(xc, axis=1), jnp.concatenate(xs, axis=1)

    def position_dft(xc, xs):
        f = (jnp.dot(cl_ref[...], xc, preferred_element_type=F32)
             - jnp.dot(sl_ref[...], xs, preferred_element_type=F32)) * scale
        o = jnp.dot(f.astype(BF16), w_ref[...], preferred_element_type=F32) + b_ref[...]
        o_ref[...] = x_ref[...] + mod[2:3] * o

    if nseg == 1:
        position_dft(*channel_dft())
        return

    xc_s, xs_s = scratch
    ph = pl.program_id(1)
    s = pl.program_id(2)

    @pl.when(ph == 0)
    def _():
        r0 = pl.multiple_of(s * TILE, TILE)
        xc, xs = channel_dft()
        xc_s[pl.ds(r0, TILE), :] = xc
        xs_s[pl.ds(r0, TILE), :] = xs

    @pl.when(ph == 1)
    def _():
        position_dft(xc_s[...], xs_s[...])


def _fourier(x, mods, layer, g, cc, cs, cl, sl, w, b, *, nseq, nseg, col_off, cond_off, cond_stride):
    ln = nseg * TILE
    if nseg == 1:
        grid = (nseq,)
        idx = lambda f: (lambda q: f(q, 1, 0))
        scratch = []
    else:
        grid = (nseq, 2, nseg)
        idx = lambda f: f
        scratch = [pltpu.VMEM((ln, D), BF16), pltpu.VMEM((ln, D), BF16)]
    return pl.pallas_call(
        functools.partial(_fourier_kernel, scale=1.0 / math.sqrt(ln * FGC), nseg=nseg),
        grid=grid,
        in_specs=[pl.BlockSpec((TILE, D), idx(lambda q, ph, s: (col_off + q * nseg + s, 0))),
                  pl.BlockSpec((None, None, 6, D), idx(lambda q, ph, s: (layer, cond_off + q * cond_stride, 0, 0))),
                  pl.BlockSpec((1, D), idx(lambda q, ph, s: (0, 0))),
                  pl.BlockSpec((FGC, FGC), idx(lambda q, ph, s: (0, 0))),
                  pl.BlockSpec((FGC, FGC), idx(lambda q, ph, s: (0, 0))),
                  pl.BlockSpec((TILE, ln), idx(lambda q, ph, s: (s * ph, 0))),
                  pl.BlockSpec((TILE, ln), idx(lambda q, ph, s: (s * ph, 0))),
                  pl.BlockSpec((None, D, D), idx(lambda q, ph, s: (layer // 2, 0, 0))),
                  pl.BlockSpec((1, D), idx(lambda q, ph, s: (0, 0)))],
        out_specs=pl.BlockSpec((TILE, D), idx(lambda q, ph, s: (col_off + q * nseg + s * ph, 0))),
        out_shape=jax.ShapeDtypeStruct((NCOL * TILE, D), F32),
        scratch_shapes=scratch,
        input_output_aliases={0: 0},
        compiler_params=_cparams(len(grid)),
        name="fourier%d" % nseg,
    )(x, mods, g, cc, cs, cl, sl, w, b)


@functools.lru_cache(maxsize=None)
def _dft_mats(n):
    k = np.arange(n, dtype=np.int64)
    ang = ((k[:, None] * k[None, :]) % n).astype(np.float64) * (2.0 * math.pi / n)
    return np.cos(ang).astype(np.float32), np.sin(ang).astype(np.float32)


def _ffn_kernel(x_ref, mod_ref, g_ref, wup_ref, cw_ref, cb_ref, wdn_ref, gf_ref, *o_refs, final):
    i = pl.program_id(0)
    x = x_ref[...]
    mod = mod_ref[...]
    h = _normmod(x, g_ref[...], mod[4:5], mod[3:4]).astype(BF16)
    sub = lax.broadcasted_iota(jnp.int32, (8, FC), 0)
    one = jnp.ones((8, FC), F32)
    first0 = (sub != 0).astype(F32)
    last0 = (sub != 7).astype(F32)
    inner_first = jnp.where(i < NPC, one, first0)
    inner_last = jnp.where(i < NPC, one, last0)
    prev_slabs = [(0, first0)] + [(r, inner_first) for r in range(GRID_W, TILE, GRID_W)]
    next_slabs = [(r - 8, inner_last) for r in range(GRID_W, TILE, GRID_W)] + [(TILE - 8, last0)]

    def mask_rows(a, slabs):
        parts, last = [], 0
        for r0, m in slabs:
            if r0 > last:
                parts.append(a[last:r0])
            parts.append(a[r0:r0 + 8] * m)
            last = r0 + 8
        if last < TILE:
            parts.append(a[last:])
        return jnp.concatenate(parts, axis=0)

    def conv(up, off):
        w = cw_ref[:, off:off + FC]
        prev = mask_rows(pltpu.roll(up, 1, 0), prev_slabs)
        nxt = mask_rows(pltpu.roll(up, TILE - 1, 0), next_slabs)
        return w[0:1] * prev + w[1:2] * up + w[2:3] * nxt + cb_ref[:, off:off + FC]

    def up_chunk(c):
        og = c * FC
        ov = DFF + c * FC
        return (jnp.dot(h, wup_ref[:, og:og + FC], preferred_element_type=F32),
                jnp.dot(h, wup_ref[:, ov:ov + FC], preferred_element_type=F32))

    def down_chunk(act, c):
        return jnp.dot(act, wdn_ref[c * FC:(c + 1) * FC, :], preferred_element_type=F32)

    nchunk = DFF // FC
    acc = jnp.zeros((TILE, D), F32)
    nxt_up = up_chunk(0)
    act = None
    for c in range(nchunk):
        cur_up = nxt_up
        if c + 1 < nchunk:
            nxt_up = up_chunk(c + 1)
        if act is not None:
            acc = acc + down_chunk(act, c - 1)
        gate = conv(cur_up[0], c * FC)
        val = conv(cur_up[1], DFF + c * FC)
        hg = 0.5 * gate
        act = ((hg + hg * jnp.tanh(hg)) * val).astype(BF16)
    acc = acc + down_chunk(act, nchunk - 1)
    y = x + mod[5:6] * acc
    if not final:
        o_refs[0][...] = y
    else:
        ms = jnp.mean(y * y, axis=-1, keepdims=True)
        y = y * lax.rsqrt(ms + EPS) * gf_ref[...]

        @pl.when(i < NPC)
        def _():
            o_refs[0][...] = y

        @pl.when(i >= NPC)
        def _():
            o_refs[1][...] = y


def _ffn(x, mods, layer, g, wup, cw, cb, wdn, gf, *, final):
    if final:
        out_specs = [pl.BlockSpec((TILE, D), lambda i: (jnp.minimum(i, NPC - 1), 0)),
                     pl.BlockSpec((TILE, D), lambda i: (jnp.maximum(i - NPC, 0), 0))]
        out_shape = [jax.ShapeDtypeStruct((NPC * TILE, D), F32)] * 2
    else:
        out_specs = pl.BlockSpec((TILE, D), lambda i: (i, 0))
        out_shape = jax.ShapeDtypeStruct((NCOL * TILE, D), F32)
    return pl.pallas_call(
        functools.partial(_ffn_kernel, final=final),
        grid=(NCOL,),
        in_specs=[pl.BlockSpec((TILE, D), lambda i: (i, 0)),
                  pl.BlockSpec((None, None, 6, D), lambda i: (layer, _cond_of_col(i), 0, 0)),
                  pl.BlockSpec((1, D), lambda i: (0, 0)),
                  pl.BlockSpec((None, D, 2 * DFF), lambda i: (layer, 0, 0)),
                  pl.BlockSpec((None, 3, 2 * DFF), lambda i: (layer, 0, 0)),
                  pl.BlockSpec((None, 1, 2 * DFF), lambda i: (layer, 0, 0)),
                  pl.BlockSpec((None, DFF, D), lambda i: (layer, 0, 0)),
                  pl.BlockSpec((1, D), lambda i: (0, 0))],
        out_specs=out_specs,
        out_shape=out_shape,
        compiler_params=_cparams(1),
        name="ffn",
    )(x, mods, g, wup, cw, cb, wdn, gf)


def kernel(x_prompt, x_sample, state_ssm_re, state_ssm_im, c, c_ctx, w_ada, b_ada, g_mix, g_ffn,
           ssm_lam_re, ssm_lam_im, ssm_log_dt, ssm_b_re, ssm_b_im, ssm_c_re, ssm_c_im, ssm_d,
           w_glu, b_glu, w_fourier, b_fourier, w_up, conv_w, conv_b, w_down, g_final):
    nb = x_prompt.shape[0]
    xp = x_prompt.reshape(NPC * TILE, D)
    xs = x_sample.reshape(NPC * TILE, D)
    x = None

    cond8 = jnp.concatenate([c_ctx[None, :], c, jnp.zeros((5, D), F32)], axis=0)
    mods = _ada(cond8, w_ada, b_ada).reshape(DEPTH, 8, 6, D)
    col_cond = jnp.asarray([0] * NPC + [1] * SEG + [2] * SEG, jnp.int32)
    mods_t = mods.transpose(0, 2, 1, 3)[:, :, col_cond]

    wup_bf = _cast_bf16(w_up.reshape(DEPTH * D, 2 * DFF), 256).reshape(DEPTH, D, 2 * DFF)
    wdn_bf = _cast_bf16(w_down.reshape(DEPTH * DFF, D), DFF).reshape(DEPTH, DFF, D)
    wf_bf = _cast_bf16(w_fourier.reshape(2 * D, D), D).reshape(2, D, D)

    cc, cs = (jnp.asarray(m).astype(BF16) for m in _dft_mats(FGC))
    cl1, sl1 = cc, cs
    cl8, sl8 = (jnp.asarray(m).astype(BF16) for m in _dft_mats(SEG * TILE))

    njd = ssm_lam_re.shape[0] * 2
    lam_re = ssm_lam_re.reshape(njd, 1, NST)
    lam_im = ssm_lam_im.reshape(njd, 1, NST)
    dts = jnp.repeat(ssm_log_dt.reshape(njd, GS), PS, axis=-1).reshape(njd, 1, NST)
    bt_re = ssm_b_re.transpose(0, 1, 4, 2, 3).reshape(njd, GC, NST)
    bt_im = ssm_b_im.transpose(0, 1, 4, 2, 3).reshape(njd, GC, NST)
    cw_re = jnp.tile(ssm_c_re.transpose(0, 1, 2, 4, 3).reshape(njd, NST, GC), (1, 1, 8))
    cw_im = jnp.tile(ssm_c_im.transpose(0, 1, 2, 4, 3).reshape(njd, NST, GC), (1, 1, 8))
    st_re4 = state_ssm_re.reshape(2, njd, 1, NST)
    st_im4 = state_ssm_im.reshape(2, njd, 1, NST)

    edges = []
    for i in range(DEPTH):
        j = i // 2
        if i % 2 == 0:
            g = g_mix[i][None, :]
            xa, xb, off_b = (xp, xs, 0) if x is None else (x, x, NPC)
            xa3 = xa.reshape(-1, TILE, D)
            xb3 = xb.reshape(-1, TILE, D)
            par = (mods_t, i, g, lam_re, lam_im, dts, bt_re, bt_im, cw_re, cw_im)
            fin = [_s5_scan(xb3, off_b, *par, jd=2 * j + d, rev=d == 1, trunk=1, init="state",
                            h0=(st_re4, st_im4), emit_y=False, emit_edge=False, emit_fin=True)
                   for d in range(2)]
            carry = _carry(*fin[0], *fin[1], st_re4, st_im4, lam_re, lam_im, dts, j=j)
            ya, yb = [], []
            for d in range(2):
                y, er, ei = _s5_scan(xa3, 0, *par, jd=2 * j + d, rev=d == 1, trunk=0, init="zero",
                                     emit_y=True, emit_edge=True, emit_fin=False)
                ya.append(y.reshape(NPC * TILE, D))
                edges += [er, ei]
                (y,) = _s5_scan(xb3, off_b, *par, jd=2 * j + d, rev=d == 1, trunk=1, init="carry",
                                h0=carry, emit_y=True, emit_edge=False, emit_fin=False)
                yb.append(y.reshape(NPC * TILE, D))
            x = _glu(xa, xb, off_b, ya + yb, mods, i, g, ssm_d[j][None, :], w_glu, b_glu[j][None, :])
        else:
            g = g_mix[i][None, :]
            bf = b_fourier[j][None, :]
            x = _fourier(x, mods, i, g, cc, cs, cl1, sl1, wf_bf, bf,
                         nseq=NPC, nseg=1, col_off=0, cond_off=0, cond_stride=0)
            x = _fourier(x, mods, i, g, cc, cs, cl8, sl8, wf_bf, bf,
                         nseq=2, nseg=SEG, col_off=NPC, cond_off=1, cond_stride=1)
        x = _ffn(x, mods, i, g_ffn[i][None, :], wup_bf, conv_w, conv_b.reshape(DEPTH, 1, 2 * DFF),
                 wdn_bf, g_final[None, :], final=(i == DEPTH - 1))

    y_prompt = x[0].reshape(NPC, TILE, D)
    y_sample = x[1].reshape(2, SEG * TILE, D)
    ed = jnp.stack(edges, axis=0).reshape(DEPTH // 2, 2, 2, NPC, NST)[:, :, :, :nb]
    new_re = ed[:, :, 0].transpose(2, 0, 1, 3).reshape(nb, DEPTH // 2, 2, GS, PS)
    new_im = ed[:, :, 1].transpose(2, 0, 1, 3).reshape(nb, DEPTH // 2, 2, GS, PS)
    return (y_prompt, y_sample, new_re, new_im)
```

```python
import functools
import math

import jax
import jax.numpy as jnp
import numpy as np
from jax import lax
from jax.experimental import pallas as pl
from jax.experimental.pallas import tpu as pltpu

F32 = jnp.float32
BF16 = jnp.bfloat16

D = 1024
TILE = 256
NCOL = 32
NPC = 16
SEG = 8
DEPTH = 4
GS = 64
GC = 16
PS = 64
NST = GS * PS
FG = 4
FGC = 256
DFF = 2816
EPS = 1e-6
GRID_W = 64

TT = 32
NCH = TILE // TT
SW = 512
NSLAB = NST // SW
FC = 256

VMEM_LIMIT = 56 * 1024 * 1024


def _cparams(n_axes):
    return pltpu.CompilerParams(dimension_semantics=("arbitrary",) * n_axes,
                                vmem_limit_bytes=VMEM_LIMIT)


def _normmod(x, g, sc, sh):
    ms = jnp.mean(x * x, axis=-1, keepdims=True)
    return x * lax.rsqrt(ms + EPS) * (g * (1.0 + sc)) + sh


def _sigmoid(x):
    return 1.0 / (1.0 + jnp.exp(-x))


def _cond_of_col(i):
    return jnp.where(i < NPC, 0, 1 + (i - NPC) // SEG)


def _ada_kernel(c_ref, w_ref, b_ref, o_ref):
    c = c_ref[...]
    s = (c * _sigmoid(c)).astype(BF16)
    o_ref[...] = jnp.dot(s, w_ref[...].astype(BF16), preferred_element_type=F32) + b_ref[...]


def _ada(cond8, w_ada, b_ada):
    tn = 1536
    return pl.pallas_call(
        _ada_kernel,
        grid=(DEPTH, 6 * D // tn),
        in_specs=[pl.BlockSpec((8, D), lambda l, n: (0, 0)),
                  pl.BlockSpec((None, D, tn), lambda l, n: (l, 0, n)),
                  pl.BlockSpec((None, 1, tn), lambda l, n: (l, 0, n))],
        out_specs=pl.BlockSpec((None, 8, tn), lambda l, n: (l, 0, n)),
        out_shape=jax.ShapeDtypeStruct((DEPTH, 8, 6 * D), F32),
        compiler_params=_cparams(2),
        name="ada",
    )(cond8, w_ada, b_ada.reshape(DEPTH, 1, 6 * D))


def _cast_kernel(w_ref, o_ref):
    o_ref[...] = w_ref[...].astype(BF16)


def _cast_bf16(w2d, block_rows):
    r, c = w2d.shape
    return pl.pallas_call(
        _cast_kernel,
        grid=(r // block_rows,),
        in_specs=[pl.BlockSpec((block_rows, c), lambda i: (i, 0))],
        out_specs=pl.BlockSpec((block_rows, c), lambda i: (i, 0)),
        out_shape=jax.ShapeDtypeStruct((r, c), BF16),
        compiler_params=_cparams(1),
        name="cast",
    )(w2d)


def _abar(lr, li, logdt):
    dt = jnp.exp(logdt)
    mag = jnp.exp(lr * dt)
    return mag * jnp.cos(li * dt), mag * jnp.sin(li * dt)


def _s5_kernel(*refs, rev, col_base, init, emit_y, emit_edge, emit_fin):
    x_hbm, sc_ref, sh_ref, g_ref, lr_ref, li_ref, dt_ref, btr_ref, bti_ref, cwr_ref, cwi_ref = refs[:11]
    pos = 11
    if init != "zero":
        h0r_ref, h0i_ref = refs[pos:pos + 2]
        pos += 2
    if emit_y:
        y_hbm = refs[pos]
        pos += 1
    if emit_edge:
        er_ref, ei_ref = refs[pos:pos + 2]
        pos += 2
    if emit_fin:
        fr_ref, fi_ref = refs[pos:pos + 2]
        pos += 2
    scratch = refs[pos:]
    bu_re = scratch[0:NSLAB]
    bu_im = scratch[NSLAB:2 * NSLAB]
    bt_s, ct_s, ar_s, ai_s, hs_re, hs_im, xbuf, sem_in = scratch[2 * NSLAB:2 * NSLAB + 8]
    pos = 2 * NSLAB + 8
    if emit_y:
        ybuf, sem_out = scratch[pos:pos + 2]
        hb = scratch[pos + 2:pos + 2 + NSLAB]
        pos += 2 + NSLAB
    if emit_edge:
        ed_re, ed_im = scratch[pos:pos + 2]

    i = pl.program_id(0)
    slot = lax.rem(i, 2)

    def t_start(i_):
        return pl.multiple_of((NCH - 1 - i_ if rev else i_) * TT, TT)

    def x_copies(i_, slot_):
        return [pltpu.make_async_copy(x_hbm.at[col_base + c, pl.ds(t_start(i_), TT), :],
                                      xbuf.at[slot_, :, c, :], sem_in.at[slot_]) for c in range(NPC)]

    def y_copies(i_, slot_):
        return [pltpu.make_async_copy(ybuf.at[slot_, :, c, :],
                                      y_hbm.at[c, pl.ds(t_start(i_), TT), :],
                                      sem_out.at[slot_]) for c in range(NPC)]

    @pl.when(i == 0)
    def _first_fetch():
        for cp in x_copies(i, slot):
            cp.start()

    if emit_y:
        @pl.when(i >= 2)
        def _ybuf_free():
            for cp in y_copies(i, slot):
                cp.wait()

    @pl.when(i == 0)
    def _prep():
        lr = lr_ref[...]
        li = li_ref[...]
        ar, ai = _abar(lr, li, dt_ref[...])
        ar_s[...] = jnp.broadcast_to(ar, (8, NST))
        ai_s[...] = jnp.broadcast_to(ai, (8, NST))
        xr = ar - 1.0
        den = lr * lr + li * li
        fr = (xr * lr + ai * li) / den
        fi = (ai * lr - xr * li) / den
        btr = btr_ref[...]
        bti = bti_ref[...]
        bbr = fr * btr - fi * bti
        bbi = fr * bti + fi * btr
        r = lax.broadcasted_iota(jnp.int32, (128, SW), 0)
        c = lax.broadcasted_iota(jnp.int32, (128, SW), 1)
        bmask = ((r >> 4) == (c >> 6)).astype(F32)
        for m in range(NSLAB):
            sl = slice(m * SW, (m + 1) * SW)
            tr = jnp.concatenate([bbr[:, sl]] * 8, axis=0) * bmask
            ti = jnp.concatenate([bbi[:, sl]] * 8, axis=0) * bmask
            bt_s[m, :, 0:SW] = tr.astype(BF16)
            bt_s[m, :, SW:2 * SW] = ti.astype(BF16)
        r2 = lax.broadcasted_iota(jnp.int32, (SW, 128), 0)
        c2 = lax.broadcasted_iota(jnp.int32, (SW, 128), 1)
        cmask = ((r2 >> 6) == (c2 >> 4)).astype(F32)
        for q in range(NSLAB):
            sl = slice(q * SW, (q + 1) * SW)
            ct_s[q, 0:SW, :] = (cwr_ref[sl, :] * cmask).astype(BF16)
            ct_s[q, SW:2 * SW, :] = (-(cwi_ref[sl, :] * cmask)).astype(BF16)
        if init == "carry":
            hs_re[...] = h0r_ref[...]
            hs_im[...] = h0i_ref[...]
        else:
            hs_re[...] = jnp.zeros((NPC, NST), F32)
            hs_im[...] = jnp.zeros((NPC, NST), F32)
            if init == "state":
                seg0 = SEG - 1 if rev else 0
                for b in range(2):
                    row = b * SEG + seg0
                    hs_re[row:row + 1, :] = h0r_ref[b]
                    hs_im[row:row + 1, :] = h0i_ref[b]

    for cp in x_copies(i, slot):
        cp.wait()
    for cp in x_copies(jnp.minimum(i + 1, NCH - 1), 1 - slot):
        cp.start()

    x3 = xbuf[slot]
    ms = jnp.mean(x3 * x3, axis=-1, keepdims=True)
    gain = g_ref[...] * (1.0 + sc_ref[...])
    u3 = x3 * lax.rsqrt(ms + EPS) * gain[None] + sh_ref[...][None]
    u = u3.reshape(TT * NPC, D).astype(BF16)

    def flip_time(a):
        if not rev:
            return a
        return jnp.concatenate([a[(TT - 1 - t) * NPC:(TT - t) * NPC] for t in range(TT)], axis=0)

    u = flip_time(u)

    def expand(m):
        res = jnp.dot(u[:, m * 128:(m + 1) * 128], bt_s[m], preferred_element_type=F32)
        bu_re[m][...] = res[:, 0:SW]
        bu_im[m][...] = res[:, SW:2 * SW]

    def recur(s):
        lanes = slice(s * SW, (s + 1) * SW)
        ar = ar_s[:, lanes]
        ai = ai_s[:, lanes]
        h = [hs_re[0:8, lanes], hs_im[0:8, lanes], hs_re[8:16, lanes], hs_im[8:16, lanes]]
        for t in range(TT):
            for q in range(2):
                rows = slice(t * NPC + 8 * q, t * NPC + 8 * q + 8)
                hr, hi = h[2 * q], h[2 * q + 1]
                nr = ar * hr - ai * hi + bu_re[s][rows, :]
                ni = ar * hi + ai * hr + bu_im[s][rows, :]
                h[2 * q], h[2 * q + 1] = nr, ni
            if emit_y:
                trow = slice(t * NPC, (t + 1) * NPC)
                hb[s][trow, 0:SW] = jnp.concatenate([h[0], h[2]], axis=0).astype(BF16)
                hb[s][trow, SW:2 * SW] = jnp.concatenate([h[1], h[3]], axis=0).astype(BF16)
            if emit_edge and t == 0:
                ed_re[0:8, lanes], ed_im[0:8, lanes] = h[0], h[1]
                ed_re[8:16, lanes], ed_im[8:16, lanes] = h[2], h[3]
        hs_re[0:8, lanes] = h[0]
        hs_im[0:8, lanes] = h[1]
        hs_re[8:16, lanes] = h[2]
        hs_im[8:16, lanes] = h[3]

    def contract(q):
        yq = jnp.dot(hb[q][...], ct_s[q], preferred_element_type=F32)
        ybuf[slot, :, :, q * 128:(q + 1) * 128] = flip_time(yq).reshape(TT, NPC, 128)

    expand(0)
    for s in range(NSLAB):
        if s + 1 < NSLAB:
            expand(s + 1)
        if emit_y and s >= 1:
            contract(s - 1)
        recur(s)
    if emit_y:
        contract(NSLAB - 1)
        for cp in y_copies(i, slot):
            cp.start()

    if emit_edge:
        @pl.when(i == 0)
        def _edge():
            er_ref[...] = ed_re[...]
            ei_ref[...] = ed_im[...]

    @pl.when(i == NCH - 1)
    def _last():
        if emit_fin:
            fr_ref[...] = hs_re[...]
            fi_ref[...] = hs_im[...]
        for cp in x_copies(i, 1 - slot):
            cp.wait()
        if emit_y:
            for cp in y_copies(i, slot) + y_copies(i, 1 - slot):
                cp.wait()


def _s5_scan(x3, col_base, mods_t, layer, g, lam_re, lam_im, dts, bt_re, bt_im, cw_re, cw_im, *,
             jd, rev, trunk, init, h0=None, emit_y, emit_edge, emit_fin):
    par = lambda i: (jd, 0, 0)
    in_specs = [
        pl.BlockSpec(memory_space=pl.ANY),
        pl.BlockSpec((None, None, NPC, D), lambda i: (layer, 1, trunk, 0)),
        pl.BlockSpec((None, None, NPC, D), lambda i: (layer, 0, trunk, 0)),
        pl.BlockSpec((1, D), lambda i: (0, 0)),
        pl.BlockSpec((None, 1, NST), par),
        pl.BlockSpec((None, 1, NST), par),
        pl.BlockSpec((None, 1, NST), par),
        pl.BlockSpec((None, GC, NST), par),
        pl.BlockSpec((None, GC, NST), par),
        pl.BlockSpec((None, NST, 128), par),
        pl.BlockSpec((None, NST, 128), par),
    ]
    args = [x3, mods_t, mods_t, g, lam_re, lam_im, dts, bt_re, bt_im, cw_re, cw_im]
    if init == "state":
        in_specs += [pl.BlockSpec((2, None, 1, NST), lambda i: (0, jd, 0, 0))] * 2
        args += list(h0)
    elif init == "carry":
        in_specs += [pl.BlockSpec((None, NPC, NST), lambda i: (1 if rev else 0, 0, 0))] * 2
        args += list(h0)
    out_specs, out_shape = [], []
    slab = pltpu.VMEM((TT * NPC, SW), F32)
    scratch = [slab] * (2 * NSLAB) + [
        pltpu.VMEM((NSLAB, 128, 2 * SW), BF16), pltpu.VMEM((NSLAB, 2 * SW, 128), BF16),
        pltpu.VMEM((8, NST), F32), pltpu.VMEM((8, NST), F32),
        pltpu.VMEM((NPC, NST), F32), pltpu.VMEM((NPC, NST), F32),
        pltpu.VMEM((2, TT, NPC, D), F32), pltpu.SemaphoreType.DMA((2,))]
    if emit_y:
        out_specs.append(pl.BlockSpec(memory_space=pl.ANY))
        out_shape.append(jax.ShapeDtypeStruct((NPC, TILE, D), F32))
        scratch += [pltpu.VMEM((2, TT, NPC, D), F32), pltpu.SemaphoreType.DMA((2,))]
        scratch += [pltpu.VMEM((TT * NPC, 2 * SW), BF16)] * NSLAB
    if emit_edge:
        scratch += [pltpu.VMEM((NPC, NST), F32)] * 2
    n_state_outs = 2 * (int(emit_edge) + int(emit_fin))
    out_specs += [pl.BlockSpec((NPC, NST), lambda i: (0, 0))] * n_state_outs
    out_shape += [jax.ShapeDtypeStruct((NPC, NST), F32)] * n_state_outs
    return pl.pallas_call(
        functools.partial(_s5_kernel, rev=rev, col_base=col_base, init=init,
                          emit_y=emit_y, emit_edge=emit_edge, emit_fin=emit_fin),
        grid=(NCH,),
        in_specs=in_specs,
        out_specs=out_specs,
        out_shape=out_shape,
        scratch_shapes=scratch,
        compiler_params=_cparams(1),
        name=("s5_scan" if emit_y else "s5_states") + ("_bwd" if rev else "_fwd") + str(trunk),
    )(*args)


def _carry_kernel(frf_ref, fif_ref, frb_ref, fib_ref, sr_ref, si_ref, lr_ref, li_ref, dt_ref,
                  or_ref, oi_ref, *, j):
    for d, (fr_ref, fi_ref) in enumerate(((frf_ref, fif_ref), (frb_ref, fib_ref))):
        jd = 2 * j + d
        ar, ai = _abar(lr_ref[jd], li_ref[jd], dt_ref[jd])
        for _ in range(8):
            ar, ai = ar * ar - ai * ai, 2.0 * ar * ai
        for b in range(2):
            order = list(range(SEG)) if d == 0 else list(range(SEG - 1, -1, -1))
            r = b * SEG + order[0]
            or_ref[d, r:r + 1, :] = sr_ref[b, jd]
            oi_ref[d, r:r + 1, :] = si_ref[b, jd]
            tr = fr_ref[r:r + 1, :]
            ti = fi_ref[r:r + 1, :]
            for s in order[1:]:
                r = b * SEG + s
                or_ref[d, r:r + 1, :] = tr
                oi_ref[d, r:r + 1, :] = ti
                tr, ti = (fr_ref[r:r + 1, :] + ar * tr - ai * ti,
                          fi_ref[r:r + 1, :] + ar * ti + ai * tr)


def _carry(frf, fif, frb, fib, st_re4, st_im4, lam_re, lam_im, dts, *, j):
    shp = jax.ShapeDtypeStruct((2, NPC, NST), F32)
    return pl.pallas_call(functools.partial(_carry_kernel, j=j), out_shape=[shp, shp], name="s5_carry")(
        frf, fif, frb, fib, st_re4, st_im4, lam_re, lam_im, dts)


GT = 2 * TILE


def _glu_kernel(xa_ref, xb_ref, yfa_ref, yba_ref, yfb_ref, ybb_ref, mod_ref, g_ref, d_ref, w_ref, b_ref,
                o_ref, wbf, ge_s, z_s):
    i = pl.program_id(0)

    @pl.when(i == 0)
    def _():
        wbf[...] = w_ref[...].astype(BF16)

    mod = mod_ref[...]
    first_prompt = i < NPC // 2

    nq = 4
    cw = D // nq
    rw = TILE // nq

    def x_rows(r0, r1, c0=0, c1=D):
        return jnp.where(first_prompt, xa_ref[r0:r1, c0:c1], xb_ref[r0:r1, c0:c1])

    def pre(r0, r1):
        h = _normmod(x_rows(r0, r1), g_ref[...], mod[1:2], mod[0:1])
        y_ssm = jnp.where(first_prompt, yfa_ref[r0:r1, :] + yba_ref[r0:r1, :], yfb_ref[r0:r1, :] + ybb_ref[r0:r1, :])
        y = y_ssm + d_ref[...] * h
        k1 = -2.0 * math.sqrt(2.0 / math.pi)
        ge = y / (1.0 + jnp.exp(y * (k1 + (k1 * 0.044715) * (y * y))))
        ge_s[r0:r1, :] = ge.astype(BF16)

    def gate(q, j):
        rows = slice(q * TILE, (q + 1) * TILE)
        for c0 in (j * cw, D + j * cw):
            z_s[rows, c0:c0 + cw] = (jnp.dot(ge_s[rows, :], wbf[:, c0:c0 + cw], preferred_element_type=F32)
                                     + b_ref[:, c0:c0 + cw])

    def post(q, j):
        rows = slice(q * TILE, (q + 1) * TILE)
        c0, c1 = j * cw, (j + 1) * cw
        out = z_s[rows, c0:c1] * _sigmoid(z_s[rows, D + c0:D + c1])
        o_ref[rows, c0:c1] = x_rows(q * TILE, (q + 1) * TILE, c0, c1) + mod[2:3, c0:c1] * out

    for j in range(nq):
        pre(j * rw, (j + 1) * rw)
    for j in range(nq):
        gate(0, j)
        pre(TILE + j * rw, TILE + (j + 1) * rw)
    for j in range(nq):
        gate(1, j)
        post(0, j)
    for j in range(nq):
        post(1, j)


def _glu(xa, xb, off_b, ys, mods, layer, g, dvec, w, b):
    nblk = NCOL // 2
    half = NPC // 2
    pa = lambda i: (jnp.minimum(i, half - 1), 0)
    pb = lambda i: (jnp.maximum(i - half, 0), 0)
    return pl.pallas_call(
        _glu_kernel,
        grid=(nblk,),
        in_specs=[pl.BlockSpec((GT, D), pa),
                  pl.BlockSpec((GT, D), lambda i: (off_b // 2 + jnp.maximum(i - half, 0), 0)),
                  pl.BlockSpec((GT, D), pa),
                  pl.BlockSpec((GT, D), pa),
                  pl.BlockSpec((GT, D), pb),
                  pl.BlockSpec((GT, D), pb),
                  pl.BlockSpec((None, None, 6, D), lambda i: (layer, _cond_of_col(2 * i), 0, 0)),
                  pl.BlockSpec((1, D), lambda i: (0, 0)),
                  pl.BlockSpec((1, D), lambda i: (0, 0)),
                  pl.BlockSpec((None, D, 2 * D), lambda i: (layer // 2, 0, 0)),
                  pl.BlockSpec((1, 2 * D), lambda i: (0, 0))],
        out_specs=pl.BlockSpec((GT, D), lambda i: (i, 0)),
        out_shape=jax.ShapeDtypeStruct((NCOL * TILE, D), F32),
        scratch_shapes=[pltpu.VMEM((D, 2 * D), BF16), pltpu.VMEM((GT, D), BF16), pltpu.VMEM((GT, 2 * D), F32)],
        compiler_params=_cparams(1),
        name="s5_glu",
    )(xa, xb, *ys, mods, g, dvec, w, b)


def _fourier_kernel(x_ref, mod_ref, g_ref, cc_ref, cs_ref, cl_ref, sl_ref, w_ref, b_ref,
                    o_ref, *scratch, scale, nseg):
    mod = mod_ref[...]

    def channel_dft():
        h = _normmod(x_ref[...], g_ref[...], mod[1:2], mod[0:1]).astype(BF16)
        xc, xs = [], []
        for q in range(FG):
            hq = h[:, q * FGC:(q + 1) * FGC]
            xc.append(jnp.dot(hq, cc_ref[...], preferred_element_type=F32).astype(BF16))
            xs.append(jnp.dot(hq, cs_ref[...], preferred_element_type=F32).astype(BF16))
        return jnp.concatenate(xc, axis=1), jnp.concatenate(xs, axis=1)

    def position_dft(xc, xs):
        f = (jnp.dot(cl_ref[...], xc, preferred_element_type=F32)
             - jnp.dot(sl_ref[...], xs, preferred_element_type=F32)) * scale
        o = jnp.dot(f.astype(BF16), w_ref[...], preferred_element_type=F32) + b_ref[...]
        o_ref[...] = x_ref[...] + mod[2:3] * o

    if nseg == 1:
        position_dft(*channel_dft())
        return

    xc_s, xs_s = scratch
    ph = pl.program_id(1)
    s = pl.program_id(2)

    @pl.when(ph == 0)
    def _():
        r0 = pl.multiple_of(s * TILE, TILE)
        xc, xs = channel_dft()
        xc_s[pl.ds(r0, TILE), :] = xc
        xs_s[pl.ds(r0, TILE), :] = xs

    @pl.when(ph == 1)
    def _():
        position_dft(xc_s[...], xs_s[...])


def _fourier(x, mods, layer, g, cc, cs, cl, sl, w, b, *, nseq, nseg, col_off, cond_off, cond_stride):
    ln = nseg * TILE
    if nseg == 1:
        grid = (nseq,)
        idx = lambda f: (lambda q: f(q, 1, 0))
        scratch = []
    else:
        grid = (nseq, 2, nseg)
        idx = lambda f: f
        scratch = [pltpu.VMEM((ln, D), BF16), pltpu.VMEM((ln, D), BF16)]
    return pl.pallas_call(
        functools.partial(_fourier_kernel, scale=1.0 / math.sqrt(ln * FGC), nseg=nseg),
        grid=grid,
        in_specs=[pl.BlockSpec((TILE, D), idx(lambda q, ph, s: (col_off + q * nseg + s, 0))),
                  pl.BlockSpec((None, None, 6, D), idx(lambda q, ph, s: (layer, cond_off + q * cond_stride, 0, 0))),
                  pl.BlockSpec((1, D), idx(lambda q, ph, s: (0, 0))),
                  pl.BlockSpec((FGC, FGC), idx(lambda q, ph, s: (0, 0))),
                  pl.BlockSpec((FGC, FGC), idx(lambda q, ph, s: (0, 0))),
                  pl.BlockSpec((TILE, ln), idx(lambda q, ph, s: (s * ph, 0))),
                  pl.BlockSpec((TILE, ln), idx(lambda q, ph, s: (s * ph, 0))),
                  pl.BlockSpec((None, D, D), idx(lambda q, ph, s: (layer // 2, 0, 0))),
                  pl.BlockSpec((1, D), idx(lambda q, ph, s: (0, 0)))],
        out_specs=pl.BlockSpec((TILE, D), idx(lambda q, ph, s: (col_off + q * nseg + s * ph, 0))),
        out_shape=jax.ShapeDtypeStruct((NCOL * TILE, D), F32),
        scratch_shapes=scratch,
        input_output_aliases={0: 0},
        compiler_params=_cparams(len(grid)),
        name="fourier%d" % nseg,
    )(x, mods, g, cc, cs, cl, sl, w, b)


@functools.lru_cache(maxsize=None)
def _dft_mats(n):
    k = np.arange(n, dtype=np.int64)
    ang = ((k[:, None] * k[None, :]) % n).astype(np.float64) * (2.0 * math.pi / n)
    return np.cos(ang).astype(np.float32), np.sin(ang).astype(np.float32)


WCH = DFF // FC


def _ffn_kernel(x_ref, mod_ref, g_ref, wup_hbm, cw_ref, cb_ref, wdn_hbm, gf_ref, *rest, final, layer):
    n_out = 2 if final else 1
    o_refs = rest[:n_out]
    wup_ref, wdn_ref, stg_u, stg_d, sem_u, sem_d = rest[n_out:]
    i = pl.program_id(0)

    @pl.when(i == 0)
    def _load_weights():
        def copies(c):
            slot = c % 2
            return (pltpu.make_async_copy(wup_hbm.at[layer, :, pl.ds(c * 2 * FC, 2 * FC)], stg_u.at[slot],
                                          sem_u.at[slot]),
                    pltpu.make_async_copy(wdn_hbm.at[layer, pl.ds(c * FC, FC), :], stg_d.at[slot],
                                          sem_d.at[slot]))

        for cp in copies(0):
            cp.start()
        for c in range(WCH):
            if c + 1 < WCH:
                for cp in copies(c + 1):
                    cp.start()
            for cp in copies(c):
                cp.wait()
            wup_ref[:, c * 2 * FC:(c + 1) * 2 * FC] = stg_u[c % 2].astype(BF16)
            wdn_ref[c * FC:(c + 1) * FC, :] = stg_d[c % 2].astype(BF16)

    x = x_ref[...]
    mod = mod_ref[...]
    h = _normmod(x, g_ref[...], mod[4:5], mod[3:4]).astype(BF16)
    sub = lax.broadcasted_iota(jnp.int32, (8, FC), 0)
    one = jnp.ones((8, FC), F32)
    first0 = (sub != 0).astype(F32)
    last0 = (sub != 7).astype(F32)
    inner_first = jnp.where(i < NPC, one, first0)
    inner_last = jnp.where(i < NPC, one, last0)
    prev_slabs = [(0, first0)] + [(r, inner_first) for r in range(GRID_W, TILE, GRID_W)]
    next_slabs = [(r - 8, inner_last) for r in range(GRID_W, TILE, GRID_W)] + [(TILE - 8, last0)]

    def mask_rows(a, slabs):
        parts, last = [], 0
        for r0, m in slabs:
            if r0 > last:
                parts.append(a[last:r0])
            parts.append(a[r0:r0 + 8] * m)
            last = r0 + 8
        if last < TILE:
            parts.append(a[last:])
        return jnp.concatenate(parts, axis=0)

    def conv(up, off):
        w = cw_ref[:, off:off + FC]
        prev = mask_rows(pltpu.roll(up, 1, 0), prev_slabs)
        nxt = mask_rows(pltpu.roll(up, TILE - 1, 0), next_slabs)
        return w[0:1] * prev + w[1:2] * up + w[2:3] * nxt + cb_ref[:, off:off + FC]

    def up_chunk(c):
        og = c * FC
        ov = DFF + c * FC
        return (jnp.dot(h, wup_ref[:, og:og + FC], preferred_element_type=F32),
                jnp.dot(h, wup_ref[:, ov:ov + FC], preferred_element_type=F32))

    def down_chunk(act, c):
        return jnp.dot(act, wdn_ref[c * FC:(c + 1) * FC, :], preferred_element_type=F32)

    nchunk = DFF // FC
    acc = jnp.zeros((TILE, D), F32)
    nxt_up = up_chunk(0)
    act = None
    for c in range(nchunk):
        cur_up = nxt_up
        if c + 1 < nchunk:
            nxt_up = up_chunk(c + 1)
        if act is not None:
            acc = acc + down_chunk(act, c - 1)
        gate = conv(cur_up[0], c * FC)
        val = conv(cur_up[1], DFF + c * FC)
        hg = 0.5 * gate
        act = ((hg + hg * jnp.tanh(hg)) * val).astype(BF16)
    acc = acc + down_chunk(act, nchunk - 1)
    y = x + mod[5:6] * acc
    if not final:
        o_refs[0][...] = y
    else:
        ms = jnp.mean(y * y, axis=-1, keepdims=True)
        y = y * lax.rsqrt(ms + EPS) * gf_ref[...]

        @pl.when(i < NPC)
        def _():
            o_refs[0][...] = y

        @pl.when(i >= NPC)
        def _():
            o_refs[1][...] = y


def _ffn(x, mods, layer, g, wup, cw, cb, wdn, gf, *, final):
    if final:
        out_specs = [pl.BlockSpec((TILE, D), lambda i: (jnp.minimum(i, NPC - 1), 0)),
                     pl.BlockSpec((TILE, D), lambda i: (jnp.maximum(i - NPC, 0), 0))]
        out_shape = [jax.ShapeDtypeStruct((NPC * TILE, D), F32)] * 2
    else:
        out_specs = pl.BlockSpec((TILE, D), lambda i: (i, 0))
        out_shape = jax.ShapeDtypeStruct((NCOL * TILE, D), F32)
    return pl.pallas_call(
        functools.partial(_ffn_kernel, final=final, layer=layer),
        grid=(NCOL,),
        in_specs=[pl.BlockSpec((TILE, D), lambda i: (i, 0)),
                  pl.BlockSpec((None, None, 6, D), lambda i: (layer, _cond_of_col(i), 0, 0)),
                  pl.BlockSpec((1, D), lambda i: (0, 0)),
                  pl.BlockSpec(memory_space=pl.ANY),
                  pl.BlockSpec((None, 3, 2 * DFF), lambda i: (layer, 0, 0)),
                  pl.BlockSpec((None, 1, 2 * DFF), lambda i: (layer, 0, 0)),
                  pl.BlockSpec(memory_space=pl.ANY),
                  pl.BlockSpec((1, D), lambda i: (0, 0))],
        out_specs=out_specs,
        out_shape=out_shape,
        scratch_shapes=[pltpu.VMEM((D, 2 * DFF), BF16), pltpu.VMEM((DFF, D), BF16),
                        pltpu.VMEM((2, D, 2 * FC), F32), pltpu.VMEM((2, FC, D), F32),
                        pltpu.SemaphoreType.DMA((2,)), pltpu.SemaphoreType.DMA((2,))],
        compiler_params=_cparams(1),
        name="ffn",
    )(x, mods, g, wup, cw, cb, wdn, gf)


def kernel(x_prompt, x_sample, state_ssm_re, state_ssm_im, c, c_ctx, w_ada, b_ada, g_mix, g_ffn,
           ssm_lam_re, ssm_lam_im, ssm_log_dt, ssm_b_re, ssm_b_im, ssm_c_re, ssm_c_im, ssm_d,
           w_glu, b_glu, w_fourier, b_fourier, w_up, conv_w, conv_b, w_down, g_final):
    nb = x_prompt.shape[0]
    xp = x_prompt.reshape(NPC * TILE, D)
    xs = x_sample.reshape(NPC * TILE, D)
    x = None

    cond8 = jnp.concatenate([c_ctx[None, :], c, jnp.zeros((5, D), F32)], axis=0)
    mods = _ada(cond8, w_ada, b_ada).reshape(DEPTH, 8, 6, D)
    col_cond = jnp.asarray([0] * NPC + [1] * SEG + [2] * SEG, jnp.int32)
    mods_t = mods.transpose(0, 2, 1, 3)[:, :, col_cond]

    wf_bf = _cast_bf16(w_fourier.reshape(2 * D, D), D).reshape(2, D, D)

    cc, cs = (jnp.asarray(m).astype(BF16) for m in _dft_mats(FGC))
    cl1, sl1 = cc, cs
    cl8, sl8 = (jnp.asarray(m).astype(BF16) for m in _dft_mats(SEG * TILE))

    njd = ssm_lam_re.shape[0] * 2
    lam_re = ssm_lam_re.reshape(njd, 1, NST)
    lam_im = ssm_lam_im.reshape(njd, 1, NST)
    dts = jnp.repeat(ssm_log_dt.reshape(njd, GS), PS, axis=-1).reshape(njd, 1, NST)
    bt_re = ssm_b_re.transpose(0, 1, 4, 2, 3).reshape(njd, GC, NST)
    bt_im = ssm_b_im.transpose(0, 1, 4, 2, 3).reshape(njd, GC, NST)
    cw_re = jnp.tile(ssm_c_re.transpose(0, 1, 2, 4, 3).reshape(njd, NST, GC), (1, 1, 8))
    cw_im = jnp.tile(ssm_c_im.transpose(0, 1, 2, 4, 3).reshape(njd, NST, GC), (1, 1, 8))
    st_re4 = state_ssm_re.reshape(2, njd, 1, NST)
    st_im4 = state_ssm_im.reshape(2, njd, 1, NST)

    edges = []
    for i in range(DEPTH):
        j = i // 2
        if i % 2 == 0:
            g = g_mix[i][None, :]
            xa, xb, off_b = (xp, xs, 0) if x is None else (x, x, NPC)
            xa3 = xa.reshape(-1, TILE, D)
            xb3 = xb.reshape(-1, TILE, D)
            par = (mods_t, i, g, lam_re, lam_im, dts, bt_re, bt_im, cw_re, cw_im)
            fin = [_s5_scan(xb3, off_b, *par, jd=2 * j + d, rev=d == 1, trunk=1, init="state",
                            h0=(st_re4, st_im4), emit_y=False, emit_edge=False, emit_fin=True)
                   for d in range(2)]
            carry = _carry(*fin[0], *fin[1], st_re4, st_im4, lam_re, lam_im, dts, j=j)
            ya, yb = [], []
            for d in range(2):
                y, er, ei = _s5_scan(xa3, 0, *par, jd=2 * j + d, rev=d == 1, trunk=0, init="zero",
                                     emit_y=True, emit_edge=True, emit_fin=False)
                ya.append(y.reshape(NPC * TILE, D))
                edges += [er, ei]
                (y,) = _s5_scan(xb3, off_b, *par, jd=2 * j + d, rev=d == 1, trunk=1, init="carry",
                                h0=carry, emit_y=True, emit_edge=False, emit_fin=False)
                yb.append(y.reshape(NPC * TILE, D))
            x = _glu(xa, xb, off_b, ya + yb, mods, i, g, ssm_d[j][None, :], w_glu, b_glu[j][None, :])
        else:
            g = g_mix[i][None, :]
            bf = b_fourier[j][None, :]
            x = _fourier(x, mods, i, g, cc, cs, cl1, sl1, wf_bf, bf,
                         nseq=NPC, nseg=1, col_off=0, cond_off=0, cond_stride=0)
            x = _fourier(x, mods, i, g, cc, cs, cl8, sl8, wf_bf, bf,
                         nseq=2, nseg=SEG, col_off=NPC, cond_off=1, cond_stride=1)
        x = _ffn(x, mods, i, g_ffn[i][None, :], w_up, conv_w, conv_b.reshape(DEPTH, 1, 2 * DFF),
                 w_down, g_final[None, :], final=(i == DEPTH - 1))

    y_prompt = x[0].reshape(NPC, TILE, D)
    y_sample = x[1].reshape(2, SEG * TILE, D)
    ed = jnp.stack(edges, axis=0).reshape(DEPTH // 2, 2, 2, NPC, NST)[:, :, :, :nb]
    new_re = ed[:, :, 0].transpose(2, 0, 1, 3).reshape(nb, DEPTH // 2, 2, GS, PS)
    new_im = ed[:, :, 1].transpose(2, 0, 1, 3).reshape(nb, DEPTH // 2, 2, GS, PS)
    return (y_prompt, y_sample, new_re, new_im)
```

```python
import functools
import math

import jax
import jax.numpy as jnp
import numpy as np
from jax import lax
from jax.experimental import pallas as pl
from jax.experimental.pallas import tpu as pltpu

F32 = jnp.float32
BF16 = jnp.bfloat16

D = 1024
TILE = 256
NCOL = 32
NPC = 16
SEG = 8
DEPTH = 4
GS = 64
GC = 16
PS = 64
NST = GS * PS
FG = 4
FGC = 256
DFF = 2816
EPS = 1e-6
GRID_W = 64

TT = 32
NCH = TILE // TT
SW = 512
NSLAB = NST // SW
FC = 256

VMEM_LIMIT = 56 * 1024 * 1024


def _cparams(n_axes):
    return pltpu.CompilerParams(dimension_semantics=("arbitrary",) * n_axes,
                                vmem_limit_bytes=VMEM_LIMIT)


def _normmod(x, g, sc, sh):
    ms = jnp.mean(x * x, axis=-1, keepdims=True)
    return x * lax.rsqrt(ms + EPS) * (g * (1.0 + sc)) + sh


def _sigmoid(x):
    return 1.0 / (1.0 + jnp.exp(-x))


def _cond_of_col(i):
    return jnp.where(i < NPC, 0, 1 + (i - NPC) // SEG)


def _ada_kernel(c_ref, w_ref, b_ref, o_ref):
    c = c_ref[...]
    s = (c * _sigmoid(c)).astype(BF16)
    o_ref[...] = jnp.dot(s, w_ref[...].astype(BF16), preferred_element_type=F32) + b_ref[...]


def _ada(cond8, w_ada, b_ada):
    tn = 1536
    return pl.pallas_call(
        _ada_kernel,
        grid=(DEPTH, 6 * D // tn),
        in_specs=[pl.BlockSpec((8, D), lambda l, n: (0, 0)),
                  pl.BlockSpec((None, D, tn), lambda l, n: (l, 0, n)),
                  pl.BlockSpec((None, 1, tn), lambda l, n: (l, 0, n))],
        out_specs=pl.BlockSpec((None, 8, tn), lambda l, n: (l, 0, n)),
        out_shape=jax.ShapeDtypeStruct((DEPTH, 8, 6 * D), F32),
        compiler_params=_cparams(2),
        name="ada",
    )(cond8, w_ada, b_ada.reshape(DEPTH, 1, 6 * D))


def _cast_kernel(w_ref, o_ref):
    o_ref[...] = w_ref[...].astype(BF16)


def _cast_bf16(w2d, block_rows):
    r, c = w2d.shape
    return pl.pallas_call(
        _cast_kernel,
        grid=(r // block_rows,),
        in_specs=[pl.BlockSpec((block_rows, c), lambda i: (i, 0))],
        out_specs=pl.BlockSpec((block_rows, c), lambda i: (i, 0)),
        out_shape=jax.ShapeDtypeStruct((r, c), BF16),
        compiler_params=_cparams(1),
        name="cast",
    )(w2d)


def _abar(lr, li, logdt):
    dt = jnp.exp(logdt)
    mag = jnp.exp(lr * dt)
    return mag * jnp.cos(li * dt), mag * jnp.sin(li * dt)


def _s5_kernel(*refs, rev, col_base, init, u_mode, y_mode, emit_edge, emit_fin):
    src_hbm, sc_ref, sh_ref, g_ref, lr_ref, li_ref, dt_ref, btr_ref, bti_ref, cwr_ref, cwi_ref = refs[:11]
    pos = 11
    if init != "zero":
        h0r_ref, h0i_ref = refs[pos:pos + 2]
        pos += 2
    if y_mode == "sum":
        yo_hbm = refs[pos]
        pos += 1
    if y_mode is not None:
        y_hbm = refs[pos]
        pos += 1
    if u_mode == "make":
        u_hbm = refs[pos]
        pos += 1
    if emit_edge:
        er_ref, ei_ref = refs[pos:pos + 2]
        pos += 2
    if emit_fin:
        fr_ref, fi_ref = refs[pos:pos + 2]
        pos += 2
    scratch = refs[pos:]
    bu_re = scratch[0:NSLAB]
    bu_im = scratch[NSLAB:2 * NSLAB]
    bt_s, ct_s, ar_s, ai_s, hs_re, hs_im, inbuf, sem_in = scratch[2 * NSLAB:2 * NSLAB + 8]
    pos = 2 * NSLAB + 8
    if u_mode == "make":
        ubuf, sem_u = scratch[pos:pos + 2]
        pos += 2
    if y_mode is not None:
        ybuf, sem_out = scratch[pos:pos + 2]
        hb = scratch[pos + 2:pos + 2 + NSLAB]
        pos += 2 + NSLAB
    if y_mode == "sum":
        yobuf, sem_yo = scratch[pos:pos + 2]
        pos += 2
    if emit_edge:
        ed_re, ed_im = scratch[pos:pos + 2]

    i = pl.program_id(0)
    slot = lax.rem(i, 2)

    def chunk(i_):
        return NCH - 1 - i_ if rev else i_

    def in_copies(i_, slot_):
        if u_mode == "load":
            return [pltpu.make_async_copy(src_hbm.at[chunk(i_)], inbuf.at[slot_], sem_in.at[slot_])]
        t0 = pl.multiple_of(chunk(i_) * TT, TT)
        return [pltpu.make_async_copy(src_hbm.at[col_base + c, pl.ds(t0, TT), :],
                                      inbuf.at[slot_, :, c, :], sem_in.at[slot_]) for c in range(NPC)]

    def u_copies(i_, slot_):
        return [pltpu.make_async_copy(ubuf.at[slot_], u_hbm.at[chunk(i_)], sem_u.at[slot_])]

    def yo_copies(i_, slot_):
        return [pltpu.make_async_copy(yo_hbm.at[chunk(i_)], yobuf.at[slot_], sem_yo.at[slot_])]

    def y_copies(i_, slot_):
        if y_mode == "tm":
            return [pltpu.make_async_copy(ybuf.at[slot_], y_hbm.at[chunk(i_)], sem_out.at[slot_])]
        t0 = pl.multiple_of(chunk(i_) * TT, TT)
        return [pltpu.make_async_copy(ybuf.at[slot_, :, c, :], y_hbm.at[c, pl.ds(t0, TT), :],
                                      sem_out.at[slot_]) for c in range(NPC)]

    def fetches(i_, slot_):
        return in_copies(i_, slot_) + (yo_copies(i_, slot_) if y_mode == "sum" else [])

    def stores(i_, slot_):
        return ((y_copies(i_, slot_) if y_mode is not None else [])
                + (u_copies(i_, slot_) if u_mode == "make" else []))

    @pl.when(i == 0)
    def _first_fetch():
        for cp in fetches(i, slot):
            cp.start()

    if y_mode is not None or u_mode == "make":
        @pl.when(i >= 2)
        def _out_bufs_free():
            for cp in stores(i, slot):
                cp.wait()

    @pl.when(i == 0)
    def _prep():
        lr = lr_ref[...]
        li = li_ref[...]
        ar, ai = _abar(lr, li, dt_ref[...])
        ar_s[...] = jnp.broadcast_to(ar, (8, NST))
        ai_s[...] = jnp.broadcast_to(ai, (8, NST))
        xr = ar - 1.0
        den = lr * lr + li * li
        fr = (xr * lr + ai * li) / den
        fi = (ai * lr - xr * li) / den
        btr = btr_ref[...]
        bti = bti_ref[...]
        bbr = fr * btr - fi * bti
        bbi = fr * bti + fi * btr
        r = lax.broadcasted_iota(jnp.int32, (128, SW), 0)
        c = lax.broadcasted_iota(jnp.int32, (128, SW), 1)
        bmask = ((r >> 4) == (c >> 6)).astype(F32)
        for m in range(NSLAB):
            sl = slice(m * SW, (m + 1) * SW)
            tr = jnp.concatenate([bbr[:, sl]] * 8, axis=0) * bmask
            ti = jnp.concatenate([bbi[:, sl]] * 8, axis=0) * bmask
            bt_s[m, :, 0:SW] = tr.astype(BF16)
            bt_s[m, :, SW:2 * SW] = ti.astype(BF16)
        r2 = lax.broadcasted_iota(jnp.int32, (SW, 128), 0)
        c2 = lax.broadcasted_iota(jnp.int32, (SW, 128), 1)
        cmask = ((r2 >> 6) == (c2 >> 4)).astype(F32)
        for q in range(NSLAB):
            sl = slice(q * SW, (q + 1) * SW)
            ct_s[q, 0:SW, :] = (cwr_ref[sl, :] * cmask).astype(BF16)
            ct_s[q, SW:2 * SW, :] = (-(cwi_ref[sl, :] * cmask)).astype(BF16)
        if init == "carry":
            hs_re[...] = h0r_ref[...]
            hs_im[...] = h0i_ref[...]
        else:
            hs_re[...] = jnp.zeros((NPC, NST), F32)
            hs_im[...] = jnp.zeros((NPC, NST), F32)
            if init == "state":
                seg0 = SEG - 1 if rev else 0
                for b in range(2):
                    row = b * SEG + seg0
                    hs_re[row:row + 1, :] = h0r_ref[b]
                    hs_im[row:row + 1, :] = h0i_ref[b]

    for cp in fetches(i, slot):
        cp.wait()
    for cp in fetches(jnp.minimum(i + 1, NCH - 1), 1 - slot):
        cp.start()

    if u_mode == "make":
        x3 = inbuf[slot]
        ms = jnp.mean(x3 * x3, axis=-1, keepdims=True)
        gain = g_ref[...] * (1.0 + sc_ref[...])
        u3 = x3 * lax.rsqrt(ms + EPS) * gain[None] + sh_ref[...][None]
        u = u3.reshape(TT * NPC, D).astype(BF16)
        ubuf[slot] = u
    else:
        u = inbuf[slot]

    def flip_time(a):
        if not rev:
            return a
        return jnp.concatenate([a[(TT - 1 - t) * NPC:(TT - t) * NPC] for t in range(TT)], axis=0)

    u = flip_time(u)

    def expand(m):
        res = jnp.dot(u[:, m * 128:(m + 1) * 128], bt_s[m], preferred_element_type=F32)
        bu_re[m][...] = res[:, 0:SW]
        bu_im[m][...] = res[:, SW:2 * SW]

    def recur(s):
        lanes = slice(s * SW, (s + 1) * SW)
        ar = ar_s[:, lanes]
        ai = ai_s[:, lanes]
        h = [hs_re[0:8, lanes], hs_im[0:8, lanes], hs_re[8:16, lanes], hs_im[8:16, lanes]]
        for t in range(TT):
            for q in range(2):
                rows = slice(t * NPC + 8 * q, t * NPC + 8 * q + 8)
                hr, hi = h[2 * q], h[2 * q + 1]
                nr = ar * hr - ai * hi + bu_re[s][rows, :]
                ni = ar * hi + ai * hr + bu_im[s][rows, :]
                h[2 * q], h[2 * q + 1] = nr, ni
            if y_mode is not None:
                trow = slice(t * NPC, (t + 1) * NPC)
                hb[s][trow, 0:SW] = jnp.concatenate([h[0], h[2]], axis=0).astype(BF16)
                hb[s][trow, SW:2 * SW] = jnp.concatenate([h[1], h[3]], axis=0).astype(BF16)
            if emit_edge and t == 0:
                ed_re[0:8, lanes], ed_im[0:8, lanes] = h[0], h[1]
                ed_re[8:16, lanes], ed_im[8:16, lanes] = h[2], h[3]
        hs_re[0:8, lanes] = h[0]
        hs_im[0:8, lanes] = h[1]
        hs_re[8:16, lanes] = h[2]
        hs_im[8:16, lanes] = h[3]

    def contract(q):
        yq = jnp.dot(hb[q][...], ct_s[q], preferred_element_type=F32)
        yq = flip_time(yq).reshape(TT, NPC, 128)
        if y_mode == "sum":
            yq = yq + yobuf[slot, :, :, q * 128:(q + 1) * 128]
        ybuf[slot, :, :, q * 128:(q + 1) * 128] = yq

    expand(0)
    for s in range(NSLAB):
        if s + 1 < NSLAB:
            expand(s + 1)
        if y_mode is not None and s >= 1:
            contract(s - 1)
        recur(s)
    if y_mode is not None:
        contract(NSLAB - 1)
    for cp in stores(i, slot):
        cp.start()

    if emit_edge:
        @pl.when(i == 0)
        def _edge():
            er_ref[...] = ed_re[...]
            ei_ref[...] = ed_im[...]

    @pl.when(i == NCH - 1)
    def _last():
        if emit_fin:
            fr_ref[...] = hs_re[...]
            fi_ref[...] = hs_im[...]
        for cp in fetches(i, 1 - slot) + stores(i, slot) + stores(i, 1 - slot):
            cp.wait()


def _s5_scan(src, col_base, mods_t, layer, g, lam_re, lam_im, dts, bt_re, bt_im, cw_re, cw_im, *,
             jd, rev, trunk, init, h0=None, u_mode, y_mode, y_other=None, emit_edge, emit_fin):
    par = lambda i: (jd, 0, 0)
    in_specs = [
        pl.BlockSpec(memory_space=pl.ANY),
        pl.BlockSpec((None, None, NPC, D), lambda i: (layer, 1, trunk, 0)),
        pl.BlockSpec((None, None, NPC, D), lambda i: (layer, 0, trunk, 0)),
        pl.BlockSpec((1, D), lambda i: (0, 0)),
        pl.BlockSpec((None, 1, NST), par),
        pl.BlockSpec((None, 1, NST), par),
        pl.BlockSpec((None, 1, NST), par),
        pl.BlockSpec((None, GC, NST), par),
        pl.BlockSpec((None, GC, NST), par),
        pl.BlockSpec((None, NST, 128), par),
        pl.BlockSpec((None, NST, 128), par),
    ]
    args = [src, mods_t, mods_t, g, lam_re, lam_im, dts, bt_re, bt_im, cw_re, cw_im]
    if init == "state":
        in_specs += [pl.BlockSpec((2, None, 1, NST), lambda i: (0, jd, 0, 0))] * 2
        args += list(h0)
    elif init == "carry":
        in_specs += [pl.BlockSpec((None, NPC, NST), lambda i: (1 if rev else 0, 0, 0))] * 2
        args += list(h0)
    if y_mode == "sum":
        in_specs.append(pl.BlockSpec(memory_space=pl.ANY))
        args.append(y_other)
    out_specs, out_shape = [], []
    slab = pltpu.VMEM((TT * NPC, SW), F32)
    scratch = [slab] * (2 * NSLAB) + [
        pltpu.VMEM((NSLAB, 128, 2 * SW), BF16), pltpu.VMEM((NSLAB, 2 * SW, 128), BF16),
        pltpu.VMEM((8, NST), F32), pltpu.VMEM((8, NST), F32),
        pltpu.VMEM((NPC, NST), F32), pltpu.VMEM((NPC, NST), F32),
        pltpu.VMEM((2, TT, NPC, D), F32) if u_mode == "make" else pltpu.VMEM((2, TT * NPC, D), BF16),
        pltpu.SemaphoreType.DMA((2,))]
    if y_mode is not None:
        out_specs.append(pl.BlockSpec(memory_space=pl.ANY))
        out_shape.append(jax.ShapeDtypeStruct((NCH, TT, NPC, D) if y_mode == "tm" else (NPC, TILE, D), F32))
    if u_mode == "make":
        out_specs.append(pl.BlockSpec(memory_space=pl.ANY))
        out_shape.append(jax.ShapeDtypeStruct((NCH, TT * NPC, D), BF16))
        scratch += [pltpu.VMEM((2, TT * NPC, D), BF16), pltpu.SemaphoreType.DMA((2,))]
    if y_mode is not None:
        scratch += [pltpu.VMEM((2, TT, NPC, D), F32), pltpu.SemaphoreType.DMA((2,))]
        scratch += [pltpu.VMEM((TT * NPC, 2 * SW), BF16)] * NSLAB
    if y_mode == "sum":
        scratch += [pltpu.VMEM((2, TT, NPC, D), F32), pltpu.SemaphoreType.DMA((2,))]
    if emit_edge:
        scratch += [pltpu.VMEM((NPC, NST), F32)] * 2
    n_state_outs = 2 * (int(emit_edge) + int(emit_fin))
    out_specs += [pl.BlockSpec((NPC, NST), lambda i: (0, 0))] * n_state_outs
    out_shape += [jax.ShapeDtypeStruct((NPC, NST), F32)] * n_state_outs
    return pl.pallas_call(
        functools.partial(_s5_kernel, rev=rev, col_base=col_base, init=init, u_mode=u_mode, y_mode=y_mode,
                          emit_edge=emit_edge, emit_fin=emit_fin),
        grid=(NCH,),
        in_specs=in_specs,
        out_specs=out_specs,
        out_shape=out_shape,
        scratch_shapes=scratch,
        compiler_params=_cparams(1),
        name=("s5_scan" if y_mode is not None else "s5_states") + ("_bwd" if rev else "_fwd") + str(trunk),
    )(*args)


def _carry_kernel(frf_ref, fif_ref, frb_ref, fib_ref, sr_ref, si_ref, lr_ref, li_ref, dt_ref,
                  or_ref, oi_ref, *, j):
    for d, (fr_ref, fi_ref) in enumerate(((frf_ref, fif_ref), (frb_ref, fib_ref))):
        jd = 2 * j + d
        ar, ai = _abar(lr_ref[jd], li_ref[jd], dt_ref[jd])
        for _ in range(8):
            ar, ai = ar * ar - ai * ai, 2.0 * ar * ai
        for b in range(2):
            order = list(range(SEG)) if d == 0 else list(range(SEG - 1, -1, -1))
            r = b * SEG + order[0]
            or_ref[d, r:r + 1, :] = sr_ref[b, jd]
            oi_ref[d, r:r + 1, :] = si_ref[b, jd]
            tr = fr_ref[r:r + 1, :]
            ti = fi_ref[r:r + 1, :]
            for s in order[1:]:
                r = b * SEG + s
                or_ref[d, r:r + 1, :] = tr
                oi_ref[d, r:r + 1, :] = ti
                tr, ti = (fr_ref[r:r + 1, :] + ar * tr - ai * ti,
                          fi_ref[r:r + 1, :] + ar * ti + ai * tr)


def _carry(frf, fif, frb, fib, st_re4, st_im4, lam_re, lam_im, dts, *, j):
    shp = jax.ShapeDtypeStruct((2, NPC, NST), F32)
    return pl.pallas_call(functools.partial(_carry_kernel, j=j), out_shape=[shp, shp], name="s5_carry")(
        frf, fif, frb, fib, st_re4, st_im4, lam_re, lam_im, dts)


GT = 2 * TILE


def _glu_kernel(xa_ref, xb_ref, ya_ref, yb_ref, mod_ref, g_ref, d_ref, w_ref, b_ref,
                o_ref, wbf, ge_s, z_s):
    i = pl.program_id(0)

    @pl.when(i == 0)
    def _():
        wbf[...] = w_ref[...].astype(BF16)

    mod = mod_ref[...]
    first_prompt = i < NPC // 2

    nq = 4
    cw = D // nq
    rw = TILE // nq

    def x_rows(r0, r1, c0=0, c1=D):
        return jnp.where(first_prompt, xa_ref[r0:r1, c0:c1], xb_ref[r0:r1, c0:c1])

    def pre(r0, r1):
        h = _normmod(x_rows(r0, r1), g_ref[...], mod[1:2], mod[0:1])
        y_ssm = jnp.where(first_prompt, ya_ref[r0:r1, :], yb_ref[r0:r1, :])
        y = y_ssm + d_ref[...] * h
        k1 = -2.0 * math.sqrt(2.0 / math.pi)
        ge = y / (1.0 + jnp.exp(y * (k1 + (k1 * 0.044715) * (y * y))))
        ge_s[r0:r1, :] = ge.astype(BF16)

    def gate(q, j):
        rows = slice(q * TILE, (q + 1) * TILE)
        for c0 in (j * cw, D + j * cw):
            z_s[rows, c0:c0 + cw] = (jnp.dot(ge_s[rows, :], wbf[:, c0:c0 + cw], preferred_element_type=F32)
                                     + b_ref[:, c0:c0 + cw])

    def post(q, j):
        rows = slice(q * TILE, (q + 1) * TILE)
        c0, c1 = j * cw, (j + 1) * cw
        out = z_s[rows, c0:c1] * _sigmoid(z_s[rows, D + c0:D + c1])
        o_ref[rows, c0:c1] = x_rows(q * TILE, (q + 1) * TILE, c0, c1) + mod[2:3, c0:c1] * out

    for j in range(nq):
        pre(j * rw, (j + 1) * rw)
    for j in range(nq):
        gate(0, j)
        pre(TILE + j * rw, TILE + (j + 1) * rw)
    for j in range(nq):
        gate(1, j)
        post(0, j)
    for j in range(nq):
        post(1, j)


def _glu(xa, xb, off_b, ys, mods, layer, g, dvec, w, b):
    nblk = NCOL // 2
    half = NPC // 2
    pa = lambda i: (jnp.minimum(i, half - 1), 0)
    pb = lambda i: (jnp.maximum(i - half, 0), 0)
    return pl.pallas_call(
        _glu_kernel,
        grid=(nblk,),
        in_specs=[pl.BlockSpec((GT, D), pa),
                  pl.BlockSpec((GT, D), lambda i: (off_b // 2 + jnp.maximum(i - half, 0), 0)),
                  pl.BlockSpec((GT, D), pa),
                  pl.BlockSpec((GT, D), pb),
                  pl.BlockSpec((None, None, 6, D), lambda i: (layer, _cond_of_col(2 * i), 0, 0)),
                  pl.BlockSpec((1, D), lambda i: (0, 0)),
                  pl.BlockSpec((1, D), lambda i: (0, 0)),
                  pl.BlockSpec((None, D, 2 * D), lambda i: (layer // 2, 0, 0)),
                  pl.BlockSpec((1, 2 * D), lambda i: (0, 0))],
        out_specs=pl.BlockSpec((GT, D), lambda i: (i, 0)),
        out_shape=jax.ShapeDtypeStruct((NCOL * TILE, D), F32),
        scratch_shapes=[pltpu.VMEM((D, 2 * D), BF16), pltpu.VMEM((GT, D), BF16), pltpu.VMEM((GT, 2 * D), F32)],
        compiler_params=_cparams(1),
        name="s5_glu",
    )(xa, xb, *ys, mods, g, dvec, w, b)


def _fourier_kernel(x_ref, mod_ref, g_ref, cc_ref, cs_ref, cl_ref, sl_ref, w_ref, b_ref,
                    o_ref, *scratch, scale, nseg):
    mod = mod_ref[...]

    def channel_dft():
        h = _normmod(x_ref[...], g_ref[...], mod[1:2], mod[0:1]).astype(BF16)
        xc, xs = [], []
        for q in range(FG):
            hq = h[:, q * FGC:(q + 1) * FGC]
            xc.append(jnp.dot(hq, cc_ref[...], preferred_element_type=F32).astype(BF16))
            xs.append(jnp.dot(hq, cs_ref[...], preferred_element_type=F32).astype(BF16))
        return jnp.concatenate(xc, axis=1), jnp.concatenate(xs, axis=1)

    def position_dft(xc, xs):
        f = (jnp.dot(cl_ref[...], xc, preferred_element_type=F32)
             - jnp.dot(sl_ref[...], xs, preferred_element_type=F32)) * scale
        o = jnp.dot(f.astype(BF16), w_ref[...], preferred_element_type=F32) + b_ref[...]
        o_ref[...] = x_ref[...] + mod[2:3] * o

    if nseg == 1:
        position_dft(*channel_dft())
        return

    xc_s, xs_s = scratch
    ph = pl.program_id(1)
    s = pl.program_id(2)

    @pl.when(ph == 0)
    def _():
        r0 = pl.multiple_of(s * TILE, TILE)
        xc, xs = channel_dft()
        xc_s[pl.ds(r0, TILE), :] = xc
        xs_s[pl.ds(r0, TILE), :] = xs

    @pl.when(ph == 1)
    def _():
        position_dft(xc_s[...], xs_s[...])


def _fourier(x, mods, layer, g, cc, cs, cl, sl, w, b, *, nseq, nseg, col_off, cond_off, cond_stride):
    ln = nseg * TILE
    if nseg == 1:
        grid = (nseq,)
        idx = lambda f: (lambda q: f(q, 1, 0))
        scratch = []
    else:
        grid = (nseq, 2, nseg)
        idx = lambda f: f
        scratch = [pltpu.VMEM((ln, D), BF16), pltpu.VMEM((ln, D), BF16)]
    return pl.pallas_call(
        functools.partial(_fourier_kernel, scale=1.0 / math.sqrt(ln * FGC), nseg=nseg),
        grid=grid,
        in_specs=[pl.BlockSpec((TILE, D), idx(lambda q, ph, s: (col_off + q * nseg + s, 0))),
                  pl.BlockSpec((None, None, 6, D), idx(lambda q, ph, s: (layer, cond_off + q * cond_stride, 0, 0))),
                  pl.BlockSpec((1, D), idx(lambda q, ph, s: (0, 0))),
                  pl.BlockSpec((FGC, FGC), idx(lambda q, ph, s: (0, 0))),
                  pl.BlockSpec((FGC, FGC), idx(lambda q, ph, s: (0, 0))),
                  pl.BlockSpec((TILE, ln), idx(lambda q, ph, s: (s * ph, 0))),
                  pl.BlockSpec((TILE, ln), idx(lambda q, ph, s: (s * ph, 0))),
                  pl.BlockSpec((None, D, D), idx(lambda q, ph, s: (layer // 2, 0, 0))),
                  pl.BlockSpec((1, D), idx(lambda q, ph, s: (0, 0)))],
        out_specs=pl.BlockSpec((TILE, D), idx(lambda q, ph, s: (col_off + q * nseg + s * ph, 0))),
        out_shape=jax.ShapeDtypeStruct((NCOL * TILE, D), F32),
        scratch_shapes=scratch,
        input_output_aliases={0: 0},
        compiler_params=_cparams(len(grid)),
        name="fourier%d" % nseg,
    )(x, mods, g, cc, cs, cl, sl, w, b)


@functools.lru_cache(maxsize=None)
def _dft_mats(n):
    k = np.arange(n, dtype=np.int64)
    ang = ((k[:, None] * k[None, :]) % n).astype(np.float64) * (2.0 * math.pi / n)
    return np.cos(ang).astype(np.float32), np.sin(ang).astype(np.float32)


WCH = DFF // FC


def _ffn_kernel(x_ref, mod_ref, g_ref, wup_hbm, cw_ref, cb_ref, wdn_hbm, gf_ref, *rest, final, layer):
    n_out = 2 if final else 1
    o_refs = rest[:n_out]
    wup_ref, wdn_ref, stg_u, stg_d, sem_u, sem_d = rest[n_out:]
    i = pl.program_id(0)

    @pl.when(i == 0)
    def _load_weights():
        def copies(c):
            slot = c % 2
            return (pltpu.make_async_copy(wup_hbm.at[layer, :, pl.ds(c * 2 * FC, 2 * FC)], stg_u.at[slot],
                                          sem_u.at[slot]),
                    pltpu.make_async_copy(wdn_hbm.at[layer, pl.ds(c * FC, FC), :], stg_d.at[slot],
                                          sem_d.at[slot]))

        for cp in copies(0):
            cp.start()
        for c in range(WCH):
            if c + 1 < WCH:
                for cp in copies(c + 1):
                    cp.start()
            for cp in copies(c):
                cp.wait()
            wup_ref[:, c * 2 * FC:(c + 1) * 2 * FC] = stg_u[c % 2].astype(BF16)
            wdn_ref[c * FC:(c + 1) * FC, :] = stg_d[c % 2].astype(BF16)

    x = x_ref[...]
    mod = mod_ref[...]
    h = _normmod(x, g_ref[...], mod[4:5], mod[3:4]).astype(BF16)
    sub = lax.broadcasted_iota(jnp.int32, (8, FC), 0)
    one = jnp.ones((8, FC), F32)
    first0 = (sub != 0).astype(F32)
    last0 = (sub != 7).astype(F32)
    inner_first = jnp.where(i < NPC, one, first0)
    inner_last = jnp.where(i < NPC, one, last0)
    prev_slabs = [(0, first0)] + [(r, inner_first) for r in range(GRID_W, TILE, GRID_W)]
    next_slabs = [(r - 8, inner_last) for r in range(GRID_W, TILE, GRID_W)] + [(TILE - 8, last0)]

    def mask_rows(a, slabs):
        parts, last = [], 0
        for r0, m in slabs:
            if r0 > last:
                parts.append(a[last:r0])
            parts.append(a[r0:r0 + 8] * m)
            last = r0 + 8
        if last < TILE:
            parts.append(a[last:])
        return jnp.concatenate(parts, axis=0)

    def conv(up, off):
        w = cw_ref[:, off:off + FC]
        prev = mask_rows(pltpu.roll(up, 1, 0), prev_slabs)
        nxt = mask_rows(pltpu.roll(up, TILE - 1, 0), next_slabs)
        return w[0:1] * prev + w[1:2] * up + w[2:3] * nxt + cb_ref[:, off:off + FC]

    def up_chunk(c):
        og = c * FC
        ov = DFF + c * FC
        return (jnp.dot(h, wup_ref[:, og:og + FC], preferred_element_type=F32),
                jnp.dot(h, wup_ref[:, ov:ov + FC], preferred_element_type=F32))

    def down_chunk(act, c):
        return jnp.dot(act, wdn_ref[c * FC:(c + 1) * FC, :], preferred_element_type=F32)

    nchunk = DFF // FC
    acc = jnp.zeros((TILE, D), F32)
    nxt_up = up_chunk(0)
    act = None
    for c in range(nchunk):
        cur_up = nxt_up
        if c + 1 < nchunk:
            nxt_up = up_chunk(c + 1)
        if act is not None:
            acc = acc + down_chunk(act, c - 1)
        gate = conv(cur_up[0], c * FC)
        val = conv(cur_up[1], DFF + c * FC)
        hg = 0.5 * gate
        act = ((hg + hg * jnp.tanh(hg)) * val).astype(BF16)
    acc = acc + down_chunk(act, nchunk - 1)
    y = x + mod[5:6] * acc
    if not final:
        o_refs[0][...] = y
    else:
        ms = jnp.mean(y * y, axis=-1, keepdims=True)
        y = y * lax.rsqrt(ms + EPS) * gf_ref[...]

        @pl.when(i < NPC)
        def _():
            o_refs[0][...] = y

        @pl.when(i >= NPC)
        def _():
            o_refs[1][...] = y


def _ffn(x, mods, layer, g, wup, cw, cb, wdn, gf, *, final):
    if final:
        out_specs = [pl.BlockSpec((TILE, D), lambda i: (jnp.minimum(i, NPC - 1), 0)),
                     pl.BlockSpec((TILE, D), lambda i: (jnp.maximum(i - NPC, 0), 0))]
        out_shape = [jax.ShapeDtypeStruct((NPC * TILE, D), F32)] * 2
    else:
        out_specs = pl.BlockSpec((TILE, D), lambda i: (i, 0))
        out_shape = jax.ShapeDtypeStruct((NCOL * TILE, D), F32)
    return pl.pallas_call(
        functools.partial(_ffn_kernel, final=final, layer=layer),
        grid=(NCOL,),
        in_specs=[pl.BlockSpec((TILE, D), lambda i: (i, 0)),
                  pl.BlockSpec((None, None, 6, D), lambda i: (layer, _cond_of_col(i), 0, 0)),
                  pl.BlockSpec((1, D), lambda i: (0, 0)),
                  pl.BlockSpec(memory_space=pl.ANY),
                  pl.BlockSpec((None, 3, 2 * DFF), lambda i: (layer, 0, 0)),
                  pl.BlockSpec((None, 1, 2 * DFF), lambda i: (layer, 0, 0)),
                  pl.BlockSpec(memory_space=pl.ANY),
                  pl.BlockSpec((1, D), lambda i: (0, 0))],
        out_specs=out_specs,
        out_shape=out_shape,
        scratch_shapes=[pltpu.VMEM((D, 2 * DFF), BF16), pltpu.VMEM((DFF, D), BF16),
                        pltpu.VMEM((2, D, 2 * FC), F32), pltpu.VMEM((2, FC, D), F32),
                        pltpu.SemaphoreType.DMA((2,)), pltpu.SemaphoreType.DMA((2,))],
        compiler_params=_cparams(1),
        name="ffn",
    )(x, mods, g, wup, cw, cb, wdn, gf)


def kernel(x_prompt, x_sample, state_ssm_re, state_ssm_im, c, c_ctx, w_ada, b_ada, g_mix, g_ffn,
           ssm_lam_re, ssm_lam_im, ssm_log_dt, ssm_b_re, ssm_b_im, ssm_c_re, ssm_c_im, ssm_d,
           w_glu, b_glu, w_fourier, b_fourier, w_up, conv_w, conv_b, w_down, g_final):
    nb = x_prompt.shape[0]
    xp = x_prompt.reshape(NPC * TILE, D)
    xs = x_sample.reshape(NPC * TILE, D)
    x = None

    cond8 = jnp.concatenate([c_ctx[None, :], c, jnp.zeros((5, D), F32)], axis=0)
    mods = _ada(cond8, w_ada, b_ada).reshape(DEPTH, 8, 6, D)
    col_cond = jnp.asarray([0] * NPC + [1] * SEG + [2] * SEG, jnp.int32)
    mods_t = mods.transpose(0, 2, 1, 3)[:, :, col_cond]

    wf_bf = _cast_bf16(w_fourier.reshape(2 * D, D), D).reshape(2, D, D)

    cc, cs = (jnp.asarray(m).astype(BF16) for m in _dft_mats(FGC))
    cl1, sl1 = cc, cs
    cl8, sl8 = (jnp.asarray(m).astype(BF16) for m in _dft_mats(SEG * TILE))

    njd = ssm_lam_re.shape[0] * 2
    lam_re = ssm_lam_re.reshape(njd, 1, NST)
    lam_im = ssm_lam_im.reshape(njd, 1, NST)
    dts = jnp.repeat(ssm_log_dt.reshape(njd, GS), PS, axis=-1).reshape(njd, 1, NST)
    bt_re = ssm_b_re.transpose(0, 1, 4, 2, 3).reshape(njd, GC, NST)
    bt_im = ssm_b_im.transpose(0, 1, 4, 2, 3).reshape(njd, GC, NST)
    cw_re = jnp.tile(ssm_c_re.transpose(0, 1, 2, 4, 3).reshape(njd, NST, GC), (1, 1, 8))
    cw_im = jnp.tile(ssm_c_im.transpose(0, 1, 2, 4, 3).reshape(njd, NST, GC), (1, 1, 8))
    st_re4 = state_ssm_re.reshape(2, njd, 1, NST)
    st_im4 = state_ssm_im.reshape(2, njd, 1, NST)

    edges = []
    for i in range(DEPTH):
        j = i // 2
        if i % 2 == 0:
            g = g_mix[i][None, :]
            xa, xb, off_b = (xp, xs, 0) if x is None else (x, x, NPC)
            xa3 = xa.reshape(-1, TILE, D)
            xb3 = xb.reshape(-1, TILE, D)
            par = (mods_t, i, g, lam_re, lam_im, dts, bt_re, bt_im, cw_re, cw_im)
            kw = dict(emit_edge=False, emit_fin=True, y_mode=None, trunk=1, init="state", h0=(st_re4, st_im4))
            u_s, frf, fif = _s5_scan(xb3, off_b, *par, jd=2 * j, rev=False, u_mode="make", **kw)
            frb, fib = _s5_scan(u_s, 0, *par, jd=2 * j + 1, rev=True, u_mode="load", **kw)
            carry = _carry(frf, fif, frb, fib, st_re4, st_im4, lam_re, lam_im, dts, j=j)
            kw = dict(trunk=0, init="zero", emit_fin=False)
            yf, u_p, er, ei = _s5_scan(xa3, 0, *par, jd=2 * j, rev=False, u_mode="make", y_mode="tm",
                                       emit_edge=True, **kw)
            edges += [er, ei]
            ya, er, ei = _s5_scan(u_p, 0, *par, jd=2 * j + 1, rev=True, u_mode="load", y_mode="sum",
                                  y_other=yf, emit_edge=True, **kw)
            edges += [er, ei]
            kw = dict(trunk=1, init="carry", h0=carry, u_mode="load", emit_edge=False, emit_fin=False)
            (yf,) = _s5_scan(u_s, 0, *par, jd=2 * j, rev=False, y_mode="tm", **kw)
            (yb,) = _s5_scan(u_s, 0, *par, jd=2 * j + 1, rev=True, y_mode="sum", y_other=yf, **kw)
            ys = [ya.reshape(NPC * TILE, D), yb.reshape(NPC * TILE, D)]
            x = _glu(xa, xb, off_b, ys, mods, i, g, ssm_d[j][None, :], w_glu, b_glu[j][None, :])
        else:
            g = g_mix[i][None, :]
            bf = b_fourier[j][None, :]
            x = _fourier(x, mods, i, g, cc, cs, cl1, sl1, wf_bf, bf,
                         nseq=NPC, nseg=1, col_off=0, cond_off=0, cond_stride=0)
            x = _fourier(x, mods, i, g, cc, cs, cl8, sl8, wf_bf, bf,
                         nseq=2, nseg=SEG, col_off=NPC, cond_off=1, cond_stride=1)
        x = _ffn(x, mods, i, g_ffn[i][None, :], w_up, conv_w, conv_b.reshape(DEPTH, 1, 2 * DFF),
                 w_down, g_final[None, :], final=(i == DEPTH - 1))

    y_prompt = x[0].reshape(NPC, TILE, D)
    y_sample = x[1].reshape(2, SEG * TILE, D)
    ed = jnp.stack(edges, axis=0).reshape(DEPTH // 2, 2, 2, NPC, NST)[:, :, :, :nb]
    new_re = ed[:, :, 0].transpose(2, 0, 1, 3).reshape(nb, DEPTH // 2, 2, GS, PS)
    new_im = ed[:, :, 1].transpose(2, 0, 1, 3).reshape(nb, DEPTH // 2, 2, GS, PS)
    return (y_prompt, y_sample, new_re, new_im)
```

```python
import functools
import math

import jax
import jax.numpy as jnp
import numpy as np
from jax import lax
from jax.experimental import pallas as pl
from jax.experimental.pallas import tpu as pltpu

F32 = jnp.float32
BF16 = jnp.bfloat16

D = 1024
TILE = 256
NCOL = 32
NPC = 16
SEG = 8
DEPTH = 4
GS = 64
GC = 16
PS = 64
NST = GS * PS
FG = 4
FGC = 256
DFF = 2816
EPS = 1e-6
GRID_W = 64

TT = 32
NCH = TILE // TT
SW = 512
NSLAB = NST // SW
FC = 256

VMEM_LIMIT = 56 * 1024 * 1024


def _cparams(n_axes):
    return pltpu.CompilerParams(dimension_semantics=("arbitrary",) * n_axes,
                                vmem_limit_bytes=VMEM_LIMIT)


def _normmod(x, g, sc, sh):
    ms = jnp.mean(x * x, axis=-1, keepdims=True)
    return x * lax.rsqrt(ms + EPS) * (g * (1.0 + sc)) + sh


def _sigmoid(x):
    return 1.0 / (1.0 + jnp.exp(-x))


def _cond_of_col(i):
    return jnp.where(i < NPC, 0, 1 + (i - NPC) // SEG)


def _ada_kernel(c_ref, w_ref, b_ref, o_ref):
    c = c_ref[...]
    s = (c * _sigmoid(c)).astype(BF16)
    o_ref[...] = jnp.dot(s, w_ref[...].astype(BF16), preferred_element_type=F32) + b_ref[...]


def _ada(cond8, w_ada, b_ada):
    tn = 1536
    return pl.pallas_call(
        _ada_kernel,
        grid=(DEPTH, 6 * D // tn),
        in_specs=[pl.BlockSpec((8, D), lambda l, n: (0, 0)),
                  pl.BlockSpec((None, D, tn), lambda l, n: (l, 0, n)),
                  pl.BlockSpec((None, 1, tn), lambda l, n: (l, 0, n))],
        out_specs=pl.BlockSpec((None, 8, tn), lambda l, n: (l, 0, n)),
        out_shape=jax.ShapeDtypeStruct((DEPTH, 8, 6 * D), F32),
        compiler_params=_cparams(2),
        name="ada",
    )(cond8, w_ada, b_ada.reshape(DEPTH, 1, 6 * D))


def _cast_kernel(w_ref, o_ref):
    o_ref[...] = w_ref[...].astype(BF16)


def _cast_bf16(w2d, block_rows):
    r, c = w2d.shape
    return pl.pallas_call(
        _cast_kernel,
        grid=(r // block_rows,),
        in_specs=[pl.BlockSpec((block_rows, c), lambda i: (i, 0))],
        out_specs=pl.BlockSpec((block_rows, c), lambda i: (i, 0)),
        out_shape=jax.ShapeDtypeStruct((r, c), BF16),
        compiler_params=_cparams(1),
        name="cast",
    )(w2d)


def _abar(lr, li, logdt):
    dt = jnp.exp(logdt)
    mag = jnp.exp(lr * dt)
    return mag * jnp.cos(li * dt), mag * jnp.sin(li * dt)


def _s5_kernel(*refs, rev, col_base, init, u_mode, emit_y, emit_edge, emit_fin):
    src_hbm, sc_ref, sh_ref, g_ref, lr_ref, li_ref, dt_ref, btr_ref, bti_ref, cwr_ref, cwi_ref = refs[:11]
    pos = 11
    if init != "zero":
        h0r_ref, h0i_ref = refs[pos:pos + 2]
        pos += 2
    if emit_y:
        y_hbm = refs[pos]
        pos += 1
    if u_mode == "make":
        u_hbm = refs[pos]
        pos += 1
    if emit_edge:
        er_ref, ei_ref = refs[pos:pos + 2]
        pos += 2
    if emit_fin:
        fr_ref, fi_ref = refs[pos:pos + 2]
        pos += 2
    scratch = refs[pos:]
    bu_re = scratch[0:NSLAB]
    bu_im = scratch[NSLAB:2 * NSLAB]
    bt_s, ct_s, ar_s, ai_s, hs_re, hs_im, inbuf, sem_in = scratch[2 * NSLAB:2 * NSLAB + 8]
    pos = 2 * NSLAB + 8
    if u_mode == "make":
        ubuf, sem_u = scratch[pos:pos + 2]
        pos += 2
    if emit_y:
        ybuf, sem_out = scratch[pos:pos + 2]
        hb = scratch[pos + 2:pos + 2 + NSLAB]
        pos += 2 + NSLAB
    if emit_edge:
        ed_re, ed_im = scratch[pos:pos + 2]

    i = pl.program_id(0)
    slot = lax.rem(i, 2)

    def chunk(i_):
        return NCH - 1 - i_ if rev else i_

    def in_copies(i_, slot_):
        if u_mode == "load":
            return [pltpu.make_async_copy(src_hbm.at[chunk(i_)], inbuf.at[slot_], sem_in.at[slot_])]
        t0 = pl.multiple_of(chunk(i_) * TT, TT)
        return [pltpu.make_async_copy(src_hbm.at[col_base + c, pl.ds(t0, TT), :],
                                      inbuf.at[slot_, :, c, :], sem_in.at[slot_]) for c in range(NPC)]

    def u_copies(i_, slot_):
        return [pltpu.make_async_copy(ubuf.at[slot_], u_hbm.at[chunk(i_)], sem_u.at[slot_])]

    def y_copies(i_, slot_):
        t0 = pl.multiple_of(chunk(i_) * TT, TT)
        return [pltpu.make_async_copy(ybuf.at[slot_, :, c, :], y_hbm.at[c, pl.ds(t0, TT), :],
                                      sem_out.at[slot_]) for c in range(NPC)]

    fetches = in_copies

    def stores(i_, slot_):
        return (y_copies(i_, slot_) if emit_y else []) + (u_copies(i_, slot_) if u_mode == "make" else [])

    @pl.when(i == 0)
    def _first_fetch():
        for cp in fetches(i, slot):
            cp.start()

    if emit_y or u_mode == "make":
        @pl.when(i >= 2)
        def _out_bufs_free():
            for cp in stores(i, slot):
                cp.wait()

    @pl.when(i == 0)
    def _prep():
        lr = lr_ref[...]
        li = li_ref[...]
        ar, ai = _abar(lr, li, dt_ref[...])
        ar_s[...] = jnp.broadcast_to(ar, (8, NST))
        ai_s[...] = jnp.broadcast_to(ai, (8, NST))
        xr = ar - 1.0
        den = lr * lr + li * li
        fr = (xr * lr + ai * li) / den
        fi = (ai * lr - xr * li) / den
        btr = btr_ref[...]
        bti = bti_ref[...]
        bbr = fr * btr - fi * bti
        bbi = fr * bti + fi * btr
        r = lax.broadcasted_iota(jnp.int32, (128, SW), 0)
        c = lax.broadcasted_iota(jnp.int32, (128, SW), 1)
        bmask = ((r >> 4) == (c >> 6)).astype(F32)
        for m in range(NSLAB):
            sl = slice(m * SW, (m + 1) * SW)
            tr = jnp.concatenate([bbr[:, sl]] * 8, axis=0) * bmask
            ti = jnp.concatenate([bbi[:, sl]] * 8, axis=0) * bmask
            bt_s[m, :, 0:SW] = tr.astype(BF16)
            bt_s[m, :, SW:2 * SW] = ti.astype(BF16)
        r2 = lax.broadcasted_iota(jnp.int32, (SW, 128), 0)
        c2 = lax.broadcasted_iota(jnp.int32, (SW, 128), 1)
        cmask = ((r2 >> 6) == (c2 >> 4)).astype(F32)
        for q in range(NSLAB):
            sl = slice(q * SW, (q + 1) * SW)
            ct_s[q, 0:SW, :] = (cwr_ref[sl, :] * cmask).astype(BF16)
            ct_s[q, SW:2 * SW, :] = (-(cwi_ref[sl, :] * cmask)).astype(BF16)
        if init == "carry":
            hs_re[...] = h0r_ref[...]
            hs_im[...] = h0i_ref[...]
        else:
            hs_re[...] = jnp.zeros((NPC, NST), F32)
            hs_im[...] = jnp.zeros((NPC, NST), F32)
            if init == "state":
                seg0 = SEG - 1 if rev else 0
                for b in range(2):
                    row = b * SEG + seg0
                    hs_re[row:row + 1, :] = h0r_ref[b]
                    hs_im[row:row + 1, :] = h0i_ref[b]

    for cp in fetches(i, slot):
        cp.wait()
    for cp in fetches(jnp.minimum(i + 1, NCH - 1), 1 - slot):
        cp.start()

    if u_mode == "make":
        x3 = inbuf[slot]
        ms = jnp.mean(x3 * x3, axis=-1, keepdims=True)
        gain = g_ref[...] * (1.0 + sc_ref[...])
        u3 = x3 * lax.rsqrt(ms + EPS) * gain[None] + sh_ref[...][None]
        u = u3.reshape(TT * NPC, D).astype(BF16)
        ubuf[slot] = u
    else:
        u = inbuf[slot]

    def flip_time(a):
        if not rev:
            return a
        return jnp.concatenate([a[(TT - 1 - t) * NPC:(TT - t) * NPC] for t in range(TT)], axis=0)

    u = flip_time(u)

    def expand(m):
        res = jnp.dot(u[:, m * 128:(m + 1) * 128], bt_s[m], preferred_element_type=F32)
        bu_re[m][...] = res[:, 0:SW]
        bu_im[m][...] = res[:, SW:2 * SW]

    def recur(s):
        lanes = slice(s * SW, (s + 1) * SW)
        ar = ar_s[:, lanes]
        ai = ai_s[:, lanes]
        h = [hs_re[0:8, lanes], hs_im[0:8, lanes], hs_re[8:16, lanes], hs_im[8:16, lanes]]
        for t in range(TT):
            for q in range(2):
                rows = slice(t * NPC + 8 * q, t * NPC + 8 * q + 8)
                hr, hi = h[2 * q], h[2 * q + 1]
                nr = ar * hr - ai * hi + bu_re[s][rows, :]
                ni = ar * hi + ai * hr + bu_im[s][rows, :]
                h[2 * q], h[2 * q + 1] = nr, ni
            if emit_y:
                trow = slice(t * NPC, (t + 1) * NPC)
                hb[s][trow, 0:SW] = jnp.concatenate([h[0], h[2]], axis=0).astype(BF16)
                hb[s][trow, SW:2 * SW] = jnp.concatenate([h[1], h[3]], axis=0).astype(BF16)
            if emit_edge and t == 0:
                ed_re[0:8, lanes], ed_im[0:8, lanes] = h[0], h[1]
                ed_re[8:16, lanes], ed_im[8:16, lanes] = h[2], h[3]
        hs_re[0:8, lanes] = h[0]
        hs_im[0:8, lanes] = h[1]
        hs_re[8:16, lanes] = h[2]
        hs_im[8:16, lanes] = h[3]

    def contract(q):
        yq = jnp.dot(hb[q][...], ct_s[q], preferred_element_type=F32)
        ybuf[slot, :, :, q * 128:(q + 1) * 128] = flip_time(yq).reshape(TT, NPC, 128)

    expand(0)
    for s in range(NSLAB):
        if s + 1 < NSLAB:
            expand(s + 1)
        if emit_y and s >= 1:
            contract(s - 1)
        recur(s)
    if emit_y:
        contract(NSLAB - 1)
    for cp in stores(i, slot):
        cp.start()

    if emit_edge:
        @pl.when(i == 0)
        def _edge():
            er_ref[...] = ed_re[...]
            ei_ref[...] = ed_im[...]

    @pl.when(i == NCH - 1)
    def _last():
        if emit_fin:
            fr_ref[...] = hs_re[...]
            fi_ref[...] = hs_im[...]
        for cp in fetches(i, 1 - slot) + stores(i, slot) + stores(i, 1 - slot):
            cp.wait()


def _s5_scan(src, col_base, mods_t, layer, g, lam_re, lam_im, dts, bt_re, bt_im, cw_re, cw_im, *,
             jd, rev, trunk, init, h0=None, u_mode, emit_y, emit_edge, emit_fin):
    par = lambda i: (jd, 0, 0)
    in_specs = [
        pl.BlockSpec(memory_space=pl.ANY),
        pl.BlockSpec((None, None, NPC, D), lambda i: (layer, 1, trunk, 0)),
        pl.BlockSpec((None, None, NPC, D), lambda i: (layer, 0, trunk, 0)),
        pl.BlockSpec((1, D), lambda i: (0, 0)),
        pl.BlockSpec((None, 1, NST), par),
        pl.BlockSpec((None, 1, NST), par),
        pl.BlockSpec((None, 1, NST), par),
        pl.BlockSpec((None, GC, NST), par),
        pl.BlockSpec((None, GC, NST), par),
        pl.BlockSpec((None, NST, 128), par),
        pl.BlockSpec((None, NST, 128), par),
    ]
    args = [src, mods_t, mods_t, g, lam_re, lam_im, dts, bt_re, bt_im, cw_re, cw_im]
    if init == "state":
        in_specs += [pl.BlockSpec((2, None, 1, NST), lambda i: (0, jd, 0, 0))] * 2
        args += list(h0)
    elif init == "carry":
        in_specs += [pl.BlockSpec((None, NPC, NST), lambda i: (1 if rev else 0, 0, 0))] * 2
        args += list(h0)
    out_specs, out_shape = [], []
    slab = pltpu.VMEM((TT * NPC, SW), F32)
    scratch = [slab] * (2 * NSLAB) + [
        pltpu.VMEM((NSLAB, 128, 2 * SW), BF16), pltpu.VMEM((NSLAB, 2 * SW, 128), BF16),
        pltpu.VMEM((8, NST), F32), pltpu.VMEM((8, NST), F32),
        pltpu.VMEM((NPC, NST), F32), pltpu.VMEM((NPC, NST), F32),
        pltpu.VMEM((2, TT, NPC, D), F32) if u_mode == "make" else pltpu.VMEM((2, TT * NPC, D), BF16),
        pltpu.SemaphoreType.DMA((2,))]
    if emit_y:
        out_specs.append(pl.BlockSpec(memory_space=pl.ANY))
        out_shape.append(jax.ShapeDtypeStruct((NPC, TILE, D), F32))
    if u_mode == "make":
        out_specs.append(pl.BlockSpec(memory_space=pl.ANY))
        out_shape.append(jax.ShapeDtypeStruct((NCH, TT * NPC, D), BF16))
        scratch += [pltpu.VMEM((2, TT * NPC, D), BF16), pltpu.SemaphoreType.DMA((2,))]
    if emit_y:
        scratch += [pltpu.VMEM((2, TT, NPC, D), F32), pltpu.SemaphoreType.DMA((2,))]
        scratch += [pltpu.VMEM((TT * NPC, 2 * SW), BF16)] * NSLAB
    if emit_edge:
        scratch += [pltpu.VMEM((NPC, NST), F32)] * 2
    n_state_outs = 2 * (int(emit_edge) + int(emit_fin))
    out_specs += [pl.BlockSpec((NPC, NST), lambda i: (0, 0))] * n_state_outs
    out_shape += [jax.ShapeDtypeStruct((NPC, NST), F32)] * n_state_outs
    return pl.pallas_call(
        functools.partial(_s5_kernel, rev=rev, col_base=col_base, init=init, u_mode=u_mode, emit_y=emit_y,
                          emit_edge=emit_edge, emit_fin=emit_fin),
        grid=(NCH,),
        in_specs=in_specs,
        out_specs=out_specs,
        out_shape=out_shape,
        scratch_shapes=scratch,
        compiler_params=_cparams(1),
        name=("s5_scan" if emit_y else "s5_states") + ("_bwd" if rev else "_fwd") + str(trunk),
    )(*args)


def _carry_kernel(frf_ref, fif_ref, frb_ref, fib_ref, sr_ref, si_ref, lr_ref, li_ref, dt_ref,
                  or_ref, oi_ref, *, j):
    for d, (fr_ref, fi_ref) in enumerate(((frf_ref, fif_ref), (frb_ref, fib_ref))):
        jd = 2 * j + d
        ar, ai = _abar(lr_ref[jd], li_ref[jd], dt_ref[jd])
        for _ in range(8):
            ar, ai = ar * ar - ai * ai, 2.0 * ar * ai
        for b in range(2):
            order = list(range(SEG)) if d == 0 else list(range(SEG - 1, -1, -1))
            r = b * SEG + order[0]
            or_ref[d, r:r + 1, :] = sr_ref[b, jd]
            oi_ref[d, r:r + 1, :] = si_ref[b, jd]
            tr = fr_ref[r:r + 1, :]
            ti = fi_ref[r:r + 1, :]
            for s in order[1:]:
                r = b * SEG + s
                or_ref[d, r:r + 1, :] = tr
                oi_ref[d, r:r + 1, :] = ti
                tr, ti = (fr_ref[r:r + 1, :] + ar * tr - ai * ti,
                          fi_ref[r:r + 1, :] + ar * ti + ai * tr)


def _carry(frf, fif, frb, fib, st_re4, st_im4, lam_re, lam_im, dts, *, j):
    shp = jax.ShapeDtypeStruct((2, NPC, NST), F32)
    return pl.pallas_call(functools.partial(_carry_kernel, j=j), out_shape=[shp, shp], name="s5_carry")(
        frf, fif, frb, fib, st_re4, st_im4, lam_re, lam_im, dts)


GT = 2 * TILE


def _glu_kernel(xa_ref, xb_ref, yfa_ref, yba_ref, yfb_ref, ybb_ref, mod_ref, g_ref, d_ref, w_ref, b_ref,
                o_ref, wbf, ge_s, z_s):
    i = pl.program_id(0)

    @pl.when(i == 0)
    def _():
        wbf[...] = w_ref[...].astype(BF16)

    mod = mod_ref[...]
    first_prompt = i < NPC // 2

    nq = 4
    cw = D // nq
    rw = TILE // nq

    def x_rows(r0, r1, c0=0, c1=D):
        return jnp.where(first_prompt, xa_ref[r0:r1, c0:c1], xb_ref[r0:r1, c0:c1])

    def pre(r0, r1):
        h = _normmod(x_rows(r0, r1), g_ref[...], mod[1:2], mod[0:1])
        y_ssm = jnp.where(first_prompt, yfa_ref[r0:r1, :] + yba_ref[r0:r1, :], yfb_ref[r0:r1, :] + ybb_ref[r0:r1, :])
        y = y_ssm + d_ref[...] * h
        k1 = -2.0 * math.sqrt(2.0 / math.pi)
        ge = y / (1.0 + jnp.exp(y * (k1 + (k1 * 0.044715) * (y * y))))
        ge_s[r0:r1, :] = ge.astype(BF16)

    def gate(q, j):
        rows = slice(q * TILE, (q + 1) * TILE)
        for c0 in (j * cw, D + j * cw):
            z_s[rows, c0:c0 + cw] = (jnp.dot(ge_s[rows, :], wbf[:, c0:c0 + cw], preferred_element_type=F32)
                                     + b_ref[:, c0:c0 + cw])

    def post(q, j):
        rows = slice(q * TILE, (q + 1) * TILE)
        c0, c1 = j * cw, (j + 1) * cw
        out = z_s[rows, c0:c1] * _sigmoid(z_s[rows, D + c0:D + c1])
        o_ref[rows, c0:c1] = x_rows(q * TILE, (q + 1) * TILE, c0, c1) + mod[2:3, c0:c1] * out

    for j in range(nq):
        pre(j * rw, (j + 1) * rw)
    for j in range(nq):
        gate(0, j)
        pre(TILE + j * rw, TILE + (j + 1) * rw)
    for j in range(nq):
        gate(1, j)
        post(0, j)
    for j in range(nq):
        post(1, j)


def _glu(xa, xb, off_b, ys, mods, layer, g, dvec, w, b):
    nblk = NCOL // 2
    half = NPC // 2
    pa = lambda i: (jnp.minimum(i, half - 1), 0)
    pb = lambda i: (jnp.maximum(i - half, 0), 0)
    return pl.pallas_call(
        _glu_kernel,
        grid=(nblk,),
        in_specs=[pl.BlockSpec((GT, D), pa),
                  pl.BlockSpec((GT, D), lambda i: (off_b // 2 + jnp.maximum(i - half, 0), 0)),
                  pl.BlockSpec((GT, D), pa),
                  pl.BlockSpec((GT, D), pa),
                  pl.BlockSpec((GT, D), pb),
                  pl.BlockSpec((GT, D), pb),
                  pl.BlockSpec((None, None, 6, D), lambda i: (layer, _cond_of_col(2 * i), 0, 0)),
                  pl.BlockSpec((1, D), lambda i: (0, 0)),
                  pl.BlockSpec((1, D), lambda i: (0, 0)),
                  pl.BlockSpec((None, D, 2 * D), lambda i: (layer // 2, 0, 0)),
                  pl.BlockSpec((1, 2 * D), lambda i: (0, 0))],
        out_specs=pl.BlockSpec((GT, D), lambda i: (i, 0)),
        out_shape=jax.ShapeDtypeStruct((NCOL * TILE, D), F32),
        scratch_shapes=[pltpu.VMEM((D, 2 * D), BF16), pltpu.VMEM((GT, D), BF16), pltpu.VMEM((GT, 2 * D), F32)],
        compiler_params=_cparams(1),
        name="s5_glu",
    )(xa, xb, *ys, mods, g, dvec, w, b)


def _fourier_kernel(x_ref, mod_ref, g_ref, cc_ref, cs_ref, cl_ref, sl_ref, w_ref, b_ref,
                    o_ref, *scratch, scale, nseg):
    mod = mod_ref[...]

    def channel_dft():
        h = _normmod(x_ref[...], g_ref[...], mod[1:2], mod[0:1]).astype(BF16)
        xc, xs = [], []
        for q in range(FG):
            hq = h[:, q * FGC:(q + 1) * FGC]
            xc.append(jnp.dot(hq, cc_ref[...], preferred_element_type=F32).astype(BF16))
            xs.append(jnp.dot(hq, cs_ref[...], preferred_element_type=F32).astype(BF16))
        return jnp.concatenate(xc, axis=1), jnp.concatenate(xs, axis=1)

    def position_dft(xc, xs):
        f = (jnp.dot(cl_ref[...], xc, preferred_element_type=F32)
             - jnp.dot(sl_ref[...], xs, preferred_element_type=F32)) * scale
        o = jnp.dot(f.astype(BF16), w_ref[...], preferred_element_type=F32) + b_ref[...]
        o_ref[...] = x_ref[...] + mod[2:3] * o

    if nseg == 1:
        position_dft(*channel_dft())
        return

    xc_s, xs_s = scratch
    ph = pl.program_id(1)
    s = pl.program_id(2)

    @pl.when(ph == 0)
    def _():
        r0 = pl.multiple_of(s * TILE, TILE)
        xc, xs = channel_dft()
        xc_s[pl.ds(r0, TILE), :] = xc
        xs_s[pl.ds(r0, TILE), :] = xs

    @pl.when(ph == 1)
    def _():
        position_dft(xc_s[...], xs_s[...])


def _fourier(x, mods, layer, g, cc, cs, cl, sl, w, b, *, nseq, nseg, col_off, cond_off, cond_stride):
    ln = nseg * TILE
    if nseg == 1:
        grid = (nseq,)
        idx = lambda f: (lambda q: f(q, 1, 0))
        scratch = []
    else:
        grid = (nseq, 2, nseg)
        idx = lambda f: f
        scratch = [pltpu.VMEM((ln, D), BF16), pltpu.VMEM((ln, D), BF16)]
    return pl.pallas_call(
        functools.partial(_fourier_kernel, scale=1.0 / math.sqrt(ln * FGC), nseg=nseg),
        grid=grid,
        in_specs=[pl.BlockSpec((TILE, D), idx(lambda q, ph, s: (col_off + q * nseg + s, 0))),
                  pl.BlockSpec((None, None, 6, D), idx(lambda q, ph, s: (layer, cond_off + q * cond_stride, 0, 0))),
                  pl.BlockSpec((1, D), idx(lambda q, ph, s: (0, 0))),
                  pl.BlockSpec((FGC, FGC), idx(lambda q, ph, s: (0, 0))),
                  pl.BlockSpec((FGC, FGC), idx(lambda q, ph, s: (0, 0))),
                  pl.BlockSpec((TILE, ln), idx(lambda q, ph, s: (s * ph, 0))),
                  pl.BlockSpec((TILE, ln), idx(lambda q, ph, s: (s * ph, 0))),
                  pl.BlockSpec((None, D, D), idx(lambda q, ph, s: (layer // 2, 0, 0))),
                  pl.BlockSpec((1, D), idx(lambda q, ph, s: (0, 0)))],
        out_specs=pl.BlockSpec((TILE, D), idx(lambda q, ph, s: (col_off + q * nseg + s * ph, 0))),
        out_shape=jax.ShapeDtypeStruct((NCOL * TILE, D), F32),
        scratch_shapes=scratch,
        input_output_aliases={0: 0},
        compiler_params=_cparams(len(grid)),
        name="fourier%d" % nseg,
    )(x, mods, g, cc, cs, cl, sl, w, b)


@functools.lru_cache(maxsize=None)
def _dft_mats(n):
    k = np.arange(n, dtype=np.int64)
    ang = ((k[:, None] * k[None, :]) % n).astype(np.float64) * (2.0 * math.pi / n)
    return np.cos(ang).astype(np.float32), np.sin(ang).astype(np.float32)


WCH = DFF // FC


def _ffn_kernel(x_ref, mod_ref, g_ref, wup_hbm, cw_ref, cb_ref, wdn_hbm, gf_ref, *rest, final, layer):
    n_out = 2 if final else 1
    o_refs = rest[:n_out]
    wup_ref, wdn_ref, stg_u, stg_d, sem_u, sem_d = rest[n_out:]
    i = pl.program_id(0)

    @pl.when(i == 0)
    def _load_weights():
        def copies(c):
            slot = c % 2
            return (pltpu.make_async_copy(wup_hbm.at[layer, :, pl.ds(c * 2 * FC, 2 * FC)], stg_u.at[slot],
                                          sem_u.at[slot]),
                    pltpu.make_async_copy(wdn_hbm.at[layer, pl.ds(c * FC, FC), :], stg_d.at[slot],
                                          sem_d.at[slot]))

        for cp in copies(0):
            cp.start()
        for c in range(WCH):
            if c + 1 < WCH:
                for cp in copies(c + 1):
                    cp.start()
            for cp in copies(c):
                cp.wait()
            wup_ref[:, c * 2 * FC:(c + 1) * 2 * FC] = stg_u[c % 2].astype(BF16)
            wdn_ref[c * FC:(c + 1) * FC, :] = stg_d[c % 2].astype(BF16)

    x = x_ref[...]
    mod = mod_ref[...]
    h = _normmod(x, g_ref[...], mod[4:5], mod[3:4]).astype(BF16)
    sub = lax.broadcasted_iota(jnp.int32, (8, FC), 0)
    one = jnp.ones((8, FC), F32)
    first0 = (sub != 0).astype(F32)
    last0 = (sub != 7).astype(F32)
    inner_first = jnp.where(i < NPC, one, first0)
    inner_last = jnp.where(i < NPC, one, last0)
    prev_slabs = [(0, first0)] + [(r, inner_first) for r in range(GRID_W, TILE, GRID_W)]
    next_slabs = [(r - 8, inner_last) for r in range(GRID_W, TILE, GRID_W)] + [(TILE - 8, last0)]

    def mask_rows(a, slabs):
        parts, last = [], 0
        for r0, m in slabs:
            if r0 > last:
                parts.append(a[last:r0])
            parts.append(a[r0:r0 + 8] * m)
            last = r0 + 8
        if last < TILE:
            parts.append(a[last:])
        return jnp.concatenate(parts, axis=0)

    def conv(up, off):
        w = cw_ref[:, off:off + FC]
        prev = mask_rows(pltpu.roll(up, 1, 0), prev_slabs)
        nxt = mask_rows(pltpu.roll(up, TILE - 1, 0), next_slabs)
        return w[0:1] * prev + w[1:2] * up + w[2:3] * nxt + cb_ref[:, off:off + FC]

    def up_chunk(c):
        og = c * FC
        ov = DFF + c * FC
        return (jnp.dot(h, wup_ref[:, og:og + FC], preferred_element_type=F32),
                jnp.dot(h, wup_ref[:, ov:ov + FC], preferred_element_type=F32))

    def down_chunk(act, c):
        return jnp.dot(act, wdn_ref[c * FC:(c + 1) * FC, :], preferred_element_type=F32)

    nchunk = DFF // FC
    acc = jnp.zeros((TILE, D), F32)
    nxt_up = up_chunk(0)
    act = None
    for c in range(nchunk):
        cur_up = nxt_up
        if c + 1 < nchunk:
            nxt_up = up_chunk(c + 1)
        if act is not None:
            acc = acc + down_chunk(act, c - 1)
        gate = conv(cur_up[0], c * FC)
        val = conv(cur_up[1], DFF + c * FC)
        hg = 0.5 * gate
        act = ((hg + hg * jnp.tanh(hg)) * val).astype(BF16)
    acc = acc + down_chunk(act, nchunk - 1)
    y = x + mod[5:6] * acc
    if not final:
        o_refs[0][...] = y
    else:
        ms = jnp.mean(y * y, axis=-1, keepdims=True)
        y = y * lax.rsqrt(ms + EPS) * gf_ref[...]

        @pl.when(i < NPC)
        def _():
            o_refs[0][...] = y

        @pl.when(i >= NPC)
        def _():
            o_refs[1][...] = y


def _ffn(x, mods, layer, g, wup, cw, cb, wdn, gf, *, final):
    if final:
        out_specs = [pl.BlockSpec((TILE, D), lambda i: (jnp.minimum(i, NPC - 1), 0)),
                     pl.BlockSpec((TILE, D), lambda i: (jnp.maximum(i - NPC, 0), 0))]
        out_shape = [jax.ShapeDtypeStruct((NPC * TILE, D), F32)] * 2
    else:
        out_specs = pl.BlockSpec((TILE, D), lambda i: (i, 0))
        out_shape = jax.ShapeDtypeStruct((NCOL * TILE, D), F32)
    return pl.pallas_call(
        functools.partial(_ffn_kernel, final=final, layer=layer),
        grid=(NCOL,),
        in_specs=[pl.BlockSpec((TILE, D), lambda i: (i, 0)),
                  pl.BlockSpec((None, None, 6, D), lambda i: (layer, _cond_of_col(i), 0, 0)),
                  pl.BlockSpec((1, D), lambda i: (0, 0)),
                  pl.BlockSpec(memory_space=pl.ANY),
                  pl.BlockSpec((None, 3, 2 * DFF), lambda i: (layer, 0, 0)),
                  pl.BlockSpec((None, 1, 2 * DFF), lambda i: (layer, 0, 0)),
                  pl.BlockSpec(memory_space=pl.ANY),
                  pl.BlockSpec((1, D), lambda i: (0, 0))],
        out_specs=out_specs,
        out_shape=out_shape,
        scratch_shapes=[pltpu.VMEM((D, 2 * DFF), BF16), pltpu.VMEM((DFF, D), BF16),
                        pltpu.VMEM((2, D, 2 * FC), F32), pltpu.VMEM((2, FC, D), F32),
                        pltpu.SemaphoreType.DMA((2,)), pltpu.SemaphoreType.DMA((2,))],
        compiler_params=_cparams(1),
        name="ffn",
    )(x, mods, g, wup, cw, cb, wdn, gf)


def kernel(x_prompt, x_sample, state_ssm_re, state_ssm_im, c, c_ctx, w_ada, b_ada, g_mix, g_ffn,
           ssm_lam_re, ssm_lam_im, ssm_log_dt, ssm_b_re, ssm_b_im, ssm_c_re, ssm_c_im, ssm_d,
           w_glu, b_glu, w_fourier, b_fourier, w_up, conv_w, conv_b, w_down, g_final):
    nb = x_prompt.shape[0]
    xp = x_prompt.reshape(NPC * TILE, D)
    xs = x_sample.reshape(NPC * TILE, D)
    x = None

    cond8 = jnp.concatenate([c_ctx[None, :], c, jnp.zeros((5, D), F32)], axis=0)
    mods = _ada(cond8, w_ada, b_ada).reshape(DEPTH, 8, 6, D)
    col_cond = jnp.asarray([0] * NPC + [1] * SEG + [2] * SEG, jnp.int32)
    mods_t = mods.transpose(0, 2, 1, 3)[:, :, col_cond]

    wf_bf = _cast_bf16(w_fourier.reshape(2 * D, D), D).reshape(2, D, D)

    cc, cs = (jnp.asarray(m).astype(BF16) for m in _dft_mats(FGC))
    cl1, sl1 = cc, cs
    cl8, sl8 = (jnp.asarray(m).astype(BF16) for m in _dft_mats(SEG * TILE))

    njd = ssm_lam_re.shape[0] * 2
    lam_re = ssm_lam_re.reshape(njd, 1, NST)
    lam_im = ssm_lam_im.reshape(njd, 1, NST)
    dts = jnp.repeat(ssm_log_dt.reshape(njd, GS), PS, axis=-1).reshape(njd, 1, NST)
    bt_re = ssm_b_re.transpose(0, 1, 4, 2, 3).reshape(njd, GC, NST)
    bt_im = ssm_b_im.transpose(0, 1, 4, 2, 3).reshape(njd, GC, NST)
    cw_re = jnp.tile(ssm_c_re.transpose(0, 1, 2, 4, 3).reshape(njd, NST, GC), (1, 1, 8))
    cw_im = jnp.tile(ssm_c_im.transpose(0, 1, 2, 4, 3).reshape(njd, NST, GC), (1, 1, 8))
    st_re4 = state_ssm_re.reshape(2, njd, 1, NST)
    st_im4 = state_ssm_im.reshape(2, njd, 1, NST)

    edges = []
    for i in range(DEPTH):
        j = i // 2
        if i % 2 == 0:
            g = g_mix[i][None, :]
            xa, xb, off_b = (xp, xs, 0) if x is None else (x, x, NPC)
            xa3 = xa.reshape(-1, TILE, D)
            xb3 = xb.reshape(-1, TILE, D)
            par = (mods_t, i, g, lam_re, lam_im, dts, bt_re, bt_im, cw_re, cw_im)
            kw = dict(emit_edge=False, emit_fin=True, emit_y=False, trunk=1, init="state", h0=(st_re4, st_im4))
            u_s, frf, fif = _s5_scan(xb3, off_b, *par, jd=2 * j, rev=False, u_mode="make", **kw)
            frb, fib = _s5_scan(u_s, 0, *par, jd=2 * j + 1, rev=True, u_mode="load", **kw)
            carry = _carry(frf, fif, frb, fib, st_re4, st_im4, lam_re, lam_im, dts, j=j)
            kw = dict(trunk=0, init="zero", emit_y=True, emit_edge=True, emit_fin=False)
            yfa, u_p, er, ei = _s5_scan(xa3, 0, *par, jd=2 * j, rev=False, u_mode="make", **kw)
            edges += [er, ei]
            yba, er, ei = _s5_scan(u_p, 0, *par, jd=2 * j + 1, rev=True, u_mode="load", **kw)
            edges += [er, ei]
            kw = dict(trunk=1, init="carry", h0=carry, u_mode="load", emit_y=True, emit_edge=False, emit_fin=False)
            (yfb,) = _s5_scan(u_s, 0, *par, jd=2 * j, rev=False, **kw)
            (ybb,) = _s5_scan(u_s, 0, *par, jd=2 * j + 1, rev=True, **kw)
            ys = [y.reshape(NPC * TILE, D) for y in (yfa, yba, yfb, ybb)]
            x = _glu(xa, xb, off_b, ys, mods, i, g, ssm_d[j][None, :], w_glu, b_glu[j][None, :])
        else:
            g = g_mix[i][None, :]
            bf = b_fourier[j][None, :]
            x = _fourier(x, mods, i, g, cc, cs, cl1, sl1, wf_bf, bf,
                         nseq=NPC, nseg=1, col_off=0, cond_off=0, cond_stride=0)
            x = _fourier(x, mods, i, g, cc, cs, cl8, sl8, wf_bf, bf,
                         nseq=2, nseg=SEG, col_off=NPC, cond_off=1, cond_stride=1)
        x = _ffn(x, mods, i, g_ffn[i][None, :], w_up, conv_w, conv_b.reshape(DEPTH, 1, 2 * DFF),
                 w_down, g_final[None, :], final=(i == DEPTH - 1))

    y_prompt = x[0].reshape(NPC, TILE, D)
    y_sample = x[1].reshape(2, SEG * TILE, D)
    ed = jnp.stack(edges, axis=0).reshape(DEPTH // 2, 2, 2, NPC, NST)[:, :, :, :nb]
    new_re = ed[:, :, 0].transpose(2, 0, 1, 3).reshape(nb, DEPTH // 2, 2, GS, PS)
    new_im = ed[:, :, 1].transpose(2, 0, 1, 3).reshape(nb, DEPTH // 2, 2, GS, PS)
    return (y_prompt, y_sample, new_re, new_im)
```

```python
import functools
import math

import jax
import jax.numpy as jnp
import numpy as np
from jax import lax
from jax.experimental import pallas as pl
from jax.experimental.pallas import tpu as pltpu

F32 = jnp.float32
BF16 = jnp.bfloat16

D = 1024
TILE = 256
NCOL = 32
NPC = 16
SEG = 8
DEPTH = 4
GS = 64
GC = 16
PS = 64
NST = GS * PS
FG = 4
FGC = 256
DFF = 2816
EPS = 1e-6
GRID_W = 64

TT = 32
NCH = TILE // TT
SW = 512
NSLAB = NST // SW
FC = 256

VMEM_LIMIT = 56 * 1024 * 1024


def _cparams(n_axes):
    return pltpu.CompilerParams(dimension_semantics=("arbitrary",) * n_axes,
                                vmem_limit_bytes=VMEM_LIMIT)


def _normmod(x, g, sc, sh):
    ms = jnp.mean(x * x, axis=-1, keepdims=True)
    return x * lax.rsqrt(ms + EPS) * (g * (1.0 + sc)) + sh


def _sigmoid(x):
    return 1.0 / (1.0 + jnp.exp(-x))


def _cond_of_col(i):
    return jnp.where(i < NPC, 0, 1 + (i - NPC) // SEG)


def _ada_kernel(c_ref, w_ref, b_ref, o_ref):
    c = c_ref[...]
    s = (c * _sigmoid(c)).astype(BF16)
    o_ref[...] = jnp.dot(s, w_ref[...].astype(BF16), preferred_element_type=F32) + b_ref[...]


def _ada(cond8, w_ada, b_ada):
    tn = 1536
    return pl.pallas_call(
        _ada_kernel,
        grid=(DEPTH, 6 * D // tn),
        in_specs=[pl.BlockSpec((8, D), lambda l, n: (0, 0)),
                  pl.BlockSpec((None, D, tn), lambda l, n: (l, 0, n)),
                  pl.BlockSpec((None, 1, tn), lambda l, n: (l, 0, n))],
        out_specs=pl.BlockSpec((None, 8, tn), lambda l, n: (l, 0, n)),
        out_shape=jax.ShapeDtypeStruct((DEPTH, 8, 6 * D), F32),
        compiler_params=_cparams(2),
        name="ada",
    )(cond8, w_ada, b_ada.reshape(DEPTH, 1, 6 * D))


def _cast_kernel(w_ref, o_ref):
    o_ref[...] = w_ref[...].astype(BF16)


def _cast_bf16(w2d, block_rows):
    r, c = w2d.shape
    return pl.pallas_call(
        _cast_kernel,
        grid=(r // block_rows,),
        in_specs=[pl.BlockSpec((block_rows, c), lambda i: (i, 0))],
        out_specs=pl.BlockSpec((block_rows, c), lambda i: (i, 0)),
        out_shape=jax.ShapeDtypeStruct((r, c), BF16),
        compiler_params=_cparams(1),
        name="cast",
    )(w2d)


def _abar(lr, li, logdt):
    dt = jnp.exp(logdt)
    mag = jnp.exp(lr * dt)
    return mag * jnp.cos(li * dt), mag * jnp.sin(li * dt)


PK = 2
NG = TT // PK
GROWS = NG * NPC


def _nt_dot(a, b):
    return lax.dot_general(a, b, (((1,), (1,)), ((), ())), preferred_element_type=F32)


def _s5_kernel(*refs, rev, col_base, init, u_mode, emit_y, emit_edge, emit_fin):
    src_hbm, sc_ref, sh_ref, g_ref, lr_ref, li_ref, dt_ref, btr_ref, bti_ref, ctr_ref, cti_ref = refs[:11]
    pos = 11
    if init != "zero":
        h0r_ref, h0i_ref = refs[pos:pos + 2]
        pos += 2
    if emit_y:
        y_hbm = refs[pos]
        pos += 1
    if u_mode == "make":
        u_hbm = refs[pos]
        pos += 1
    if emit_edge:
        er_ref, ei_ref = refs[pos:pos + 2]
        pos += 2
    if emit_fin:
        fr_ref, fi_ref = refs[pos:pos + 2]
        pos += 2
    scratch = refs[pos:]
    d_re = scratch[0:NSLAB]
    d_im = scratch[NSLAB:2 * NSLAB]
    bp_s, ap_r, ap_i, hs_re, hs_im, inbuf, sem_in = scratch[2 * NSLAB:2 * NSLAB + 7]
    pos = 2 * NSLAB + 7
    if u_mode == "make":
        ubuf, sem_u = scratch[pos:pos + 2]
        pos += 2
    if emit_y:
        ybuf, sem_out, cp_s, ki_s = scratch[pos:pos + 4]
        hb = scratch[pos + 4:pos + 4 + NSLAB]
        pos += 4 + NSLAB
    if emit_edge:
        ed_re, ed_im, a1_r, a1_i = scratch[pos:pos + 4]

    i = pl.program_id(0)
    slot = lax.rem(i, 2)

    def chunk(i_):
        return NCH - 1 - i_ if rev else i_

    def in_copies(i_, slot_):
        if u_mode == "load":
            return [pltpu.make_async_copy(src_hbm.at[chunk(i_)], inbuf.at[slot_], sem_in.at[slot_])]
        t0 = pl.multiple_of(chunk(i_) * TT, TT)
        return [pltpu.make_async_copy(src_hbm.at[col_base + c, pl.ds(t0, TT), :],
                                      inbuf.at[slot_, :, c, :], sem_in.at[slot_]) for c in range(NPC)]

    def u_copies(i_, slot_):
        return [pltpu.make_async_copy(ubuf.at[slot_], u_hbm.at[chunk(i_)], sem_u.at[slot_])]

    def y_copies(i_, slot_):
        t0 = pl.multiple_of(chunk(i_) * TT, TT)
        return [pltpu.make_async_copy(ybuf.at[slot_, :, c, :], y_hbm.at[c, pl.ds(t0, TT), :],
                                      sem_out.at[slot_]) for c in range(NPC)]

    fetches = in_copies

    def stores(i_, slot_):
        return (y_copies(i_, slot_) if emit_y else []) + (u_copies(i_, slot_) if u_mode == "make" else [])

    @pl.when(i == 0)
    def _first_fetch():
        for cp in fetches(i, slot):
            cp.start()

    if emit_y or u_mode == "make":
        @pl.when(i >= 2)
        def _out_bufs_free():
            for cp in stores(i, slot):
                cp.wait()

    @pl.when(i == 0)
    def _prep():
        lr = lr_ref[...]
        li = li_ref[...]
        ar, ai = _abar(lr, li, dt_ref[...])
        pw = [(jnp.ones_like(ar), jnp.zeros_like(ai)), (ar, ai)]
        while len(pw) <= PK:
            pr, pi_ = pw[-1]
            pw.append((pr * ar - pi_ * ai, pr * ai + pi_ * ar))
        ap_r[...] = jnp.broadcast_to(pw[PK][0], (8, NST))
        ap_i[...] = jnp.broadcast_to(pw[PK][1], (8, NST))
        if emit_edge:
            a1_r[...] = jnp.broadcast_to(ar, (NPC, NST))
            a1_i[...] = jnp.broadcast_to(ai, (NPC, NST))
        xr = ar - 1.0
        den = lr * lr + li * li
        fr = (xr * lr + ai * li) / den
        fi = (ai * lr - xr * li) / den
        btr = btr_ref[...]
        bti = bti_ref[...]
        bbr = fr * btr - fi * bti
        bbi = fr * bti + fi * btr
        r = lax.broadcasted_iota(jnp.int32, (128, SW), 0)
        c = lax.broadcasted_iota(jnp.int32, (128, SW), 1)
        gmask = ((r >> 4) == (c >> 6)).astype(F32)

        def tiles(vr, vi, m, sign):
            sl = slice(m * SW, (m + 1) * SW)
            tr = jnp.concatenate([vr[:, sl]] * 8, axis=0) * gmask
            ti = jnp.concatenate([vi[:, sl]] * 8, axis=0) * (sign * gmask)
            return jnp.concatenate([tr, ti], axis=1).astype(BF16)

        def times(vr, vi, k):
            pr, pi_ = pw[k]
            return vr * pr - vi * pi_, vr * pi_ + vi * pr

        ab = [times(bbr, bbi, k) for k in range(PK)]
        for m in range(NSLAB):
            for p in range(PK):
                bp_s[m, p * 128:(p + 1) * 128, :] = tiles(*ab[PK - 1 - p], m, 1.0)
        if emit_y:
            ctr = ctr_ref[...]
            cti = cti_ref[...]
            ca = [times(ctr, cti, k) for k in range(PK + 1)]
            for m in range(NSLAB):
                for p in range(PK):
                    cp_s[m, p * 128:(p + 1) * 128, :] = tiles(*ca[p + 1], m, -1.0)
                c0 = tiles(*ca[0], m, -1.0)
                kt = [_nt_dot(tiles(*ab[k], m, 1.0), c0).astype(BF16) for k in range(PK)]
                zero = jnp.zeros((128, 128), BF16)
                for q in range(PK):
                    for p in range(PK):
                        ki_s[m, q * 128:(q + 1) * 128, p * 128:(p + 1) * 128] = kt[p - q] if p >= q else zero
        if init == "carry":
            hs_re[...] = h0r_ref[...]
            hs_im[...] = h0i_ref[...]
        else:
            hs_re[...] = jnp.zeros((NPC, NST), F32)
            hs_im[...] = jnp.zeros((NPC, NST), F32)
            if init == "state":
                seg0 = SEG - 1 if rev else 0
                for b in range(2):
                    row = b * SEG + seg0
                    hs_re[row:row + 1, :] = h0r_ref[b]
                    hs_im[row:row + 1, :] = h0i_ref[b]

    for cp in fetches(i, slot):
        cp.wait()
    for cp in fetches(jnp.minimum(i + 1, NCH - 1), 1 - slot):
        cp.start()

    if u_mode == "make":
        x3 = inbuf[slot]
        ms = jnp.mean(x3 * x3, axis=-1, keepdims=True)
        gain = g_ref[...] * (1.0 + sc_ref[...])
        u3 = x3 * lax.rsqrt(ms + EPS) * gain[None] + sh_ref[...][None]
        u = u3.reshape(TT * NPC, D).astype(BF16)
        ubuf[slot] = u
    else:
        u = inbuf[slot]

    def step_rows(a, b):
        t = TT - 1 - b if rev else b
        return a[t * NPC:(t + 1) * NPC]

    def packed(m):
        lanes = slice(m * 128, (m + 1) * 128)
        return jnp.concatenate(
            [jnp.concatenate([step_rows(u, PK * j + p)[:, lanes] for p in range(PK)], axis=1) for j in range(NG)],
            axis=0)

    def expand(m):
        res = jnp.dot(packed(m), bp_s[m], preferred_element_type=F32)
        d_re[m][...] = res[:, 0:SW]
        d_im[m][...] = res[:, SW:2 * SW]

    def recur(s):
        lanes = slice(s * SW, (s + 1) * SW)
        ar = ap_r[:, lanes]
        ai = ap_i[:, lanes]
        h = [hs_re[0:8, lanes], hs_im[0:8, lanes], hs_re[8:16, lanes], hs_im[8:16, lanes]]
        for j in range(NG):
            if emit_y:
                jrow = slice(j * NPC, (j + 1) * NPC)
                hb[s][jrow, 0:SW] = jnp.concatenate([h[0], h[2]], axis=0).astype(BF16)
                hb[s][jrow, SW:2 * SW] = jnp.concatenate([h[1], h[3]], axis=0).astype(BF16)
            for q in range(2):
                rows = slice(j * NPC + 8 * q, j * NPC + 8 * q + 8)
                hr, hi = h[2 * q], h[2 * q + 1]
                h[2 * q] = ar * hr - ai * hi + d_re[s][rows, :]
                h[2 * q + 1] = ar * hi + ai * hr + d_im[s][rows, :]
        hs_re[0:8, lanes] = h[0]
        hs_im[0:8, lanes] = h[1]
        hs_re[8:16, lanes] = h[2]
        hs_im[8:16, lanes] = h[3]

    def contract(m):
        yp = _nt_dot(hb[m][...], cp_s[m]) + jnp.dot(packed(m), ki_s[m], preferred_element_type=F32)
        for j in range(NG):
            for p in range(PK):
                b = PK * j + p
                t = TT - 1 - b if rev else b
                ybuf[slot, t, :, m * 128:(m + 1) * 128] = yp[j * NPC:(j + 1) * NPC, p * 128:(p + 1) * 128]

    if emit_edge:
        u0 = step_rows(u, 0)
        for m in range(NSLAB):
            lanes = slice(m * SW, (m + 1) * SW)
            bu0 = jnp.dot(u0[:, m * 128:(m + 1) * 128], bp_s[m, (PK - 1) * 128:PK * 128, :],
                          preferred_element_type=F32)
            hr, hi = hs_re[:, lanes], hs_im[:, lanes]
            ed_re[:, lanes] = a1_r[:, lanes] * hr - a1_i[:, lanes] * hi + bu0[:, 0:SW]
            ed_im[:, lanes] = a1_r[:, lanes] * hi + a1_i[:, lanes] * hr + bu0[:, SW:2 * SW]

    expand(0)
    for s in range(NSLAB):
        if s + 1 < NSLAB:
            expand(s + 1)
        if emit_y and s >= 1:
            contract(s - 1)
        recur(s)
    if emit_y:
        contract(NSLAB - 1)
    for cp in stores(i, slot):
        cp.start()

    if emit_edge:
        @pl.when(i == 0)
        def _edge():
            er_ref[...] = ed_re[...]
            ei_ref[...] = ed_im[...]

    @pl.when(i == NCH - 1)
    def _last():
        if emit_fin:
            fr_ref[...] = hs_re[...]
            fi_ref[...] = hs_im[...]
        for cp in fetches(i, 1 - slot) + stores(i, slot) + stores(i, 1 - slot):
            cp.wait()


def _s5_scan(src, col_base, mods_t, layer, g, lam_re, lam_im, dts, bt_re, bt_im, ct_re, ct_im, *,
             jd, rev, trunk, init, h0=None, u_mode, emit_y, emit_edge, emit_fin):
    par = lambda i: (jd, 0, 0)
    in_specs = [
        pl.BlockSpec(memory_space=pl.ANY),
        pl.BlockSpec((None, None, NPC, D), lambda i: (layer, 1, trunk, 0)),
        pl.BlockSpec((None, None, NPC, D), lambda i: (layer, 0, trunk, 0)),
        pl.BlockSpec((1, D), lambda i: (0, 0)),
        pl.BlockSpec((None, 1, NST), par),
        pl.BlockSpec((None, 1, NST), par),
        pl.BlockSpec((None, 1, NST), par),
        pl.BlockSpec((None, GC, NST), par),
        pl.BlockSpec((None, GC, NST), par),
        pl.BlockSpec((None, GC, NST), par),
        pl.BlockSpec((None, GC, NST), par),
    ]
    args = [src, mods_t, mods_t, g, lam_re, lam_im, dts, bt_re, bt_im, ct_re, ct_im]
    if init == "state":
        in_specs += [pl.BlockSpec((2, None, 1, NST), lambda i: (0, jd, 0, 0))] * 2
        args += list(h0)
    elif init == "carry":
        in_specs += [pl.BlockSpec((None, NPC, NST), lambda i: (1 if rev else 0, 0, 0))] * 2
        args += list(h0)
    out_specs, out_shape = [], []
    slab = pltpu.VMEM((GROWS, SW), F32)
    scratch = [slab] * (2 * NSLAB) + [
        pltpu.VMEM((NSLAB, PK * 128, 2 * SW), BF16),
        pltpu.VMEM((8, NST), F32), pltpu.VMEM((8, NST), F32),
        pltpu.VMEM((NPC, NST), F32), pltpu.VMEM((NPC, NST), F32),
        pltpu.VMEM((2, TT, NPC, D), F32) if u_mode == "make" else pltpu.VMEM((2, TT * NPC, D), BF16),
        pltpu.SemaphoreType.DMA((2,))]
    if emit_y:
        out_specs.append(pl.BlockSpec(memory_space=pl.ANY))
        out_shape.append(jax.ShapeDtypeStruct((NPC, TILE, D), F32))
    if u_mode == "make":
        out_specs.append(pl.BlockSpec(memory_space=pl.ANY))
        out_shape.append(jax.ShapeDtypeStruct((NCH, TT * NPC, D), BF16))
        scratch += [pltpu.VMEM((2, TT * NPC, D), BF16), pltpu.SemaphoreType.DMA((2,))]
    if emit_y:
        scratch += [pltpu.VMEM((2, TT, NPC, D), F32), pltpu.SemaphoreType.DMA((2,)),
                    pltpu.VMEM((NSLAB, PK * 128, 2 * SW), BF16), pltpu.VMEM((NSLAB, PK * 128, PK * 128), BF16)]
        scratch += [pltpu.VMEM((GROWS, 2 * SW), BF16)] * NSLAB
    if emit_edge:
        scratch += [pltpu.VMEM((NPC, NST), F32)] * 4
    n_state_outs = 2 * (int(emit_edge) + int(emit_fin))
    out_specs += [pl.BlockSpec((NPC, NST), lambda i: (0, 0))] * n_state_outs
    out_shape += [jax.ShapeDtypeStruct((NPC, NST), F32)] * n_state_outs
    return pl.pallas_call(
        functools.partial(_s5_kernel, rev=rev, col_base=col_base, init=init, u_mode=u_mode, emit_y=emit_y,
                          emit_edge=emit_edge, emit_fin=emit_fin),
        grid=(NCH,),
        in_specs=in_specs,
        out_specs=out_specs,
        out_shape=out_shape,
        scratch_shapes=scratch,
        compiler_params=_cparams(1),
        name=("s5_scan" if emit_y else "s5_states") + ("_bwd" if rev else "_fwd") + str(trunk),
    )(*args)


def _carry_kernel(frf_ref, fif_ref, frb_ref, fib_ref, sr_ref, si_ref, lr_ref, li_ref, dt_ref,
                  or_ref, oi_ref, *, j):
    for d, (fr_ref, fi_ref) in enumerate(((frf_ref, fif_ref), (frb_ref, fib_ref))):
        jd = 2 * j + d
        ar, ai = _abar(lr_ref[jd], li_ref[jd], dt_ref[jd])
        for _ in range(8):
            ar, ai = ar * ar - ai * ai, 2.0 * ar * ai
        for b in range(2):
            order = list(range(SEG)) if d == 0 else list(range(SEG - 1, -1, -1))
            r = b * SEG + order[0]
            or_ref[d, r:r + 1, :] = sr_ref[b, jd]
            oi_ref[d, r:r + 1, :] = si_ref[b, jd]
            tr = fr_ref[r:r + 1, :]
            ti = fi_ref[r:r + 1, :]
            for s in order[1:]:
                r = b * SEG + s
                or_ref[d, r:r + 1, :] = tr
                oi_ref[d, r:r + 1, :] = ti
                tr, ti = (fr_ref[r:r + 1, :] + ar * tr - ai * ti,
                          fi_ref[r:r + 1, :] + ar * ti + ai * tr)


def _carry(frf, fif, frb, fib, st_re4, st_im4, lam_re, lam_im, dts, *, j):
    shp = jax.ShapeDtypeStruct((2, NPC, NST), F32)
    return pl.pallas_call(functools.partial(_carry_kernel, j=j), out_shape=[shp, shp], name="s5_carry")(
        frf, fif, frb, fib, st_re4, st_im4, lam_re, lam_im, dts)


GT = 2 * TILE


def _glu_kernel(xa_ref, xb_ref, yfa_ref, yba_ref, yfb_ref, ybb_ref, mod_ref, g_ref, d_ref, w_ref, b_ref,
                o_ref, wbf, ge_s, z_s):
    i = pl.program_id(0)

    @pl.when(i == 0)
    def _():
        wbf[...] = w_ref[...].astype(BF16)

    mod = mod_ref[...]
    first_prompt = i < NPC // 2

    nq = 4
    cw = D // nq
    rw = TILE // nq

    def x_rows(r0, r1, c0=0, c1=D):
        return jnp.where(first_prompt, xa_ref[r0:r1, c0:c1], xb_ref[r0:r1, c0:c1])

    def pre(r0, r1):
        h = _normmod(x_rows(r0, r1), g_ref[...], mod[1:2], mod[0:1])
        y_ssm = jnp.where(first_prompt, yfa_ref[r0:r1, :] + yba_ref[r0:r1, :], yfb_ref[r0:r1, :] + ybb_ref[r0:r1, :])
        y = y_ssm + d_ref[...] * h
        k1 = -2.0 * math.sqrt(2.0 / math.pi)
        ge = y / (1.0 + jnp.exp(y * (k1 + (k1 * 0.044715) * (y * y))))
        ge_s[r0:r1, :] = ge.astype(BF16)

    def gate(q, j):
        rows = slice(q * TILE, (q + 1) * TILE)
        for c0 in (j * cw, D + j * cw):
            z_s[rows, c0:c0 + cw] = (jnp.dot(ge_s[rows, :], wbf[:, c0:c0 + cw], preferred_element_type=F32)
                                     + b_ref[:, c0:c0 + cw])

    def post(q, j):
        rows = slice(q * TILE, (q + 1) * TILE)
        c0, c1 = j * cw, (j + 1) * cw
        out = z_s[rows, c0:c1] * _sigmoid(z_s[rows, D + c0:D + c1])
        o_ref[rows, c0:c1] = x_rows(q * TILE, (q + 1) * TILE, c0, c1) + mod[2:3, c0:c1] * out

    for j in range(nq):
        pre(j * rw, (j + 1) * rw)
    for j in range(nq):
        gate(0, j)
        pre(TILE + j * rw, TILE + (j + 1) * rw)
    for j in range(nq):
        gate(1, j)
        post(0, j)
    for j in range(nq):
        post(1, j)


def _glu(xa, xb, off_b, ys, mods, layer, g, dvec, w, b):
    nblk = NCOL // 2
    half = NPC // 2
    pa = lambda i: (jnp.minimum(i, half - 1), 0)
    pb = lambda i: (jnp.maximum(i - half, 0), 0)
    return pl.pallas_call(
        _glu_kernel,
        grid=(nblk,),
        in_specs=[pl.BlockSpec((GT, D), pa),
                  pl.BlockSpec((GT, D), lambda i: (off_b // 2 + jnp.maximum(i - half, 0), 0)),
                  pl.BlockSpec((GT, D), pa),
                  pl.BlockSpec((GT, D), pa),
                  pl.BlockSpec((GT, D), pb),
                  pl.BlockSpec((GT, D), pb),
                  pl.BlockSpec((None, None, 6, D), lambda i: (layer, _cond_of_col(2 * i), 0, 0)),
                  pl.BlockSpec((1, D), lambda i: (0, 0)),
                  pl.BlockSpec((1, D), lambda i: (0, 0)),
                  pl.BlockSpec((None, D, 2 * D), lambda i: (layer // 2, 0, 0)),
                  pl.BlockSpec((1, 2 * D), lambda i: (0, 0))],
        out_specs=pl.BlockSpec((GT, D), lambda i: (i, 0)),
        out_shape=jax.ShapeDtypeStruct((NCOL * TILE, D), F32),
        scratch_shapes=[pltpu.VMEM((D, 2 * D), BF16), pltpu.VMEM((GT, D), BF16), pltpu.VMEM((GT, 2 * D), F32)],
        compiler_params=_cparams(1),
        name="s5_glu",
    )(xa, xb, *ys, mods, g, dvec, w, b)


def _fourier_kernel(x_ref, mod_ref, g_ref, cc_ref, cs_ref, cl_ref, sl_ref, w_ref, b_ref,
                    o_ref, *scratch, scale, nseg):
    mod = mod_ref[...]

    def channel_dft():
        h = _normmod(x_ref[...], g_ref[...], mod[1:2], mod[0:1]).astype(BF16)
        xc, xs = [], []
        for q in range(FG):
            hq = h[:, q * FGC:(q + 1) * FGC]
            xc.append(jnp.dot(hq, cc_ref[...], preferred_element_type=F32).astype(BF16))
            xs.append(jnp.dot(hq, cs_ref[...], preferred_element_type=F32).astype(BF16))
        return jnp.concatenate(xc, axis=1), jnp.concatenate(xs, axis=1)

    def position_dft(xc, xs):
        f = (jnp.dot(cl_ref[...], xc, preferred_element_type=F32)
             - jnp.dot(sl_ref[...], xs, preferred_element_type=F32)) * scale
        o = jnp.dot(f.astype(BF16), w_ref[...], preferred_element_type=F32) + b_ref[...]
        o_ref[...] = x_ref[...] + mod[2:3] * o

    if nseg == 1:
        position_dft(*channel_dft())
        return

    xc_s, xs_s = scratch
    ph = pl.program_id(1)
    s = pl.program_id(2)

    @pl.when(ph == 0)
    def _():
        r0 = pl.multiple_of(s * TILE, TILE)
        xc, xs = channel_dft()
        xc_s[pl.ds(r0, TILE), :] = xc
        xs_s[pl.ds(r0, TILE), :] = xs

    @pl.when(ph == 1)
    def _():
        position_dft(xc_s[...], xs_s[...])


def _fourier(x, mods, layer, g, cc, cs, cl, sl, w, b, *, nseq, nseg, col_off, cond_off, cond_stride):
    ln = nseg * TILE
    if nseg == 1:
        grid = (nseq,)
        idx = lambda f: (lambda q: f(q, 1, 0))
        scratch = []
    else:
        grid = (nseq, 2, nseg)
        idx = lambda f: f
        scratch = [pltpu.VMEM((ln, D), BF16), pltpu.VMEM((ln, D), BF16)]
    return pl.pallas_call(
        functools.partial(_fourier_kernel, scale=1.0 / math.sqrt(ln * FGC), nseg=nseg),
        grid=grid,
        in_specs=[pl.BlockSpec((TILE, D), idx(lambda q, ph, s: (col_off + q * nseg + s, 0))),
                  pl.BlockSpec((None, None, 6, D), idx(lambda q, ph, s: (layer, cond_off + q * cond_stride, 0, 0))),
                  pl.BlockSpec((1, D), idx(lambda q, ph, s: (0, 0))),
                  pl.BlockSpec((FGC, FGC), idx(lambda q, ph, s: (0, 0))),
                  pl.BlockSpec((FGC, FGC), idx(lambda q, ph, s: (0, 0))),
                  pl.BlockSpec((TILE, ln), idx(lambda q, ph, s: (s * ph, 0))),
                  pl.BlockSpec((TILE, ln), idx(lambda q, ph, s: (s * ph, 0))),
                  pl.BlockSpec((None, D, D), idx(lambda q, ph, s: (layer // 2, 0, 0))),
                  pl.BlockSpec((1, D), idx(lambda q, ph, s: (0, 0)))],
        out_specs=pl.BlockSpec((TILE, D), idx(lambda q, ph, s: (col_off + q * nseg + s * ph, 0))),
        out_shape=jax.ShapeDtypeStruct((NCOL * TILE, D), F32),
        scratch_shapes=scratch,
        input_output_aliases={0: 0},
        compiler_params=_cparams(len(grid)),
        name="fourier%d" % nseg,
    )(x, mods, g, cc, cs, cl, sl, w, b)


@functools.lru_cache(maxsize=None)
def _dft_mats(n):
    k = np.arange(n, dtype=np.int64)
    ang = ((k[:, None] * k[None, :]) % n).astype(np.float64) * (2.0 * math.pi / n)
    return np.cos(ang).astype(np.float32), np.sin(ang).astype(np.float32)


WCH = DFF // FC


def _ffn_kernel(x_ref, mod_ref, g_ref, wup_hbm, cw_ref, cb_ref, wdn_hbm, gf_ref, *rest, final, layer):
    n_out = 2 if final else 1
    o_refs = rest[:n_out]
    wup_ref, wdn_ref, stg_u, stg_d, sem_u, sem_d = rest[n_out:]
    i = pl.program_id(0)

    @pl.when(i == 0)
    def _load_weights():
        def copies(c):
            slot = c % 2
            return (pltpu.make_async_copy(wup_hbm.at[layer, :, pl.ds(c * 2 * FC, 2 * FC)], stg_u.at[slot],
                                          sem_u.at[slot]),
                    pltpu.make_async_copy(wdn_hbm.at[layer, pl.ds(c * FC, FC), :], stg_d.at[slot],
                                          sem_d.at[slot]))

        for cp in copies(0):
            cp.start()
        for c in range(WCH):
            if c + 1 < WCH:
                for cp in copies(c + 1):
                    cp.start()
            for cp in copies(c):
                cp.wait()
            wup_ref[:, c * 2 * FC:(c + 1) * 2 * FC] = stg_u[c % 2].astype(BF16)
            wdn_ref[c * FC:(c + 1) * FC, :] = stg_d[c % 2].astype(BF16)

    x = x_ref[...]
    mod = mod_ref[...]
    h = _normmod(x, g_ref[...], mod[4:5], mod[3:4]).astype(BF16)
    sub = lax.broadcasted_iota(jnp.int32, (8, FC), 0)
    one = jnp.ones((8, FC), F32)
    first0 = (sub != 0).astype(F32)
    last0 = (sub != 7).astype(F32)
    inner_first = jnp.where(i < NPC, one, first0)
    inner_last = jnp.where(i < NPC, one, last0)
    prev_slabs = [(0, first0)] + [(r, inner_first) for r in range(GRID_W, TILE, GRID_W)]
    next_slabs = [(r - 8, inner_last) for r in range(GRID_W, TILE, GRID_W)] + [(TILE - 8, last0)]

    def mask_rows(a, slabs):
        parts, last = [], 0
        for r0, m in slabs:
            if r0 > last:
                parts.append(a[last:r0])
            parts.append(a[r0:r0 + 8] * m)
            last = r0 + 8
        if last < TILE:
            parts.append(a[last:])
        return jnp.concatenate(parts, axis=0)

    def conv(up, off):
        w = cw_ref[:, off:off + FC]
        prev = mask_rows(pltpu.roll(up, 1, 0), prev_slabs)
        nxt = mask_rows(pltpu.roll(up, TILE - 1, 0), next_slabs)
        return w[0:1] * prev + w[1:2] * up + w[2:3] * nxt + cb_ref[:, off:off + FC]

    def up_chunk(c):
        og = c * FC
        ov = DFF + c * FC
        return (jnp.dot(h, wup_ref[:, og:og + FC], preferred_element_type=F32),
                jnp.dot(h, wup_ref[:, ov:ov + FC], preferred_element_type=F32))

    def down_chunk(act, c):
        return jnp.dot(act, wdn_ref[c * FC:(c + 1) * FC, :], preferred_element_type=F32)

    nchunk = DFF // FC
    acc = jnp.zeros((TILE, D), F32)
    nxt_up = up_chunk(0)
    act = None
    for c in range(nchunk):
        cur_up = nxt_up
        if c + 1 < nchunk:
            nxt_up = up_chunk(c + 1)
        if act is not None:
            acc = acc + down_chunk(act, c - 1)
        gate = conv(cur_up[0], c * FC)
        val = conv(cur_up[1], DFF + c * FC)
        hg = 0.5 * gate
        act = ((hg + hg * jnp.tanh(hg)) * val).astype(BF16)
    acc = acc + down_chunk(act, nchunk - 1)
    y = x + mod[5:6] * acc
    if not final:
        o_refs[0][...] = y
    else:
        ms = jnp.mean(y * y, axis=-1, keepdims=True)
        y = y * lax.rsqrt(ms + EPS) * gf_ref[...]

        @pl.when(i < NPC)
        def _():
            o_refs[0][...] = y

        @pl.when(i >= NPC)
        def _():
            o_refs[1][...] = y


def _ffn(x, mods, layer, g, wup, cw, cb, wdn, gf, *, final):
    if final:
        out_specs = [pl.BlockSpec((TILE, D), lambda i: (jnp.minimum(i, NPC - 1), 0)),
                     pl.BlockSpec((TILE, D), lambda i: (jnp.maximum(i - NPC, 0), 0))]
        out_shape = [jax.ShapeDtypeStruct((NPC * TILE, D), F32)] * 2
    else:
        out_specs = pl.BlockSpec((TILE, D), lambda i: (i, 0))
        out_shape = jax.ShapeDtypeStruct((NCOL * TILE, D), F32)
    return pl.pallas_call(
        functools.partial(_ffn_kernel, final=final, layer=layer),
        grid=(NCOL,),
        in_specs=[pl.BlockSpec((TILE, D), lambda i: (i, 0)),
                  pl.BlockSpec((None, None, 6, D), lambda i: (layer, _cond_of_col(i), 0, 0)),
                  pl.BlockSpec((1, D), lambda i: (0, 0)),
                  pl.BlockSpec(memory_space=pl.ANY),
                  pl.BlockSpec((None, 3, 2 * DFF), lambda i: (layer, 0, 0)),
                  pl.BlockSpec((None, 1, 2 * DFF), lambda i: (layer, 0, 0)),
                  pl.BlockSpec(memory_space=pl.ANY),
                  pl.BlockSpec((1, D), lambda i: (0, 0))],
        out_specs=out_specs,
        out_shape=out_shape,
        scratch_shapes=[pltpu.VMEM((D, 2 * DFF), BF16), pltpu.VMEM((DFF, D), BF16),
                        pltpu.VMEM((2, D, 2 * FC), F32), pltpu.VMEM((2, FC, D), F32),
                        pltpu.SemaphoreType.DMA((2,)), pltpu.SemaphoreType.DMA((2,))],
        compiler_params=_cparams(1),
        name="ffn",
    )(x, mods, g, wup, cw, cb, wdn, gf)


def kernel(x_prompt, x_sample, state_ssm_re, state_ssm_im, c, c_ctx, w_ada, b_ada, g_mix, g_ffn,
           ssm_lam_re, ssm_lam_im, ssm_log_dt, ssm_b_re, ssm_b_im, ssm_c_re, ssm_c_im, ssm_d,
           w_glu, b_glu, w_fourier, b_fourier, w_up, conv_w, conv_b, w_down, g_final):
    nb = x_prompt.shape[0]
    xp = x_prompt.reshape(NPC * TILE, D)
    xs = x_sample.reshape(NPC * TILE, D)
    x = None

    cond8 = jnp.concatenate([c_ctx[None, :], c, jnp.zeros((5, D), F32)], axis=0)
    mods = _ada(cond8, w_ada, b_ada).reshape(DEPTH, 8, 6, D)
    col_cond = jnp.asarray([0] * NPC + [1] * SEG + [2] * SEG, jnp.int32)
    mods_t = mods.transpose(0, 2, 1, 3)[:, :, col_cond]

    wf_bf = _cast_bf16(w_fourier.reshape(2 * D, D), D).reshape(2, D, D)

    cc, cs = (jnp.asarray(m).astype(BF16) for m in _dft_mats(FGC))
    cl1, sl1 = cc, cs
    cl8, sl8 = (jnp.asarray(m).astype(BF16) for m in _dft_mats(SEG * TILE))

    njd = ssm_lam_re.shape[0] * 2
    lam_re = ssm_lam_re.reshape(njd, 1, NST)
    lam_im = ssm_lam_im.reshape(njd, 1, NST)
    dts = jnp.repeat(ssm_log_dt.reshape(njd, GS), PS, axis=-1).reshape(njd, 1, NST)
    bt_re = ssm_b_re.transpose(0, 1, 4, 2, 3).reshape(njd, GC, NST)
    bt_im = ssm_b_im.transpose(0, 1, 4, 2, 3).reshape(njd, GC, NST)
    ct_re = ssm_c_re.transpose(0, 1, 3, 2, 4).reshape(njd, GC, NST)
    ct_im = ssm_c_im.transpose(0, 1, 3, 2, 4).reshape(njd, GC, NST)
    st_re4 = state_ssm_re.reshape(2, njd, 1, NST)
    st_im4 = state_ssm_im.reshape(2, njd, 1, NST)

    edges = []
    for i in range(DEPTH):
        j = i // 2
        if i % 2 == 0:
            g = g_mix[i][None, :]
            xa, xb, off_b = (xp, xs, 0) if x is None else (x, x, NPC)
            xa3 = xa.reshape(-1, TILE, D)
            xb3 = xb.reshape(-1, TILE, D)
            par = (mods_t, i, g, lam_re, lam_im, dts, bt_re, bt_im, ct_re, ct_im)
            kw = dict(emit_edge=False, emit_fin=True, emit_y=False, trunk=1, init="state", h0=(st_re4, st_im4))
            u_s, frf, fif = _s5_scan(xb3, off_b, *par, jd=2 * j, rev=False, u_mode="make", **kw)
            frb, fib = _s5_scan(u_s, 0, *par, jd=2 * j + 1, rev=True, u_mode="load", **kw)
            carry = _carry(frf, fif, frb, fib, st_re4, st_im4, lam_re, lam_im, dts, j=j)
            kw = dict(trunk=0, init="zero", emit_y=True, emit_edge=True, emit_fin=False)
            yfa, u_p, er, ei = _s5_scan(xa3, 0, *par, jd=2 * j, rev=False, u_mode="make", **kw)
            edges += [er, ei]
            yba, er, ei = _s5_scan(u_p, 0, *par, jd=2 * j + 1, rev=True, u_mode="load", **kw)
            edges += [er, ei]
            kw = dict(trunk=1, init="carry", h0=carry, u_mode="load", emit_y=True, emit_edge=False, emit_fin=False)
            (yfb,) = _s5_scan(u_s, 0, *par, jd=2 * j, rev=False, **kw)
            (ybb,) = _s5_scan(u_s, 0, *par, jd=2 * j + 1, rev=True, **kw)
            ys = [y.reshape(NPC * TILE, D) for y in (yfa, yba, yfb, ybb)]
            x = _glu(xa, xb, off_b, ys, mods, i, g, ssm_d[j][None, :], w_glu, b_glu[j][None, :])
        else:
            g = g_mix[i][None, :]
            bf = b_fourier[j][None, :]
            x = _fourier(x, mods, i, g, cc, cs, cl1, sl1, wf_bf, bf,
                         nseq=NPC, nseg=1, col_off=0, cond_off=0, cond_stride=0)
            x = _fourier(x, mods, i, g, cc, cs, cl8, sl8, wf_bf, bf,
                         nseq=2, nseg=SEG, col_off=NPC, cond_off=1, cond_stride=1)
        x = _ffn(x, mods, i, g_ffn[i][None, :], w_up, conv_w, conv_b.reshape(DEPTH, 1, 2 * DFF),
                 w_down, g_final[None, :], final=(i == DEPTH - 1))

    y_prompt = x[0].reshape(NPC, TILE, D)
    y_sample = x[1].reshape(2, SEG * TILE, D)
    ed = jnp.stack(edges, axis=0).reshape(DEPTH // 2, 2, 2, NPC, NST)[:, :, :, :nb]
    new_re = ed[:, :, 0].transpose(2, 0, 1, 3).reshape(nb, DEPTH // 2, 2, GS, PS)
    new_im = ed[:, :, 1].transpose(2, 0, 1, 3).reshape(nb, DEPTH // 2, 2, GS, PS)
    return (y_prompt, y_sample, new_re, new_im)
```

```python
import functools
import math

import jax
import jax.numpy as jnp
import numpy as np
from jax import lax
from jax.experimental import pallas as pl
from jax.experimental.pallas import tpu as pltpu

F32 = jnp.float32
BF16 = jnp.bfloat16

D = 1024
TILE = 256
NCOL = 32
NPC = 16
SEG = 8
DEPTH = 4
GS = 64
GC = 16
PS = 64
NST = GS * PS
FG = 4
FGC = 256
DFF = 2816
EPS = 1e-6
GRID_W = 64

TT = 32
NCH = TILE // TT
SW = 512
NSLAB = NST // SW
FC = 256

VMEM_LIMIT = 56 * 1024 * 1024


def _cparams(n_axes):
    return pltpu.CompilerParams(dimension_semantics=("arbitrary",) * n_axes,
                                vmem_limit_bytes=VMEM_LIMIT)


def _normmod(x, g, sc, sh):
    ms = jnp.mean(x * x, axis=-1, keepdims=True)
    return x * lax.rsqrt(ms + EPS) * (g * (1.0 + sc)) + sh


def _sigmoid(x):
    return 1.0 / (1.0 + jnp.exp(-x))


def _cond_of_col(i):
    return jnp.where(i < NPC, 0, 1 + (i - NPC) // SEG)


def _ada_kernel(c_ref, w_ref, b_ref, o_ref):
    c = c_ref[...]
    s = (c * _sigmoid(c)).astype(BF16)
    o_ref[...] = jnp.dot(s, w_ref[...].astype(BF16), preferred_element_type=F32) + b_ref[...]


def _ada(cond8, w_ada, b_ada):
    tn = 1536
    return pl.pallas_call(
        _ada_kernel,
        grid=(DEPTH, 6 * D // tn),
        in_specs=[pl.BlockSpec((8, D), lambda l, n: (0, 0)),
                  pl.BlockSpec((None, D, tn), lambda l, n: (l, 0, n)),
                  pl.BlockSpec((None, 1, tn), lambda l, n: (l, 0, n))],
        out_specs=pl.BlockSpec((None, 8, tn), lambda l, n: (l, 0, n)),
        out_shape=jax.ShapeDtypeStruct((DEPTH, 8, 6 * D), F32),
        compiler_params=_cparams(2),
        name="ada",
    )(cond8, w_ada, b_ada.reshape(DEPTH, 1, 6 * D))


def _cast_kernel(w_ref, o_ref):
    o_ref[...] = w_ref[...].astype(BF16)


def _cast_bf16(w2d, block_rows):
    r, c = w2d.shape
    return pl.pallas_call(
        _cast_kernel,
        grid=(r // block_rows,),
        in_specs=[pl.BlockSpec((block_rows, c), lambda i: (i, 0))],
        out_specs=pl.BlockSpec((block_rows, c), lambda i: (i, 0)),
        out_shape=jax.ShapeDtypeStruct((r, c), BF16),
        compiler_params=_cparams(1),
        name="cast",
    )(w2d)


def _abar(lr, li, logdt):
    dt = jnp.exp(logdt)
    mag = jnp.exp(lr * dt)
    return mag * jnp.cos(li * dt), mag * jnp.sin(li * dt)


PK = 2
NG = TT // PK
GROWS = NG * NPC


def _nt_dot(a, b):
    return lax.dot_general(a, b, (((1,), (1,)), ((), ())), preferred_element_type=F32)


def _s5_kernel(*refs, rev, col_base, init, u_mode, emit_y, emit_edge, emit_fin):
    src_hbm, sc_ref, sh_ref, g_ref, lr_ref, li_ref, dt_ref, btr_ref, bti_ref, ctr_ref, cti_ref = refs[:11]
    pos = 11
    if init != "zero":
        h0r_ref, h0i_ref = refs[pos:pos + 2]
        pos += 2
    if emit_y:
        y_hbm = refs[pos]
        pos += 1
    if u_mode == "make":
        u_hbm = refs[pos]
        pos += 1
    if emit_edge:
        er_ref, ei_ref = refs[pos:pos + 2]
        pos += 2
    if emit_fin:
        fr_ref, fi_ref = refs[pos:pos + 2]
        pos += 2
    scratch = refs[pos:]
    d_re = scratch[0:NSLAB]
    d_im = scratch[NSLAB:2 * NSLAB]
    bp_s, ap_r, ap_i, hs_re, hs_im, inbuf, sem_in = scratch[2 * NSLAB:2 * NSLAB + 7]
    pos = 2 * NSLAB + 7
    if u_mode == "make":
        ubuf, sem_u = scratch[pos:pos + 2]
        pos += 2
    if emit_y:
        ybuf, sem_out, cp_s, ki_s = scratch[pos:pos + 4]
        hb = scratch[pos + 4:pos + 4 + NSLAB]
        pos += 4 + NSLAB
    if emit_edge:
        ed_re, ed_im, a1_r, a1_i = scratch[pos:pos + 4]

    i = pl.program_id(0)
    slot = lax.rem(i, 2)

    def chunk(i_):
        return NCH - 1 - i_ if rev else i_

    def in_copies(i_, slot_):
        if u_mode == "load":
            return [pltpu.make_async_copy(src_hbm.at[chunk(i_)], inbuf.at[slot_], sem_in.at[slot_])]
        t0 = pl.multiple_of(chunk(i_) * TT, TT)
        return [pltpu.make_async_copy(src_hbm.at[col_base + c, pl.ds(t0, TT), :],
                                      inbuf.at[slot_, :, c, :], sem_in.at[slot_]) for c in range(NPC)]

    def u_copies(i_, slot_):
        return [pltpu.make_async_copy(ubuf.at[slot_], u_hbm.at[chunk(i_)], sem_u.at[slot_])]

    def y_copies(i_, slot_):
        t0 = pl.multiple_of(chunk(i_) * TT, TT)
        return [pltpu.make_async_copy(ybuf.at[slot_, :, c, :], y_hbm.at[c, pl.ds(t0, TT), :],
                                      sem_out.at[slot_]) for c in range(NPC)]

    fetches = in_copies

    def stores(i_, slot_):
        return (y_copies(i_, slot_) if emit_y else []) + (u_copies(i_, slot_) if u_mode == "make" else [])

    @pl.when(i == 0)
    def _first_fetch():
        for cp in fetches(i, slot):
            cp.start()

    if emit_y or u_mode == "make":
        @pl.when(i >= 2)
        def _out_bufs_free():
            for cp in stores(i, slot):
                cp.wait()

    @pl.when(i == 0)
    def _prep():
        lr = lr_ref[...]
        li = li_ref[...]
        ar, ai = _abar(lr, li, dt_ref[...])
        pw = [(jnp.ones_like(ar), jnp.zeros_like(ai)), (ar, ai)]
        while len(pw) <= PK:
            pr, pi_ = pw[-1]
            pw.append((pr * ar - pi_ * ai, pr * ai + pi_ * ar))
        ap_r[...] = jnp.broadcast_to(pw[PK][0], (8, NST))
        ap_i[...] = jnp.broadcast_to(pw[PK][1], (8, NST))
        if emit_edge:
            a1_r[...] = jnp.broadcast_to(ar, (NPC, NST))
            a1_i[...] = jnp.broadcast_to(ai, (NPC, NST))
        xr = ar - 1.0
        den = lr * lr + li * li
        fr = (xr * lr + ai * li) / den
        fi = (ai * lr - xr * li) / den
        btr = btr_ref[...]
        bti = bti_ref[...]
        bbr = fr * btr - fi * bti
        bbi = fr * bti + fi * btr
        r = lax.broadcasted_iota(jnp.int32, (128, SW), 0)
        c = lax.broadcasted_iota(jnp.int32, (128, SW), 1)
        gmask = ((r >> 4) == (c >> 6)).astype(F32)

        def tiles(vr, vi, m, sign):
            sl = slice(m * SW, (m + 1) * SW)
            tr = jnp.concatenate([vr[:, sl]] * 8, axis=0) * gmask
            ti = jnp.concatenate([vi[:, sl]] * 8, axis=0) * (sign * gmask)
            return jnp.concatenate([tr, ti], axis=1).astype(BF16)

        def times(vr, vi, k):
            pr, pi_ = pw[k]
            return vr * pr - vi * pi_, vr * pi_ + vi * pr

        ab = [times(bbr, bbi, k) for k in range(PK)]
        for m in range(NSLAB):
            for p in range(PK):
                bp_s[m, p * 128:(p + 1) * 128, :] = tiles(*ab[PK - 1 - p], m, 1.0)
        if emit_y:
            ctr = ctr_ref[...]
            cti = cti_ref[...]
            ca = [times(ctr, cti, k) for k in range(PK + 1)]
            for m in range(NSLAB):
                for p in range(PK):
                    cp_s[m, p * 128:(p + 1) * 128, :] = tiles(*ca[p + 1], m, -1.0)
                c0 = tiles(*ca[0], m, -1.0)
                kt = [_nt_dot(tiles(*ab[k], m, 1.0), c0).astype(BF16) for k in range(PK)]
                zero = jnp.zeros((128, 128), BF16)
                for q in range(PK):
                    for p in range(PK):
                        ki_s[m, q * 128:(q + 1) * 128, p * 128:(p + 1) * 128] = kt[p - q] if p >= q else zero
        if init == "carry":
            hs_re[...] = h0r_ref[...]
            hs_im[...] = h0i_ref[...]
        else:
            hs_re[...] = jnp.zeros((NPC, NST), F32)
            hs_im[...] = jnp.zeros((NPC, NST), F32)
            if init == "state":
                seg0 = SEG - 1 if rev else 0
                for b in range(2):
                    row = b * SEG + seg0
                    hs_re[row:row + 1, :] = h0r_ref[b]
                    hs_im[row:row + 1, :] = h0i_ref[b]

    for cp in fetches(i, slot):
        cp.wait()
    for cp in fetches(jnp.minimum(i + 1, NCH - 1), 1 - slot):
        cp.start()

    if u_mode == "make":
        x3 = inbuf[slot]
        ms = jnp.mean(x3 * x3, axis=-1, keepdims=True)
        gain = g_ref[...] * (1.0 + sc_ref[...])
        u3 = x3 * lax.rsqrt(ms + EPS) * gain[None] + sh_ref[...][None]
        u = u3.reshape(TT * NPC, D).astype(BF16)
        ubuf[slot] = u
    else:
        u = inbuf[slot]

    def step_rows(a, b):
        t = TT - 1 - b if rev else b
        return a[t * NPC:(t + 1) * NPC]

    def packed(m):
        lanes = slice(m * 128, (m + 1) * 128)
        return jnp.concatenate(
            [jnp.concatenate([step_rows(u, PK * j + p)[:, lanes] for p in range(PK)], axis=1) for j in range(NG)],
            axis=0)

    def expand(m):
        res = jnp.dot(packed(m), bp_s[m], preferred_element_type=F32)
        d_re[m][...] = res[:, 0:SW]
        d_im[m][...] = res[:, SW:2 * SW]

    def recur(s):
        lanes = slice(s * SW, (s + 1) * SW)
        ar = ap_r[:, lanes]
        ai = ap_i[:, lanes]
        h = [hs_re[0:8, lanes], hs_im[0:8, lanes], hs_re[8:16, lanes], hs_im[8:16, lanes]]
        for j in range(NG):
            if emit_y:
                jrow = slice(j * NPC, (j + 1) * NPC)
                hb[s][jrow, 0:SW] = jnp.concatenate([h[0], h[2]], axis=0).astype(BF16)
                hb[s][jrow, SW:2 * SW] = jnp.concatenate([h[1], h[3]], axis=0).astype(BF16)
            for q in range(2):
                rows = slice(j * NPC + 8 * q, j * NPC + 8 * q + 8)
                hr, hi = h[2 * q], h[2 * q + 1]
                h[2 * q] = ar * hr - ai * hi + d_re[s][rows, :]
                h[2 * q + 1] = ar * hi + ai * hr + d_im[s][rows, :]
        hs_re[0:8, lanes] = h[0]
        hs_im[0:8, lanes] = h[1]
        hs_re[8:16, lanes] = h[2]
        hs_im[8:16, lanes] = h[3]

    def contract(m):
        yp = _nt_dot(hb[m][...], cp_s[m]) + jnp.dot(packed(m), ki_s[m], preferred_element_type=F32)
        for j in range(NG):
            for p in range(PK):
                b = PK * j + p
                t = TT - 1 - b if rev else b
                ybuf[slot, t, :, m * 128:(m + 1) * 128] = yp[j * NPC:(j + 1) * NPC, p * 128:(p + 1) * 128]

    if emit_edge:
        u0 = step_rows(u, 0)
        for m in range(NSLAB):
            lanes = slice(m * SW, (m + 1) * SW)
            bu0 = jnp.dot(u0[:, m * 128:(m + 1) * 128], bp_s[m, (PK - 1) * 128:PK * 128, :],
                          preferred_element_type=F32)
            hr, hi = hs_re[:, lanes], hs_im[:, lanes]
            ed_re[:, lanes] = a1_r[:, lanes] * hr - a1_i[:, lanes] * hi + bu0[:, 0:SW]
            ed_im[:, lanes] = a1_r[:, lanes] * hi + a1_i[:, lanes] * hr + bu0[:, SW:2 * SW]

    expand(0)
    for s in range(NSLAB):
        if s + 1 < NSLAB:
            expand(s + 1)
        if emit_y and s >= 1:
            contract(s - 1)
        recur(s)
    if emit_y:
        contract(NSLAB - 1)
    for cp in stores(i, slot):
        cp.start()

    if emit_edge:
        @pl.when(i == 0)
        def _edge():
            er_ref[...] = ed_re[...]
            ei_ref[...] = ed_im[...]

    @pl.when(i == NCH - 1)
    def _last():
        if emit_fin:
            fr_ref[...] = hs_re[...]
            fi_ref[...] = hs_im[...]
        for cp in fetches(i, 1 - slot) + stores(i, slot) + stores(i, 1 - slot):
            cp.wait()


def _s5_scan(src, col_base, mods_t, layer, g, lam_re, lam_im, dts, bt_re, bt_im, ct_re, ct_im, *,
             jd, rev, trunk, init, h0=None, u_mode, emit_y, emit_edge, emit_fin):
    par = lambda i: (jd, 0, 0)
    in_specs = [
        pl.BlockSpec(memory_space=pl.ANY),
        pl.BlockSpec((None, None, NPC, D), lambda i: (layer, 1, trunk, 0)),
        pl.BlockSpec((None, None, NPC, D), lambda i: (layer, 0, trunk, 0)),
        pl.BlockSpec((1, D), lambda i: (0, 0)),
        pl.BlockSpec((None, 1, NST), par),
        pl.BlockSpec((None, 1, NST), par),
        pl.BlockSpec((None, 1, NST), par),
        pl.BlockSpec((None, GC, NST), par),
        pl.BlockSpec((None, GC, NST), par),
        pl.BlockSpec((None, GC, NST), par),
        pl.BlockSpec((None, GC, NST), par),
    ]
    args = [src, mods_t, mods_t, g, lam_re, lam_im, dts, bt_re, bt_im, ct_re, ct_im]
    if init == "state":
        in_specs += [pl.BlockSpec((2, None, 1, NST), lambda i: (0, jd, 0, 0))] * 2
        args += list(h0)
    elif init == "carry":
        in_specs += [pl.BlockSpec((None, NPC, NST), lambda i: (1 if rev else 0, 0, 0))] * 2
        args += list(h0)
    out_specs, out_shape = [], []
    slab = pltpu.VMEM((GROWS, SW), F32)
    scratch = [slab] * (2 * NSLAB) + [
        pltpu.VMEM((NSLAB, PK * 128, 2 * SW), BF16),
        pltpu.VMEM((8, NST), F32), pltpu.VMEM((8, NST), F32),
        pltpu.VMEM((NPC, NST), F32), pltpu.VMEM((NPC, NST), F32),
        pltpu.VMEM((2, TT, NPC, D), F32) if u_mode == "make" else pltpu.VMEM((2, TT * NPC, D), BF16),
        pltpu.SemaphoreType.DMA((2,))]
    if emit_y:
        out_specs.append(pl.BlockSpec(memory_space=pl.ANY))
        out_shape.append(jax.ShapeDtypeStruct((NPC, TILE, D), F32))
    if u_mode == "make":
        out_specs.append(pl.BlockSpec(memory_space=pl.ANY))
        out_shape.append(jax.ShapeDtypeStruct((NCH, TT * NPC, D), BF16))
        scratch += [pltpu.VMEM((2, TT * NPC, D), BF16), pltpu.SemaphoreType.DMA((2,))]
    if emit_y:
        scratch += [pltpu.VMEM((2, TT, NPC, D), F32), pltpu.SemaphoreType.DMA((2,)),
                    pltpu.VMEM((NSLAB, PK * 128, 2 * SW), BF16), pltpu.VMEM((NSLAB, PK * 128, PK * 128), BF16)]
        scratch += [pltpu.VMEM((GROWS, 2 * SW), BF16)] * NSLAB
    if emit_edge:
        scratch += [pltpu.VMEM((NPC, NST), F32)] * 4
    n_state_outs = 2 * (int(emit_edge) + int(emit_fin))
    out_specs += [pl.BlockSpec((NPC, NST), lambda i: (0, 0))] * n_state_outs
    out_shape += [jax.ShapeDtypeStruct((NPC, NST), F32)] * n_state_outs
    return pl.pallas_call(
        functools.partial(_s5_kernel, rev=rev, col_base=col_base, init=init, u_mode=u_mode, emit_y=emit_y,
                          emit_edge=emit_edge, emit_fin=emit_fin),
        grid=(NCH,),
        in_specs=in_specs,
        out_specs=out_specs,
        out_shape=out_shape,
        scratch_shapes=scratch,
        compiler_params=_cparams(1),
        name=("s5_scan" if emit_y else "s5_states") + ("_bwd" if rev else "_fwd") + str(trunk),
    )(*args)


def _carry_kernel(frf_ref, fif_ref, frb_ref, fib_ref, sr_ref, si_ref, lr_ref, li_ref, dt_ref,
                  or_ref, oi_ref, *, j):
    for d, (fr_ref, fi_ref) in enumerate(((frf_ref, fif_ref), (frb_ref, fib_ref))):
        jd = 2 * j + d
        ar, ai = _abar(lr_ref[jd], li_ref[jd], dt_ref[jd])
        for _ in range(8):
            ar, ai = ar * ar - ai * ai, 2.0 * ar * ai
        for b in range(2):
            order = list(range(SEG)) if d == 0 else list(range(SEG - 1, -1, -1))
            r = b * SEG + order[0]
            or_ref[d, r:r + 1, :] = sr_ref[b, jd]
            oi_ref[d, r:r + 1, :] = si_ref[b, jd]
            tr = fr_ref[r:r + 1, :]
            ti = fi_ref[r:r + 1, :]
            for s in order[1:]:
                r = b * SEG + s
                or_ref[d, r:r + 1, :] = tr
                oi_ref[d, r:r + 1, :] = ti
                tr, ti = (fr_ref[r:r + 1, :] + ar * tr - ai * ti,
                          fi_ref[r:r + 1, :] + ar * ti + ai * tr)


def _carry(frf, fif, frb, fib, st_re4, st_im4, lam_re, lam_im, dts, *, j):
    shp = jax.ShapeDtypeStruct((2, NPC, NST), F32)
    return pl.pallas_call(functools.partial(_carry_kernel, j=j), out_shape=[shp, shp], name="s5_carry")(
        frf, fif, frb, fib, st_re4, st_im4, lam_re, lam_im, dts)


GT = 2 * TILE


def _glu_kernel(xa_ref, xb_ref, yfa_ref, yba_ref, yfb_ref, ybb_ref, mod_ref, g_ref, d_ref, w_ref, b_ref,
                o_ref, wbf, ge_s, z_s):
    i = pl.program_id(0)

    @pl.when(i == 0)
    def _():
        wbf[...] = w_ref[...].astype(BF16)

    mod = mod_ref[...]
    first_prompt = i < NPC // 2

    nq = 4
    cw = D // nq
    rw = TILE // nq

    def x_rows(r0, r1, c0=0, c1=D):
        return jnp.where(first_prompt, xa_ref[r0:r1, c0:c1], xb_ref[r0:r1, c0:c1])

    def pre(r0, r1):
        h = _normmod(x_rows(r0, r1), g_ref[...], mod[1:2], mod[0:1])
        y_ssm = jnp.where(first_prompt, yfa_ref[r0:r1, :] + yba_ref[r0:r1, :], yfb_ref[r0:r1, :] + ybb_ref[r0:r1, :])
        y = y_ssm + d_ref[...] * h
        k1 = -2.0 * math.sqrt(2.0 / math.pi)
        ge = y / (1.0 + jnp.exp(y * (k1 + (k1 * 0.044715) * (y * y))))
        ge_s[r0:r1, :] = ge.astype(BF16)

    def gate(q, j):
        rows = slice(q * TILE, (q + 1) * TILE)
        for c0 in (j * cw, D + j * cw):
            z_s[rows, c0:c0 + cw] = (jnp.dot(ge_s[rows, :], wbf[:, c0:c0 + cw], preferred_element_type=F32)
                                     + b_ref[:, c0:c0 + cw])

    def post(q, j):
        rows = slice(q * TILE, (q + 1) * TILE)
        c0, c1 = j * cw, (j + 1) * cw
        out = z_s[rows, c0:c1] * _sigmoid(z_s[rows, D + c0:D + c1])
        o_ref[rows, c0:c1] = x_rows(q * TILE, (q + 1) * TILE, c0, c1) + mod[2:3, c0:c1] * out

    for j in range(nq):
        pre(j * rw, (j + 1) * rw)
    for j in range(nq):
        gate(0, j)
        pre(TILE + j * rw, TILE + (j + 1) * rw)
    for j in range(nq):
        gate(1, j)
        post(0, j)
    for j in range(nq):
        post(1, j)


def _glu(xa, xb, off_b, ys, mods, layer, g, dvec, w, b):
    nblk = NCOL // 2
    half = NPC // 2
    pa = lambda i: (jnp.minimum(i, half - 1), 0)
    pb = lambda i: (jnp.maximum(i - half, 0), 0)
    return pl.pallas_call(
        _glu_kernel,
        grid=(nblk,),
        in_specs=[pl.BlockSpec((GT, D), pa),
                  pl.BlockSpec((GT, D), lambda i: (off_b // 2 + jnp.maximum(i - half, 0), 0)),
                  pl.BlockSpec((GT, D), pa),
                  pl.BlockSpec((GT, D), pa),
                  pl.BlockSpec((GT, D), pb),
                  pl.BlockSpec((GT, D), pb),
                  pl.BlockSpec((None, None, 6, D), lambda i: (layer, _cond_of_col(2 * i), 0, 0)),
                  pl.BlockSpec((1, D), lambda i: (0, 0)),
                  pl.BlockSpec((1, D), lambda i: (0, 0)),
                  pl.BlockSpec((None, D, 2 * D), lambda i: (layer // 2, 0, 0)),
                  pl.BlockSpec((1, 2 * D), lambda i: (0, 0))],
        out_specs=pl.BlockSpec((GT, D), lambda i: (i, 0)),
        out_shape=jax.ShapeDtypeStruct((NCOL * TILE, D), F32),
        scratch_shapes=[pltpu.VMEM((D, 2 * D), BF16), pltpu.VMEM((GT, D), BF16), pltpu.VMEM((GT, 2 * D), F32)],
        compiler_params=_cparams(1),
        name="s5_glu",
    )(xa, xb, *ys, mods, g, dvec, w, b)


HG = FGC // 2


def _fourier_kernel(x_ref, mod_ref, g_ref, mc_ref, cl_ref, sl_ref, w_ref, b_ref,
                    o_ref, *scratch, scale, nseg):
    mod = mod_ref[...]
    lane = lax.broadcasted_iota(jnp.int32, (TILE, HG), 1)
    lane0 = lane == 0

    def channel_dft():
        h = _normmod(x_ref[...], g_ref[...], mod[1:2], mod[0:1]).astype(BF16)
        zc, zs, zn = [], [], None
        for q in range(FG):
            z = jnp.dot(h[:, q * FGC:(q + 1) * FGC], mc_ref[...], preferred_element_type=F32)
            zc.append(z[:, 0:HG].astype(BF16))
            zs.append(z[:, HG:FGC].astype(BF16))
            nyq = jnp.where(lane0, z[:, HG:FGC], 0.0)
            nyq = pltpu.roll(nyq, q, 1) if q else nyq
            zn = nyq if zn is None else zn + nyq
        return jnp.concatenate(zc, axis=1), jnp.concatenate(zs, axis=1), zn.astype(BF16)

    def position_dft(xc, xs, xn):
        a = jnp.dot(cl_ref[...], xc, preferred_element_type=F32)
        bz = jnp.dot(sl_ref[...], xs, preferred_element_type=F32)
        an = jnp.dot(cl_ref[...], xn, preferred_element_type=F32)
        parts = []
        for q in range(FG):
            aq = a[:, q * HG:(q + 1) * HG]
            bq = jnp.where(lane0, 0.0, bz[:, q * HG:(q + 1) * HG])
            nq = pltpu.roll(an, HG - q, 1) if q else an
            parts += [aq - bq, jnp.where(lane0, nq, aq + bq)]
        f = jnp.concatenate(parts, axis=1) * scale
        o = jnp.dot(f.astype(BF16), w_ref[...], preferred_element_type=F32) + b_ref[...]
        o_ref[...] = x_ref[...] + mod[2:3] * o

    if nseg == 1:
        position_dft(*channel_dft())
        return

    xc_s, xs_s, xn_s = scratch
    ph = pl.program_id(1)
    s = pl.program_id(2)

    @pl.when(ph == 0)
    def _():
        r0 = pl.multiple_of(s * TILE, TILE)
        xc, xs, xn = channel_dft()
        xc_s[pl.ds(r0, TILE), :] = xc
        xs_s[pl.ds(r0, TILE), :] = xs
        xn_s[pl.ds(r0, TILE), :] = xn

    @pl.when(ph == 1)
    def _():
        position_dft(xc_s[...], xs_s[...], xn_s[...])


def _fourier(x, mods, layer, g, mc, cl, sl, w, b, *, nseq, nseg, col_off, cond_off, cond_stride):
    ln = nseg * TILE
    if nseg == 1:
        grid = (nseq,)
        idx = lambda f: (lambda q: f(q, 1, 0))
        scratch = []
    else:
        grid = (nseq, 2, nseg)
        idx = lambda f: f
        scratch = [pltpu.VMEM((ln, FG * HG), BF16), pltpu.VMEM((ln, FG * HG), BF16), pltpu.VMEM((ln, HG), BF16)]
    return pl.pallas_call(
        functools.partial(_fourier_kernel, scale=1.0 / math.sqrt(ln * FGC), nseg=nseg),
        grid=grid,
        in_specs=[pl.BlockSpec((TILE, D), idx(lambda q, ph, s: (col_off + q * nseg + s, 0))),
                  pl.BlockSpec((None, None, 6, D), idx(lambda q, ph, s: (layer, cond_off + q * cond_stride, 0, 0))),
                  pl.BlockSpec((1, D), idx(lambda q, ph, s: (0, 0))),
                  pl.BlockSpec((FGC, FGC), idx(lambda q, ph, s: (0, 0))),
                  pl.BlockSpec((TILE, ln), idx(lambda q, ph, s: (s * ph, 0))),
                  pl.BlockSpec((TILE, ln), idx(lambda q, ph, s: (s * ph, 0))),
                  pl.BlockSpec((None, D, D), idx(lambda q, ph, s: (layer // 2, 0, 0))),
                  pl.BlockSpec((1, D), idx(lambda q, ph, s: (0, 0)))],
        out_specs=pl.BlockSpec((TILE, D), idx(lambda q, ph, s: (col_off + q * nseg + s * ph, 0))),
        out_shape=jax.ShapeDtypeStruct((NCOL * TILE, D), F32),
        scratch_shapes=scratch,
        input_output_aliases={0: 0},
        compiler_params=_cparams(len(grid)),
        name="fourier%d" % nseg,
    )(x, mods, g, mc, cl, sl, w, b)


@functools.lru_cache(maxsize=None)
def _dft_mats(n):
    k = np.arange(n, dtype=np.int64)
    ang = ((k[:, None] * k[None, :]) % n).astype(np.float64) * (2.0 * math.pi / n)
    return np.cos(ang).astype(np.float32), np.sin(ang).astype(np.float32)


@functools.lru_cache(maxsize=None)
def _channel_mats():
    c, s = _dft_mats(FGC)
    mc = np.concatenate([c[:, 0:HG], c[:, HG:HG + 1], s[:, 1:HG]], axis=1)
    order = np.concatenate([np.arange(0, HG), [HG], FGC - np.arange(1, HG)])
    perm = np.concatenate([q * FGC + order for q in range(FG)])
    return mc, perm.astype(np.int32)


WCH = DFF // FC


def _ffn_kernel(x_ref, mod_ref, g_ref, wup_hbm, cw_ref, cb_ref, wdn_hbm, gf_ref, *rest, final, layer):
    n_out = 2 if final else 1
    o_refs = rest[:n_out]
    wup_ref, wdn_ref, stg_u, stg_d, sem_u, sem_d = rest[n_out:]
    i = pl.program_id(0)

    @pl.when(i == 0)
    def _load_weights():
        def copies(c):
            slot = c % 2
            return (pltpu.make_async_copy(wup_hbm.at[layer, :, pl.ds(c * 2 * FC, 2 * FC)], stg_u.at[slot],
                                          sem_u.at[slot]),
                    pltpu.make_async_copy(wdn_hbm.at[layer, pl.ds(c * FC, FC), :], stg_d.at[slot],
                                          sem_d.at[slot]))

        for cp in copies(0):
            cp.start()
        for c in range(WCH):
            if c + 1 < WCH:
                for cp in copies(c + 1):
                    cp.start()
            for cp in copies(c):
                cp.wait()
            wup_ref[:, c * 2 * FC:(c + 1) * 2 * FC] = stg_u[c % 2].astype(BF16)
            wdn_ref[c * FC:(c + 1) * FC, :] = stg_d[c % 2].astype(BF16)

    x = x_ref[...]
    mod = mod_ref[...]
    h = _normmod(x, g_ref[...], mod[4:5], mod[3:4]).astype(BF16)
    sub = lax.broadcasted_iota(jnp.int32, (8, FC), 0)
    one = jnp.ones((8, FC), F32)
    first0 = (sub != 0).astype(F32)
    last0 = (sub != 7).astype(F32)
    inner_first = jnp.where(i < NPC, one, first0)
    inner_last = jnp.where(i < NPC, one, last0)
    prev_slabs = [(0, first0)] + [(r, inner_first) for r in range(GRID_W, TILE, GRID_W)]
    next_slabs = [(r - 8, inner_last) for r in range(GRID_W, TILE, GRID_W)] + [(TILE - 8, last0)]

    def mask_rows(a, slabs):
        parts, last = [], 0
        for r0, m in slabs:
            if r0 > last:
                parts.append(a[last:r0])
            parts.append(a[r0:r0 + 8] * m)
            last = r0 + 8
        if last < TILE:
            parts.append(a[last:])
        return jnp.concatenate(parts, axis=0)

    def conv(up, off):
        w = cw_ref[:, off:off + FC]
        prev = mask_rows(pltpu.roll(up, 1, 0), prev_slabs)
        nxt = mask_rows(pltpu.roll(up, TILE - 1, 0), next_slabs)
        return w[0:1] * prev + w[1:2] * up + w[2:3] * nxt + cb_ref[:, off:off + FC]

    def up_chunk(c):
        og = c * FC
        ov = DFF + c * FC
        return (jnp.dot(h, wup_ref[:, og:og + FC], preferred_element_type=F32),
                jnp.dot(h, wup_ref[:, ov:ov + FC], preferred_element_type=F32))

    def down_chunk(act, c):
        return jnp.dot(act, wdn_ref[c * FC:(c + 1) * FC, :], preferred_element_type=F32)

    nchunk = DFF // FC
    acc = jnp.zeros((TILE, D), F32)
    nxt_up = up_chunk(0)
    act = None
    for c in range(nchunk):
        cur_up = nxt_up
        if c + 1 < nchunk:
            nxt_up = up_chunk(c + 1)
        if act is not None:
            acc = acc + down_chunk(act, c - 1)
        gate = conv(cur_up[0], c * FC)
        val = conv(cur_up[1], DFF + c * FC)
        hg = 0.5 * gate
        act = ((hg + hg * jnp.tanh(hg)) * val).astype(BF16)
    acc = acc + down_chunk(act, nchunk - 1)
    y = x + mod[5:6] * acc
    if not final:
        o_refs[0][...] = y
    else:
        ms = jnp.mean(y * y, axis=-1, keepdims=True)
        y = y * lax.rsqrt(ms + EPS) * gf_ref[...]

        @pl.when(i < NPC)
        def _():
            o_refs[0][...] = y

        @pl.when(i >= NPC)
        def _():
            o_refs[1][...] = y


def _ffn(x, mods, layer, g, wup, cw, cb, wdn, gf, *, final):
    if final:
        out_specs = [pl.BlockSpec((TILE, D), lambda i: (jnp.minimum(i, NPC - 1), 0)),
                     pl.BlockSpec((TILE, D), lambda i: (jnp.maximum(i - NPC, 0), 0))]
        out_shape = [jax.ShapeDtypeStruct((NPC * TILE, D), F32)] * 2
    else:
        out_specs = pl.BlockSpec((TILE, D), lambda i: (i, 0))
        out_shape = jax.ShapeDtypeStruct((NCOL * TILE, D), F32)
    return pl.pallas_call(
        functools.partial(_ffn_kernel, final=final, layer=layer),
        grid=(NCOL,),
        in_specs=[pl.BlockSpec((TILE, D), lambda i: (i, 0)),
                  pl.BlockSpec((None, None, 6, D), lambda i: (layer, _cond_of_col(i), 0, 0)),
                  pl.BlockSpec((1, D), lambda i: (0, 0)),
                  pl.BlockSpec(memory_space=pl.ANY),
                  pl.BlockSpec((None, 3, 2 * DFF), lambda i: (layer, 0, 0)),
                  pl.BlockSpec((None, 1, 2 * DFF), lambda i: (layer, 0, 0)),
                  pl.BlockSpec(memory_space=pl.ANY),
                  pl.BlockSpec((1, D), lambda i: (0, 0))],
        out_specs=out_specs,
        out_shape=out_shape,
        scratch_shapes=[pltpu.VMEM((D, 2 * DFF), BF16), pltpu.VMEM((DFF, D), BF16),
                        pltpu.VMEM((2, D, 2 * FC), F32), pltpu.VMEM((2, FC, D), F32),
                        pltpu.SemaphoreType.DMA((2,)), pltpu.SemaphoreType.DMA((2,))],
        compiler_params=_cparams(1),
        name="ffn",
    )(x, mods, g, wup, cw, cb, wdn, gf)


def kernel(x_prompt, x_sample, state_ssm_re, state_ssm_im, c, c_ctx, w_ada, b_ada, g_mix, g_ffn,
           ssm_lam_re, ssm_lam_im, ssm_log_dt, ssm_b_re, ssm_b_im, ssm_c_re, ssm_c_im, ssm_d,
           w_glu, b_glu, w_fourier, b_fourier, w_up, conv_w, conv_b, w_down, g_final):
    nb = x_prompt.shape[0]
    xp = x_prompt.reshape(NPC * TILE, D)
    xs = x_sample.reshape(NPC * TILE, D)
    x = None

    cond8 = jnp.concatenate([c_ctx[None, :], c, jnp.zeros((5, D), F32)], axis=0)
    mods = _ada(cond8, w_ada, b_ada).reshape(DEPTH, 8, 6, D)
    col_cond = jnp.asarray([0] * NPC + [1] * SEG + [2] * SEG, jnp.int32)
    mods_t = mods.transpose(0, 2, 1, 3)[:, :, col_cond]

    mc_np, perm = _channel_mats()
    mc = jnp.asarray(mc_np).astype(BF16)
    wf_bf = _cast_bf16(w_fourier[:, jnp.asarray(perm), :].reshape(2 * D, D), D).reshape(2, D, D)
    cl1, sl1 = (jnp.asarray(m).astype(BF16) for m in _dft_mats(TILE))
    cl8, sl8 = (jnp.asarray(m).astype(BF16) for m in _dft_mats(SEG * TILE))

    njd = ssm_lam_re.shape[0] * 2
    lam_re = ssm_lam_re.reshape(njd, 1, NST)
    lam_im = ssm_lam_im.reshape(njd, 1, NST)
    dts = jnp.repeat(ssm_log_dt.reshape(njd, GS), PS, axis=-1).reshape(njd, 1, NST)
    bt_re = ssm_b_re.transpose(0, 1, 4, 2, 3).reshape(njd, GC, NST)
    bt_im = ssm_b_im.transpose(0, 1, 4, 2, 3).reshape(njd, GC, NST)
    ct_re = ssm_c_re.transpose(0, 1, 3, 2, 4).reshape(njd, GC, NST)
    ct_im = ssm_c_im.transpose(0, 1, 3, 2, 4).reshape(njd, GC, NST)
    st_re4 = state_ssm_re.reshape(2, njd, 1, NST)
    st_im4 = state_ssm_im.reshape(2, njd, 1, NST)

    edges = []
    for i in range(DEPTH):
        j = i // 2
        if i % 2 == 0:
            g = g_mix[i][None, :]
            xa, xb, off_b = (xp, xs, 0) if x is None else (x, x, NPC)
            xa3 = xa.reshape(-1, TILE, D)
            xb3 = xb.reshape(-1, TILE, D)
            par = (mods_t, i, g, lam_re, lam_im, dts, bt_re, bt_im, ct_re, ct_im)
            kw = dict(emit_edge=False, emit_fin=True, emit_y=False, trunk=1, init="state", h0=(st_re4, st_im4))
            u_s, frf, fif = _s5_scan(xb3, off_b, *par, jd=2 * j, rev=False, u_mode="make", **kw)
            frb, fib = _s5_scan(u_s, 0, *par, jd=2 * j + 1, rev=True, u_mode="load", **kw)
            carry = _carry(frf, fif, frb, fib, st_re4, st_im4, lam_re, lam_im, dts, j=j)
            kw = dict(trunk=0, init="zero", emit_y=True, emit_edge=True, emit_fin=False)
            yfa, u_p, er, ei = _s5_scan(xa3, 0, *par, jd=2 * j, rev=False, u_mode="make", **kw)
            edges += [er, ei]
            yba, er, ei = _s5_scan(u_p, 0, *par, jd=2 * j + 1, rev=True, u_mode="load", **kw)
            edges += [er, ei]
            kw = dict(trunk=1, init="carry", h0=carry, u_mode="load", emit_y=True, emit_edge=False, emit_fin=False)
            (yfb,) = _s5_scan(u_s, 0, *par, jd=2 * j, rev=False, **kw)
            (ybb,) = _s5_scan(u_s, 0, *par, jd=2 * j + 1, rev=True, **kw)
            ys = [y.reshape(NPC * TILE, D) for y in (yfa, yba, yfb, ybb)]
            x = _glu(xa, xb, off_b, ys, mods, i, g, ssm_d[j][None, :], w_glu, b_glu[j][None, :])
        else:
            g = g_mix[i][None, :]
            bf = b_fourier[j][None, :]
            x = _fourier(x, mods, i, g, mc, cl1, sl1, wf_bf, bf,
                         nseq=NPC, nseg=1, col_off=0, cond_off=0, cond_stride=0)
            x = _fourier(x, mods, i, g, mc, cl8, sl8, wf_bf, bf,
                         nseq=2, nseg=SEG, col_off=NPC, cond_off=1, cond_stride=1)
        x = _ffn(x, mods, i, g_ffn[i][None, :], w_up, conv_w, conv_b.reshape(DEPTH, 1, 2 * DFF),
                 w_down, g_final[None, :], final=(i == DEPTH - 1))

    y_prompt = x[0].reshape(NPC, TILE, D)
    y_sample = x[1].reshape(2, SEG * TILE, D)
    ed = jnp.stack(edges, axis=0).reshape(DEPTH // 2, 2, 2, NPC, NST)[:, :, :, :nb]
    new_re = ed[:, :, 0].transpose(2, 0, 1, 3).reshape(nb, DEPTH // 2, 2, GS, PS)
    new_im = ed[:, :, 1].transpose(2, 0, 1, 3).reshape(nb, DEPTH // 2, 2, GS, PS)
    return (y_prompt, y_sample, new_re, new_im)
```

```python
import functools
import math

import jax
import jax.numpy as jnp
import numpy as np
from jax import lax
from jax.experimental import pallas as pl
from jax.experimental.pallas import tpu as pltpu

F32 = jnp.float32
BF16 = jnp.bfloat16

D = 1024
TILE = 256
NCOL = 32
NPC = 16
SEG = 8
DEPTH = 4
GS = 64
GC = 16
PS = 64
NST = GS * PS
FG = 4
FGC = 256
DFF = 2816
EPS = 1e-6
GRID_W = 64

TT = 32
NCH = TILE // TT
SW = 512
NSLAB = NST // SW
FC = 256

VMEM_LIMIT = 56 * 1024 * 1024


def _cparams(n_axes):
    return pltpu.CompilerParams(dimension_semantics=("arbitrary",) * n_axes,
                                vmem_limit_bytes=VMEM_LIMIT)


def _normmod(x, g, sc, sh):
    ms = jnp.mean(x * x, axis=-1, keepdims=True)
    return x * lax.rsqrt(ms + EPS) * (g * (1.0 + sc)) + sh


def _sigmoid(x):
    return 1.0 / (1.0 + jnp.exp(-x))


def _cond_of_col(i):
    return jnp.where(i < NPC, 0, 1 + (i - NPC) // SEG)


def _ada_kernel(c_ref, w_ref, b_ref, o_ref):
    c = c_ref[...]
    s = (c * _sigmoid(c)).astype(BF16)
    o_ref[...] = jnp.dot(s, w_ref[...].astype(BF16), preferred_element_type=F32) + b_ref[...]


def _ada(cond8, w_ada, b_ada):
    tn = 1536
    return pl.pallas_call(
        _ada_kernel,
        grid=(DEPTH, 6 * D // tn),
        in_specs=[pl.BlockSpec((8, D), lambda l, n: (0, 0)),
                  pl.BlockSpec((None, D, tn), lambda l, n: (l, 0, n)),
                  pl.BlockSpec((None, 1, tn), lambda l, n: (l, 0, n))],
        out_specs=pl.BlockSpec((None, 8, tn), lambda l, n: (l, 0, n)),
        out_shape=jax.ShapeDtypeStruct((DEPTH, 8, 6 * D), F32),
        compiler_params=_cparams(2),
        name="ada",
    )(cond8, w_ada, b_ada.reshape(DEPTH, 1, 6 * D))


def _abar(lr, li, logdt):
    dt = jnp.exp(logdt)
    mag = jnp.exp(lr * dt)
    return mag * jnp.cos(li * dt), mag * jnp.sin(li * dt)


PK = 2
NG = TT // PK
GROWS = NG * NPC


def _nt_dot(a, b):
    return lax.dot_general(a, b, (((1,), (1,)), ((), ())), preferred_element_type=F32)


def _s5_kernel(*refs, rev, col_base, init, u_mode, emit_y, emit_edge, emit_fin):
    src_hbm, sc_ref, sh_ref, g_ref, lr_ref, li_ref, dt_ref, btr_ref, bti_ref, ctr_ref, cti_ref = refs[:11]
    pos = 11
    if init != "zero":
        h0r_ref, h0i_ref = refs[pos:pos + 2]
        pos += 2
    if emit_y:
        y_hbm = refs[pos]
        pos += 1
    if u_mode == "make":
        u_hbm = refs[pos]
        pos += 1
    if emit_edge:
        er_ref, ei_ref = refs[pos:pos + 2]
        pos += 2
    if emit_fin:
        fr_ref, fi_ref = refs[pos:pos + 2]
        pos += 2
    scratch = refs[pos:]
    d_re = scratch[0:NSLAB]
    d_im = scratch[NSLAB:2 * NSLAB]
    bp_s, ap_r, ap_i, hs_re, hs_im, inbuf, sem_in = scratch[2 * NSLAB:2 * NSLAB + 7]
    pos = 2 * NSLAB + 7
    if u_mode == "make":
        ubuf, sem_u = scratch[pos:pos + 2]
        pos += 2
    if emit_y:
        ybuf, sem_out, cp_s, ki_s = scratch[pos:pos + 4]
        hb = scratch[pos + 4:pos + 4 + NSLAB]
        pos += 4 + NSLAB
    if emit_edge:
        ed_re, ed_im, a1_r, a1_i = scratch[pos:pos + 4]

    i = pl.program_id(0)
    slot = lax.rem(i, 2)

    def chunk(i_):
        return NCH - 1 - i_ if rev else i_

    def in_copies(i_, slot_):
        if u_mode == "load":
            return [pltpu.make_async_copy(src_hbm.at[chunk(i_)], inbuf.at[slot_], sem_in.at[slot_])]
        t0 = pl.multiple_of(chunk(i_) * TT, TT)
        return [pltpu.make_async_copy(src_hbm.at[col_base + c, pl.ds(t0, TT), :],
                                      inbuf.at[slot_, :, c, :], sem_in.at[slot_]) for c in range(NPC)]

    def u_copies(i_, slot_):
        return [pltpu.make_async_copy(ubuf.at[slot_], u_hbm.at[chunk(i_)], sem_u.at[slot_])]

    def y_copies(i_, slot_):
        t0 = pl.multiple_of(chunk(i_) * TT, TT)
        return [pltpu.make_async_copy(ybuf.at[slot_, :, c, :], y_hbm.at[c, pl.ds(t0, TT), :],
                                      sem_out.at[slot_]) for c in range(NPC)]

    fetches = in_copies

    def stores(i_, slot_):
        return (y_copies(i_, slot_) if emit_y else []) + (u_copies(i_, slot_) if u_mode == "make" else [])

    @pl.when(i == 0)
    def _first_fetch():
        for cp in fetches(i, slot):
            cp.start()

    if emit_y or u_mode == "make":
        @pl.when(i >= 2)
        def _out_bufs_free():
            for cp in stores(i, slot):
                cp.wait()

    @pl.when(i == 0)
    def _prep():
        lr = lr_ref[...]
        li = li_ref[...]
        ar, ai = _abar(lr, li, dt_ref[...])
        pw = [(jnp.ones_like(ar), jnp.zeros_like(ai)), (ar, ai)]
        while len(pw) <= PK:
            pr, pi_ = pw[-1]
            pw.append((pr * ar - pi_ * ai, pr * ai + pi_ * ar))
        ap_r[...] = jnp.broadcast_to(pw[PK][0], (8, NST))
        ap_i[...] = jnp.broadcast_to(pw[PK][1], (8, NST))
        if emit_edge:
            a1_r[...] = jnp.broadcast_to(ar, (NPC, NST))
            a1_i[...] = jnp.broadcast_to(ai, (NPC, NST))
        xr = ar - 1.0
        den = lr * lr + li * li
        fr = (xr * lr + ai * li) / den
        fi = (ai * lr - xr * li) / den
        btr = btr_ref[...]
        bti = bti_ref[...]
        bbr = fr * btr - fi * bti
        bbi = fr * bti + fi * btr
        r = lax.broadcasted_iota(jnp.int32, (128, SW), 0)
        c = lax.broadcasted_iota(jnp.int32, (128, SW), 1)
        gmask = ((r >> 4) == (c >> 6)).astype(F32)

        def tiles(vr, vi, m, sign):
            sl = slice(m * SW, (m + 1) * SW)
            tr = jnp.concatenate([vr[:, sl]] * 8, axis=0) * gmask
            ti = jnp.concatenate([vi[:, sl]] * 8, axis=0) * (sign * gmask)
            return jnp.concatenate([tr, ti], axis=1).astype(BF16)

        def times(vr, vi, k):
            pr, pi_ = pw[k]
            return vr * pr - vi * pi_, vr * pi_ + vi * pr

        ab = [times(bbr, bbi, k) for k in range(PK)]
        for m in range(NSLAB):
            for p in range(PK):
                bp_s[m, p * 128:(p + 1) * 128, :] = tiles(*ab[PK - 1 - p], m, 1.0)
        if emit_y:
            ctr = ctr_ref[...]
            cti = cti_ref[...]
            ca = [times(ctr, cti, k) for k in range(PK + 1)]
            for m in range(NSLAB):
                for p in range(PK):
                    cp_s[m, p * 128:(p + 1) * 128, :] = tiles(*ca[p + 1], m, -1.0)
                c0 = tiles(*ca[0], m, -1.0)
                kt = [_nt_dot(tiles(*ab[k], m, 1.0), c0).astype(BF16) for k in range(PK)]
                zero = jnp.zeros((128, 128), BF16)
                for q in range(PK):
                    for p in range(PK):
                        ki_s[m, q * 128:(q + 1) * 128, p * 128:(p + 1) * 128] = kt[p - q] if p >= q else zero
        if init == "carry":
            hs_re[...] = h0r_ref[...]
            hs_im[...] = h0i_ref[...]
        else:
            hs_re[...] = jnp.zeros((NPC, NST), F32)
            hs_im[...] = jnp.zeros((NPC, NST), F32)
            if init == "state":
                seg0 = SEG - 1 if rev else 0
                for b in range(2):
                    row = b * SEG + seg0
                    hs_re[row:row + 1, :] = h0r_ref[b]
                    hs_im[row:row + 1, :] = h0i_ref[b]

    for cp in fetches(i, slot):
        cp.wait()
    for cp in fetches(jnp.minimum(i + 1, NCH - 1), 1 - slot):
        cp.start()

    if u_mode == "make":
        x3 = inbuf[slot]
        ms = jnp.mean(x3 * x3, axis=-1, keepdims=True)
        gain = g_ref[...] * (1.0 + sc_ref[...])
        u3 = x3 * lax.rsqrt(ms + EPS) * gain[None] + sh_ref[...][None]
        u = u3.reshape(TT * NPC, D).astype(BF16)
        ubuf[slot] = u
    else:
        u = inbuf[slot]

    def step_rows(a, b):
        t = TT - 1 - b if rev else b
        return a[t * NPC:(t + 1) * NPC]

    def packed(m):
        lanes = slice(m * 128, (m + 1) * 128)
        return jnp.concatenate(
            [jnp.concatenate([step_rows(u, PK * j + p)[:, lanes] for p in range(PK)], axis=1) for j in range(NG)],
            axis=0)

    def expand(m):
        res = jnp.dot(packed(m), bp_s[m], preferred_element_type=F32)
        d_re[m][...] = res[:, 0:SW]
        d_im[m][...] = res[:, SW:2 * SW]

    def recur(s):
        lanes = slice(s * SW, (s + 1) * SW)
        ar = ap_r[:, lanes]
        ai = ap_i[:, lanes]
        h = [hs_re[0:8, lanes], hs_im[0:8, lanes], hs_re[8:16, lanes], hs_im[8:16, lanes]]
        for j in range(NG):
            if emit_y:
                jrow = slice(j * NPC, (j + 1) * NPC)
                hb[s][jrow, 0:SW] = jnp.concatenate([h[0], h[2]], axis=0).astype(BF16)
                hb[s][jrow, SW:2 * SW] = jnp.concatenate([h[1], h[3]], axis=0).astype(BF16)
            for q in range(2):
                rows = slice(j * NPC + 8 * q, j * NPC + 8 * q + 8)
                hr, hi = h[2 * q], h[2 * q + 1]
                h[2 * q] = ar * hr - ai * hi + d_re[s][rows, :]
                h[2 * q + 1] = ar * hi + ai * hr + d_im[s][rows, :]
        hs_re[0:8, lanes] = h[0]
        hs_im[0:8, lanes] = h[1]
        hs_re[8:16, lanes] = h[2]
        hs_im[8:16, lanes] = h[3]

    def contract(m):
        yp = _nt_dot(hb[m][...], cp_s[m]) + jnp.dot(packed(m), ki_s[m], preferred_element_type=F32)
        for j in range(NG):
            for p in range(PK):
                b = PK * j + p
                t = TT - 1 - b if rev else b
                ybuf[slot, t, :, m * 128:(m + 1) * 128] = yp[j * NPC:(j + 1) * NPC, p * 128:(p + 1) * 128]

    if emit_edge:
        u0 = step_rows(u, 0)
        for m in range(NSLAB):
            lanes = slice(m * SW, (m + 1) * SW)
            bu0 = jnp.dot(u0[:, m * 128:(m + 1) * 128], bp_s[m, (PK - 1) * 128:PK * 128, :],
                          preferred_element_type=F32)
            hr, hi = hs_re[:, lanes], hs_im[:, lanes]
            ed_re[:, lanes] = a1_r[:, lanes] * hr - a1_i[:, lanes] * hi + bu0[:, 0:SW]
            ed_im[:, lanes] = a1_r[:, lanes] * hi + a1_i[:, lanes] * hr + bu0[:, SW:2 * SW]

    expand(0)
    for s in range(NSLAB):
        if s + 1 < NSLAB:
            expand(s + 1)
        if emit_y and s >= 1:
            contract(s - 1)
        recur(s)
    if emit_y:
        contract(NSLAB - 1)
    for cp in stores(i, slot):
        cp.start()

    if emit_edge:
        @pl.when(i == 0)
        def _edge():
            er_ref[...] = ed_re[...]
            ei_ref[...] = ed_im[...]

    @pl.when(i == NCH - 1)
    def _last():
        if emit_fin:
            fr_ref[...] = hs_re[...]
            fi_ref[...] = hs_im[...]
        for cp in fetches(i, 1 - slot) + stores(i, slot) + stores(i, 1 - slot):
            cp.wait()


def _s5_scan(src, col_base, mods_t, layer, g, lam_re, lam_im, dts, bt_re, bt_im, ct_re, ct_im, *,
             jd, rev, trunk, init, h0=None, u_mode, emit_y, emit_edge, emit_fin):
    par = lambda i: (jd, 0, 0)
    in_specs = [
        pl.BlockSpec(memory_space=pl.ANY),
        pl.BlockSpec((None, None, NPC, D), lambda i: (layer, 1, trunk, 0)),
        pl.BlockSpec((None, None, NPC, D), lambda i: (layer, 0, trunk, 0)),
        pl.BlockSpec((1, D), lambda i: (0, 0)),
        pl.BlockSpec((None, 1, NST), par),
        pl.BlockSpec((None, 1, NST), par),
        pl.BlockSpec((None, 1, NST), par),
        pl.BlockSpec((None, GC, NST), par),
        pl.BlockSpec((None, GC, NST), par),
        pl.BlockSpec((None, GC, NST), par),
        pl.BlockSpec((None, GC, NST), par),
    ]
    args = [src, mods_t, mods_t, g, lam_re, lam_im, dts, bt_re, bt_im, ct_re, ct_im]
    if init == "state":
        in_specs += [pl.BlockSpec((2, None, 1, NST), lambda i: (0, jd, 0, 0))] * 2
        args += list(h0)
    elif init == "carry":
        in_specs += [pl.BlockSpec((None, NPC, NST), lambda i: (1 if rev else 0, 0, 0))] * 2
        args += list(h0)
    out_specs, out_shape = [], []
    slab = pltpu.VMEM((GROWS, SW), F32)
    scratch = [slab] * (2 * NSLAB) + [
        pltpu.VMEM((NSLAB, PK * 128, 2 * SW), BF16),
        pltpu.VMEM((8, NST), F32), pltpu.VMEM((8, NST), F32),
        pltpu.VMEM((NPC, NST), F32), pltpu.VMEM((NPC, NST), F32),
        pltpu.VMEM((2, TT, NPC, D), F32) if u_mode == "make" else pltpu.VMEM((2, TT * NPC, D), BF16),
        pltpu.SemaphoreType.DMA((2,))]
    if emit_y:
        out_specs.append(pl.BlockSpec(memory_space=pl.ANY))
        out_shape.append(jax.ShapeDtypeStruct((NPC, TILE, D), F32))
    if u_mode == "make":
        out_specs.append(pl.BlockSpec(memory_space=pl.ANY))
        out_shape.append(jax.ShapeDtypeStruct((NCH, TT * NPC, D), BF16))
        scratch += [pltpu.VMEM((2, TT * NPC, D), BF16), pltpu.SemaphoreType.DMA((2,))]
    if emit_y:
        scratch += [pltpu.VMEM((2, TT, NPC, D), F32), pltpu.SemaphoreType.DMA((2,)),
                    pltpu.VMEM((NSLAB, PK * 128, 2 * SW), BF16), pltpu.VMEM((NSLAB, PK * 128, PK * 128), BF16)]
        scratch += [pltpu.VMEM((GROWS, 2 * SW), BF16)] * NSLAB
    if emit_edge:
        scratch += [pltpu.VMEM((NPC, NST), F32)] * 4
    n_state_outs = 2 * (int(emit_edge) + int(emit_fin))
    out_specs += [pl.BlockSpec((NPC, NST), lambda i: (0, 0))] * n_state_outs
    out_shape += [jax.ShapeDtypeStruct((NPC, NST), F32)] * n_state_outs
    return pl.pallas_call(
        functools.partial(_s5_kernel, rev=rev, col_base=col_base, init=init, u_mode=u_mode, emit_y=emit_y,
                          emit_edge=emit_edge, emit_fin=emit_fin),
        grid=(NCH,),
        in_specs=in_specs,
        out_specs=out_specs,
        out_shape=out_shape,
        scratch_shapes=scratch,
        compiler_params=_cparams(1),
        name=("s5_scan" if emit_y else "s5_states") + ("_bwd" if rev else "_fwd") + str(trunk),
    )(*args)


def _carry_kernel(frf_ref, fif_ref, frb_ref, fib_ref, sr_ref, si_ref, lr_ref, li_ref, dt_ref,
                  or_ref, oi_ref, *, j):
    for d, (fr_ref, fi_ref) in enumerate(((frf_ref, fif_ref), (frb_ref, fib_ref))):
        jd = 2 * j + d
        ar, ai = _abar(lr_ref[jd], li_ref[jd], dt_ref[jd])
        for _ in range(8):
            ar, ai = ar * ar - ai * ai, 2.0 * ar * ai
        for b in range(2):
            order = list(range(SEG)) if d == 0 else list(range(SEG - 1, -1, -1))
            r = b * SEG + order[0]
            or_ref[d, r:r + 1, :] = sr_ref[b, jd]
            oi_ref[d, r:r + 1, :] = si_ref[b, jd]
            tr = fr_ref[r:r + 1, :]
            ti = fi_ref[r:r + 1, :]
            for s in order[1:]:
                r = b * SEG + s
                or_ref[d, r:r + 1, :] = tr
                oi_ref[d, r:r + 1, :] = ti
                tr, ti = (fr_ref[r:r + 1, :] + ar * tr - ai * ti,
                          fi_ref[r:r + 1, :] + ar * ti + ai * tr)


def _carry(frf, fif, frb, fib, st_re4, st_im4, lam_re, lam_im, dts, *, j):
    shp = jax.ShapeDtypeStruct((2, NPC, NST), F32)
    return pl.pallas_call(functools.partial(_carry_kernel, j=j), out_shape=[shp, shp], name="s5_carry")(
        frf, fif, frb, fib, st_re4, st_im4, lam_re, lam_im, dts)


GT = 2 * TILE


def _glu_kernel(xa_ref, xb_ref, yfa_ref, yba_ref, yfb_ref, ybb_ref, mod_ref, g_ref, d_ref, w_ref, b_ref,
                o_ref, wbf, ge_s, z_s):
    i = pl.program_id(0)

    @pl.when(i == 0)
    def _():
        wbf[...] = w_ref[...].astype(BF16)

    mod = mod_ref[...]
    first_prompt = i < NPC // 2

    nq = 4
    cw = D // nq
    rw = TILE // nq

    def x_rows(r0, r1, c0=0, c1=D):
        return jnp.where(first_prompt, xa_ref[r0:r1, c0:c1], xb_ref[r0:r1, c0:c1])

    def pre(r0, r1):
        h = _normmod(x_rows(r0, r1), g_ref[...], mod[1:2], mod[0:1])
        y_ssm = jnp.where(first_prompt, yfa_ref[r0:r1, :] + yba_ref[r0:r1, :], yfb_ref[r0:r1, :] + ybb_ref[r0:r1, :])
        y = y_ssm + d_ref[...] * h
        k1 = -2.0 * math.sqrt(2.0 / math.pi)
        ge = y / (1.0 + jnp.exp(y * (k1 + (k1 * 0.044715) * (y * y))))
        ge_s[r0:r1, :] = ge.astype(BF16)

    def gate(q, j):
        rows = slice(q * TILE, (q + 1) * TILE)
        for c0 in (j * cw, D + j * cw):
            z_s[rows, c0:c0 + cw] = (jnp.dot(ge_s[rows, :], wbf[:, c0:c0 + cw], preferred_element_type=F32)
                                     + b_ref[:, c0:c0 + cw])

    def post(q, j):
        rows = slice(q * TILE, (q + 1) * TILE)
        c0, c1 = j * cw, (j + 1) * cw
        out = z_s[rows, c0:c1] * _sigmoid(z_s[rows, D + c0:D + c1])
        o_ref[rows, c0:c1] = x_rows(q * TILE, (q + 1) * TILE, c0, c1) + mod[2:3, c0:c1] * out

    for j in range(nq):
        pre(j * rw, (j + 1) * rw)
    for j in range(nq):
        gate(0, j)
        pre(TILE + j * rw, TILE + (j + 1) * rw)
    for j in range(nq):
        gate(1, j)
        post(0, j)
    for j in range(nq):
        post(1, j)


def _glu(xa, xb, off_b, ys, mods, layer, g, dvec, w, b):
    nblk = NCOL // 2
    half = NPC // 2
    pa = lambda i: (jnp.minimum(i, half - 1), 0)
    pb = lambda i: (jnp.maximum(i - half, 0), 0)
    return pl.pallas_call(
        _glu_kernel,
        grid=(nblk,),
        in_specs=[pl.BlockSpec((GT, D), pa),
                  pl.BlockSpec((GT, D), lambda i: (off_b // 2 + jnp.maximum(i - half, 0), 0)),
                  pl.BlockSpec((GT, D), pa),
                  pl.BlockSpec((GT, D), pa),
                  pl.BlockSpec((GT, D), pb),
                  pl.BlockSpec((GT, D), pb),
                  pl.BlockSpec((None, None, 6, D), lambda i: (layer, _cond_of_col(2 * i), 0, 0)),
                  pl.BlockSpec((1, D), lambda i: (0, 0)),
                  pl.BlockSpec((1, D), lambda i: (0, 0)),
                  pl.BlockSpec((None, D, 2 * D), lambda i: (layer // 2, 0, 0)),
                  pl.BlockSpec((1, 2 * D), lambda i: (0, 0))],
        out_specs=pl.BlockSpec((GT, D), lambda i: (i, 0)),
        out_shape=jax.ShapeDtypeStruct((NCOL * TILE, D), F32),
        scratch_shapes=[pltpu.VMEM((D, 2 * D), BF16), pltpu.VMEM((GT, D), BF16), pltpu.VMEM((GT, 2 * D), F32)],
        compiler_params=_cparams(1),
        name="s5_glu",
    )(xa, xb, *ys, mods, g, dvec, w, b)


HG = FGC // 2


def _fourier_kernel(x_ref, mod_ref, g_ref, mc_ref, cl_ref, sl_ref, w_ref, b_ref,
                    o_ref, *scratch, scale, nseg):
    mod = mod_ref[...]
    lane = lax.broadcasted_iota(jnp.int32, (TILE, HG), 1)
    lane0 = lane == 0

    def channel_dft():
        h = _normmod(x_ref[...], g_ref[...], mod[1:2], mod[0:1]).astype(BF16)
        zc, zs, zn = [], [], None
        for q in range(FG):
            z = jnp.dot(h[:, q * FGC:(q + 1) * FGC], mc_ref[...], preferred_element_type=F32)
            zc.append(z[:, 0:HG].astype(BF16))
            zs.append(z[:, HG:FGC].astype(BF16))
            nyq = jnp.where(lane0, z[:, HG:FGC], 0.0)
            nyq = pltpu.roll(nyq, q, 1) if q else nyq
            zn = nyq if zn is None else zn + nyq
        return jnp.concatenate(zc, axis=1), jnp.concatenate(zs, axis=1), zn.astype(BF16)

    def position_dft(xc, xs, xn):
        a = jnp.dot(cl_ref[...], xc, preferred_element_type=F32)
        bz = jnp.dot(sl_ref[...], xs, preferred_element_type=F32)
        an = jnp.dot(cl_ref[...], xn, preferred_element_type=F32)
        parts = []
        for q in range(FG):
            aq = a[:, q * HG:(q + 1) * HG]
            bq = jnp.where(lane0, 0.0, bz[:, q * HG:(q + 1) * HG])
            nq = pltpu.roll(an, HG - q, 1) if q else an
            parts += [aq - bq, jnp.where(lane0, nq, aq + bq)]
        f = jnp.concatenate(parts, axis=1) * scale
        o = jnp.dot(f.astype(BF16), w_ref[...], preferred_element_type=F32) + b_ref[...]
        o_ref[...] = x_ref[...] + mod[2:3] * o

    if nseg == 1:
        position_dft(*channel_dft())
        return

    xc_s, xs_s, xn_s = scratch
    ph = pl.program_id(1)
    s = pl.program_id(2)

    @pl.when(ph == 0)
    def _():
        r0 = pl.multiple_of(s * TILE, TILE)
        xc, xs, xn = channel_dft()
        xc_s[pl.ds(r0, TILE), :] = xc
        xs_s[pl.ds(r0, TILE), :] = xs
        xn_s[pl.ds(r0, TILE), :] = xn

    @pl.when(ph == 1)
    def _():
        position_dft(xc_s[...], xs_s[...], xn_s[...])


def _fourier(x, mods, layer, g, mc, cl, sl, w, b, *, nseq, nseg, col_off, cond_off, cond_stride):
    ln = nseg * TILE
    if nseg == 1:
        grid = (nseq,)
        idx = lambda f: (lambda q: f(q, 1, 0))
        scratch = []
    else:
        grid = (nseq, 2, nseg)
        idx = lambda f: f
        scratch = [pltpu.VMEM((ln, FG * HG), BF16), pltpu.VMEM((ln, FG * HG), BF16), pltpu.VMEM((ln, HG), BF16)]
    return pl.pallas_call(
        functools.partial(_fourier_kernel, scale=1.0 / math.sqrt(ln * FGC), nseg=nseg),
        grid=grid,
        in_specs=[pl.BlockSpec((TILE, D), idx(lambda q, ph, s: (col_off + q * nseg + s, 0))),
                  pl.BlockSpec((None, None, 6, D), idx(lambda q, ph, s: (layer, cond_off + q * cond_stride, 0, 0))),
                  pl.BlockSpec((1, D), idx(lambda q, ph, s: (0, 0))),
                  pl.BlockSpec((FGC, FGC), idx(lambda q, ph, s: (0, 0))),
                  pl.BlockSpec((TILE, ln), idx(lambda q, ph, s: (s * ph, 0))),
                  pl.BlockSpec((TILE, ln), idx(lambda q, ph, s: (s * ph, 0))),
                  pl.BlockSpec((None, D, D), idx(lambda q, ph, s: (layer // 2, 0, 0))),
                  pl.BlockSpec((1, D), idx(lambda q, ph, s: (0, 0)))],
        out_specs=pl.BlockSpec((TILE, D), idx(lambda q, ph, s: (col_off + q * nseg + s * ph, 0))),
        out_shape=jax.ShapeDtypeStruct((NCOL * TILE, D), F32),
        scratch_shapes=scratch,
        input_output_aliases={0: 0},
        compiler_params=_cparams(len(grid)),
        name="fourier%d" % nseg,
    )(x, mods, g, mc, cl, sl, w, b)


@functools.lru_cache(maxsize=None)
def _dft_mats(n):
    k = np.arange(n, dtype=np.int64)
    ang = ((k[:, None] * k[None, :]) % n).astype(np.float64) * (2.0 * math.pi / n)
    return np.cos(ang).astype(np.float32), np.sin(ang).astype(np.float32)


@functools.lru_cache(maxsize=None)
def _channel_mats():
    c, s = _dft_mats(FGC)
    return np.concatenate([c[:, 0:HG], c[:, HG:HG + 1], s[:, 1:HG]], axis=1)


WCH = DFF // FC


def _ffn_kernel(x_ref, mod_ref, g_ref, wup_hbm, cw_ref, cb_ref, wdn_hbm, gf_ref, *rest, final, layer):
    n_out = 2 if final else 1
    o_refs = rest[:n_out]
    wup_ref, wdn_ref, stg_u, stg_d, sem_u, sem_d = rest[n_out:]
    i = pl.program_id(0)

    @pl.when(i == 0)
    def _load_weights():
        def copies(c):
            slot = c % 2
            return (pltpu.make_async_copy(wup_hbm.at[layer, :, pl.ds(c * 2 * FC, 2 * FC)], stg_u.at[slot],
                                          sem_u.at[slot]),
                    pltpu.make_async_copy(wdn_hbm.at[layer, pl.ds(c * FC, FC), :], stg_d.at[slot],
                                          sem_d.at[slot]))

        for cp in copies(0):
            cp.start()
        for c in range(WCH):
            if c + 1 < WCH:
                for cp in copies(c + 1):
                    cp.start()
            for cp in copies(c):
                cp.wait()
            wup_ref[:, c * 2 * FC:(c + 1) * 2 * FC] = stg_u[c % 2].astype(BF16)
            wdn_ref[c * FC:(c + 1) * FC, :] = stg_d[c % 2].astype(BF16)

    x = x_ref[...]
    mod = mod_ref[...]
    h = _normmod(x, g_ref[...], mod[4:5], mod[3:4]).astype(BF16)
    sub = lax.broadcasted_iota(jnp.int32, (8, FC), 0)
    one = jnp.ones((8, FC), F32)
    first0 = (sub != 0).astype(F32)
    last0 = (sub != 7).astype(F32)
    inner_first = jnp.where(i < NPC, one, first0)
    inner_last = jnp.where(i < NPC, one, last0)
    prev_slabs = [(0, first0)] + [(r, inner_first) for r in range(GRID_W, TILE, GRID_W)]
    next_slabs = [(r - 8, inner_last) for r in range(GRID_W, TILE, GRID_W)] + [(TILE - 8, last0)]

    def mask_rows(a, slabs):
        parts, last = [], 0
        for r0, m in slabs:
            if r0 > last:
                parts.append(a[last:r0])
            parts.append(a[r0:r0 + 8] * m)
            last = r0 + 8
        if last < TILE:
            parts.append(a[last:])
        return jnp.concatenate(parts, axis=0)

    def conv(up, off):
        w = cw_ref[:, off:off + FC]
        prev = mask_rows(pltpu.roll(up, 1, 0), prev_slabs)
        nxt = mask_rows(pltpu.roll(up, TILE - 1, 0), next_slabs)
        return w[0:1] * prev + w[1:2] * up + w[2:3] * nxt + cb_ref[:, off:off + FC]

    def up_chunk(c):
        og = c * FC
        ov = DFF + c * FC
        return (jnp.dot(h, wup_ref[:, og:og + FC], preferred_element_type=F32),
                jnp.dot(h, wup_ref[:, ov:ov + FC], preferred_element_type=F32))

    def down_chunk(act, c):
        return jnp.dot(act, wdn_ref[c * FC:(c + 1) * FC, :], preferred_element_type=F32)

    nchunk = DFF // FC
    acc = jnp.zeros((TILE, D), F32)
    nxt_up = up_chunk(0)
    act = None
    for c in range(nchunk):
        cur_up = nxt_up
        if c + 1 < nchunk:
            nxt_up = up_chunk(c + 1)
        if act is not None:
            acc = acc + down_chunk(act, c - 1)
        gate = conv(cur_up[0], c * FC)
        val = conv(cur_up[1], DFF + c * FC)
        hg = 0.5 * gate
        act = ((hg + hg * jnp.tanh(hg)) * val).astype(BF16)
    acc = acc + down_chunk(act, nchunk - 1)
    y = x + mod[5:6] * acc
    if not final:
        o_refs[0][...] = y
    else:
        ms = jnp.mean(y * y, axis=-1, keepdims=True)
        y = y * lax.rsqrt(ms + EPS) * gf_ref[...]

        @pl.when(i < NPC)
        def _():
            o_refs[0][...] = y

        @pl.when(i >= NPC)
        def _():
            o_refs[1][...] = y


def _ffn(x, mods, layer, g, wup, cw, cb, wdn, gf, *, final):
    if final:
        out_specs = [pl.BlockSpec((TILE, D), lambda i: (jnp.minimum(i, NPC - 1), 0)),
                     pl.BlockSpec((TILE, D), lambda i: (jnp.maximum(i - NPC, 0), 0))]
        out_shape = [jax.ShapeDtypeStruct((NPC * TILE, D), F32)] * 2
    else:
        out_specs = pl.BlockSpec((TILE, D), lambda i: (i, 0))
        out_shape = jax.ShapeDtypeStruct((NCOL * TILE, D), F32)
    return pl.pallas_call(
        functools.partial(_ffn_kernel, final=final, layer=layer),
        grid=(NCOL,),
        in_specs=[pl.BlockSpec((TILE, D), lambda i: (i, 0)),
                  pl.BlockSpec((None, None, 6, D), lambda i: (layer, _cond_of_col(i), 0, 0)),
                  pl.BlockSpec((1, D), lambda i: (0, 0)),
                  pl.BlockSpec(memory_space=pl.ANY),
                  pl.BlockSpec((None, 3, 2 * DFF), lambda i: (layer, 0, 0)),
                  pl.BlockSpec((None, 1, 2 * DFF), lambda i: (layer, 0, 0)),
                  pl.BlockSpec(memory_space=pl.ANY),
                  pl.BlockSpec((1, D), lambda i: (0, 0))],
        out_specs=out_specs,
        out_shape=out_shape,
        scratch_shapes=[pltpu.VMEM((D, 2 * DFF), BF16), pltpu.VMEM((DFF, D), BF16),
                        pltpu.VMEM((2, D, 2 * FC), F32), pltpu.VMEM((2, FC, D), F32),
                        pltpu.SemaphoreType.DMA((2,)), pltpu.SemaphoreType.DMA((2,))],
        compiler_params=_cparams(1),
        name="ffn",
    )(x, mods, g, wup, cw, cb, wdn, gf)


def kernel(x_prompt, x_sample, state_ssm_re, state_ssm_im, c, c_ctx, w_ada, b_ada, g_mix, g_ffn,
           ssm_lam_re, ssm_lam_im, ssm_log_dt, ssm_b_re, ssm_b_im, ssm_c_re, ssm_c_im, ssm_d,
           w_glu, b_glu, w_fourier, b_fourier, w_up, conv_w, conv_b, w_down, g_final):
    nb = x_prompt.shape[0]
    xp = x_prompt.reshape(NPC * TILE, D)
    xs = x_sample.reshape(NPC * TILE, D)
    x = None

    cond8 = jnp.concatenate([c_ctx[None, :], c, jnp.zeros((5, D), F32)], axis=0)
    mods = _ada(cond8, w_ada, b_ada).reshape(DEPTH, 8, 6, D)
    col_cond = jnp.asarray([0] * NPC + [1] * SEG + [2] * SEG, jnp.int32)
    mods_t = mods.transpose(0, 2, 1, 3)[:, :, col_cond]

    mc = jnp.asarray(_channel_mats()).astype(BF16)
    wf4 = w_fourier.reshape(-1, FG, FGC, D)
    wf_perm = jnp.concatenate([wf4[:, :, :HG + 1], jnp.flip(wf4[:, :, HG + 1:], axis=2)], axis=2)
    wf_bf = wf_perm.astype(BF16).reshape(-1, D, D)
    cl1, sl1 = (jnp.asarray(m).astype(BF16) for m in _dft_mats(TILE))
    cl8, sl8 = (jnp.asarray(m).astype(BF16) for m in _dft_mats(SEG * TILE))

    njd = ssm_lam_re.shape[0] * 2
    lam_re = ssm_lam_re.reshape(njd, 1, NST)
    lam_im = ssm_lam_im.reshape(njd, 1, NST)
    dts = jnp.repeat(ssm_log_dt.reshape(njd, GS), PS, axis=-1).reshape(njd, 1, NST)
    bt_re = ssm_b_re.transpose(0, 1, 4, 2, 3).reshape(njd, GC, NST)
    bt_im = ssm_b_im.transpose(0, 1, 4, 2, 3).reshape(njd, GC, NST)
    ct_re = ssm_c_re.transpose(0, 1, 3, 2, 4).reshape(njd, GC, NST)
    ct_im = ssm_c_im.transpose(0, 1, 3, 2, 4).reshape(njd, GC, NST)
    st_re4 = state_ssm_re.reshape(2, njd, 1, NST)
    st_im4 = state_ssm_im.reshape(2, njd, 1, NST)

    edges = []
    for i in range(DEPTH):
        j = i // 2
        if i % 2 == 0:
            g = g_mix[i][None, :]
            xa, xb, off_b = (xp, xs, 0) if x is None else (x, x, NPC)
            xa3 = xa.reshape(-1, TILE, D)
            xb3 = xb.reshape(-1, TILE, D)
            par = (mods_t, i, g, lam_re, lam_im, dts, bt_re, bt_im, ct_re, ct_im)
            kw = dict(emit_edge=False, emit_fin=True, emit_y=False, trunk=1, init="state", h0=(st_re4, st_im4))
            u_s, frf, fif = _s5_scan(xb3, off_b, *par, jd=2 * j, rev=False, u_mode="make", **kw)
            frb, fib = _s5_scan(u_s, 0, *par, jd=2 * j + 1, rev=True, u_mode="load", **kw)
            carry = _carry(frf, fif, frb, fib, st_re4, st_im4, lam_re, lam_im, dts, j=j)
            kw = dict(trunk=0, init="zero", emit_y=True, emit_edge=True, emit_fin=False)
            yfa, u_p, er, ei = _s5_scan(xa3, 0, *par, jd=2 * j, rev=False, u_mode="make", **kw)
            edges += [er, ei]
            yba, er, ei = _s5_scan(u_p, 0, *par, jd=2 * j + 1, rev=True, u_mode="load", **kw)
            edges += [er, ei]
            kw = dict(trunk=1, init="carry", h0=carry, u_mode="load", emit_y=True, emit_edge=False, emit_fin=False)
            (yfb,) = _s5_scan(u_s, 0, *par, jd=2 * j, rev=False, **kw)
            (ybb,) = _s5_scan(u_s, 0, *par, jd=2 * j + 1, rev=True, **kw)
            ys = [y.reshape(NPC * TILE, D) for y in (yfa, yba, yfb, ybb)]
            x = _glu(xa, xb, off_b, ys, mods, i, g, ssm_d[j][None, :], w_glu, b_glu[j][None, :])
        else:
            g = g_mix[i][None, :]
            bf = b_fourier[j][None, :]
            x = _fourier(x, mods, i, g, mc, cl1, sl1, wf_bf, bf,
                         nseq=NPC, nseg=1, col_off=0, cond_off=0, cond_stride=0)
            x = _fourier(x, mods, i, g, mc, cl8, sl8, wf_bf, bf,
                         nseq=2, nseg=SEG, col_off=NPC, cond_off=1, cond_stride=1)
        x = _ffn(x, mods, i, g_ffn[i][None, :], w_up, conv_w, conv_b.reshape(DEPTH, 1, 2 * DFF),
                 w_down, g_final[None, :], final=(i == DEPTH - 1))

    y_prompt = x[0].reshape(NPC, TILE, D)
    y_sample = x[1].reshape(2, SEG * TILE, D)
    ed = jnp.stack(edges, axis=0).reshape(DEPTH // 2, 2, 2, NPC, NST)[:, :, :, :nb]
    new_re = ed[:, :, 0].transpose(2, 0, 1, 3).reshape(nb, DEPTH // 2, 2, GS, PS)
    new_im = ed[:, :, 1].transpose(2, 0, 1, 3).reshape(nb, DEPTH // 2, 2, GS, PS)
    return (y_prompt, y_sample, new_re, new_im)
```

```python
import functools
import math

import jax
import jax.numpy as jnp
import numpy as np
from jax import lax
from jax.experimental import pallas as pl
from jax.experimental.pallas import tpu as pltpu

F32 = jnp.float32
BF16 = jnp.bfloat16

D = 1024
TILE = 256
NCOL = 32
NPC = 16
SEG = 8
DEPTH = 4
GS = 64
GC = 16
PS = 64
NST = GS * PS
FG = 4
FGC = 256
DFF = 2816
EPS = 1e-6
GRID_W = 64

TT = 32
NCH = TILE // TT
SW = 512
NSLAB = NST // SW
FC = 256

VMEM_LIMIT = 56 * 1024 * 1024


def _cparams(n_axes):
    return pltpu.CompilerParams(dimension_semantics=("arbitrary",) * n_axes,
                                vmem_limit_bytes=VMEM_LIMIT)


def _normmod(x, g, sc, sh):
    ms = jnp.mean(x * x, axis=-1, keepdims=True)
    return x * lax.rsqrt(ms + EPS) * (g * (1.0 + sc)) + sh


def _sigmoid(x):
    return 1.0 / (1.0 + jnp.exp(-x))


def _cond_of_col(i):
    return jnp.where(i < NPC, 0, 1 + (i - NPC) // SEG)


def _ada_kernel(c_ref, w_ref, b_ref, o_ref):
    c = c_ref[...]
    s = (c * _sigmoid(c)).astype(BF16)
    o_ref[...] = jnp.dot(s, w_ref[...].astype(BF16), preferred_element_type=F32) + b_ref[...]


def _ada(cond8, w_ada, b_ada):
    tn = 1536
    return pl.pallas_call(
        _ada_kernel,
        grid=(DEPTH, 6 * D // tn),
        in_specs=[pl.BlockSpec((8, D), lambda l, n: (0, 0)),
                  pl.BlockSpec((None, D, tn), lambda l, n: (l, 0, n)),
                  pl.BlockSpec((None, 1, tn), lambda l, n: (l, 0, n))],
        out_specs=pl.BlockSpec((None, 8, tn), lambda l, n: (l, 0, n)),
        out_shape=jax.ShapeDtypeStruct((DEPTH, 8, 6 * D), F32),
        compiler_params=_cparams(2),
        name="ada",
    )(cond8, w_ada, b_ada.reshape(DEPTH, 1, 6 * D))


def _abar(lr, li, logdt):
    dt = jnp.exp(logdt)
    mag = jnp.exp(lr * dt)
    return mag * jnp.cos(li * dt), mag * jnp.sin(li * dt)


PK = 2
NG = TT // PK
GROWS = NG * NPC


def _nt_dot(a, b):
    return lax.dot_general(a, b, (((1,), (1,)), ((), ())), preferred_element_type=F32)


def _s5_kernel(*refs, rev, col_base, init, u_mode, emit_y, emit_edge, emit_fin, add_skip):
    src_hbm, sc_ref, sh_ref, g_ref, lr_ref, li_ref, dt_ref, btr_ref, bti_ref, ctr_ref, cti_ref = refs[:11]
    pos = 11
    if init != "zero":
        h0r_ref, h0i_ref = refs[pos:pos + 2]
        pos += 2
    if add_skip:
        skip_ref = refs[pos]
        pos += 1
    if emit_y:
        y_hbm = refs[pos]
        pos += 1
    if u_mode == "make":
        u_hbm = refs[pos]
        pos += 1
    if emit_edge:
        er_ref, ei_ref = refs[pos:pos + 2]
        pos += 2
    if emit_fin:
        fr_ref, fi_ref = refs[pos:pos + 2]
        pos += 2
    scratch = refs[pos:]
    d_re = scratch[0:NSLAB]
    d_im = scratch[NSLAB:2 * NSLAB]
    bp_s, ap_r, ap_i, hs_re, hs_im, inbuf, sem_in = scratch[2 * NSLAB:2 * NSLAB + 7]
    pos = 2 * NSLAB + 7
    if u_mode == "make":
        ubuf, sem_u = scratch[pos:pos + 2]
        pos += 2
    if emit_y:
        ybuf, sem_out, cp_s, ki_s = scratch[pos:pos + 4]
        hb = scratch[pos + 4:pos + 4 + NSLAB]
        pos += 4 + NSLAB
    if emit_edge:
        ed_re, ed_im, a1_r, a1_i = scratch[pos:pos + 4]

    i = pl.program_id(0)
    slot = lax.rem(i, 2)

    def chunk(i_):
        return NCH - 1 - i_ if rev else i_

    def in_copies(i_, slot_):
        if u_mode == "load":
            return [pltpu.make_async_copy(src_hbm.at[chunk(i_)], inbuf.at[slot_], sem_in.at[slot_])]
        t0 = pl.multiple_of(chunk(i_) * TT, TT)
        return [pltpu.make_async_copy(src_hbm.at[col_base + c, pl.ds(t0, TT), :],
                                      inbuf.at[slot_, :, c, :], sem_in.at[slot_]) for c in range(NPC)]

    def u_copies(i_, slot_):
        return [pltpu.make_async_copy(ubuf.at[slot_], u_hbm.at[chunk(i_)], sem_u.at[slot_])]

    def y_copies(i_, slot_):
        t0 = pl.multiple_of(chunk(i_) * TT, TT)
        return [pltpu.make_async_copy(ybuf.at[slot_, :, c, :], y_hbm.at[c, pl.ds(t0, TT), :],
                                      sem_out.at[slot_]) for c in range(NPC)]

    fetches = in_copies

    def stores(i_, slot_):
        return (y_copies(i_, slot_) if emit_y else []) + (u_copies(i_, slot_) if u_mode == "make" else [])

    @pl.when(i == 0)
    def _first_fetch():
        for cp in fetches(i, slot):
            cp.start()

    if emit_y or u_mode == "make":
        @pl.when(i >= 2)
        def _out_bufs_free():
            for cp in stores(i, slot):
                cp.wait()

    @pl.when(i == 0)
    def _prep():
        lr = lr_ref[...]
        li = li_ref[...]
        ar, ai = _abar(lr, li, dt_ref[...])
        pw = [(jnp.ones_like(ar), jnp.zeros_like(ai)), (ar, ai)]
        while len(pw) <= PK:
            pr, pi_ = pw[-1]
            pw.append((pr * ar - pi_ * ai, pr * ai + pi_ * ar))
        ap_r[...] = jnp.broadcast_to(pw[PK][0], (8, NST))
        ap_i[...] = jnp.broadcast_to(pw[PK][1], (8, NST))
        if emit_edge:
            a1_r[...] = jnp.broadcast_to(ar, (NPC, NST))
            a1_i[...] = jnp.broadcast_to(ai, (NPC, NST))
        xr = ar - 1.0
        den = lr * lr + li * li
        fr = (xr * lr + ai * li) / den
        fi = (ai * lr - xr * li) / den
        btr = btr_ref[...]
        bti = bti_ref[...]
        bbr = fr * btr - fi * bti
        bbi = fr * bti + fi * btr
        r = lax.broadcasted_iota(jnp.int32, (128, SW), 0)
        c = lax.broadcasted_iota(jnp.int32, (128, SW), 1)
        gmask = ((r >> 4) == (c >> 6)).astype(F32)

        def tiles(vr, vi, m, sign):
            sl = slice(m * SW, (m + 1) * SW)
            tr = jnp.concatenate([vr[:, sl]] * 8, axis=0) * gmask
            ti = jnp.concatenate([vi[:, sl]] * 8, axis=0) * (sign * gmask)
            return jnp.concatenate([tr, ti], axis=1).astype(BF16)

        def times(vr, vi, k):
            pr, pi_ = pw[k]
            return vr * pr - vi * pi_, vr * pi_ + vi * pr

        ab = [times(bbr, bbi, k) for k in range(PK)]
        for m in range(NSLAB):
            for p in range(PK):
                bp_s[m, p * 128:(p + 1) * 128, :] = tiles(*ab[PK - 1 - p], m, 1.0)
        if emit_y:
            ctr = ctr_ref[...]
            cti = cti_ref[...]
            ca = [times(ctr, cti, k) for k in range(PK + 1)]
            for m in range(NSLAB):
                for p in range(PK):
                    cp_s[m, p * 128:(p + 1) * 128, :] = tiles(*ca[p + 1], m, -1.0)
                c0 = tiles(*ca[0], m, -1.0)
                kt = [_nt_dot(tiles(*ab[k], m, 1.0), c0) for k in range(PK)]
                if add_skip:
                    eye = (lax.broadcasted_iota(jnp.int32, (128, 128), 0)
                           == lax.broadcasted_iota(jnp.int32, (128, 128), 1))
                    kt[0] = kt[0] + jnp.where(eye, skip_ref[:, m * 128:(m + 1) * 128], 0.0)
                kt = [k_.astype(BF16) for k_ in kt]
                zero = jnp.zeros((128, 128), BF16)
                for q in range(PK):
                    for p in range(PK):
                        ki_s[m, q * 128:(q + 1) * 128, p * 128:(p + 1) * 128] = kt[p - q] if p >= q else zero
        if init == "carry":
            hs_re[...] = h0r_ref[...]
            hs_im[...] = h0i_ref[...]
        else:
            hs_re[...] = jnp.zeros((NPC, NST), F32)
            hs_im[...] = jnp.zeros((NPC, NST), F32)
            if init == "state":
                seg0 = SEG - 1 if rev else 0
                for b in range(2):
                    row = b * SEG + seg0
                    hs_re[row:row + 1, :] = h0r_ref[b]
                    hs_im[row:row + 1, :] = h0i_ref[b]

    for cp in fetches(i, slot):
        cp.wait()
    for cp in fetches(jnp.minimum(i + 1, NCH - 1), 1 - slot):
        cp.start()

    if u_mode == "make":
        x3 = inbuf[slot]
        ms = jnp.mean(x3 * x3, axis=-1, keepdims=True)
        gain = g_ref[...] * (1.0 + sc_ref[...])
        u3 = x3 * lax.rsqrt(ms + EPS) * gain[None] + sh_ref[...][None]
        u = u3.reshape(TT * NPC, D).astype(BF16)
        ubuf[slot] = u
    else:
        u = inbuf[slot]

    def step_rows(a, b):
        t = TT - 1 - b if rev else b
        return a[t * NPC:(t + 1) * NPC]

    def packed(m):
        lanes = slice(m * 128, (m + 1) * 128)
        return jnp.concatenate(
            [jnp.concatenate([step_rows(u, PK * j + p)[:, lanes] for p in range(PK)], axis=1) for j in range(NG)],
            axis=0)

    def expand(m):
        res = jnp.dot(packed(m), bp_s[m], preferred_element_type=F32)
        d_re[m][...] = res[:, 0:SW]
        d_im[m][...] = res[:, SW:2 * SW]

    def recur(s):
        lanes = slice(s * SW, (s + 1) * SW)
        ar = ap_r[:, lanes]
        ai = ap_i[:, lanes]
        h = [hs_re[0:8, lanes], hs_im[0:8, lanes], hs_re[8:16, lanes], hs_im[8:16, lanes]]
        for j in range(NG):
            if emit_y:
                jrow = slice(j * NPC, (j + 1) * NPC)
                hb[s][jrow, 0:SW] = jnp.concatenate([h[0], h[2]], axis=0).astype(BF16)
                hb[s][jrow, SW:2 * SW] = jnp.concatenate([h[1], h[3]], axis=0).astype(BF16)
            for q in range(2):
                rows = slice(j * NPC + 8 * q, j * NPC + 8 * q + 8)
                hr, hi = h[2 * q], h[2 * q + 1]
                h[2 * q] = ar * hr - ai * hi + d_re[s][rows, :]
                h[2 * q + 1] = ar * hi + ai * hr + d_im[s][rows, :]
        hs_re[0:8, lanes] = h[0]
        hs_im[0:8, lanes] = h[1]
        hs_re[8:16, lanes] = h[2]
        hs_im[8:16, lanes] = h[3]

    def contract(m):
        yp = _nt_dot(hb[m][...], cp_s[m]) + jnp.dot(packed(m), ki_s[m], preferred_element_type=F32)
        for j in range(NG):
            for p in range(PK):
                b = PK * j + p
                t = TT - 1 - b if rev else b
                ybuf[slot, t, :, m * 128:(m + 1) * 128] = yp[j * NPC:(j + 1) * NPC, p * 128:(p + 1) * 128]

    if emit_edge:
        u0 = step_rows(u, 0)
        for m in range(NSLAB):
            lanes = slice(m * SW, (m + 1) * SW)
            bu0 = jnp.dot(u0[:, m * 128:(m + 1) * 128], bp_s[m, (PK - 1) * 128:PK * 128, :],
                          preferred_element_type=F32)
            hr, hi = hs_re[:, lanes], hs_im[:, lanes]
            ed_re[:, lanes] = a1_r[:, lanes] * hr - a1_i[:, lanes] * hi + bu0[:, 0:SW]
            ed_im[:, lanes] = a1_r[:, lanes] * hi + a1_i[:, lanes] * hr + bu0[:, SW:2 * SW]

    expand(0)
    for s in range(NSLAB):
        if s + 1 < NSLAB:
            expand(s + 1)
        if emit_y and s >= 1:
            contract(s - 1)
        recur(s)
    if emit_y:
        contract(NSLAB - 1)
    for cp in stores(i, slot):
        cp.start()

    if emit_edge:
        @pl.when(i == 0)
        def _edge():
            er_ref[...] = ed_re[...]
            ei_ref[...] = ed_im[...]

    @pl.when(i == NCH - 1)
    def _last():
        if emit_fin:
            fr_ref[...] = hs_re[...]
            fi_ref[...] = hs_im[...]
        for cp in fetches(i, 1 - slot) + stores(i, slot) + stores(i, 1 - slot):
            cp.wait()


def _s5_scan(src, col_base, mods_t, layer, g, lam_re, lam_im, dts, bt_re, bt_im, ct_re, ct_im, *,
             jd, rev, trunk, init, h0=None, skip=None, u_mode, emit_y, emit_edge, emit_fin):
    par = lambda i: (jd, 0, 0)
    in_specs = [
        pl.BlockSpec(memory_space=pl.ANY),
        pl.BlockSpec((None, None, NPC, D), lambda i: (layer, 1, trunk, 0)),
        pl.BlockSpec((None, None, NPC, D), lambda i: (layer, 0, trunk, 0)),
        pl.BlockSpec((1, D), lambda i: (0, 0)),
        pl.BlockSpec((None, 1, NST), par),
        pl.BlockSpec((None, 1, NST), par),
        pl.BlockSpec((None, 1, NST), par),
        pl.BlockSpec((None, GC, NST), par),
        pl.BlockSpec((None, GC, NST), par),
        pl.BlockSpec((None, GC, NST), par),
        pl.BlockSpec((None, GC, NST), par),
    ]
    args = [src, mods_t, mods_t, g, lam_re, lam_im, dts, bt_re, bt_im, ct_re, ct_im]
    if init == "state":
        in_specs += [pl.BlockSpec((2, None, 1, NST), lambda i: (0, jd, 0, 0))] * 2
        args += list(h0)
    elif init == "carry":
        in_specs += [pl.BlockSpec((None, NPC, NST), lambda i: (1 if rev else 0, 0, 0))] * 2
        args += list(h0)
    if skip is not None:
        in_specs.append(pl.BlockSpec((1, D), lambda i: (0, 0)))
        args.append(skip)
    out_specs, out_shape = [], []
    slab = pltpu.VMEM((GROWS, SW), F32)
    scratch = [slab] * (2 * NSLAB) + [
        pltpu.VMEM((NSLAB, PK * 128, 2 * SW), BF16),
        pltpu.VMEM((8, NST), F32), pltpu.VMEM((8, NST), F32),
        pltpu.VMEM((NPC, NST), F32), pltpu.VMEM((NPC, NST), F32),
        pltpu.VMEM((2, TT, NPC, D), F32) if u_mode == "make" else pltpu.VMEM((2, TT * NPC, D), BF16),
        pltpu.SemaphoreType.DMA((2,))]
    if emit_y:
        out_specs.append(pl.BlockSpec(memory_space=pl.ANY))
        out_shape.append(jax.ShapeDtypeStruct((NPC, TILE, D), F32))
    if u_mode == "make":
        out_specs.append(pl.BlockSpec(memory_space=pl.ANY))
        out_shape.append(jax.ShapeDtypeStruct((NCH, TT * NPC, D), BF16))
        scratch += [pltpu.VMEM((2, TT * NPC, D), BF16), pltpu.SemaphoreType.DMA((2,))]
    if emit_y:
        scratch += [pltpu.VMEM((2, TT, NPC, D), F32), pltpu.SemaphoreType.DMA((2,)),
                    pltpu.VMEM((NSLAB, PK * 128, 2 * SW), BF16), pltpu.VMEM((NSLAB, PK * 128, PK * 128), BF16)]
        scratch += [pltpu.VMEM((GROWS, 2 * SW), BF16)] * NSLAB
    if emit_edge:
        scratch += [pltpu.VMEM((NPC, NST), F32)] * 4
    n_state_outs = 2 * (int(emit_edge) + int(emit_fin))
    out_specs += [pl.BlockSpec((NPC, NST), lambda i: (0, 0))] * n_state_outs
    out_shape += [jax.ShapeDtypeStruct((NPC, NST), F32)] * n_state_outs
    return pl.pallas_call(
        functools.partial(_s5_kernel, rev=rev, col_base=col_base, init=init, u_mode=u_mode, emit_y=emit_y,
                          emit_edge=emit_edge, emit_fin=emit_fin, add_skip=skip is not None),
        grid=(NCH,),
        in_specs=in_specs,
        out_specs=out_specs,
        out_shape=out_shape,
        scratch_shapes=scratch,
        compiler_params=_cparams(1),
        name=("s5_scan" if emit_y else "s5_states") + ("_bwd" if rev else "_fwd") + str(trunk),
    )(*args)


def _carry_kernel(frf_ref, fif_ref, frb_ref, fib_ref, sr_ref, si_ref, lr_ref, li_ref, dt_ref,
                  or_ref, oi_ref, *, j):
    for d, (fr_ref, fi_ref) in enumerate(((frf_ref, fif_ref), (frb_ref, fib_ref))):
        jd = 2 * j + d
        ar, ai = _abar(lr_ref[jd], li_ref[jd], dt_ref[jd])
        for _ in range(8):
            ar, ai = ar * ar - ai * ai, 2.0 * ar * ai
        for b in range(2):
            order = list(range(SEG)) if d == 0 else list(range(SEG - 1, -1, -1))
            r = b * SEG + order[0]
            or_ref[d, r:r + 1, :] = sr_ref[b, jd]
            oi_ref[d, r:r + 1, :] = si_ref[b, jd]
            tr = fr_ref[r:r + 1, :]
            ti = fi_ref[r:r + 1, :]
            for s in order[1:]:
                r = b * SEG + s
                or_ref[d, r:r + 1, :] = tr
                oi_ref[d, r:r + 1, :] = ti
                tr, ti = (fr_ref[r:r + 1, :] + ar * tr - ai * ti,
                          fi_ref[r:r + 1, :] + ar * ti + ai * tr)


def _carry(frf, fif, frb, fib, st_re4, st_im4, lam_re, lam_im, dts, *, j):
    shp = jax.ShapeDtypeStruct((2, NPC, NST), F32)
    return pl.pallas_call(functools.partial(_carry_kernel, j=j), out_shape=[shp, shp], name="s5_carry")(
        frf, fif, frb, fib, st_re4, st_im4, lam_re, lam_im, dts)


GT = 2 * TILE


def _glu_kernel(xa_ref, xb_ref, yfa_ref, yba_ref, yfb_ref, ybb_ref, mod_ref, w_ref, b_ref,
                o_ref, wbf, ge_s, z_s):
    i = pl.program_id(0)

    @pl.when(i == 0)
    def _():
        wbf[...] = w_ref[...].astype(BF16)

    mod = mod_ref[...]
    first_prompt = i < NPC // 2

    nq = 4
    cw = D // nq
    rw = TILE // nq

    def x_rows(r0, r1, c0=0, c1=D):
        return jnp.where(first_prompt, xa_ref[r0:r1, c0:c1], xb_ref[r0:r1, c0:c1])

    def pre(r0, r1):
        y = jnp.where(first_prompt, yfa_ref[r0:r1, :] + yba_ref[r0:r1, :], yfb_ref[r0:r1, :] + ybb_ref[r0:r1, :])
        k1 = -2.0 * math.sqrt(2.0 / math.pi)
        ge = y / (1.0 + jnp.exp(y * (k1 + (k1 * 0.044715) * (y * y))))
        ge_s[r0:r1, :] = ge.astype(BF16)

    def gate(q, j):
        rows = slice(q * TILE, (q + 1) * TILE)
        for c0 in (j * cw, D + j * cw):
            z_s[rows, c0:c0 + cw] = (jnp.dot(ge_s[rows, :], wbf[:, c0:c0 + cw], preferred_element_type=F32)
                                     + b_ref[:, c0:c0 + cw])

    def post(q, j):
        rows = slice(q * TILE, (q + 1) * TILE)
        c0, c1 = j * cw, (j + 1) * cw
        out = z_s[rows, c0:c1] * _sigmoid(z_s[rows, D + c0:D + c1])
        o_ref[rows, c0:c1] = x_rows(q * TILE, (q + 1) * TILE, c0, c1) + mod[2:3, c0:c1] * out

    for j in range(nq):
        pre(j * rw, (j + 1) * rw)
    for j in range(nq):
        gate(0, j)
        pre(TILE + j * rw, TILE + (j + 1) * rw)
    for j in range(nq):
        gate(1, j)
        post(0, j)
    for j in range(nq):
        post(1, j)


def _glu(xa, xb, off_b, ys, mods, layer, w, b):
    nblk = NCOL // 2
    half = NPC // 2
    pa = lambda i: (jnp.minimum(i, half - 1), 0)
    pb = lambda i: (jnp.maximum(i - half, 0), 0)
    return pl.pallas_call(
        _glu_kernel,
        grid=(nblk,),
        in_specs=[pl.BlockSpec((GT, D), pa),
                  pl.BlockSpec((GT, D), lambda i: (off_b // 2 + jnp.maximum(i - half, 0), 0)),
                  pl.BlockSpec((GT, D), pa),
                  pl.BlockSpec((GT, D), pa),
                  pl.BlockSpec((GT, D), pb),
                  pl.BlockSpec((GT, D), pb),
                  pl.BlockSpec((None, None, 6, D), lambda i: (layer, _cond_of_col(2 * i), 0, 0)),
                  pl.BlockSpec((None, D, 2 * D), lambda i: (layer // 2, 0, 0)),
                  pl.BlockSpec((1, 2 * D), lambda i: (0, 0))],
        out_specs=pl.BlockSpec((GT, D), lambda i: (i, 0)),
        out_shape=jax.ShapeDtypeStruct((NCOL * TILE, D), F32),
        scratch_shapes=[pltpu.VMEM((D, 2 * D), BF16), pltpu.VMEM((GT, D), BF16), pltpu.VMEM((GT, 2 * D), F32)],
        compiler_params=_cparams(1),
        name="s5_glu",
    )(xa, xb, *ys, mods, w, b)


HG = FGC // 2


def _fourier_kernel(x_ref, mod_ref, g_ref, mc_ref, cl_ref, sl_ref, w_ref, b_ref,
                    o_ref, *scratch, scale, nseg):
    mod = mod_ref[...]
    lane = lax.broadcasted_iota(jnp.int32, (TILE, HG), 1)
    lane0 = lane == 0

    def channel_dft():
        h = _normmod(x_ref[...], g_ref[...], mod[1:2], mod[0:1]).astype(BF16)
        zc, zs, zn = [], [], None
        for q in range(FG):
            z = jnp.dot(h[:, q * FGC:(q + 1) * FGC], mc_ref[...], preferred_element_type=F32)
            zc.append(z[:, 0:HG].astype(BF16))
            zs.append(z[:, HG:FGC].astype(BF16))
            nyq = jnp.where(lane0, z[:, HG:FGC], 0.0)
            nyq = pltpu.roll(nyq, q, 1) if q else nyq
            zn = nyq if zn is None else zn + nyq
        return jnp.concatenate(zc, axis=1), jnp.concatenate(zs, axis=1), zn.astype(BF16)

    def position_dft(xc, xs, xn):
        a = jnp.dot(cl_ref[...], xc, preferred_element_type=F32)
        bz = jnp.dot(sl_ref[...], xs, preferred_element_type=F32)
        an = jnp.dot(cl_ref[...], xn, preferred_element_type=F32)
        parts = []
        for q in range(FG):
            aq = a[:, q * HG:(q + 1) * HG]
            bq = jnp.where(lane0, 0.0, bz[:, q * HG:(q + 1) * HG])
            nq = pltpu.roll(an, HG - q, 1) if q else an
            parts += [aq - bq, jnp.where(lane0, nq, aq + bq)]
        f = jnp.concatenate(parts, axis=1) * scale
        o = jnp.dot(f.astype(BF16), w_ref[...], preferred_element_type=F32) + b_ref[...]
        o_ref[...] = x_ref[...] + mod[2:3] * o

    if nseg == 1:
        position_dft(*channel_dft())
        return

    xc_s, xs_s, xn_s = scratch
    ph = pl.program_id(1)
    s = pl.program_id(2)

    @pl.when(ph == 0)
    def _():
        r0 = pl.multiple_of(s * TILE, TILE)
        xc, xs, xn = channel_dft()
        xc_s[pl.ds(r0, TILE), :] = xc
        xs_s[pl.ds(r0, TILE), :] = xs
        xn_s[pl.ds(r0, TILE), :] = xn

    @pl.when(ph == 1)
    def _():
        position_dft(xc_s[...], xs_s[...], xn_s[...])


def _fourier(x, mods, layer, g, mc, cl, sl, w, b, *, nseq, nseg, col_off, cond_off, cond_stride):
    ln = nseg * TILE
    if nseg == 1:
        grid = (nseq,)
        idx = lambda f: (lambda q: f(q, 1, 0))
        scratch = []
    else:
        grid = (nseq, 2, nseg)
        idx = lambda f: f
        scratch = [pltpu.VMEM((ln, FG * HG), BF16), pltpu.VMEM((ln, FG * HG), BF16), pltpu.VMEM((ln, HG), BF16)]
    return pl.pallas_call(
        functools.partial(_fourier_kernel, scale=1.0 / math.sqrt(ln * FGC), nseg=nseg),
        grid=grid,
        in_specs=[pl.BlockSpec((TILE, D), idx(lambda q, ph, s: (col_off + q * nseg + s, 0))),
                  pl.BlockSpec((None, None, 6, D), idx(lambda q, ph, s: (layer, cond_off + q * cond_stride, 0, 0))),
                  pl.BlockSpec((1, D), idx(lambda q, ph, s: (0, 0))),
                  pl.BlockSpec((FGC, FGC), idx(lambda q, ph, s: (0, 0))),
                  pl.BlockSpec((TILE, ln), idx(lambda q, ph, s: (s * ph, 0))),
                  pl.BlockSpec((TILE, ln), idx(lambda q, ph, s: (s * ph, 0))),
                  pl.BlockSpec((None, D, D), idx(lambda q, ph, s: (layer // 2, 0, 0))),
                  pl.BlockSpec((1, D), idx(lambda q, ph, s: (0, 0)))],
        out_specs=pl.BlockSpec((TILE, D), idx(lambda q, ph, s: (col_off + q * nseg + s * ph, 0))),
        out_shape=jax.ShapeDtypeStruct((NCOL * TILE, D), F32),
        scratch_shapes=scratch,
        input_output_aliases={0: 0},
        compiler_params=_cparams(len(grid)),
        name="fourier%d" % nseg,
    )(x, mods, g, mc, cl, sl, w, b)


@functools.lru_cache(maxsize=None)
def _dft_mats(n):
    k = np.arange(n, dtype=np.int64)
    ang = ((k[:, None] * k[None, :]) % n).astype(np.float64) * (2.0 * math.pi / n)
    return np.cos(ang).astype(np.float32), np.sin(ang).astype(np.float32)


@functools.lru_cache(maxsize=None)
def _channel_mats():
    c, s = _dft_mats(FGC)
    return np.concatenate([c[:, 0:HG], c[:, HG:HG + 1], s[:, 1:HG]], axis=1)


WCH = DFF // FC


def _ffn_kernel(x_ref, mod_ref, g_ref, wup_hbm, cw_ref, cb_ref, wdn_hbm, gf_ref, *rest, final, layer):
    n_out = 2 if final else 1
    o_refs = rest[:n_out]
    wup_ref, wdn_ref, stg_u, stg_d, sem_u, sem_d = rest[n_out:]
    i = pl.program_id(0)

    @pl.when(i == 0)
    def _load_weights():
        def copies(c):
            slot = c % 2
            return (pltpu.make_async_copy(wup_hbm.at[layer, :, pl.ds(c * 2 * FC, 2 * FC)], stg_u.at[slot],
                                          sem_u.at[slot]),
                    pltpu.make_async_copy(wdn_hbm.at[layer, pl.ds(c * FC, FC), :], stg_d.at[slot],
                                          sem_d.at[slot]))

        for cp in copies(0):
            cp.start()
        for c in range(WCH):
            if c + 1 < WCH:
                for cp in copies(c + 1):
                    cp.start()
            for cp in copies(c):
                cp.wait()
            wup_ref[:, c * 2 * FC:(c + 1) * 2 * FC] = stg_u[c % 2].astype(BF16)
            wdn_ref[c * FC:(c + 1) * FC, :] = stg_d[c % 2].astype(BF16)

    x = x_ref[...]
    mod = mod_ref[...]
    h = _normmod(x, g_ref[...], mod[4:5], mod[3:4]).astype(BF16)
    sub = lax.broadcasted_iota(jnp.int32, (8, FC), 0)
    one = jnp.ones((8, FC), F32)
    first0 = (sub != 0).astype(F32)
    last0 = (sub != 7).astype(F32)
    inner_first = jnp.where(i < NPC, one, first0)
    inner_last = jnp.where(i < NPC, one, last0)
    prev_slabs = [(0, first0)] + [(r, inner_first) for r in range(GRID_W, TILE, GRID_W)]
    next_slabs = [(r - 8, inner_last) for r in range(GRID_W, TILE, GRID_W)] + [(TILE - 8, last0)]

    def mask_rows(a, slabs):
        parts, last = [], 0
        for r0, m in slabs:
            if r0 > last:
                parts.append(a[last:r0])
            parts.append(a[r0:r0 + 8] * m)
            last = r0 + 8
        if last < TILE:
            parts.append(a[last:])
        return jnp.concatenate(parts, axis=0)

    def conv(up, off):
        w = cw_ref[:, off:off + FC]
        prev = mask_rows(pltpu.roll(up, 1, 0), prev_slabs)
        nxt = mask_rows(pltpu.roll(up, TILE - 1, 0), next_slabs)
        return w[0:1] * prev + w[1:2] * up + w[2:3] * nxt + cb_ref[:, off:off + FC]

    def up_chunk(c):
        og = c * FC
        ov = DFF + c * FC
        return (jnp.dot(h, wup_ref[:, og:og + FC], preferred_element_type=F32),
                jnp.dot(h, wup_ref[:, ov:ov + FC], preferred_element_type=F32))

    def down_chunk(act, c):
        return jnp.dot(act, wdn_ref[c * FC:(c + 1) * FC, :], preferred_element_type=F32)

    nchunk = DFF // FC
    acc = jnp.zeros((TILE, D), F32)
    nxt_up = up_chunk(0)
    act = None
    for c in range(nchunk):
        cur_up = nxt_up
        if c + 1 < nchunk:
            nxt_up = up_chunk(c + 1)
        if act is not None:
            acc = acc + down_chunk(act, c - 1)
        gate = conv(cur_up[0], c * FC)
        val = conv(cur_up[1], DFF + c * FC)
        hg = 0.5 * gate
        act = ((hg + hg * jnp.tanh(hg)) * val).astype(BF16)
    acc = acc + down_chunk(act, nchunk - 1)
    y = x + mod[5:6] * acc
    if not final:
        o_refs[0][...] = y
    else:
        ms = jnp.mean(y * y, axis=-1, keepdims=True)
        y = y * lax.rsqrt(ms + EPS) * gf_ref[...]

        @pl.when(i < NPC)
        def _():
            o_refs[0][...] = y

        @pl.when(i >= NPC)
        def _():
            o_refs[1][...] = y


def _ffn(x, mods, layer, g, wup, cw, cb, wdn, gf, *, final):
    if final:
        out_specs = [pl.BlockSpec((TILE, D), lambda i: (jnp.minimum(i, NPC - 1), 0)),
                     pl.BlockSpec((TILE, D), lambda i: (jnp.maximum(i - NPC, 0), 0))]
        out_shape = [jax.ShapeDtypeStruct((NPC * TILE, D), F32)] * 2
    else:
        out_specs = pl.BlockSpec((TILE, D), lambda i: (i, 0))
        out_shape = jax.ShapeDtypeStruct((NCOL * TILE, D), F32)
    return pl.pallas_call(
        functools.partial(_ffn_kernel, final=final, layer=layer),
        grid=(NCOL,),
        in_specs=[pl.BlockSpec((TILE, D), lambda i: (i, 0)),
                  pl.BlockSpec((None, None, 6, D), lambda i: (layer, _cond_of_col(i), 0, 0)),
                  pl.BlockSpec((1, D), lambda i: (0, 0)),
                  pl.BlockSpec(memory_space=pl.ANY),
                  pl.BlockSpec((None, 3, 2 * DFF), lambda i: (layer, 0, 0)),
                  pl.BlockSpec((None, 1, 2 * DFF), lambda i: (layer, 0, 0)),
                  pl.BlockSpec(memory_space=pl.ANY),
                  pl.BlockSpec((1, D), lambda i: (0, 0))],
        out_specs=out_specs,
        out_shape=out_shape,
        scratch_shapes=[pltpu.VMEM((D, 2 * DFF), BF16), pltpu.VMEM((DFF, D), BF16),
                        pltpu.VMEM((2, D, 2 * FC), F32), pltpu.VMEM((2, FC, D), F32),
                        pltpu.SemaphoreType.DMA((2,)), pltpu.SemaphoreType.DMA((2,))],
        compiler_params=_cparams(1),
        name="ffn",
    )(x, mods, g, wup, cw, cb, wdn, gf)


def kernel(x_prompt, x_sample, state_ssm_re, state_ssm_im, c, c_ctx, w_ada, b_ada, g_mix, g_ffn,
           ssm_lam_re, ssm_lam_im, ssm_log_dt, ssm_b_re, ssm_b_im, ssm_c_re, ssm_c_im, ssm_d,
           w_glu, b_glu, w_fourier, b_fourier, w_up, conv_w, conv_b, w_down, g_final):
    nb = x_prompt.shape[0]
    xp = x_prompt.reshape(NPC * TILE, D)
    xs = x_sample.reshape(NPC * TILE, D)
    x = None

    cond8 = jnp.concatenate([c_ctx[None, :], c, jnp.zeros((5, D), F32)], axis=0)
    mods = _ada(cond8, w_ada, b_ada).reshape(DEPTH, 8, 6, D)
    col_cond = jnp.asarray([0] * NPC + [1] * SEG + [2] * SEG, jnp.int32)
    mods_t = mods.transpose(0, 2, 1, 3)[:, :, col_cond]

    mc = jnp.asarray(_channel_mats()).astype(BF16)
    wf4 = w_fourier.reshape(-1, FG, FGC, D)
    wf_perm = jnp.concatenate([wf4[:, :, :HG + 1], jnp.flip(wf4[:, :, HG + 1:], axis=2)], axis=2)
    wf_bf = wf_perm.astype(BF16).reshape(-1, D, D)
    cl1, sl1 = (jnp.asarray(m).astype(BF16) for m in _dft_mats(TILE))
    cl8, sl8 = (jnp.asarray(m).astype(BF16) for m in _dft_mats(SEG * TILE))

    njd = ssm_lam_re.shape[0] * 2
    lam_re = ssm_lam_re.reshape(njd, 1, NST)
    lam_im = ssm_lam_im.reshape(njd, 1, NST)
    dts = jnp.repeat(ssm_log_dt.reshape(njd, GS), PS, axis=-1).reshape(njd, 1, NST)
    bt_re = ssm_b_re.transpose(0, 1, 4, 2, 3).reshape(njd, GC, NST)
    bt_im = ssm_b_im.transpose(0, 1, 4, 2, 3).reshape(njd, GC, NST)
    ct_re = ssm_c_re.transpose(0, 1, 3, 2, 4).reshape(njd, GC, NST)
    ct_im = ssm_c_im.transpose(0, 1, 3, 2, 4).reshape(njd, GC, NST)
    st_re4 = state_ssm_re.reshape(2, njd, 1, NST)
    st_im4 = state_ssm_im.reshape(2, njd, 1, NST)

    edges = []
    for i in range(DEPTH):
        j = i // 2
        if i % 2 == 0:
            g = g_mix[i][None, :]
            xa, xb, off_b = (xp, xs, 0) if x is None else (x, x, NPC)
            xa3 = xa.reshape(-1, TILE, D)
            xb3 = xb.reshape(-1, TILE, D)
            par = (mods_t, i, g, lam_re, lam_im, dts, bt_re, bt_im, ct_re, ct_im)
            kw = dict(emit_edge=False, emit_fin=True, emit_y=False, trunk=1, init="state", h0=(st_re4, st_im4))
            u_s, frf, fif = _s5_scan(xb3, off_b, *par, jd=2 * j, rev=False, u_mode="make", **kw)
            frb, fib = _s5_scan(u_s, 0, *par, jd=2 * j + 1, rev=True, u_mode="load", **kw)
            carry = _carry(frf, fif, frb, fib, st_re4, st_im4, lam_re, lam_im, dts, j=j)
            skip = ssm_d[j][None, :]
            kw = dict(trunk=0, init="zero", emit_y=True, emit_edge=True, emit_fin=False)
            yfa, u_p, er, ei = _s5_scan(xa3, 0, *par, jd=2 * j, rev=False, u_mode="make", skip=skip, **kw)
            edges += [er, ei]
            yba, er, ei = _s5_scan(u_p, 0, *par, jd=2 * j + 1, rev=True, u_mode="load", **kw)
            edges += [er, ei]
            kw = dict(trunk=1, init="carry", h0=carry, u_mode="load", emit_y=True, emit_edge=False, emit_fin=False)
            (yfb,) = _s5_scan(u_s, 0, *par, jd=2 * j, rev=False, skip=skip, **kw)
            (ybb,) = _s5_scan(u_s, 0, *par, jd=2 * j + 1, rev=True, **kw)
            ys = [y.reshape(NPC * TILE, D) for y in (yfa, yba, yfb, ybb)]
            x = _glu(xa, xb, off_b, ys, mods, i, w_glu, b_glu[j][None, :])
        else:
            g = g_mix[i][None, :]
            bf = b_fourier[j][None, :]
            x = _fourier(x, mods, i, g, mc, cl1, sl1, wf_bf, bf,
                         nseq=NPC, nseg=1, col_off=0, cond_off=0, cond_stride=0)
            x = _fourier(x, mods, i, g, mc, cl8, sl8, wf_bf, bf,
                         nseq=2, nseg=SEG, col_off=NPC, cond_off=1, cond_stride=1)
        x = _ffn(x, mods, i, g_ffn[i][None, :], w_up, conv_w, conv_b.reshape(DEPTH, 1, 2 * DFF),
                 w_down, g_final[None, :], final=(i == DEPTH - 1))

    y_prompt = x[0].reshape(NPC, TILE, D)
    y_sample = x[1].reshape(2, SEG * TILE, D)
    ed = jnp.stack(edges, axis=0).reshape(DEPTH // 2, 2, 2, NPC, NST)[:, :, :, :nb]
    new_re = ed[:, :, 0].transpose(2, 0, 1, 3).reshape(nb, DEPTH // 2, 2, GS, PS)
    new_im = ed[:, :, 1].transpose(2, 0, 1, 3).reshape(nb, DEPTH // 2, 2, GS, PS)
    return (y_prompt, y_sample, new_re, new_im)
```

```python
import functools
import math

import jax
import jax.numpy as jnp
import numpy as np
from jax import lax
from jax.experimental import pallas as pl
from jax.experimental.pallas import tpu as pltpu

F32 = jnp.float32
BF16 = jnp.bfloat16

D = 1024
TILE = 256
NCOL = 32
NPC = 16
SEG = 8
DEPTH = 4
GS = 64
GC = 16
PS = 64
NST = GS * PS
FG = 4
FGC = 256
DFF = 2816
EPS = 1e-6
GRID_W = 64

TT = 32
NCH = TILE // TT
SW = 512
NSLAB = NST // SW
FC = 256

VMEM_LIMIT = 56 * 1024 * 1024


def _cparams(n_axes):
    return pltpu.CompilerParams(dimension_semantics=("arbitrary",) * n_axes,
                                vmem_limit_bytes=VMEM_LIMIT)


def _normmod(x, g, sc, sh):
    ms = jnp.mean(x * x, axis=-1, keepdims=True)
    return x * lax.rsqrt(ms + EPS) * (g * (1.0 + sc)) + sh


def _sigmoid(x):
    return 1.0 / (1.0 + jnp.exp(-x))


def _cond_of_col(i):
    return jnp.where(i < NPC, 0, 1 + (i - NPC) // SEG)


def _ada_kernel(c_ref, w_ref, b_ref, o_ref):
    c = c_ref[...]
    s = (c * _sigmoid(c)).astype(BF16)
    o_ref[...] = jnp.dot(s, w_ref[...].astype(BF16), preferred_element_type=F32) + b_ref[...]


def _ada(cond8, w_ada, b_ada):
    tn = 1536
    return pl.pallas_call(
        _ada_kernel,
        grid=(DEPTH, 6 * D // tn),
        in_specs=[pl.BlockSpec((8, D), lambda l, n: (0, 0)),
                  pl.BlockSpec((None, D, tn), lambda l, n: (l, 0, n)),
                  pl.BlockSpec((None, 1, tn), lambda l, n: (l, 0, n))],
        out_specs=pl.BlockSpec((None, 8, tn), lambda l, n: (l, 0, n)),
        out_shape=jax.ShapeDtypeStruct((DEPTH, 8, 6 * D), F32),
        compiler_params=_cparams(2),
        name="ada",
    )(cond8, w_ada, b_ada.reshape(DEPTH, 1, 6 * D))


def _abar(lr, li, logdt):
    dt = jnp.exp(logdt)
    mag = jnp.exp(lr * dt)
    return mag * jnp.cos(li * dt), mag * jnp.sin(li * dt)


PK = 2
NG = TT // PK
GROWS = NG * NPC


def _nt_dot(a, b):
    return lax.dot_general(a, b, (((1,), (1,)), ((), ())), preferred_element_type=F32)


def _s5_kernel(*refs, rev, col_base, init, u_mode, emit_y, emit_edge, emit_fin, add_skip):
    src_hbm, sc_ref, sh_ref, g_ref, lr_ref, li_ref, dt_ref, btr_ref, bti_ref, ctr_ref, cti_ref = refs[:11]
    pos = 11
    if init != "zero":
        h0r_ref, h0i_ref = refs[pos:pos + 2]
        pos += 2
    if add_skip:
        skip_ref = refs[pos]
        pos += 1
    if emit_y:
        y_hbm = refs[pos]
        pos += 1
    if u_mode == "make":
        u_hbm = refs[pos]
        pos += 1
    if emit_edge:
        er_ref, ei_ref = refs[pos:pos + 2]
        pos += 2
    if emit_fin:
        fr_ref, fi_ref = refs[pos:pos + 2]
        pos += 2
    scratch = refs[pos:]
    d_re = scratch[0:NSLAB]
    d_im = scratch[NSLAB:2 * NSLAB]
    bp_s, ap_r, ap_i, hs_re, hs_im, inbuf, sem_in = scratch[2 * NSLAB:2 * NSLAB + 7]
    pos = 2 * NSLAB + 7
    if u_mode == "make":
        ubuf, sem_u = scratch[pos:pos + 2]
        pos += 2
    if emit_y:
        ybuf, sem_out, cp_s, ki_s = scratch[pos:pos + 4]
        hb = scratch[pos + 4:pos + 4 + NSLAB]
        pos += 4 + NSLAB
    if emit_edge:
        ed_re, ed_im, a1_r, a1_i = scratch[pos:pos + 4]

    i = pl.program_id(0)
    slot = lax.rem(i, 2)

    def chunk(i_):
        return NCH - 1 - i_ if rev else i_

    def in_copies(i_, slot_):
        if u_mode == "load":
            return [pltpu.make_async_copy(src_hbm.at[chunk(i_)], inbuf.at[slot_], sem_in.at[slot_])]
        t0 = pl.multiple_of(chunk(i_) * TT, TT)
        return [pltpu.make_async_copy(src_hbm.at[col_base + c, pl.ds(t0, TT), :],
                                      inbuf.at[slot_, :, c, :], sem_in.at[slot_]) for c in range(NPC)]

    def u_copies(i_, slot_):
        return [pltpu.make_async_copy(ubuf.at[slot_], u_hbm.at[chunk(i_)], sem_u.at[slot_])]

    def y_copies(i_, slot_):
        t0 = pl.multiple_of(chunk(i_) * TT, TT)
        return [pltpu.make_async_copy(ybuf.at[slot_, :, c, :], y_hbm.at[c, pl.ds(t0, TT), :],
                                      sem_out.at[slot_]) for c in range(NPC)]

    fetches = in_copies

    def stores(i_, slot_):
        return (y_copies(i_, slot_) if emit_y else []) + (u_copies(i_, slot_) if u_mode == "make" else [])

    @pl.when(i == 0)
    def _first_fetch():
        for cp in fetches(i, slot):
            cp.start()

    if emit_y or u_mode == "make":
        @pl.when(i >= 2)
        def _out_bufs_free():
            for cp in stores(i, slot):
                cp.wait()

    @pl.when(i == 0)
    def _prep():
        lr = lr_ref[...]
        li = li_ref[...]
        ar, ai = _abar(lr, li, dt_ref[...])
        pw = [(jnp.ones_like(ar), jnp.zeros_like(ai)), (ar, ai)]
        while len(pw) <= PK:
            pr, pi_ = pw[-1]
            pw.append((pr * ar - pi_ * ai, pr * ai + pi_ * ar))
        ap_r[...] = jnp.broadcast_to(pw[PK][0], (8, NST))
        ap_i[...] = jnp.broadcast_to(pw[PK][1], (8, NST))
        if emit_edge:
            a1_r[...] = jnp.broadcast_to(ar, (NPC, NST))
            a1_i[...] = jnp.broadcast_to(ai, (NPC, NST))
        xr = ar - 1.0
        den = lr * lr + li * li
        fr = (xr * lr + ai * li) / den
        fi = (ai * lr - xr * li) / den
        btr = btr_ref[...]
        bti = bti_ref[...]
        bbr = fr * btr - fi * bti
        bbi = fr * bti + fi * btr
        r = lax.broadcasted_iota(jnp.int32, (128, SW), 0)
        c = lax.broadcasted_iota(jnp.int32, (128, SW), 1)
        gmask = ((r >> 4) == (c >> 6)).astype(F32)

        def tiles(vr, vi, m, sign):
            sl = slice(m * SW, (m + 1) * SW)
            tr = jnp.concatenate([vr[:, sl]] * 8, axis=0) * gmask
            ti = jnp.concatenate([vi[:, sl]] * 8, axis=0) * (sign * gmask)
            return jnp.concatenate([tr, ti], axis=1).astype(BF16)

        def times(vr, vi, k):
            pr, pi_ = pw[k]
            return vr * pr - vi * pi_, vr * pi_ + vi * pr

        ab = [times(bbr, bbi, k) for k in range(PK)]
        for m in range(NSLAB):
            for p in range(PK):
                bp_s[m, p * 128:(p + 1) * 128, :] = tiles(*ab[PK - 1 - p], m, 1.0)
        if emit_y:
            ctr = ctr_ref[...]
            cti = cti_ref[...]
            ca = [times(ctr, cti, k) for k in range(PK + 1)]
            for m in range(NSLAB):
                for p in range(PK):
                    cp_s[m, p * 128:(p + 1) * 128, :] = tiles(*ca[p + 1], m, -1.0)
                c0 = tiles(*ca[0], m, -1.0)
                kt = [_nt_dot(tiles(*ab[k], m, 1.0), c0) for k in range(PK)]
                if add_skip:
                    eye = (lax.broadcasted_iota(jnp.int32, (128, 128), 0)
                           == lax.broadcasted_iota(jnp.int32, (128, 128), 1))
                    kt[0] = kt[0] + jnp.where(eye, skip_ref[:, m * 128:(m + 1) * 128], 0.0)
                kt = [k_.astype(BF16) for k_ in kt]
                zero = jnp.zeros((128, 128), BF16)
                for q in range(PK):
                    for p in range(PK):
                        ki_s[m, q * 128:(q + 1) * 128, p * 128:(p + 1) * 128] = kt[p - q] if p >= q else zero
        if init == "carry":
            hs_re[...] = h0r_ref[...]
            hs_im[...] = h0i_ref[...]
        else:
            hs_re[...] = jnp.zeros((NPC, NST), F32)
            hs_im[...] = jnp.zeros((NPC, NST), F32)
            if init == "state":
                seg0 = SEG - 1 if rev else 0
                for b in range(2):
                    row = b * SEG + seg0
                    hs_re[row:row + 1, :] = h0r_ref[b]
                    hs_im[row:row + 1, :] = h0i_ref[b]

    for cp in fetches(i, slot):
        cp.wait()
    for cp in fetches(jnp.minimum(i + 1, NCH - 1), 1 - slot):
        cp.start()

    if u_mode == "make":
        x3 = inbuf[slot]
        ms = jnp.mean(x3 * x3, axis=-1, keepdims=True)
        gain = g_ref[...] * (1.0 + sc_ref[...])
        u3 = x3 * lax.rsqrt(ms + EPS) * gain[None] + sh_ref[...][None]
        u = u3.reshape(TT * NPC, D).astype(BF16)
        ubuf[slot] = u
    else:
        u = inbuf[slot]

    def step_rows(a, b):
        t = TT - 1 - b if rev else b
        return a[t * NPC:(t + 1) * NPC]

    def packed(m):
        lanes = slice(m * 128, (m + 1) * 128)
        return jnp.concatenate(
            [jnp.concatenate([step_rows(u, PK * j + p)[:, lanes] for p in range(PK)], axis=1) for j in range(NG)],
            axis=0)

    def expand(m):
        res = jnp.dot(packed(m), bp_s[m], preferred_element_type=F32)
        d_re[m][...] = res[:, 0:SW]
        d_im[m][...] = res[:, SW:2 * SW]

    def recur(s):
        lanes = slice(s * SW, (s + 1) * SW)
        ar = ap_r[:, lanes]
        ai = ap_i[:, lanes]
        h = [hs_re[0:8, lanes], hs_im[0:8, lanes], hs_re[8:16, lanes], hs_im[8:16, lanes]]
        for j in range(NG):
            if emit_y:
                jrow = slice(j * NPC, (j + 1) * NPC)
                hb[s][jrow, 0:SW] = jnp.concatenate([h[0], h[2]], axis=0).astype(BF16)
                hb[s][jrow, SW:2 * SW] = jnp.concatenate([h[1], h[3]], axis=0).astype(BF16)
            for q in range(2):
                rows = slice(j * NPC + 8 * q, j * NPC + 8 * q + 8)
                hr, hi = h[2 * q], h[2 * q + 1]
                h[2 * q] = ar * hr - ai * hi + d_re[s][rows, :]
                h[2 * q + 1] = ar * hi + ai * hr + d_im[s][rows, :]
        hs_re[0:8, lanes] = h[0]
        hs_im[0:8, lanes] = h[1]
        hs_re[8:16, lanes] = h[2]
        hs_im[8:16, lanes] = h[3]

    def contract(m):
        yp = _nt_dot(hb[m][...], cp_s[m]) + jnp.dot(packed(m), ki_s[m], preferred_element_type=F32)
        for j in range(NG):
            for p in range(PK):
                b = PK * j + p
                t = TT - 1 - b if rev else b
                ybuf[slot, t, :, m * 128:(m + 1) * 128] = yp[j * NPC:(j + 1) * NPC, p * 128:(p + 1) * 128]

    if emit_edge:
        u0 = step_rows(u, 0)
        for m in range(NSLAB):
            lanes = slice(m * SW, (m + 1) * SW)
            bu0 = jnp.dot(u0[:, m * 128:(m + 1) * 128], bp_s[m, (PK - 1) * 128:PK * 128, :],
                          preferred_element_type=F32)
            hr, hi = hs_re[:, lanes], hs_im[:, lanes]
            ed_re[:, lanes] = a1_r[:, lanes] * hr - a1_i[:, lanes] * hi + bu0[:, 0:SW]
            ed_im[:, lanes] = a1_r[:, lanes] * hi + a1_i[:, lanes] * hr + bu0[:, SW:2 * SW]

    expand(0)
    for s in range(NSLAB):
        if s + 1 < NSLAB:
            expand(s + 1)
        if emit_y and s >= 1:
            contract(s - 1)
        recur(s)
    if emit_y:
        contract(NSLAB - 1)
    for cp in stores(i, slot):
        cp.start()

    if emit_edge:
        @pl.when(i == 0)
        def _edge():
            er_ref[...] = ed_re[...]
            ei_ref[...] = ed_im[...]

    @pl.when(i == NCH - 1)
    def _last():
        if emit_fin:
            fr_ref[...] = hs_re[...]
            fi_ref[...] = hs_im[...]
        for cp in fetches(i, 1 - slot) + stores(i, slot) + stores(i, 1 - slot):
            cp.wait()


def _s5_scan(src, col_base, mods_t, layer, g, lam_re, lam_im, dts, bt_re, bt_im, ct_re, ct_im, *,
             jd, rev, trunk, init, h0=None, skip=None, u_mode, emit_y, emit_edge, emit_fin):
    par = lambda i: (jd, 0, 0)
    in_specs = [
        pl.BlockSpec(memory_space=pl.ANY),
        pl.BlockSpec((None, None, NPC, D), lambda i: (layer, 1, trunk, 0)),
        pl.BlockSpec((None, None, NPC, D), lambda i: (layer, 0, trunk, 0)),
        pl.BlockSpec((1, D), lambda i: (0, 0)),
        pl.BlockSpec((None, 1, NST), par),
        pl.BlockSpec((None, 1, NST), par),
        pl.BlockSpec((None, 1, NST), par),
        pl.BlockSpec((None, GC, NST), par),
        pl.BlockSpec((None, GC, NST), par),
        pl.BlockSpec((None, GC, NST), par),
        pl.BlockSpec((None, GC, NST), par),
    ]
    args = [src, mods_t, mods_t, g, lam_re, lam_im, dts, bt_re, bt_im, ct_re, ct_im]
    if init == "state":
        in_specs += [pl.BlockSpec((2, None, 1, NST), lambda i: (0, jd, 0, 0))] * 2
        args += list(h0)
    elif init == "carry":
        in_specs += [pl.BlockSpec((None, NPC, NST), lambda i: (1 if rev else 0, 0, 0))] * 2
        args += list(h0)
    if skip is not None:
        in_specs.append(pl.BlockSpec((1, D), lambda i: (0, 0)))
        args.append(skip)
    out_specs, out_shape = [], []
    slab = pltpu.VMEM((GROWS, SW), F32)
    scratch = [slab] * (2 * NSLAB) + [
        pltpu.VMEM((NSLAB, PK * 128, 2 * SW), BF16),
        pltpu.VMEM((8, NST), F32), pltpu.VMEM((8, NST), F32),
        pltpu.VMEM((NPC, NST), F32), pltpu.VMEM((NPC, NST), F32),
        pltpu.VMEM((2, TT, NPC, D), F32) if u_mode == "make" else pltpu.VMEM((2, TT * NPC, D), BF16),
        pltpu.SemaphoreType.DMA((2,))]
    if emit_y:
        out_specs.append(pl.BlockSpec(memory_space=pl.ANY))
        out_shape.append(jax.ShapeDtypeStruct((NPC, TILE, D), F32))
    if u_mode == "make":
        out_specs.append(pl.BlockSpec(memory_space=pl.ANY))
        out_shape.append(jax.ShapeDtypeStruct((NCH, TT * NPC, D), BF16))
        scratch += [pltpu.VMEM((2, TT * NPC, D), BF16), pltpu.SemaphoreType.DMA((2,))]
    if emit_y:
        scratch += [pltpu.VMEM((2, TT, NPC, D), F32), pltpu.SemaphoreType.DMA((2,)),
                    pltpu.VMEM((NSLAB, PK * 128, 2 * SW), BF16), pltpu.VMEM((NSLAB, PK * 128, PK * 128), BF16)]
        scratch += [pltpu.VMEM((GROWS, 2 * SW), BF16)] * NSLAB
    if emit_edge:
        scratch += [pltpu.VMEM((NPC, NST), F32)] * 4
    n_state_outs = 2 * (int(emit_edge) + int(emit_fin))
    out_specs += [pl.BlockSpec((NPC, NST), lambda i: (0, 0))] * n_state_outs
    out_shape += [jax.ShapeDtypeStruct((NPC, NST), F32)] * n_state_outs
    return pl.pallas_call(
        functools.partial(_s5_kernel, rev=rev, col_base=col_base, init=init, u_mode=u_mode, emit_y=emit_y,
                          emit_edge=emit_edge, emit_fin=emit_fin, add_skip=skip is not None),
        grid=(NCH,),
        in_specs=in_specs,
        out_specs=out_specs,
        out_shape=out_shape,
        scratch_shapes=scratch,
        compiler_params=_cparams(1),
        name=("s5_scan" if emit_y else "s5_states") + ("_bwd" if rev else "_fwd") + str(trunk),
    )(*args)


def _carry_kernel(frf_ref, fif_ref, frb_ref, fib_ref, sr_ref, si_ref, lr_ref, li_ref, dt_ref,
                  or_ref, oi_ref, *, j):
    for d, (fr_ref, fi_ref) in enumerate(((frf_ref, fif_ref), (frb_ref, fib_ref))):
        jd = 2 * j + d
        ar, ai = _abar(lr_ref[jd], li_ref[jd], dt_ref[jd])
        for _ in range(8):
            ar, ai = ar * ar - ai * ai, 2.0 * ar * ai
        for b in range(2):
            order = list(range(SEG)) if d == 0 else list(range(SEG - 1, -1, -1))
            r = b * SEG + order[0]
            or_ref[d, r:r + 1, :] = sr_ref[b, jd]
            oi_ref[d, r:r + 1, :] = si_ref[b, jd]
            tr = fr_ref[r:r + 1, :]
            ti = fi_ref[r:r + 1, :]
            for s in order[1:]:
                r = b * SEG + s
                or_ref[d, r:r + 1, :] = tr
                oi_ref[d, r:r + 1, :] = ti
                tr, ti = (fr_ref[r:r + 1, :] + ar * tr - ai * ti,
                          fi_ref[r:r + 1, :] + ar * ti + ai * tr)


def _carry(frf, fif, frb, fib, st_re4, st_im4, lam_re, lam_im, dts, *, j):
    shp = jax.ShapeDtypeStruct((2, NPC, NST), F32)
    return pl.pallas_call(functools.partial(_carry_kernel, j=j), out_shape=[shp, shp], name="s5_carry")(
        frf, fif, frb, fib, st_re4, st_im4, lam_re, lam_im, dts)


GT = 2 * TILE


def _glu_kernel(xa_ref, xb_ref, yfa_ref, yba_ref, yfb_ref, ybb_ref, mod_ref, w_ref, b_ref,
                o_ref, wbf, ge_s, z_s):
    i = pl.program_id(0)

    @pl.when(i == 0)
    def _():
        wbf[...] = w_ref[...].astype(BF16)

    mod = mod_ref[...]
    first_prompt = i < NPC // 2

    nq = 4
    cw = D // nq
    rw = TILE // nq

    def x_rows(r0, r1, c0=0, c1=D):
        return jnp.where(first_prompt, xa_ref[r0:r1, c0:c1], xb_ref[r0:r1, c0:c1])

    def pre(r0, r1):
        y = jnp.where(first_prompt, yfa_ref[r0:r1, :] + yba_ref[r0:r1, :], yfb_ref[r0:r1, :] + ybb_ref[r0:r1, :])
        k1 = -2.0 * math.sqrt(2.0 / math.pi)
        ge = y / (1.0 + jnp.exp(y * (k1 + (k1 * 0.044715) * (y * y))))
        ge_s[r0:r1, :] = ge.astype(BF16)

    def gate(q, j):
        rows = slice(q * TILE, (q + 1) * TILE)
        for c0 in (j * cw, D + j * cw):
            z_s[rows, c0:c0 + cw] = (jnp.dot(ge_s[rows, :], wbf[:, c0:c0 + cw], preferred_element_type=F32)
                                     + b_ref[:, c0:c0 + cw])

    def post(q, j):
        rows = slice(q * TILE, (q + 1) * TILE)
        c0, c1 = j * cw, (j + 1) * cw
        out = z_s[rows, c0:c1] * _sigmoid(z_s[rows, D + c0:D + c1])
        o_ref[rows, c0:c1] = x_rows(q * TILE, (q + 1) * TILE, c0, c1) + mod[2:3, c0:c1] * out

    for j in range(nq):
        pre(j * rw, (j + 1) * rw)
    for j in range(nq):
        gate(0, j)
        pre(TILE + j * rw, TILE + (j + 1) * rw)
    for j in range(nq):
        gate(1, j)
        post(0, j)
    for j in range(nq):
        post(1, j)


def _glu(xa, xb, off_b, ys, mods, layer, w, b):
    nblk = NCOL // 2
    half = NPC // 2
    pa = lambda i: (jnp.minimum(i, half - 1), 0)
    pb = lambda i: (jnp.maximum(i - half, 0), 0)
    return pl.pallas_call(
        _glu_kernel,
        grid=(nblk,),
        in_specs=[pl.BlockSpec((GT, D), pa),
                  pl.BlockSpec((GT, D), lambda i: (off_b // 2 + jnp.maximum(i - half, 0), 0)),
                  pl.BlockSpec((GT, D), pa),
                  pl.BlockSpec((GT, D), pa),
                  pl.BlockSpec((GT, D), pb),
                  pl.BlockSpec((GT, D), pb),
                  pl.BlockSpec((None, None, 6, D), lambda i: (layer, _cond_of_col(2 * i), 0, 0)),
                  pl.BlockSpec((None, D, 2 * D), lambda i: (layer // 2, 0, 0)),
                  pl.BlockSpec((1, 2 * D), lambda i: (0, 0))],
        out_specs=pl.BlockSpec((GT, D), lambda i: (i, 0)),
        out_shape=jax.ShapeDtypeStruct((NCOL * TILE, D), F32),
        scratch_shapes=[pltpu.VMEM((D, 2 * D), BF16), pltpu.VMEM((GT, D), BF16), pltpu.VMEM((GT, 2 * D), F32)],
        compiler_params=_cparams(1),
        name="s5_glu",
    )(xa, xb, *ys, mods, w, b)


HG = FGC // 2


def _fourier_kernel(x_ref, mod_ref, g_ref, mc_ref, cl_ref, sl_ref, w_ref, b_ref,
                    o_ref, *scratch, scale, nseg):
    mod = mod_ref[...]
    lane = lax.broadcasted_iota(jnp.int32, (TILE, HG), 1)
    lane0 = lane == 0

    def channel_dft():
        h = _normmod(x_ref[...], g_ref[...], mod[1:2], mod[0:1]).astype(BF16)
        zc, zs, zn = [], [], None
        for q in range(FG):
            z = jnp.dot(h[:, q * FGC:(q + 1) * FGC], mc_ref[...], preferred_element_type=F32)
            zc.append(z[:, 0:HG].astype(BF16))
            zs.append(z[:, HG:FGC].astype(BF16))
            nyq = jnp.where(lane0, z[:, HG:FGC], 0.0)
            nyq = pltpu.roll(nyq, q, 1) if q else nyq
            zn = nyq if zn is None else zn + nyq
        return jnp.concatenate(zc, axis=1), jnp.concatenate(zs, axis=1), zn.astype(BF16)

    def position_dft(xc, xs, xn):
        a = jnp.dot(cl_ref[...], xc, preferred_element_type=F32)
        bz = jnp.dot(sl_ref[...], xs, preferred_element_type=F32)
        an = jnp.dot(cl_ref[...], xn, preferred_element_type=F32)
        parts = []
        for q in range(FG):
            aq = a[:, q * HG:(q + 1) * HG]
            bq = jnp.where(lane0, 0.0, bz[:, q * HG:(q + 1) * HG])
            nq = pltpu.roll(an, HG - q, 1) if q else an
            parts += [aq - bq, jnp.where(lane0, nq, aq + bq)]
        f = jnp.concatenate(parts, axis=1) * scale
        o = jnp.dot(f.astype(BF16), w_ref[...], preferred_element_type=F32) + b_ref[...]
        o_ref[...] = x_ref[...] + mod[2:3] * o

    if nseg == 1:
        position_dft(*channel_dft())
        return

    xc_s, xs_s, xn_s = scratch
    ph = pl.program_id(1)
    s = pl.program_id(2)

    @pl.when(ph == 0)
    def _():
        r0 = pl.multiple_of(s * TILE, TILE)
        xc, xs, xn = channel_dft()
        xc_s[pl.ds(r0, TILE), :] = xc
        xs_s[pl.ds(r0, TILE), :] = xs
        xn_s[pl.ds(r0, TILE), :] = xn

    @pl.when(ph == 1)
    def _():
        position_dft(xc_s[...], xs_s[...], xn_s[...])


def _fourier(x, mods, layer, g, mc, cl, sl, w, b, *, nseq, nseg, col_off, cond_off, cond_stride):
    ln = nseg * TILE
    if nseg == 1:
        grid = (nseq,)
        idx = lambda f: (lambda q: f(q, 1, 0))
        scratch = []
    else:
        grid = (nseq, 2, nseg)
        idx = lambda f: f
        scratch = [pltpu.VMEM((ln, FG * HG), BF16), pltpu.VMEM((ln, FG * HG), BF16), pltpu.VMEM((ln, HG), BF16)]
    return pl.pallas_call(
        functools.partial(_fourier_kernel, scale=1.0 / math.sqrt(ln * FGC), nseg=nseg),
        grid=grid,
        in_specs=[pl.BlockSpec((TILE, D), idx(lambda q, ph, s: (col_off + q * nseg + s, 0))),
                  pl.BlockSpec((None, None, 6, D), idx(lambda q, ph, s: (layer, cond_off + q * cond_stride, 0, 0))),
                  pl.BlockSpec((1, D), idx(lambda q, ph, s: (0, 0))),
                  pl.BlockSpec((FGC, FGC), idx(lambda q, ph, s: (0, 0))),
                  pl.BlockSpec((TILE, ln), idx(lambda q, ph, s: (s * ph, 0))),
                  pl.BlockSpec((TILE, ln), idx(lambda q, ph, s: (s * ph, 0))),
                  pl.BlockSpec((None, D, D), idx(lambda q, ph, s: (layer // 2, 0, 0))),
                  pl.BlockSpec((1, D), idx(lambda q, ph, s: (0, 0)))],
        out_specs=pl.BlockSpec((TILE, D), idx(lambda q, ph, s: (col_off + q * nseg + s * ph, 0))),
        out_shape=jax.ShapeDtypeStruct((NCOL * TILE, D), F32),
        scratch_shapes=scratch,
        input_output_aliases={0: 0},
        compiler_params=_cparams(len(grid)),
        name="fourier%d" % nseg,
    )(x, mods, g, mc, cl, sl, w, b)


@functools.lru_cache(maxsize=None)
def _dft_mats(n):
    k = np.arange(n, dtype=np.int64)
    ang = ((k[:, None] * k[None, :]) % n).astype(np.float64) * (2.0 * math.pi / n)
    return np.cos(ang).astype(np.float32), np.sin(ang).astype(np.float32)


@functools.lru_cache(maxsize=None)
def _channel_mats():
    c, s = _dft_mats(FGC)
    return np.concatenate([c[:, 0:HG], c[:, HG:HG + 1], s[:, 1:HG]], axis=1)


WCH = DFF // FC


def _ffn_kernel(x_ref, mod_ref, g_ref, wup_hbm, cw_ref, cb_ref, wdn_hbm, gf_ref, *rest, final, layer):
    n_out = 2 if final else 1
    o_refs = rest[:n_out]
    wup_ref, wdn_ref, stg_u, stg_d, sem_u, sem_d = rest[n_out:]
    i = pl.program_id(0)

    @pl.when(i == 0)
    def _load_weights():
        def copies(c):
            slot = c % 2
            return (pltpu.make_async_copy(wup_hbm.at[layer, :, pl.ds(c * 2 * FC, 2 * FC)], stg_u.at[slot],
                                          sem_u.at[slot]),
                    pltpu.make_async_copy(wdn_hbm.at[layer, pl.ds(c * FC, FC), :], stg_d.at[slot],
                                          sem_d.at[slot]))

        for cp in copies(0):
            cp.start()
        for c in range(WCH):
            if c + 1 < WCH:
                for cp in copies(c + 1):
                    cp.start()
            for cp in copies(c):
                cp.wait()
            wup_ref[:, c * 2 * FC:(c + 1) * 2 * FC] = stg_u[c % 2].astype(BF16)
            wdn_ref[c * FC:(c + 1) * FC, :] = stg_d[c % 2].astype(BF16)

    mod = mod_ref[...]
    prompt = i < NPC // 2
    sub = lax.broadcasted_iota(jnp.int32, (8, FC), 0)
    one = jnp.ones((8, FC), F32)
    first0 = (sub != 0).astype(F32)
    last0 = (sub != 7).astype(F32)
    inner_first = jnp.where(prompt, one, first0)
    inner_last = jnp.where(prompt, one, last0)
    prev_slabs = [(0, first0)] + [(r, inner_first) for r in range(GRID_W, TILE, GRID_W)]
    next_slabs = [(r - 8, inner_last) for r in range(GRID_W, TILE, GRID_W)] + [(TILE - 8, last0)]

    def mask_rows(a, slabs):
        parts, last = [], 0
        for r0, m in slabs:
            if r0 > last:
                parts.append(a[last:r0])
            parts.append(a[r0:r0 + 8] * m)
            last = r0 + 8
        if last < TILE:
            parts.append(a[last:])
        return jnp.concatenate(parts, axis=0)

    def conv(up, off):
        w = cw_ref[:, off:off + FC]
        prev = mask_rows(pltpu.roll(up, 1, 0), prev_slabs)
        nxt = mask_rows(pltpu.roll(up, TILE - 1, 0), next_slabs)
        return w[0:1] * prev + w[1:2] * up + w[2:3] * nxt + cb_ref[:, off:off + FC]

    def up_chunk(h, c):
        og = c * FC
        ov = DFF + c * FC
        return (jnp.dot(h, wup_ref[:, og:og + FC], preferred_element_type=F32),
                jnp.dot(h, wup_ref[:, ov:ov + FC], preferred_element_type=F32))

    def down_chunk(act, c):
        return jnp.dot(act, wdn_ref[c * FC:(c + 1) * FC, :], preferred_element_type=F32)

    nchunk = DFF // FC
    cols = (0, 1)
    rows = [slice(q * TILE, (q + 1) * TILE) for q in cols]
    h = [None, None]
    nxt_up = [None, None]
    act = [None, None]
    acc = [jnp.zeros((TILE, D), F32), jnp.zeros((TILE, D), F32)]
    for q in cols:
        h[q] = _normmod(x_ref[rows[q], :], g_ref[...], mod[4:5], mod[3:4]).astype(BF16)
        nxt_up[q] = up_chunk(h[q], 0)
    for c in range(nchunk):
        for q in cols:
            cur_up = nxt_up[q]
            if c + 1 < nchunk:
                nxt_up[q] = up_chunk(h[q], c + 1)
            if act[q] is not None:
                acc[q] = acc[q] + down_chunk(act[q], c - 1)
            gate = conv(cur_up[0], c * FC)
            val = conv(cur_up[1], DFF + c * FC)
            hg = 0.5 * gate
            act[q] = ((hg + hg * jnp.tanh(hg)) * val).astype(BF16)
    ys = []
    for q in cols:
        acc[q] = acc[q] + down_chunk(act[q], nchunk - 1)
        y = x_ref[rows[q], :] + mod[5:6] * acc[q]
        if final:
            ms = jnp.mean(y * y, axis=-1, keepdims=True)
            y = y * lax.rsqrt(ms + EPS) * gf_ref[...]
        ys.append(y)
    if not final:
        for q in cols:
            o_refs[0][rows[q], :] = ys[q]
    else:
        @pl.when(prompt)
        def _():
            for q in cols:
                o_refs[0][rows[q], :] = ys[q]

        @pl.when(jnp.logical_not(prompt))
        def _():
            for q in cols:
                o_refs[1][rows[q], :] = ys[q]


def _ffn(x, mods, layer, g, wup, cw, cb, wdn, gf, *, final):
    nblk = NCOL // 2
    half = NPC // 2
    if final:
        out_specs = [pl.BlockSpec((2 * TILE, D), lambda i: (jnp.minimum(i, half - 1), 0)),
                     pl.BlockSpec((2 * TILE, D), lambda i: (jnp.maximum(i - half, 0), 0))]
        out_shape = [jax.ShapeDtypeStruct((NPC * TILE, D), F32)] * 2
    else:
        out_specs = pl.BlockSpec((2 * TILE, D), lambda i: (i, 0))
        out_shape = jax.ShapeDtypeStruct((NCOL * TILE, D), F32)
    return pl.pallas_call(
        functools.partial(_ffn_kernel, final=final, layer=layer),
        grid=(nblk,),
        in_specs=[pl.BlockSpec((2 * TILE, D), lambda i: (i, 0)),
                  pl.BlockSpec((None, None, 6, D), lambda i: (layer, _cond_of_col(2 * i), 0, 0)),
                  pl.BlockSpec((1, D), lambda i: (0, 0)),
                  pl.BlockSpec(memory_space=pl.ANY),
                  pl.BlockSpec((None, 3, 2 * DFF), lambda i: (layer, 0, 0)),
                  pl.BlockSpec((None, 1, 2 * DFF), lambda i: (layer, 0, 0)),
                  pl.BlockSpec(memory_space=pl.ANY),
                  pl.BlockSpec((1, D), lambda i: (0, 0))],
        out_specs=out_specs,
        out_shape=out_shape,
        scratch_shapes=[pltpu.VMEM((D, 2 * DFF), BF16), pltpu.VMEM((DFF, D), BF16),
                        pltpu.VMEM((2, D, 2 * FC), F32), pltpu.VMEM((2, FC, D), F32),
                        pltpu.SemaphoreType.DMA((2,)), pltpu.SemaphoreType.DMA((2,))],
        compiler_params=_cparams(1),
        name="ffn",
    )(x, mods, g, wup, cw, cb, wdn, gf)


def kernel(x_prompt, x_sample, state_ssm_re, state_ssm_im, c, c_ctx, w_ada, b_ada, g_mix, g_ffn,
           ssm_lam_re, ssm_lam_im, ssm_log_dt, ssm_b_re, ssm_b_im, ssm_c_re, ssm_c_im, ssm_d,
           w_glu, b_glu, w_fourier, b_fourier, w_up, conv_w, conv_b, w_down, g_final):
    nb = x_prompt.shape[0]
    xp = x_prompt.reshape(NPC * TILE, D)
    xs = x_sample.reshape(NPC * TILE, D)
    x = None

    cond8 = jnp.concatenate([c_ctx[None, :], c, jnp.zeros((5, D), F32)], axis=0)
    mods = _ada(cond8, w_ada, b_ada).reshape(DEPTH, 8, 6, D)
    col_cond = jnp.asarray([0] * NPC + [1] * SEG + [2] * SEG, jnp.int32)
    mods_t = mods.transpose(0, 2, 1, 3)[:, :, col_cond]

    mc = jnp.asarray(_channel_mats()).astype(BF16)
    wf4 = w_fourier.reshape(-1, FG, FGC, D)
    wf_perm = jnp.concatenate([wf4[:, :, :HG + 1], jnp.flip(wf4[:, :, HG + 1:], axis=2)], axis=2)
    wf_bf = wf_perm.astype(BF16).reshape(-1, D, D)
    cl1, sl1 = (jnp.asarray(m).astype(BF16) for m in _dft_mats(TILE))
    cl8, sl8 = (jnp.asarray(m).astype(BF16) for m in _dft_mats(SEG * TILE))

    njd = ssm_lam_re.shape[0] * 2
    lam_re = ssm_lam_re.reshape(njd, 1, NST)
    lam_im = ssm_lam_im.reshape(njd, 1, NST)
    dts = jnp.repeat(ssm_log_dt.reshape(njd, GS), PS, axis=-1).reshape(njd, 1, NST)
    bt_re = ssm_b_re.transpose(0, 1, 4, 2, 3).reshape(njd, GC, NST)
    bt_im = ssm_b_im.transpose(0, 1, 4, 2, 3).reshape(njd, GC, NST)
    ct_re = ssm_c_re.transpose(0, 1, 3, 2, 4).reshape(njd, GC, NST)
    ct_im = ssm_c_im.transpose(0, 1, 3, 2, 4).reshape(njd, GC, NST)
    st_re4 = state_ssm_re.reshape(2, njd, 1, NST)
    st_im4 = state_ssm_im.reshape(2, njd, 1, NST)

    edges = []
    for i in range(DEPTH):
        j = i // 2
        if i % 2 == 0:
            g = g_mix[i][None, :]
            xa, xb, off_b = (xp, xs, 0) if x is None else (x, x, NPC)
            xa3 = xa.reshape(-1, TILE, D)
            xb3 = xb.reshape(-1, TILE, D)
            par = (mods_t, i, g, lam_re, lam_im, dts, bt_re, bt_im, ct_re, ct_im)
            kw = dict(emit_edge=False, emit_fin=True, emit_y=False, trunk=1, init="state", h0=(st_re4, st_im4))
            u_s, frf, fif = _s5_scan(xb3, off_b, *par, jd=2 * j, rev=False, u_mode="make", **kw)
            frb, fib = _s5_scan(u_s, 0, *par, jd=2 * j + 1, rev=True, u_mode="load", **kw)
            carry = _carry(frf, fif, frb, fib, st_re4, st_im4, lam_re, lam_im, dts, j=j)
            skip = ssm_d[j][None, :]
            kw = dict(trunk=0, init="zero", emit_y=True, emit_edge=True, emit_fin=False)
            yfa, u_p, er, ei = _s5_scan(xa3, 0, *par, jd=2 * j, rev=False, u_mode="make", skip=skip, **kw)
            edges += [er, ei]
            yba, er, ei = _s5_scan(u_p, 0, *par, jd=2 * j + 1, rev=True, u_mode="load", **kw)
            edges += [er, ei]
            kw = dict(trunk=1, init="carry", h0=carry, u_mode="load", emit_y=True, emit_edge=False, emit_fin=False)
            (yfb,) = _s5_scan(u_s, 0, *par, jd=2 * j, rev=False, skip=skip, **kw)
            (ybb,) = _s5_scan(u_s, 0, *par, jd=2 * j + 1, rev=True, **kw)
            ys = [y.reshape(NPC * TILE, D) for y in (yfa, yba, yfb, ybb)]
            x = _glu(xa, xb, off_b, ys, mods, i, w_glu, b_glu[j][None, :])
        else:
            g = g_mix[i][None, :]
            bf = b_fourier[j][None, :]
            x = _fourier(x, mods, i, g, mc, cl1, sl1, wf_bf, bf,
                         nseq=NPC, nseg=1, col_off=0, cond_off=0, cond_stride=0)
            x = _fourier(x, mods, i, g, mc, cl8, sl8, wf_bf, bf,
                         nseq=2, nseg=SEG, col_off=NPC, cond_off=1, cond_stride=1)
        x = _ffn(x, mods, i, g_ffn[i][None, :], w_up, conv_w, conv_b.reshape(DEPTH, 1, 2 * DFF),
                 w_down, g_final[None, :], final=(i == DEPTH - 1))

    y_prompt = x[0].reshape(NPC, TILE, D)
    y_sample = x[1].reshape(2, SEG * TILE, D)
    ed = jnp.stack(edges, axis=0).reshape(DEPTH // 2, 2, 2, NPC, NST)[:, :, :, :nb]
    new_re = ed[:, :, 0].transpose(2, 0, 1, 3).reshape(nb, DEPTH // 2, 2, GS, PS)
    new_im = ed[:, :, 1].transpose(2, 0, 1, 3).reshape(nb, DEPTH // 2, 2, GS, PS)
    return (y_prompt, y_sample, new_re, new_im)
```

```python
import functools
import math

import jax
import jax.numpy as jnp
import numpy as np
from jax import lax
from jax.experimental import pallas as pl
from jax.experimental.pallas import tpu as pltpu

F32 = jnp.float32
BF16 = jnp.bfloat16

D = 1024
TILE = 256
NCOL = 32
NPC = 16
SEG = 8
DEPTH = 4
GS = 64
GC = 16
PS = 64
NST = GS * PS
FG = 4
FGC = 256
DFF = 2816
EPS = 1e-6
GRID_W = 64

TT = 32
NCH = TILE // TT
SW = 512
NSLAB = NST // SW
FC = 256

VMEM_LIMIT = 56 * 1024 * 1024


def _cparams(n_axes):
    return pltpu.CompilerParams(dimension_semantics=("arbitrary",) * n_axes,
                                vmem_limit_bytes=VMEM_LIMIT)


def _normmod(x, g, sc, sh):
    ms = jnp.mean(x * x, axis=-1, keepdims=True)
    return x * lax.rsqrt(ms + EPS) * (g * (1.0 + sc)) + sh


def _sigmoid(x):
    return 1.0 / (1.0 + jnp.exp(-x))


def _cond_of_col(i):
    return jnp.where(i < NPC, 0, 1 + (i - NPC) // SEG)


def _ada_kernel(c_ref, w_ref, b_ref, o_ref):
    c = c_ref[...]
    s = (c * _sigmoid(c)).astype(BF16)
    o_ref[...] = jnp.dot(s, w_ref[...].astype(BF16), preferred_element_type=F32) + b_ref[...]


def _ada(cond8, w_ada, b_ada):
    tn = 1536
    return pl.pallas_call(
        _ada_kernel,
        grid=(DEPTH, 6 * D // tn),
        in_specs=[pl.BlockSpec((8, D), lambda l, n: (0, 0)),
                  pl.BlockSpec((None, D, tn), lambda l, n: (l, 0, n)),
                  pl.BlockSpec((None, 1, tn), lambda l, n: (l, 0, n))],
        out_specs=pl.BlockSpec((None, 8, tn), lambda l, n: (l, 0, n)),
        out_shape=jax.ShapeDtypeStruct((DEPTH, 8, 6 * D), F32),
        compiler_params=_cparams(2),
        name="ada",
    )(cond8, w_ada, b_ada.reshape(DEPTH, 1, 6 * D))


def _abar(lr, li, logdt):
    dt = jnp.exp(logdt)
    mag = jnp.exp(lr * dt)
    return mag * jnp.cos(li * dt), mag * jnp.sin(li * dt)


PK = 2
NG = TT // PK
GROWS = NG * NPC


def _nt_dot(a, b):
    return lax.dot_general(a, b, (((1,), (1,)), ((), ())), preferred_element_type=F32)


def _s5_kernel(*refs, rev, col_base, init, u_mode, emit_y, emit_edge, emit_fin, add_skip):
    src_hbm, sc_ref, sh_ref, g_ref, lr_ref, li_ref, dt_ref, btr_ref, bti_ref, ctr_ref, cti_ref = refs[:11]
    pos = 11
    if init != "zero":
        h0r_ref, h0i_ref = refs[pos:pos + 2]
        pos += 2
    if add_skip:
        skip_ref = refs[pos]
        pos += 1
    if emit_y:
        y_hbm = refs[pos]
        pos += 1
    if u_mode == "make":
        u_hbm = refs[pos]
        pos += 1
    if emit_edge:
        er_ref, ei_ref = refs[pos:pos + 2]
        pos += 2
    if emit_fin:
        fr_ref, fi_ref = refs[pos:pos + 2]
        pos += 2
    scratch = refs[pos:]
    d_re = scratch[0:NSLAB]
    d_im = scratch[NSLAB:2 * NSLAB]
    bp_s, ap_r, ap_i, hs_re, hs_im, inbuf, sem_in = scratch[2 * NSLAB:2 * NSLAB + 7]
    pos = 2 * NSLAB + 7
    if u_mode == "make":
        ubuf, sem_u = scratch[pos:pos + 2]
        pos += 2
    if emit_y:
        ybuf, sem_out, cp_s, ki_s = scratch[pos:pos + 4]
        hb = scratch[pos + 4:pos + 4 + NSLAB]
        pos += 4 + NSLAB
    if emit_edge:
        ed_re, ed_im, a1_r, a1_i = scratch[pos:pos + 4]

    i = pl.program_id(0)
    slot = lax.rem(i, 2)

    def chunk(i_):
        return NCH - 1 - i_ if rev else i_

    def in_copies(i_, slot_):
        if u_mode == "load":
            return [pltpu.make_async_copy(src_hbm.at[chunk(i_)], inbuf.at[slot_], sem_in.at[slot_])]
        t0 = pl.multiple_of(chunk(i_) * TT, TT)
        return [pltpu.make_async_copy(src_hbm.at[col_base + c, pl.ds(t0, TT), :],
                                      inbuf.at[slot_, :, c, :], sem_in.at[slot_]) for c in range(NPC)]

    def u_copies(i_, slot_):
        return [pltpu.make_async_copy(ubuf.at[slot_], u_hbm.at[chunk(i_)], sem_u.at[slot_])]

    def y_copies(i_, slot_):
        t0 = pl.multiple_of(chunk(i_) * TT, TT)
        return [pltpu.make_async_copy(ybuf.at[slot_, :, c, :], y_hbm.at[c, pl.ds(t0, TT), :],
                                      sem_out.at[slot_]) for c in range(NPC)]

    fetches = in_copies

    def stores(i_, slot_):
        return (y_copies(i_, slot_) if emit_y else []) + (u_copies(i_, slot_) if u_mode == "make" else [])

    @pl.when(i == 0)
    def _first_fetch():
        for cp in fetches(i, slot):
            cp.start()

    if emit_y or u_mode == "make":
        @pl.when(i >= 2)
        def _out_bufs_free():
            for cp in stores(i, slot):
                cp.wait()

    @pl.when(i == 0)
    def _prep():
        lr = lr_ref[...]
        li = li_ref[...]
        ar, ai = _abar(lr, li, dt_ref[...])
        pw = [(jnp.ones_like(ar), jnp.zeros_like(ai)), (ar, ai)]
        while len(pw) <= PK:
            pr, pi_ = pw[-1]
            pw.append((pr * ar - pi_ * ai, pr * ai + pi_ * ar))
        ap_r[...] = jnp.broadcast_to(pw[PK][0], (8, NST))
        ap_i[...] = jnp.broadcast_to(pw[PK][1], (8, NST))
        if emit_edge:
            a1_r[...] = jnp.broadcast_to(ar, (NPC, NST))
            a1_i[...] = jnp.broadcast_to(ai, (NPC, NST))
        xr = ar - 1.0
        den = lr * lr + li * li
        fr = (xr * lr + ai * li) / den
        fi = (ai * lr - xr * li) / den
        btr = btr_ref[...]
        bti = bti_ref[...]
        bbr = fr * btr - fi * bti
        bbi = fr * bti + fi * btr
        r = lax.broadcasted_iota(jnp.int32, (128, SW), 0)
        c = lax.broadcasted_iota(jnp.int32, (128, SW), 1)
        gmask = ((r >> 4) == (c >> 6)).astype(F32)

        def tiles(vr, vi, m, sign):
            sl = slice(m * SW, (m + 1) * SW)
            tr = jnp.concatenate([vr[:, sl]] * 8, axis=0) * gmask
            ti = jnp.concatenate([vi[:, sl]] * 8, axis=0) * (sign * gmask)
            return jnp.concatenate([tr, ti], axis=1).astype(BF16)

        def times(vr, vi, k):
            pr, pi_ = pw[k]
            return vr * pr - vi * pi_, vr * pi_ + vi * pr

        ab = [times(bbr, bbi, k) for k in range(PK)]
        for m in range(NSLAB):
            for p in range(PK):
                bp_s[m, p * 128:(p + 1) * 128, :] = tiles(*ab[PK - 1 - p], m, 1.0)
        if emit_y:
            ctr = ctr_ref[...]
            cti = cti_ref[...]
            ca = [times(ctr, cti, k) for k in range(PK + 1)]
            for m in range(NSLAB):
                for p in range(PK):
                    cp_s[m, p * 128:(p + 1) * 128, :] = tiles(*ca[p + 1], m, -1.0)
                c0 = tiles(*ca[0], m, -1.0)
                kt = [_nt_dot(tiles(*ab[k], m, 1.0), c0) for k in range(PK)]
                if add_skip:
                    eye = (lax.broadcasted_iota(jnp.int32, (128, 128), 0)
                           == lax.broadcasted_iota(jnp.int32, (128, 128), 1))
                    kt[0] = kt[0] + jnp.where(eye, skip_ref[:, m * 128:(m + 1) * 128], 0.0)
                kt = [k_.astype(BF16) for k_ in kt]
                zero = jnp.zeros((128, 128), BF16)
                for q in range(PK):
                    for p in range(PK):
                        ki_s[m, q * 128:(q + 1) * 128, p * 128:(p + 1) * 128] = kt[p - q] if p >= q else zero
        if init == "carry":
            hs_re[...] = h0r_ref[...]
            hs_im[...] = h0i_ref[...]
        else:
            hs_re[...] = jnp.zeros((NPC, NST), F32)
            hs_im[...] = jnp.zeros((NPC, NST), F32)
            if init == "state":
                seg0 = SEG - 1 if rev else 0
                for b in range(2):
                    row = b * SEG + seg0
                    hs_re[row:row + 1, :] = h0r_ref[b]
                    hs_im[row:row + 1, :] = h0i_ref[b]

    for cp in fetches(i, slot):
        cp.wait()
    for cp in fetches(jnp.minimum(i + 1, NCH - 1), 1 - slot):
        cp.start()

    if u_mode == "make":
        x3 = inbuf[slot]
        ms = jnp.mean(x3 * x3, axis=-1, keepdims=True)
        gain = g_ref[...] * (1.0 + sc_ref[...])
        u3 = x3 * lax.rsqrt(ms + EPS) * gain[None] + sh_ref[...][None]
        u = u3.reshape(TT * NPC, D).astype(BF16)
        ubuf[slot] = u
    else:
        u = inbuf[slot]

    def step_rows(a, b):
        t = TT - 1 - b if rev else b
        return a[t * NPC:(t + 1) * NPC]

    def packed(m):
        lanes = slice(m * 128, (m + 1) * 128)
        return jnp.concatenate(
            [jnp.concatenate([step_rows(u, PK * j + p)[:, lanes] for p in range(PK)], axis=1) for j in range(NG)],
            axis=0)

    def expand(m):
        res = jnp.dot(packed(m), bp_s[m], preferred_element_type=F32)
        d_re[m][...] = res[:, 0:SW]
        d_im[m][...] = res[:, SW:2 * SW]

    def recur(s):
        lanes = slice(s * SW, (s + 1) * SW)
        ar = ap_r[:, lanes]
        ai = ap_i[:, lanes]
        h = [hs_re[0:8, lanes], hs_im[0:8, lanes], hs_re[8:16, lanes], hs_im[8:16, lanes]]
        for j in range(NG):
            if emit_y:
                jrow = slice(j * NPC, (j + 1) * NPC)
                hb[s][jrow, 0:SW] = jnp.concatenate([h[0], h[2]], axis=0).astype(BF16)
                hb[s][jrow, SW:2 * SW] = jnp.concatenate([h[1], h[3]], axis=0).astype(BF16)
            for q in range(2):
                rows = slice(j * NPC + 8 * q, j * NPC + 8 * q + 8)
                hr, hi = h[2 * q], h[2 * q + 1]
                h[2 * q] = ar * hr - ai * hi + d_re[s][rows, :]
                h[2 * q + 1] = ar * hi + ai * hr + d_im[s][rows, :]
        hs_re[0:8, lanes] = h[0]
        hs_im[0:8, lanes] = h[1]
        hs_re[8:16, lanes] = h[2]
        hs_im[8:16, lanes] = h[3]

    def contract(m):
        yp = _nt_dot(hb[m][...], cp_s[m]) + jnp.dot(packed(m), ki_s[m], preferred_element_type=F32)
        for j in range(NG):
            for p in range(PK):
                b = PK * j + p
                t = TT - 1 - b if rev else b
                ybuf[slot, t, :, m * 128:(m + 1) * 128] = yp[j * NPC:(j + 1) * NPC, p * 128:(p + 1) * 128]

    if emit_edge:
        u0 = step_rows(u, 0)
        for m in range(NSLAB):
            lanes = slice(m * SW, (m + 1) * SW)
            bu0 = jnp.dot(u0[:, m * 128:(m + 1) * 128], bp_s[m, (PK - 1) * 128:PK * 128, :],
                          preferred_element_type=F32)
            hr, hi = hs_re[:, lanes], hs_im[:, lanes]
            ed_re[:, lanes] = a1_r[:, lanes] * hr - a1_i[:, lanes] * hi + bu0[:, 0:SW]
            ed_im[:, lanes] = a1_r[:, lanes] * hi + a1_i[:, lanes] * hr + bu0[:, SW:2 * SW]

    expand(0)
    for s in range(NSLAB):
        if s + 1 < NSLAB:
            expand(s + 1)
        if emit_y and s >= 1:
            contract(s - 1)
        recur(s)
    if emit_y:
        contract(NSLAB - 1)
    for cp in stores(i, slot):
        cp.start()

    if emit_edge:
        @pl.when(i == 0)
        def _edge():
            er_ref[...] = ed_re[...]
            ei_ref[...] = ed_im[...]

    @pl.when(i == NCH - 1)
    def _last():
        if emit_fin:
            fr_ref[...] = hs_re[...]
            fi_ref[...] = hs_im[...]
        for cp in fetches(i, 1 - slot) + stores(i, slot) + stores(i, 1 - slot):
            cp.wait()


def _s5_scan(src, col_base, mods_t, layer, g, lam_re, lam_im, dts, bt_re, bt_im, ct_re, ct_im, *,
             jd, rev, trunk, init, h0=None, skip=None, u_mode, emit_y, emit_edge, emit_fin):
    par = lambda i: (jd, 0, 0)
    in_specs = [
        pl.BlockSpec(memory_space=pl.ANY),
        pl.BlockSpec((None, None, NPC, D), lambda i: (layer, 1, trunk, 0)),
        pl.BlockSpec((None, None, NPC, D), lambda i: (layer, 0, trunk, 0)),
        pl.BlockSpec((1, D), lambda i: (0, 0)),
        pl.BlockSpec((None, 1, NST), par),
        pl.BlockSpec((None, 1, NST), par),
        pl.BlockSpec((None, 1, NST), par),
        pl.BlockSpec((None, GC, NST), par),
        pl.BlockSpec((None, GC, NST), par),
        pl.BlockSpec((None, GC, NST), par),
        pl.BlockSpec((None, GC, NST), par),
    ]
    args = [src, mods_t, mods_t, g, lam_re, lam_im, dts, bt_re, bt_im, ct_re, ct_im]
    if init == "state":
        in_specs += [pl.BlockSpec((2, None, 1, NST), lambda i: (0, jd, 0, 0))] * 2
        args += list(h0)
    elif init == "carry":
        in_specs += [pl.BlockSpec((None, NPC, NST), lambda i: (1 if rev else 0, 0, 0))] * 2
        args += list(h0)
    if skip is not None:
        in_specs.append(pl.BlockSpec((1, D), lambda i: (0, 0)))
        args.append(skip)
    out_specs, out_shape = [], []
    slab = pltpu.VMEM((GROWS, SW), F32)
    scratch = [slab] * (2 * NSLAB) + [
        pltpu.VMEM((NSLAB, PK * 128, 2 * SW), BF16),
        pltpu.VMEM((8, NST), F32), pltpu.VMEM((8, NST), F32),
        pltpu.VMEM((NPC, NST), F32), pltpu.VMEM((NPC, NST), F32),
        pltpu.VMEM((2, TT, NPC, D), F32) if u_mode == "make" else pltpu.VMEM((2, TT * NPC, D), BF16),
        pltpu.SemaphoreType.DMA((2,))]
    if emit_y:
        out_specs.append(pl.BlockSpec(memory_space=pl.ANY))
        out_shape.append(jax.ShapeDtypeStruct((NPC, TILE, D), F32))
    if u_mode == "make":
        out_specs.append(pl.BlockSpec(memory_space=pl.ANY))
        out_shape.append(jax.ShapeDtypeStruct((NCH, TT * NPC, D), BF16))
        scratch += [pltpu.VMEM((2, TT * NPC, D), BF16), pltpu.SemaphoreType.DMA((2,))]
    if emit_y:
        scratch += [pltpu.VMEM((2, TT, NPC, D), F32), pltpu.SemaphoreType.DMA((2,)),
                    pltpu.VMEM((NSLAB, PK * 128, 2 * SW), BF16), pltpu.VMEM((NSLAB, PK * 128, PK * 128), BF16)]
        scratch += [pltpu.VMEM((GROWS, 2 * SW), BF16)] * NSLAB
    if emit_edge:
        scratch += [pltpu.VMEM((NPC, NST), F32)] * 4
    n_state_outs = 2 * (int(emit_edge) + int(emit_fin))
    out_specs += [pl.BlockSpec((NPC, NST), lambda i: (0, 0))] * n_state_outs
    out_shape += [jax.ShapeDtypeStruct((NPC, NST), F32)] * n_state_outs
    return pl.pallas_call(
        functools.partial(_s5_kernel, rev=rev, col_base=col_base, init=init, u_mode=u_mode, emit_y=emit_y,
                          emit_edge=emit_edge, emit_fin=emit_fin, add_skip=skip is not None),
        grid=(NCH,),
        in_specs=in_specs,
        out_specs=out_specs,
        out_shape=out_shape,
        scratch_shapes=scratch,
        compiler_params=_cparams(1),
        name=("s5_scan" if emit_y else "s5_states") + ("_bwd" if rev else "_fwd") + str(trunk),
    )(*args)


def _carry_kernel(frf_ref, fif_ref, frb_ref, fib_ref, sr_ref, si_ref, lr_ref, li_ref, dt_ref,
                  or_ref, oi_ref, *, j):
    for d, (fr_ref, fi_ref) in enumerate(((frf_ref, fif_ref), (frb_ref, fib_ref))):
        jd = 2 * j + d
        ar, ai = _abar(lr_ref[jd], li_ref[jd], dt_ref[jd])
        for _ in range(8):
            ar, ai = ar * ar - ai * ai, 2.0 * ar * ai
        for b in range(2):
            order = list(range(SEG)) if d == 0 else list(range(SEG - 1, -1, -1))
            r = b * SEG + order[0]
            or_ref[d, r:r + 1, :] = sr_ref[b, jd]
            oi_ref[d, r:r + 1, :] = si_ref[b, jd]
            tr = fr_ref[r:r + 1, :]
            ti = fi_ref[r:r + 1, :]
            for s in order[1:]:
                r = b * SEG + s
                or_ref[d, r:r + 1, :] = tr
                oi_ref[d, r:r + 1, :] = ti
                tr, ti = (fr_ref[r:r + 1, :] + ar * tr - ai * ti,
                          fi_ref[r:r + 1, :] + ar * ti + ai * tr)


def _carry(frf, fif, frb, fib, st_re4, st_im4, lam_re, lam_im, dts, *, j):
    shp = jax.ShapeDtypeStruct((2, NPC, NST), F32)
    return pl.pallas_call(functools.partial(_carry_kernel, j=j), out_shape=[shp, shp], name="s5_carry")(
        frf, fif, frb, fib, st_re4, st_im4, lam_re, lam_im, dts)


GT = 2 * TILE


def _glu_kernel(xa_ref, xb_ref, yfa_ref, yba_ref, yfb_ref, ybb_ref, mod_ref, w_ref, b_ref,
                o_ref, wbf, ge_s, z_s):
    i = pl.program_id(0)

    @pl.when(i == 0)
    def _():
        wbf[...] = w_ref[...].astype(BF16)

    mod = mod_ref[...]
    first_prompt = i < NPC // 2

    nq = 4
    cw = D // nq
    rw = TILE // nq

    def x_rows(r0, r1, c0=0, c1=D):
        return jnp.where(first_prompt, xa_ref[r0:r1, c0:c1], xb_ref[r0:r1, c0:c1])

    def pre(r0, r1):
        y = jnp.where(first_prompt, yfa_ref[r0:r1, :] + yba_ref[r0:r1, :], yfb_ref[r0:r1, :] + ybb_ref[r0:r1, :])
        k1 = -2.0 * math.sqrt(2.0 / math.pi)
        ge = y / (1.0 + jnp.exp(y * (k1 + (k1 * 0.044715) * (y * y))))
        ge_s[r0:r1, :] = ge.astype(BF16)

    def gate(q, j):
        rows = slice(q * TILE, (q + 1) * TILE)
        for c0 in (j * cw, D + j * cw):
            z_s[rows, c0:c0 + cw] = (jnp.dot(ge_s[rows, :], wbf[:, c0:c0 + cw], preferred_element_type=F32)
                                     + b_ref[:, c0:c0 + cw])

    def post(q, j):
        rows = slice(q * TILE, (q + 1) * TILE)
        c0, c1 = j * cw, (j + 1) * cw
        out = z_s[rows, c0:c1] * _sigmoid(z_s[rows, D + c0:D + c1])
        o_ref[rows, c0:c1] = x_rows(q * TILE, (q + 1) * TILE, c0, c1) + mod[2:3, c0:c1] * out

    for j in range(nq):
        pre(j * rw, (j + 1) * rw)
    for j in range(nq):
        gate(0, j)
        pre(TILE + j * rw, TILE + (j + 1) * rw)
    for j in range(nq):
        gate(1, j)
        post(0, j)
    for j in range(nq):
        post(1, j)


def _glu(xa, xb, off_b, ys, mods, layer, w, b):
    nblk = NCOL // 2
    half = NPC // 2
    pa = lambda i: (jnp.minimum(i, half - 1), 0)
    pb = lambda i: (jnp.maximum(i - half, 0), 0)
    return pl.pallas_call(
        _glu_kernel,
        grid=(nblk,),
        in_specs=[pl.BlockSpec((GT, D), pa),
                  pl.BlockSpec((GT, D), lambda i: (off_b // 2 + jnp.maximum(i - half, 0), 0)),
                  pl.BlockSpec((GT, D), pa),
                  pl.BlockSpec((GT, D), pa),
                  pl.BlockSpec((GT, D), pb),
                  pl.BlockSpec((GT, D), pb),
                  pl.BlockSpec((None, None, 6, D), lambda i: (layer, _cond_of_col(2 * i), 0, 0)),
                  pl.BlockSpec((None, D, 2 * D), lambda i: (layer // 2, 0, 0)),
                  pl.BlockSpec((1, 2 * D), lambda i: (0, 0))],
        out_specs=pl.BlockSpec((GT, D), lambda i: (i, 0)),
        out_shape=jax.ShapeDtypeStruct((NCOL * TILE, D), F32),
        scratch_shapes=[pltpu.VMEM((D, 2 * D), BF16), pltpu.VMEM((GT, D), BF16), pltpu.VMEM((GT, 2 * D), F32)],
        compiler_params=_cparams(1),
        name="s5_glu",
    )(xa, xb, *ys, mods, w, b)


HG = FGC // 2


def _fourier_kernel(x_ref, mod_ref, g_ref, mc_ref, cl_ref, sl_ref, w_ref, b_ref,
                    o_ref, *scratch, scale, nseg):
    mod = mod_ref[...]
    lane = lax.broadcasted_iota(jnp.int32, (TILE, HG), 1)
    lane0 = lane == 0

    def channel_dft(rows):
        h = _normmod(x_ref[rows, :], g_ref[...], mod[1:2], mod[0:1]).astype(BF16)
        zc, zs, zn = [], [], None
        for q in range(FG):
            z = jnp.dot(h[:, q * FGC:(q + 1) * FGC], mc_ref[...], preferred_element_type=F32)
            zc.append(z[:, 0:HG].astype(BF16))
            zs.append(z[:, HG:FGC].astype(BF16))
            nyq = jnp.where(lane0, z[:, HG:FGC], 0.0)
            nyq = pltpu.roll(nyq, q, 1) if q else nyq
            zn = nyq if zn is None else zn + nyq
        return jnp.concatenate(zc, axis=1), jnp.concatenate(zs, axis=1), zn.astype(BF16)

    def position_dft(rows, xc, xs, xn):
        a = jnp.dot(cl_ref[...], xc, preferred_element_type=F32)
        bz = jnp.dot(sl_ref[...], xs, preferred_element_type=F32)
        an = jnp.dot(cl_ref[...], xn, preferred_element_type=F32)
        parts = []
        for q in range(FG):
            aq = a[:, q * HG:(q + 1) * HG]
            bq = jnp.where(lane0, 0.0, bz[:, q * HG:(q + 1) * HG])
            nq = pltpu.roll(an, HG - q, 1) if q else an
            parts += [aq - bq, jnp.where(lane0, nq, aq + bq)]
        f = jnp.concatenate(parts, axis=1) * scale
        o = jnp.dot(f.astype(BF16), w_ref[...], preferred_element_type=F32) + b_ref[...]
        o_ref[rows, :] = x_ref[rows, :] + mod[2:3] * o

    if nseg == 1:
        halves = [slice(q * TILE, (q + 1) * TILE) for q in range(2)]
        z = [channel_dft(r) for r in halves]
        for r, zq in zip(halves, z):
            position_dft(r, *zq)
        return

    full = slice(0, TILE)
    xc_s, xs_s, xn_s = scratch
    ph = pl.program_id(1)
    s = pl.program_id(2)

    @pl.when(ph == 0)
    def _():
        r0 = pl.multiple_of(s * TILE, TILE)
        xc, xs, xn = channel_dft(full)
        xc_s[pl.ds(r0, TILE), :] = xc
        xs_s[pl.ds(r0, TILE), :] = xs
        xn_s[pl.ds(r0, TILE), :] = xn

    @pl.when(ph == 1)
    def _():
        position_dft(full, xc_s[...], xs_s[...], xn_s[...])


def _fourier(x, mods, layer, g, mc, cl, sl, w, b, *, nseq, nseg, col_off, cond_off, cond_stride):
    ln = nseg * TILE
    if nseg == 1:
        rows = 2 * TILE
        grid = (nseq // 2,)
        idx = lambda f: (lambda q: f(q, 1, 0))
        x_in = x_out = lambda q: (col_off // 2 + q, 0)
        scratch = []
    else:
        rows = TILE
        grid = (nseq, 2, nseg)
        idx = lambda f: f
        x_in = lambda q, ph, s: (col_off + q * nseg + s, 0)
        x_out = lambda q, ph, s: (col_off + q * nseg + s * ph, 0)
        scratch = [pltpu.VMEM((ln, FG * HG), BF16), pltpu.VMEM((ln, FG * HG), BF16), pltpu.VMEM((ln, HG), BF16)]
    return pl.pallas_call(
        functools.partial(_fourier_kernel, scale=1.0 / math.sqrt(ln * FGC), nseg=nseg),
        grid=grid,
        in_specs=[pl.BlockSpec((rows, D), x_in),
                  pl.BlockSpec((None, None, 6, D), idx(lambda q, ph, s: (layer, cond_off + q * cond_stride, 0, 0))),
                  pl.BlockSpec((1, D), idx(lambda q, ph, s: (0, 0))),
                  pl.BlockSpec((FGC, FGC), idx(lambda q, ph, s: (0, 0))),
                  pl.BlockSpec((TILE, ln), idx(lambda q, ph, s: (s * ph, 0))),
                  pl.BlockSpec((TILE, ln), idx(lambda q, ph, s: (s * ph, 0))),
                  pl.BlockSpec((None, D, D), idx(lambda q, ph, s: (layer // 2, 0, 0))),
                  pl.BlockSpec((1, D), idx(lambda q, ph, s: (0, 0)))],
        out_specs=pl.BlockSpec((rows, D), x_out),
        out_shape=jax.ShapeDtypeStruct((NCOL * TILE, D), F32),
        scratch_shapes=scratch,
        input_output_aliases={0: 0},
        compiler_params=_cparams(len(grid)),
        name="fourier%d" % nseg,
    )(x, mods, g, mc, cl, sl, w, b)


@functools.lru_cache(maxsize=None)
def _dft_mats(n):
    k = np.arange(n, dtype=np.int64)
    ang = ((k[:, None] * k[None, :]) % n).astype(np.float64) * (2.0 * math.pi / n)
    return np.cos(ang).astype(np.float32), np.sin(ang).astype(np.float32)


@functools.lru_cache(maxsize=None)
def _channel_mats():
    c, s = _dft_mats(FGC)
    return np.concatenate([c[:, 0:HG], c[:, HG:HG + 1], s[:, 1:HG]], axis=1)


WCH = DFF // FC


def _ffn_kernel(x_ref, mod_ref, g_ref, wup_hbm, cw_ref, cb_ref, wdn_hbm, gf_ref, *rest, final, layer):
    n_out = 2 if final else 1
    o_refs = rest[:n_out]
    wup_ref, wdn_ref, stg_u, stg_d, sem_u, sem_d = rest[n_out:]
    i = pl.program_id(0)

    @pl.when(i == 0)
    def _load_weights():
        def copies(c):
            slot = c % 2
            return (pltpu.make_async_copy(wup_hbm.at[layer, :, pl.ds(c * 2 * FC, 2 * FC)], stg_u.at[slot],
                                          sem_u.at[slot]),
                    pltpu.make_async_copy(wdn_hbm.at[layer, pl.ds(c * FC, FC), :], stg_d.at[slot],
                                          sem_d.at[slot]))

        for cp in copies(0):
            cp.start()
        for c in range(WCH):
            if c + 1 < WCH:
                for cp in copies(c + 1):
                    cp.start()
            for cp in copies(c):
                cp.wait()
            wup_ref[:, c * 2 * FC:(c + 1) * 2 * FC] = stg_u[c % 2].astype(BF16)
            wdn_ref[c * FC:(c + 1) * FC, :] = stg_d[c % 2].astype(BF16)

    mod = mod_ref[...]
    prompt = i < NPC // 2
    sub = lax.broadcasted_iota(jnp.int32, (8, FC), 0)
    one = jnp.ones((8, FC), F32)
    first0 = (sub != 0).astype(F32)
    last0 = (sub != 7).astype(F32)
    inner_first = jnp.where(prompt, one, first0)
    inner_last = jnp.where(prompt, one, last0)
    prev_slabs = [(0, first0)] + [(r, inner_first) for r in range(GRID_W, TILE, GRID_W)]
    next_slabs = [(r - 8, inner_last) for r in range(GRID_W, TILE, GRID_W)] + [(TILE - 8, last0)]

    def mask_rows(a, slabs):
        parts, last = [], 0
        for r0, m in slabs:
            if r0 > last:
                parts.append(a[last:r0])
            parts.append(a[r0:r0 + 8] * m)
            last = r0 + 8
        if last < TILE:
            parts.append(a[last:])
        return jnp.concatenate(parts, axis=0)

    def conv(up, off):
        w = cw_ref[:, off:off + FC]
        prev = mask_rows(pltpu.roll(up, 1, 0), prev_slabs)
        nxt = mask_rows(pltpu.roll(up, TILE - 1, 0), next_slabs)
        return w[0:1] * prev + w[1:2] * up + w[2:3] * nxt + cb_ref[:, off:off + FC]

    def up_chunk(h, c):
        og = c * FC
        ov = DFF + c * FC
        return (jnp.dot(h, wup_ref[:, og:og + FC], preferred_element_type=F32),
                jnp.dot(h, wup_ref[:, ov:ov + FC], preferred_element_type=F32))

    def down_chunk(act, c):
        return jnp.dot(act, wdn_ref[c * FC:(c + 1) * FC, :], preferred_element_type=F32)

    nchunk = DFF // FC
    cols = (0, 1)
    rows = [slice(q * TILE, (q + 1) * TILE) for q in cols]
    h = [None, None]
    nxt_up = [None, None]
    act = [None, None]
    acc = [jnp.zeros((TILE, D), F32), jnp.zeros((TILE, D), F32)]
    for q in cols:
        h[q] = _normmod(x_ref[rows[q], :], g_ref[...], mod[4:5], mod[3:4]).astype(BF16)
        nxt_up[q] = up_chunk(h[q], 0)
    for c in range(nchunk):
        for q in cols:
            cur_up = nxt_up[q]
            if c + 1 < nchunk:
                nxt_up[q] = up_chunk(h[q], c + 1)
            if act[q] is not None:
                acc[q] = acc[q] + down_chunk(act[q], c - 1)
            gate = conv(cur_up[0], c * FC)
            val = conv(cur_up[1], DFF + c * FC)
            hg = 0.5 * gate
            act[q] = ((hg + hg * jnp.tanh(hg)) * val).astype(BF16)
    ys = []
    for q in cols:
        acc[q] = acc[q] + down_chunk(act[q], nchunk - 1)
        y = x_ref[rows[q], :] + mod[5:6] * acc[q]
        if final:
            ms = jnp.mean(y * y, axis=-1, keepdims=True)
            y = y * lax.rsqrt(ms + EPS) * gf_ref[...]
        ys.append(y)
    if not final:
        for q in cols:
            o_refs[0][rows[q], :] = ys[q]
    else:
        @pl.when(prompt)
        def _():
            for q in cols:
                o_refs[0][rows[q], :] = ys[q]

        @pl.when(jnp.logical_not(prompt))
        def _():
            for q in cols:
                o_refs[1][rows[q], :] = ys[q]


def _ffn(x, mods, layer, g, wup, cw, cb, wdn, gf, *, final):
    nblk = NCOL // 2
    half = NPC // 2
    if final:
        out_specs = [pl.BlockSpec((2 * TILE, D), lambda i: (jnp.minimum(i, half - 1), 0)),
                     pl.BlockSpec((2 * TILE, D), lambda i: (jnp.maximum(i - half, 0), 0))]
        out_shape = [jax.ShapeDtypeStruct((NPC * TILE, D), F32)] * 2
    else:
        out_specs = pl.BlockSpec((2 * TILE, D), lambda i: (i, 0))
        out_shape = jax.ShapeDtypeStruct((NCOL * TILE, D), F32)
    return pl.pallas_call(
        functools.partial(_ffn_kernel, final=final, layer=layer),
        grid=(nblk,),
        in_specs=[pl.BlockSpec((2 * TILE, D), lambda i: (i, 0)),
                  pl.BlockSpec((None, None, 6, D), lambda i: (layer, _cond_of_col(2 * i), 0, 0)),
                  pl.BlockSpec((1, D), lambda i: (0, 0)),
                  pl.BlockSpec(memory_space=pl.ANY),
                  pl.BlockSpec((None, 3, 2 * DFF), lambda i: (layer, 0, 0)),
                  pl.BlockSpec((None, 1, 2 * DFF), lambda i: (layer, 0, 0)),
                  pl.BlockSpec(memory_space=pl.ANY),
                  pl.BlockSpec((1, D), lambda i: (0, 0))],
        out_specs=out_specs,
        out_shape=out_shape,
        scratch_shapes=[pltpu.VMEM((D, 2 * DFF), BF16), pltpu.VMEM((DFF, D), BF16),
                        pltpu.VMEM((2, D, 2 * FC), F32), pltpu.VMEM((2, FC, D), F32),
                        pltpu.SemaphoreType.DMA((2,)), pltpu.SemaphoreType.DMA((2,))],
        compiler_params=_cparams(1),
        name="ffn",
    )(x, mods, g, wup, cw, cb, wdn, gf)


def kernel(x_prompt, x_sample, state_ssm_re, state_ssm_im, c, c_ctx, w_ada, b_ada, g_mix, g_ffn,
           ssm_lam_re, ssm_lam_im, ssm_log_dt, ssm_b_re, ssm_b_im, ssm_c_re, ssm_c_im, ssm_d,
           w_glu, b_glu, w_fourier, b_fourier, w_up, conv_w, conv_b, w_down, g_final):
    nb = x_prompt.shape[0]
    xp = x_prompt.reshape(NPC * TILE, D)
    xs = x_sample.reshape(NPC * TILE, D)
    x = None

    cond8 = jnp.concatenate([c_ctx[None, :], c, jnp.zeros((5, D), F32)], axis=0)
    mods = _ada(cond8, w_ada, b_ada).reshape(DEPTH, 8, 6, D)
    col_cond = jnp.asarray([0] * NPC + [1] * SEG + [2] * SEG, jnp.int32)
    mods_t = mods.transpose(0, 2, 1, 3)[:, :, col_cond]

    mc = jnp.asarray(_channel_mats()).astype(BF16)
    wf4 = w_fourier.reshape(-1, FG, FGC, D)
    wf_perm = jnp.concatenate([wf4[:, :, :HG + 1], jnp.flip(wf4[:, :, HG + 1:], axis=2)], axis=2)
    wf_bf = wf_perm.astype(BF16).reshape(-1, D, D)
    cl1, sl1 = (jnp.asarray(m).astype(BF16) for m in _dft_mats(TILE))
    cl8, sl8 = (jnp.asarray(m).astype(BF16) for m in _dft_mats(SEG * TILE))

    njd = ssm_lam_re.shape[0] * 2
    lam_re = ssm_lam_re.reshape(njd, 1, NST)
    lam_im = ssm_lam_im.reshape(njd, 1, NST)
    dts = jnp.repeat(ssm_log_dt.reshape(njd, GS), PS, axis=-1).reshape(njd, 1, NST)
    bt_re = ssm_b_re.transpose(0, 1, 4, 2, 3).reshape(njd, GC, NST)
    bt_im = ssm_b_im.transpose(0, 1, 4, 2, 3).reshape(njd, GC, NST)
    ct_re = ssm_c_re.transpose(0, 1, 3, 2, 4).reshape(njd, GC, NST)
    ct_im = ssm_c_im.transpose(0, 1, 3, 2, 4).reshape(njd, GC, NST)
    st_re4 = state_ssm_re.reshape(2, njd, 1, NST)
    st_im4 = state_ssm_im.reshape(2, njd, 1, NST)

    edges = []
    for i in range(DEPTH):
        j = i // 2
        if i % 2 == 0:
            g = g_mix[i][None, :]
            xa, xb, off_b = (xp, xs, 0) if x is None else (x, x, NPC)
            xa3 = xa.reshape(-1, TILE, D)
            xb3 = xb.reshape(-1, TILE, D)
            par = (mods_t, i, g, lam_re, lam_im, dts, bt_re, bt_im, ct_re, ct_im)
            kw = dict(emit_edge=False, emit_fin=True, emit_y=False, trunk=1, init="state", h0=(st_re4, st_im4))
            u_s, frf, fif = _s5_scan(xb3, off_b, *par, jd=2 * j, rev=False, u_mode="make", **kw)
            frb, fib = _s5_scan(u_s, 0, *par, jd=2 * j + 1, rev=True, u_mode="load", **kw)
            carry = _carry(frf, fif, frb, fib, st_re4, st_im4, lam_re, lam_im, dts, j=j)
            skip = ssm_d[j][None, :]
            kw = dict(trunk=0, init="zero", emit_y=True, emit_edge=True, emit_fin=False)
            yfa, u_p, er, ei = _s5_scan(xa3, 0, *par, jd=2 * j, rev=False, u_mode="make", skip=skip, **kw)
            edges += [er, ei]
            yba, er, ei = _s5_scan(u_p, 0, *par, jd=2 * j + 1, rev=True, u_mode="load", **kw)
            edges += [er, ei]
            kw = dict(trunk=1, init="carry", h0=carry, u_mode="load", emit_y=True, emit_edge=False, emit_fin=False)
            (yfb,) = _s5_scan(u_s, 0, *par, jd=2 * j, rev=False, skip=skip, **kw)
            (ybb,) = _s5_scan(u_s, 0, *par, jd=2 * j + 1, rev=True, **kw)
            ys = [y.reshape(NPC * TILE, D) for y in (yfa, yba, yfb, ybb)]
            x = _glu(xa, xb, off_b, ys, mods, i, w_glu, b_glu[j][None, :])
        else:
            g = g_mix[i][None, :]
            bf = b_fourier[j][None, :]
            x = _fourier(x, mods, i, g, mc, cl1, sl1, wf_bf, bf,
                         nseq=NPC, nseg=1, col_off=0, cond_off=0, cond_stride=0)
            x = _fourier(x, mods, i, g, mc, cl8, sl8, wf_bf, bf,
                         nseq=2, nseg=SEG, col_off=NPC, cond_off=1, cond_stride=1)
        x = _ffn(x, mods, i, g_ffn[i][None, :], w_up, conv_w, conv_b.reshape(DEPTH, 1, 2 * DFF),
                 w_down, g_final[None, :], final=(i == DEPTH - 1))

    y_prompt = x[0].reshape(NPC, TILE, D)
    y_sample = x[1].reshape(2, SEG * TILE, D)
    ed = jnp.stack(edges, axis=0).reshape(DEPTH // 2, 2, 2, NPC, NST)[:, :, :, :nb]
    new_re = ed[:, :, 0].transpose(2, 0, 1, 3).reshape(nb, DEPTH // 2, 2, GS, PS)
    new_im = ed[:, :, 1].transpose(2, 0, 1, 3).reshape(nb, DEPTH // 2, 2, GS, PS)
    return (y_prompt, y_sample, new_re, new_im)
```

```python
import functools
import math

import jax
import jax.numpy as jnp
import numpy as np
from jax import lax
from jax.experimental import pallas as pl
from jax.experimental.pallas import tpu as pltpu

F32 = jnp.float32
BF16 = jnp.bfloat16

D = 1024
TILE = 256
NCOL = 32
NPC = 16
SEG = 8
DEPTH = 4
GS = 64
GC = 16
PS = 64
NST = GS * PS
FG = 4
FGC = 256
DFF = 2816
EPS = 1e-6
GRID_W = 64

TT = 32
NCH = TILE // TT
SW = 512
NSLAB = NST // SW
FC = 256

VMEM_LIMIT = 56 * 1024 * 1024


def _cparams(n_axes):
    return pltpu.CompilerParams(dimension_semantics=("arbitrary",) * n_axes,
                                vmem_limit_bytes=VMEM_LIMIT)


def _normmod(x, g, sc, sh):
    ms = jnp.mean(x * x, axis=-1, keepdims=True)
    return x * lax.rsqrt(ms + EPS) * (g * (1.0 + sc)) + sh


def _sigmoid(x):
    return 1.0 / (1.0 + jnp.exp(-x))


def _cond_of_col(i):
    return jnp.where(i < NPC, 0, 1 + (i - NPC) // SEG)


def _ada_kernel(c_ref, w_ref, b_ref, o_ref):
    c = c_ref[...]
    s = (c * _sigmoid(c)).astype(BF16)
    o_ref[...] = jnp.dot(s, w_ref[...].astype(BF16), preferred_element_type=F32) + b_ref[...]


def _ada(cond8, w_ada, b_ada):
    tn = 1536
    return pl.pallas_call(
        _ada_kernel,
        grid=(DEPTH, 6 * D // tn),
        in_specs=[pl.BlockSpec((8, D), lambda l, n: (0, 0)),
                  pl.BlockSpec((None, D, tn), lambda l, n: (l, 0, n)),
                  pl.BlockSpec((None, 1, tn), lambda l, n: (l, 0, n))],
        out_specs=pl.BlockSpec((None, 8, tn), lambda l, n: (l, 0, n)),
        out_shape=jax.ShapeDtypeStruct((DEPTH, 8, 6 * D), F32),
        compiler_params=_cparams(2),
        name="ada",
    )(cond8, w_ada, b_ada.reshape(DEPTH, 1, 6 * D))


def _abar(lr, li, logdt):
    dt = jnp.exp(logdt)
    mag = jnp.exp(lr * dt)
    return mag * jnp.cos(li * dt), mag * jnp.sin(li * dt)


PK = 2
NG = TT // PK
GROWS = NG * NPC


def _nt_dot(a, b):
    return lax.dot_general(a, b, (((1,), (1,)), ((), ())), preferred_element_type=F32)


def _s5_kernel(*refs, rev, col_base, init, u_mode, emit_y, emit_edge, emit_fin, add_skip):
    src_hbm, sc_ref, sh_ref, g_ref, lr_ref, li_ref, dt_ref, btr_ref, bti_ref, ctr_ref, cti_ref = refs[:11]
    pos = 11
    if init != "zero":
        h0r_ref, h0i_ref = refs[pos:pos + 2]
        pos += 2
    if add_skip:
        skip_ref = refs[pos]
        pos += 1
    if emit_y:
        y_hbm = refs[pos]
        pos += 1
    if u_mode == "make":
        u_hbm = refs[pos]
        pos += 1
    if emit_edge:
        er_ref, ei_ref = refs[pos:pos + 2]
        pos += 2
    if emit_fin:
        fr_ref, fi_ref = refs[pos:pos + 2]
        pos += 2
    scratch = refs[pos:]
    d_re = scratch[0:NSLAB]
    d_im = scratch[NSLAB:2 * NSLAB]
    bp_s, ap_r, ap_i, hs_re, hs_im, inbuf, sem_in = scratch[2 * NSLAB:2 * NSLAB + 7]
    pos = 2 * NSLAB + 7
    if u_mode == "make":
        ubuf, sem_u = scratch[pos:pos + 2]
        pos += 2
    if emit_y:
        ybuf, sem_out, cp_s, ki_s = scratch[pos:pos + 4]
        hb = scratch[pos + 4:pos + 4 + NSLAB]
        pos += 4 + NSLAB
    if emit_edge:
        ed_re, ed_im, a1_r, a1_i = scratch[pos:pos + 4]

    i = pl.program_id(0)
    slot = lax.rem(i, 2)

    def chunk(i_):
        return NCH - 1 - i_ if rev else i_

    def in_copies(i_, slot_):
        if u_mode == "load":
            return [pltpu.make_async_copy(src_hbm.at[chunk(i_)], inbuf.at[slot_], sem_in.at[slot_])]
        t0 = pl.multiple_of(chunk(i_) * TT, TT)
        return [pltpu.make_async_copy(src_hbm.at[col_base + c, pl.ds(t0, TT), :],
                                      inbuf.at[slot_, :, c, :], sem_in.at[slot_]) for c in range(NPC)]

    def u_copies(i_, slot_):
        return [pltpu.make_async_copy(ubuf.at[slot_], u_hbm.at[chunk(i_)], sem_u.at[slot_])]

    def y_copies(i_, slot_):
        t0 = pl.multiple_of(chunk(i_) * TT, TT)
        return [pltpu.make_async_copy(ybuf.at[slot_, :, c, :], y_hbm.at[c, pl.ds(t0, TT), :],
                                      sem_out.at[slot_]) for c in range(NPC)]

    fetches = in_copies

    def stores(i_, slot_):
        return (y_copies(i_, slot_) if emit_y else []) + (u_copies(i_, slot_) if u_mode == "make" else [])

    @pl.when(i == 0)
    def _first_fetch():
        for cp in fetches(i, slot):
            cp.start()

    if emit_y or u_mode == "make":
        @pl.when(i >= 2)
        def _out_bufs_free():
            for cp in stores(i, slot):
                cp.wait()

    @pl.when(i == 0)
    def _prep():
        lr = lr_ref[...]
        li = li_ref[...]
        ar, ai = _abar(lr, li, dt_ref[...])
        pw = [(jnp.ones_like(ar), jnp.zeros_like(ai)), (ar, ai)]
        while len(pw) <= PK:
            pr, pi_ = pw[-1]
            pw.append((pr * ar - pi_ * ai, pr * ai + pi_ * ar))
        ap_r[...] = jnp.broadcast_to(pw[PK][0], (8, NST))
        ap_i[...] = jnp.broadcast_to(pw[PK][1], (8, NST))
        if emit_edge:
            a1_r[...] = jnp.broadcast_to(ar, (NPC, NST))
            a1_i[...] = jnp.broadcast_to(ai, (NPC, NST))
        xr = ar - 1.0
        den = lr * lr + li * li
        fr = (xr * lr + ai * li) / den
        fi = (ai * lr - xr * li) / den
        btr = btr_ref[...]
        bti = bti_ref[...]
        bbr = fr * btr - fi * bti
        bbi = fr * bti + fi * btr
        r = lax.broadcasted_iota(jnp.int32, (128, SW), 0)
        c = lax.broadcasted_iota(jnp.int32, (128, SW), 1)
        gmask = ((r >> 4) == (c >> 6)).astype(F32)

        def tiles(vr, vi, m, sign):
            sl = slice(m * SW, (m + 1) * SW)
            tr = jnp.concatenate([vr[:, sl]] * 8, axis=0) * gmask
            ti = jnp.concatenate([vi[:, sl]] * 8, axis=0) * (sign * gmask)
            return jnp.concatenate([tr, ti], axis=1).astype(BF16)

        def times(vr, vi, k):
            pr, pi_ = pw[k]
            return vr * pr - vi * pi_, vr * pi_ + vi * pr

        ab = [times(bbr, bbi, k) for k in range(PK)]
        for m in range(NSLAB):
            for p in range(PK):
                bp_s[m, p * 128:(p + 1) * 128, :] = tiles(*ab[PK - 1 - p], m, 1.0)
        if emit_y:
            ctr = ctr_ref[...]
            cti = cti_ref[...]
            ca = [times(ctr, cti, k) for k in range(PK + 1)]
            for m in range(NSLAB):
                for p in range(PK):
                    cp_s[m, p * 128:(p + 1) * 128, :] = tiles(*ca[p + 1], m, -1.0)
                c0 = tiles(*ca[0], m, -1.0)
                kt = [_nt_dot(tiles(*ab[k], m, 1.0), c0) for k in range(PK)]
                if add_skip:
                    eye = (lax.broadcasted_iota(jnp.int32, (128, 128), 0)
                           == lax.broadcasted_iota(jnp.int32, (128, 128), 1))
                    kt[0] = kt[0] + jnp.where(eye, skip_ref[:, m * 128:(m + 1) * 128], 0.0)
                kt = [k_.astype(BF16) for k_ in kt]
                zero = jnp.zeros((128, 128), BF16)
                for q in range(PK):
                    for p in range(PK):
                        ki_s[m, q * 128:(q + 1) * 128, p * 128:(p + 1) * 128] = kt[p - q] if p >= q else zero
        if init == "carry":
            hs_re[...] = h0r_ref[...]
            hs_im[...] = h0i_ref[...]
        else:
            hs_re[...] = jnp.zeros((NPC, NST), F32)
            hs_im[...] = jnp.zeros((NPC, NST), F32)
            if init == "state":
                seg0 = SEG - 1 if rev else 0
                for b in range(2):
                    row = b * SEG + seg0
                    hs_re[row:row + 1, :] = h0r_ref[b]
                    hs_im[row:row + 1, :] = h0i_ref[b]

    for cp in fetches(i, slot):
        cp.wait()
    for cp in fetches(jnp.minimum(i + 1, NCH - 1), 1 - slot):
        cp.start()

    if u_mode == "make":
        x3 = inbuf[slot]
        ms = jnp.mean(x3 * x3, axis=-1, keepdims=True)
        gain = g_ref[...] * (1.0 + sc_ref[...])
        u3 = x3 * lax.rsqrt(ms + EPS) * gain[None] + sh_ref[...][None]
        u = u3.reshape(TT * NPC, D).astype(BF16)
        ubuf[slot] = u
    else:
        u = inbuf[slot]

    def step_rows(a, b):
        t = TT - 1 - b if rev else b
        return a[t * NPC:(t + 1) * NPC]

    def packed(m):
        lanes = slice(m * 128, (m + 1) * 128)
        return jnp.concatenate(
            [jnp.concatenate([step_rows(u, PK * j + p)[:, lanes] for p in range(PK)], axis=1) for j in range(NG)],
            axis=0)

    def expand(m):
        res = jnp.dot(packed(m), bp_s[m], preferred_element_type=F32)
        d_re[m][...] = res[:, 0:SW]
        d_im[m][...] = res[:, SW:2 * SW]

    def recur(s):
        lanes = slice(s * SW, (s + 1) * SW)
        ar = ap_r[:, lanes]
        ai = ap_i[:, lanes]
        h = [hs_re[0:8, lanes], hs_im[0:8, lanes], hs_re[8:16, lanes], hs_im[8:16, lanes]]
        for j in range(NG):
            if emit_y:
                jrow = slice(j * NPC, (j + 1) * NPC)
                hb[s][jrow, 0:SW] = jnp.concatenate([h[0], h[2]], axis=0).astype(BF16)
                hb[s][jrow, SW:2 * SW] = jnp.concatenate([h[1], h[3]], axis=0).astype(BF16)
            for q in range(2):
                rows = slice(j * NPC + 8 * q, j * NPC + 8 * q + 8)
                hr, hi = h[2 * q], h[2 * q + 1]
                h[2 * q] = ar * hr - ai * hi + d_re[s][rows, :]
                h[2 * q + 1] = ar * hi + ai * hr + d_im[s][rows, :]
        hs_re[0:8, lanes] = h[0]
        hs_im[0:8, lanes] = h[1]
        hs_re[8:16, lanes] = h[2]
        hs_im[8:16, lanes] = h[3]

    def contract(m):
        yp = _nt_dot(hb[m][...], cp_s[m]) + jnp.dot(packed(m), ki_s[m], preferred_element_type=F32)
        for j in range(NG):
            for p in range(PK):
                b = PK * j + p
                t = TT - 1 - b if rev else b
                ybuf[slot, t, :, m * 128:(m + 1) * 128] = yp[j * NPC:(j + 1) * NPC, p * 128:(p + 1) * 128]

    if emit_edge:
        u0 = step_rows(u, 0)
        for m in range(NSLAB):
            lanes = slice(m * SW, (m + 1) * SW)
            bu0 = jnp.dot(u0[:, m * 128:(m + 1) * 128], bp_s[m, (PK - 1) * 128:PK * 128, :],
                          preferred_element_type=F32)
            hr, hi = hs_re[:, lanes], hs_im[:, lanes]
            ed_re[:, lanes] = a1_r[:, lanes] * hr - a1_i[:, lanes] * hi + bu0[:, 0:SW]
            ed_im[:, lanes] = a1_r[:, lanes] * hi + a1_i[:, lanes] * hr + bu0[:, SW:2 * SW]

    expand(0)
    for s in range(NSLAB):
        if s + 1 < NSLAB:
            expand(s + 1)
        if emit_y and s >= 1:
            contract(s - 1)
        recur(s)
    if emit_y:
        contract(NSLAB - 1)
    for cp in stores(i, slot):
        cp.start()

    if emit_edge:
        @pl.when(i == 0)
        def _edge():
            er_ref[...] = ed_re[...]
            ei_ref[...] = ed_im[...]

    @pl.when(i == NCH - 1)
    def _last():
        if emit_fin:
            fr_ref[...] = hs_re[...]
            fi_ref[...] = hs_im[...]
        for cp in fetches(i, 1 - slot) + stores(i, slot) + stores(i, 1 - slot):
            cp.wait()


def _s5_scan(src, col_base, mods_t, layer, g, lam_re, lam_im, dts, bt_re, bt_im, ct_re, ct_im, *,
             jd, rev, trunk, init, h0=None, skip=None, u_mode, emit_y, emit_edge, emit_fin):
    par = lambda i: (jd, 0, 0)
    in_specs = [
        pl.BlockSpec(memory_space=pl.ANY),
        pl.BlockSpec((None, None, NPC, D), lambda i: (layer, 1, trunk, 0)),
        pl.BlockSpec((None, None, NPC, D), lambda i: (layer, 0, trunk, 0)),
        pl.BlockSpec((1, D), lambda i: (0, 0)),
        pl.BlockSpec((None, 1, NST), par),
        pl.BlockSpec((None, 1, NST), par),
        pl.BlockSpec((None, 1, NST), par),
        pl.BlockSpec((None, GC, NST), par),
        pl.BlockSpec((None, GC, NST), par),
        pl.BlockSpec((None, GC, NST), par),
        pl.BlockSpec((None, GC, NST), par),
    ]
    args = [src, mods_t, mods_t, g, lam_re, lam_im, dts, bt_re, bt_im, ct_re, ct_im]
    if init == "state":
        in_specs += [pl.BlockSpec((2, None, 1, NST), lambda i: (0, jd, 0, 0))] * 2
        args += list(h0)
    elif init == "carry":
        in_specs += [pl.BlockSpec((None, NPC, NST), lambda i: (1 if rev else 0, 0, 0))] * 2
        args += list(h0)
    if skip is not None:
        in_specs.append(pl.BlockSpec((1, D), lambda i: (0, 0)))
        args.append(skip)
    out_specs, out_shape = [], []
    slab = pltpu.VMEM((GROWS, SW), F32)
    scratch = [slab] * (2 * NSLAB) + [
        pltpu.VMEM((NSLAB, PK * 128, 2 * SW), BF16),
        pltpu.VMEM((8, NST), F32), pltpu.VMEM((8, NST), F32),
        pltpu.VMEM((NPC, NST), F32), pltpu.VMEM((NPC, NST), F32),
        pltpu.VMEM((2, TT, NPC, D), F32) if u_mode == "make" else pltpu.VMEM((2, TT * NPC, D), BF16),
        pltpu.SemaphoreType.DMA((2,))]
    if emit_y:
        out_specs.append(pl.BlockSpec(memory_space=pl.ANY))
        out_shape.append(jax.ShapeDtypeStruct((NPC, TILE, D), F32))
    if u_mode == "make":
        out_specs.append(pl.BlockSpec(memory_space=pl.ANY))
        out_shape.append(jax.ShapeDtypeStruct((NCH, TT * NPC, D), BF16))
        scratch += [pltpu.VMEM((2, TT * NPC, D), BF16), pltpu.SemaphoreType.DMA((2,))]
    if emit_y:
        scratch += [pltpu.VMEM((2, TT, NPC, D), F32), pltpu.SemaphoreType.DMA((2,)),
                    pltpu.VMEM((NSLAB, PK * 128, 2 * SW), BF16), pltpu.VMEM((NSLAB, PK * 128, PK * 128), BF16)]
        scratch += [pltpu.VMEM((GROWS, 2 * SW), BF16)] * NSLAB
    if emit_edge:
        scratch += [pltpu.VMEM((NPC, NST), F32)] * 4
    n_state_outs = 2 * (int(emit_edge) + int(emit_fin))
    out_specs += [pl.BlockSpec((NPC, NST), lambda i: (0, 0))] * n_state_outs
    out_shape += [jax.ShapeDtypeStruct((NPC, NST), F32)] * n_state_outs
    return pl.pallas_call(
        functools.partial(_s5_kernel, rev=rev, col_base=col_base, init=init, u_mode=u_mode, emit_y=emit_y,
                          emit_edge=emit_edge, emit_fin=emit_fin, add_skip=skip is not None),
        grid=(NCH,),
        in_specs=in_specs,
        out_specs=out_specs,
        out_shape=out_shape,
        scratch_shapes=scratch,
        compiler_params=_cparams(1),
        name=("s5_scan" if emit_y else "s5_states") + ("_bwd" if rev else "_fwd") + str(trunk),
    )(*args)


def _carry_kernel(frf_ref, fif_ref, frb_ref, fib_ref, sr_ref, si_ref, lr_ref, li_ref, dt_ref,
                  or_ref, oi_ref, *, j):
    for d, (fr_ref, fi_ref) in enumerate(((frf_ref, fif_ref), (frb_ref, fib_ref))):
        jd = 2 * j + d
        ar, ai = _abar(lr_ref[jd], li_ref[jd], dt_ref[jd])
        for _ in range(8):
            ar, ai = ar * ar - ai * ai, 2.0 * ar * ai
        for b in range(2):
            order = list(range(SEG)) if d == 0 else list(range(SEG - 1, -1, -1))
            r = b * SEG + order[0]
            or_ref[d, r:r + 1, :] = sr_ref[b, jd]
            oi_ref[d, r:r + 1, :] = si_ref[b, jd]
            tr = fr_ref[r:r + 1, :]
            ti = fi_ref[r:r + 1, :]
            for s in order[1:]:
                r = b * SEG + s
                or_ref[d, r:r + 1, :] = tr
                oi_ref[d, r:r + 1, :] = ti
                tr, ti = (fr_ref[r:r + 1, :] + ar * tr - ai * ti,
                          fi_ref[r:r + 1, :] + ar * ti + ai * tr)


def _carry(frf, fif, frb, fib, st_re4, st_im4, lam_re, lam_im, dts, *, j):
    shp = jax.ShapeDtypeStruct((2, NPC, NST), F32)
    return pl.pallas_call(functools.partial(_carry_kernel, j=j), out_shape=[shp, shp], name="s5_carry")(
        frf, fif, frb, fib, st_re4, st_im4, lam_re, lam_im, dts)


GT = 2 * TILE


def _glu_kernel(xa_ref, xb_ref, yfa_ref, yba_ref, yfb_ref, ybb_ref, mod_ref, w_ref, b_ref,
                o_ref, wbf, ge_s, z_s):
    i = pl.program_id(0)

    @pl.when(i == 0)
    def _():
        wbf[...] = w_ref[...].astype(BF16)

    mod = mod_ref[...]
    first_prompt = i < NPC // 2

    nq = 4
    cw = D // nq
    rw = TILE // nq

    def x_rows(r0, r1, c0=0, c1=D):
        return jnp.where(first_prompt, xa_ref[r0:r1, c0:c1], xb_ref[r0:r1, c0:c1])

    def pre(r0, r1):
        y = jnp.where(first_prompt, yfa_ref[r0:r1, :] + yba_ref[r0:r1, :], yfb_ref[r0:r1, :] + ybb_ref[r0:r1, :])
        k1 = -2.0 * math.sqrt(2.0 / math.pi)
        ge = y / (1.0 + jnp.exp(y * (k1 + (k1 * 0.044715) * (y * y))))
        ge_s[r0:r1, :] = ge.astype(BF16)

    def gate(q, j):
        rows = slice(q * TILE, (q + 1) * TILE)
        for c0 in (j * cw, D + j * cw):
            z_s[rows, c0:c0 + cw] = (jnp.dot(ge_s[rows, :], wbf[:, c0:c0 + cw], preferred_element_type=F32)
                                     + b_ref[:, c0:c0 + cw])

    def post(q, j):
        rows = slice(q * TILE, (q + 1) * TILE)
        c0, c1 = j * cw, (j + 1) * cw
        out = z_s[rows, c0:c1] * _sigmoid(z_s[rows, D + c0:D + c1])
        o_ref[rows, c0:c1] = x_rows(q * TILE, (q + 1) * TILE, c0, c1) + mod[2:3, c0:c1] * out

    for j in range(nq):
        pre(j * rw, (j + 1) * rw)
    for j in range(nq):
        gate(0, j)
        pre(TILE + j * rw, TILE + (j + 1) * rw)
    for j in range(nq):
        gate(1, j)
        post(0, j)
    for j in range(nq):
        post(1, j)


def _glu(xa, xb, off_b, ys, mods, layer, w, b):
    nblk = NCOL // 2
    half = NPC // 2
    pa = lambda i: (jnp.minimum(i, half - 1), 0)
    pb = lambda i: (jnp.maximum(i - half, 0), 0)
    return pl.pallas_call(
        _glu_kernel,
        grid=(nblk,),
        in_specs=[pl.BlockSpec((GT, D), pa),
                  pl.BlockSpec((GT, D), lambda i: (off_b // 2 + jnp.maximum(i - half, 0), 0)),
                  pl.BlockSpec((GT, D), pa),
                  pl.BlockSpec((GT, D), pa),
                  pl.BlockSpec((GT, D), pb),
                  pl.BlockSpec((GT, D), pb),
                  pl.BlockSpec((None, None, 6, D), lambda i: (layer, _cond_of_col(2 * i), 0, 0)),
                  pl.BlockSpec((None, D, 2 * D), lambda i: (layer // 2, 0, 0)),
                  pl.BlockSpec((1, 2 * D), lambda i: (0, 0))],
        out_specs=pl.BlockSpec((GT, D), lambda i: (i, 0)),
        out_shape=jax.ShapeDtypeStruct((NCOL * TILE, D), F32),
        scratch_shapes=[pltpu.VMEM((D, 2 * D), BF16), pltpu.VMEM((GT, D), BF16), pltpu.VMEM((GT, 2 * D), F32)],
        compiler_params=_cparams(1),
        name="s5_glu",
    )(xa, xb, *ys, mods, w, b)


HG = FGC // 2


def _fourier_kernel(x_ref, mod_ref, g_ref, mc_ref, cl_ref, sl_ref, w_ref, b_ref,
                    o_ref, *scratch, scale, nseg):
    mod = mod_ref[...]
    lane = lax.broadcasted_iota(jnp.int32, (TILE, HG), 1)
    lane0 = lane == 0

    def channel_dft(rows):
        h = _normmod(x_ref[rows, :], g_ref[...], mod[1:2], mod[0:1]).astype(BF16)
        zc, zs, zn = [], [], None
        for q in range(FG):
            z = jnp.dot(h[:, q * FGC:(q + 1) * FGC], mc_ref[...], preferred_element_type=F32)
            zc.append(z[:, 0:HG].astype(BF16))
            zs.append(z[:, HG:FGC].astype(BF16))
            nyq = jnp.where(lane0, z[:, HG:FGC], 0.0)
            nyq = pltpu.roll(nyq, q, 1) if q else nyq
            zn = nyq if zn is None else zn + nyq
        return jnp.concatenate(zc, axis=1), jnp.concatenate(zs, axis=1), zn.astype(BF16)

    def position_dft(rows, crow, xc, xs, xn):
        a = jnp.dot(cl_ref[crow, :], xc, preferred_element_type=F32)
        bz = jnp.dot(sl_ref[crow, :], xs, preferred_element_type=F32)
        an = jnp.dot(cl_ref[crow, :], xn, preferred_element_type=F32)
        parts = []
        for q in range(FG):
            aq = a[:, q * HG:(q + 1) * HG]
            bq = jnp.where(lane0, 0.0, bz[:, q * HG:(q + 1) * HG])
            nq = pltpu.roll(an, HG - q, 1) if q else an
            parts += [aq - bq, jnp.where(lane0, nq, aq + bq)]
        f = jnp.concatenate(parts, axis=1) * scale
        o = jnp.dot(f.astype(BF16), w_ref[...], preferred_element_type=F32) + b_ref[...]
        o_ref[rows, :] = x_ref[rows, :] + mod[2:3] * o

    halves = [slice(q * TILE, (q + 1) * TILE) for q in range(2)]
    if nseg == 1:
        z = [channel_dft(r) for r in halves]
        for r, zq in zip(halves, z):
            position_dft(r, slice(0, TILE), *zq)
        return

    xc_s, xs_s, xn_s = scratch
    ph = pl.program_id(1)
    s = pl.program_id(2)

    @pl.when(ph == 0)
    def _():
        for q, r in enumerate(halves):
            r0 = pl.multiple_of((2 * s + q) * TILE, TILE)
            xc, xs, xn = channel_dft(r)
            xc_s[pl.ds(r0, TILE), :] = xc
            xs_s[pl.ds(r0, TILE), :] = xs
            xn_s[pl.ds(r0, TILE), :] = xn

    @pl.when(ph == 1)
    def _():
        for r in halves:
            position_dft(r, r, xc_s[...], xs_s[...], xn_s[...])


def _fourier(x, mods, layer, g, mc, cl, sl, w, b, *, nseq, nseg, col_off, cond_off, cond_stride):
    ln = nseg * TILE
    rows = 2 * TILE
    if nseg == 1:
        crows = TILE
        grid = (nseq // 2,)
        idx = lambda f: (lambda q: f(q, 1, 0))
        x_in = x_out = lambda q: (col_off // 2 + q, 0)
        scratch = []
    else:
        crows = rows
        grid = (nseq, 2, nseg // 2)
        idx = lambda f: f
        x_in = lambda q, ph, s: ((col_off + q * nseg) // 2 + s, 0)
        x_out = lambda q, ph, s: ((col_off + q * nseg) // 2 + s * ph, 0)
        scratch = [pltpu.VMEM((ln, FG * HG), BF16), pltpu.VMEM((ln, FG * HG), BF16), pltpu.VMEM((ln, HG), BF16)]
    return pl.pallas_call(
        functools.partial(_fourier_kernel, scale=1.0 / math.sqrt(ln * FGC), nseg=nseg),
        grid=grid,
        in_specs=[pl.BlockSpec((rows, D), x_in),
                  pl.BlockSpec((None, None, 6, D), idx(lambda q, ph, s: (layer, cond_off + q * cond_stride, 0, 0))),
                  pl.BlockSpec((1, D), idx(lambda q, ph, s: (0, 0))),
                  pl.BlockSpec((FGC, FGC), idx(lambda q, ph, s: (0, 0))),
                  pl.BlockSpec((crows, ln), idx(lambda q, ph, s: (s * ph, 0))),
                  pl.BlockSpec((crows, ln), idx(lambda q, ph, s: (s * ph, 0))),
                  pl.BlockSpec((None, D, D), idx(lambda q, ph, s: (layer // 2, 0, 0))),
                  pl.BlockSpec((1, D), idx(lambda q, ph, s: (0, 0)))],
        out_specs=pl.BlockSpec((rows, D), x_out),
        out_shape=jax.ShapeDtypeStruct((NCOL * TILE, D), F32),
        scratch_shapes=scratch,
        input_output_aliases={0: 0},
        compiler_params=_cparams(len(grid)),
        name="fourier%d" % nseg,
    )(x, mods, g, mc, cl, sl, w, b)


@functools.lru_cache(maxsize=None)
def _dft_mats(n):
    k = np.arange(n, dtype=np.int64)
    ang = ((k[:, None] * k[None, :]) % n).astype(np.float64) * (2.0 * math.pi / n)
    return np.cos(ang).astype(np.float32), np.sin(ang).astype(np.float32)


@functools.lru_cache(maxsize=None)
def _channel_mats():
    c, s = _dft_mats(FGC)
    return np.concatenate([c[:, 0:HG], c[:, HG:HG + 1], s[:, 1:HG]], axis=1)


WCH = DFF // FC


def _ffn_kernel(x_ref, mod_ref, g_ref, wup_hbm, cw_ref, cb_ref, wdn_hbm, gf_ref, *rest, final, layer):
    n_out = 2 if final else 1
    o_refs = rest[:n_out]
    wup_ref, wdn_ref, stg_u, stg_d, sem_u, sem_d = rest[n_out:]
    i = pl.program_id(0)

    @pl.when(i == 0)
    def _load_weights():
        def copies(c):
            slot = c % 2
            return (pltpu.make_async_copy(wup_hbm.at[layer, :, pl.ds(c * 2 * FC, 2 * FC)], stg_u.at[slot],
                                          sem_u.at[slot]),
                    pltpu.make_async_copy(wdn_hbm.at[layer, pl.ds(c * FC, FC), :], stg_d.at[slot],
                                          sem_d.at[slot]))

        for cp in copies(0):
            cp.start()
        for c in range(WCH):
            if c + 1 < WCH:
                for cp in copies(c + 1):
                    cp.start()
            for cp in copies(c):
                cp.wait()
            wup_ref[:, c * 2 * FC:(c + 1) * 2 * FC] = stg_u[c % 2].astype(BF16)
            wdn_ref[c * FC:(c + 1) * FC, :] = stg_d[c % 2].astype(BF16)

    mod = mod_ref[...]
    prompt = i < NPC // 2
    sub = lax.broadcasted_iota(jnp.int32, (8, FC), 0)
    one = jnp.ones((8, FC), F32)
    first0 = (sub != 0).astype(F32)
    last0 = (sub != 7).astype(F32)
    inner_first = jnp.where(prompt, one, first0)
    inner_last = jnp.where(prompt, one, last0)
    prev_slabs = [(0, first0)] + [(r, inner_first) for r in range(GRID_W, TILE, GRID_W)]
    next_slabs = [(r - 8, inner_last) for r in range(GRID_W, TILE, GRID_W)] + [(TILE - 8, last0)]

    def mask_rows(a, slabs):
        parts, last = [], 0
        for r0, m in slabs:
            if r0 > last:
                parts.append(a[last:r0])
            parts.append(a[r0:r0 + 8] * m)
            last = r0 + 8
        if last < TILE:
            parts.append(a[last:])
        return jnp.concatenate(parts, axis=0)

    def conv(up, off):
        w = cw_ref[:, off:off + FC]
        prev = mask_rows(pltpu.roll(up, 1, 0), prev_slabs)
        nxt = mask_rows(pltpu.roll(up, TILE - 1, 0), next_slabs)
        return w[0:1] * prev + w[1:2] * up + w[2:3] * nxt + cb_ref[:, off:off + FC]

    def up_chunk(h, c):
        og = c * FC
        ov = DFF + c * FC
        return (jnp.dot(h, wup_ref[:, og:og + FC], preferred_element_type=F32),
                jnp.dot(h, wup_ref[:, ov:ov + FC], preferred_element_type=F32))

    def down_chunk(act, c):
        return jnp.dot(act, wdn_ref[c * FC:(c + 1) * FC, :], preferred_element_type=F32)

    nchunk = DFF // FC
    cols = (0, 1)
    rows = [slice(q * TILE, (q + 1) * TILE) for q in cols]
    h = [None, None]
    nxt_up = [None, None]
    act = [None, None]
    acc = [jnp.zeros((TILE, D), F32), jnp.zeros((TILE, D), F32)]
    for q in cols:
        h[q] = _normmod(x_ref[rows[q], :], g_ref[...], mod[4:5], mod[3:4]).astype(BF16)
        nxt_up[q] = up_chunk(h[q], 0)
    for c in range(nchunk):
        for q in cols:
            cur_up = nxt_up[q]
            if c + 1 < nchunk:
                nxt_up[q] = up_chunk(h[q], c + 1)
            if act[q] is not None:
                acc[q] = acc[q] + down_chunk(act[q], c - 1)
            gate = conv(cur_up[0], c * FC)
            val = conv(cur_up[1], DFF + c * FC)
            hg = 0.5 * gate
            act[q] = ((hg + hg * jnp.tanh(hg)) * val).astype(BF16)
    ys = []
    for q in cols:
        acc[q] = acc[q] + down_chunk(act[q], nchunk - 1)
        y = x_ref[rows[q], :] + mod[5:6] * acc[q]
        if final:
            ms = jnp.mean(y * y, axis=-1, keepdims=True)
            y = y * lax.rsqrt(ms + EPS) * gf_ref[...]
        ys.append(y)
    if not final:
        for q in cols:
            o_refs[0][rows[q], :] = ys[q]
    else:
        @pl.when(prompt)
        def _():
            for q in cols:
                o_refs[0][rows[q], :] = ys[q]

        @pl.when(jnp.logical_not(prompt))
        def _():
            for q in cols:
                o_refs[1][rows[q], :] = ys[q]


def _ffn(x, mods, layer, g, wup, cw, cb, wdn, gf, *, final):
    nblk = NCOL // 2
    half = NPC // 2
    if final:
        out_specs = [pl.BlockSpec((2 * TILE, D), lambda i: (jnp.minimum(i, half - 1), 0)),
                     pl.BlockSpec((2 * TILE, D), lambda i: (jnp.maximum(i - half, 0), 0))]
        out_shape = [jax.ShapeDtypeStruct((NPC * TILE, D), F32)] * 2
    else:
        out_specs = pl.BlockSpec((2 * TILE, D), lambda i: (i, 0))
        out_shape = jax.ShapeDtypeStruct((NCOL * TILE, D), F32)
    return pl.pallas_call(
        functools.partial(_ffn_kernel, final=final, layer=layer),
        grid=(nblk,),
        in_specs=[pl.BlockSpec((2 * TILE, D), lambda i: (i, 0)),
                  pl.BlockSpec((None, None, 6, D), lambda i: (layer, _cond_of_col(2 * i), 0, 0)),
                  pl.BlockSpec((1, D), lambda i: (0, 0)),
                  pl.BlockSpec(memory_space=pl.ANY),
                  pl.BlockSpec((None, 3, 2 * DFF), lambda i: (layer, 0, 0)),
                  pl.BlockSpec((None, 1, 2 * DFF), lambda i: (layer, 0, 0)),
                  pl.BlockSpec(memory_space=pl.ANY),
                  pl.BlockSpec((1, D), lambda i: (0, 0))],
        out_specs=out_specs,
        out_shape=out_shape,
        scratch_shapes=[pltpu.VMEM((D, 2 * DFF), BF16), pltpu.VMEM((DFF, D), BF16),
                        pltpu.VMEM((2, D, 2 * FC), F32), pltpu.VMEM((2, FC, D), F32),
                        pltpu.SemaphoreType.DMA((2,)), pltpu.SemaphoreType.DMA((2,))],
        compiler_params=_cparams(1),
        name="ffn",
    )(x, mods, g, wup, cw, cb, wdn, gf)


def kernel(x_prompt, x_sample, state_ssm_re, state_ssm_im, c, c_ctx, w_ada, b_ada, g_mix, g_ffn,
           ssm_lam_re, ssm_lam_im, ssm_log_dt, ssm_b_re, ssm_b_im, ssm_c_re, ssm_c_im, ssm_d,
           w_glu, b_glu, w_fourier, b_fourier, w_up, conv_w, conv_b, w_down, g_final):
    nb = x_prompt.shape[0]
    xp = x_prompt.reshape(NPC * TILE, D)
    xs = x_sample.reshape(NPC * TILE, D)
    x = None

    cond8 = jnp.concatenate([c_ctx[None, :], c, jnp.zeros((5, D), F32)], axis=0)
    mods = _ada(cond8, w_ada, b_ada).reshape(DEPTH, 8, 6, D)
    col_cond = jnp.asarray([0] * NPC + [1] * SEG + [2] * SEG, jnp.int32)
    mods_t = mods.transpose(0, 2, 1, 3)[:, :, col_cond]

    mc = jnp.asarray(_channel_mats()).astype(BF16)
    wf4 = w_fourier.reshape(-1, FG, FGC, D)
    wf_perm = jnp.concatenate([wf4[:, :, :HG + 1], jnp.flip(wf4[:, :, HG + 1:], axis=2)], axis=2)
    wf_bf = wf_perm.astype(BF16).reshape(-1, D, D)
    cl1, sl1 = (jnp.asarray(m).astype(BF16) for m in _dft_mats(TILE))
    cl8, sl8 = (jnp.asarray(m).astype(BF16) for m in _dft_mats(SEG * TILE))

    njd = ssm_lam_re.shape[0] * 2
    lam_re = ssm_lam_re.reshape(njd, 1, NST)
    lam_im = ssm_lam_im.reshape(njd, 1, NST)
    dts = jnp.repeat(ssm_log_dt.reshape(njd, GS), PS, axis=-1).reshape(njd, 1, NST)
    bt_re = ssm_b_re.transpose(0, 1, 4, 2, 3).reshape(njd, GC, NST)
    bt_im = ssm_b_im.transpose(0, 1, 4, 2, 3).reshape(njd, GC, NST)
    ct_re = ssm_c_re.transpose(0, 1, 3, 2, 4).reshape(njd, GC, NST)
    ct_im = ssm_c_im.transpose(0, 1, 3, 2, 4).reshape(njd, GC, NST)
    st_re4 = state_ssm_re.reshape(2, njd, 1, NST)
    st_im4 = state_ssm_im.reshape(2, njd, 1, NST)

    edges = []
    for i in range(DEPTH):
        j = i // 2
        if i % 2 == 0:
            g = g_mix[i][None, :]
            xa, xb, off_b = (xp, xs, 0) if x is None else (x, x, NPC)
            xa3 = xa.reshape(-1, TILE, D)
            xb3 = xb.reshape(-1, TILE, D)
            par = (mods_t, i, g, lam_re, lam_im, dts, bt_re, bt_im, ct_re, ct_im)
            kw = dict(emit_edge=False, emit_fin=True, emit_y=False, trunk=1, init="state", h0=(st_re4, st_im4))
            u_s, frf, fif = _s5_scan(xb3, off_b, *par, jd=2 * j, rev=False, u_mode="make", **kw)
            frb, fib = _s5_scan(u_s, 0, *par, jd=2 * j + 1, rev=True, u_mode="load", **kw)
            carry = _carry(frf, fif, frb, fib, st_re4, st_im4, lam_re, lam_im, dts, j=j)
            skip = ssm_d[j][None, :]
            kw = dict(trunk=0, init="zero", emit_y=True, emit_edge=True, emit_fin=False)
            yfa, u_p, er, ei = _s5_scan(xa3, 0, *par, jd=2 * j, rev=False, u_mode="make", skip=skip, **kw)
            edges += [er, ei]
            yba, er, ei = _s5_scan(u_p, 0, *par, jd=2 * j + 1, rev=True, u_mode="load", **kw)
            edges += [er, ei]
            kw = dict(trunk=1, init="carry", h0=carry, u_mode="load", emit_y=True, emit_edge=False, emit_fin=False)
            (yfb,) = _s5_scan(u_s, 0, *par, jd=2 * j, rev=False, skip=skip, **kw)
            (ybb,) = _s5_scan(u_s, 0, *par, jd=2 * j + 1, rev=True, **kw)
            ys = [y.reshape(NPC * TILE, D) for y in (yfa, yba, yfb, ybb)]
            x = _glu(xa, xb, off_b, ys, mods, i, w_glu, b_glu[j][None, :])
        else:
            g = g_mix[i][None, :]
            bf = b_fourier[j][None, :]
            x = _fourier(x, mods, i, g, mc, cl1, sl1, wf_bf, bf,
                         nseq=NPC, nseg=1, col_off=0, cond_off=0, cond_stride=0)
            x = _fourier(x, mods, i, g, mc, cl8, sl8, wf_bf, bf,
                         nseq=2, nseg=SEG, col_off=NPC, cond_off=1, cond_stride=1)
        x = _ffn(x, mods, i, g_ffn[i][None, :], w_up, conv_w, conv_b.reshape(DEPTH, 1, 2 * DFF),
                 w_down, g_final[None, :], final=(i == DEPTH - 1))

    y_prompt = x[0].reshape(NPC, TILE, D)
    y_sample = x[1].reshape(2, SEG * TILE, D)
    ed = jnp.stack(edges, axis=0).reshape(DEPTH // 2, 2, 2, NPC, NST)[:, :, :, :nb]
    new_re = ed[:, :, 0].transpose(2, 0, 1, 3).reshape(nb, DEPTH // 2, 2, GS, PS)
    new_im = ed[:, :, 1].transpose(2, 0, 1, 3).reshape(nb, DEPTH // 2, 2, GS, PS)
    return (y_prompt, y_sample, new_re, new_im)
```

```python
import functools
import math

import jax
import jax.numpy as jnp
import numpy as np
from jax import lax
from jax.experimental import pallas as pl
from jax.experimental.pallas import tpu as pltpu

F32 = jnp.float32
BF16 = jnp.bfloat16

D = 1024
TILE = 256
NCOL = 32
NPC = 16
SEG = 8
DEPTH = 4
GS = 64
GC = 16
PS = 64
NST = GS * PS
FG = 4
FGC = 256
DFF = 2816
EPS = 1e-6
GRID_W = 64

TT = 32
NCH = TILE // TT
SW = 512
NSLAB = NST // SW
FC = 256

VMEM_LIMIT = 56 * 1024 * 1024


def _cparams(n_axes):
    return pltpu.CompilerParams(dimension_semantics=("arbitrary",) * n_axes,
                                vmem_limit_bytes=VMEM_LIMIT)


def _normmod(x, g, sc, sh):
    ms = jnp.mean(x * x, axis=-1, keepdims=True)
    return x * lax.rsqrt(ms + EPS) * (g * (1.0 + sc)) + sh


def _sigmoid(x):
    return 1.0 / (1.0 + jnp.exp(-x))


def _cond_of_col(i):
    return jnp.where(i < NPC, 0, 1 + (i - NPC) // SEG)


def _ada_kernel(c_ref, w_ref, b_ref, o_ref):
    c = c_ref[...]
    s = (c * _sigmoid(c)).astype(BF16)
    o_ref[...] = jnp.dot(s, w_ref[...].astype(BF16), preferred_element_type=F32) + b_ref[...]


def _ada(cond8, w_ada, b_ada):
    tn = 3072
    return pl.pallas_call(
        _ada_kernel,
        grid=(DEPTH, 6 * D // tn),
        in_specs=[pl.BlockSpec((8, D), lambda l, n: (0, 0)),
                  pl.BlockSpec((None, D, tn), lambda l, n: (l, 0, n)),
                  pl.BlockSpec((None, 1, tn), lambda l, n: (l, 0, n))],
        out_specs=pl.BlockSpec((None, 8, tn), lambda l, n: (l, 0, n)),
        out_shape=jax.ShapeDtypeStruct((DEPTH, 8, 6 * D), F32),
        compiler_params=_cparams(2),
        name="ada",
    )(cond8, w_ada, b_ada.reshape(DEPTH, 1, 6 * D))


def _abar(lr, li, logdt):
    dt = jnp.exp(logdt)
    mag = jnp.exp(lr * dt)
    return mag * jnp.cos(li * dt), mag * jnp.sin(li * dt)


PK = 2
NG = TT // PK
GROWS = NG * NPC


def _nt_dot(a, b):
    return lax.dot_general(a, b, (((1,), (1,)), ((), ())), preferred_element_type=F32)


def _s5_kernel(*refs, rev, col_base, init, u_mode, emit_y, emit_edge, emit_fin, add_skip):
    src_hbm, sc_ref, sh_ref, g_ref, lr_ref, li_ref, dt_ref, btr_ref, bti_ref, ctr_ref, cti_ref = refs[:11]
    pos = 11
    if init != "zero":
        h0r_ref, h0i_ref = refs[pos:pos + 2]
        pos += 2
    if add_skip:
        skip_ref = refs[pos]
        pos += 1
    if emit_y:
        y_hbm = refs[pos]
        pos += 1
    if u_mode == "make":
        u_hbm = refs[pos]
        pos += 1
    if emit_edge:
        er_ref, ei_ref = refs[pos:pos + 2]
        pos += 2
    if emit_fin:
        fr_ref, fi_ref = refs[pos:pos + 2]
        pos += 2
    scratch = refs[pos:]
    d_re = scratch[0:NSLAB]
    d_im = scratch[NSLAB:2 * NSLAB]
    bp_s, ap_r, ap_i, hs_re, hs_im, inbuf, sem_in = scratch[2 * NSLAB:2 * NSLAB + 7]
    pos = 2 * NSLAB + 7
    if u_mode == "make":
        ubuf, sem_u = scratch[pos:pos + 2]
        pos += 2
    if emit_y:
        ybuf, sem_out, cp_s, ki_s = scratch[pos:pos + 4]
        hb = scratch[pos + 4:pos + 4 + NSLAB]
        pos += 4 + NSLAB
    if emit_edge:
        ed_re, ed_im, a1_r, a1_i = scratch[pos:pos + 4]

    i = pl.program_id(0)
    slot = lax.rem(i, 2)

    def chunk(i_):
        return NCH - 1 - i_ if rev else i_

    def in_copies(i_, slot_):
        if u_mode == "load":
            return [pltpu.make_async_copy(src_hbm.at[chunk(i_)], inbuf.at[slot_], sem_in.at[slot_])]
        t0 = pl.multiple_of(chunk(i_) * TT, TT)
        return [pltpu.make_async_copy(src_hbm.at[col_base + c, pl.ds(t0, TT), :],
                                      inbuf.at[slot_, :, c, :], sem_in.at[slot_]) for c in range(NPC)]

    def u_copies(i_, slot_):
        return [pltpu.make_async_copy(ubuf.at[slot_], u_hbm.at[chunk(i_)], sem_u.at[slot_])]

    def y_copies(i_, slot_):
        t0 = pl.multiple_of(chunk(i_) * TT, TT)
        return [pltpu.make_async_copy(ybuf.at[slot_, :, c, :], y_hbm.at[c, pl.ds(t0, TT), :],
                                      sem_out.at[slot_]) for c in range(NPC)]

    fetches = in_copies

    def stores(i_, slot_):
        return (y_copies(i_, slot_) if emit_y else []) + (u_copies(i_, slot_) if u_mode == "make" else [])

    @pl.when(i == 0)
    def _first_fetch():
        for cp in fetches(i, slot):
            cp.start()

    if emit_y or u_mode == "make":
        @pl.when(i >= 2)
        def _out_bufs_free():
            for cp in stores(i, slot):
                cp.wait()

    @pl.when(i == 0)
    def _prep():
        lr = lr_ref[...]
        li = li_ref[...]
        ar, ai = _abar(lr, li, dt_ref[...])
        pw = [(jnp.ones_like(ar), jnp.zeros_like(ai)), (ar, ai)]
        while len(pw) <= PK:
            pr, pi_ = pw[-1]
            pw.append((pr * ar - pi_ * ai, pr * ai + pi_ * ar))
        ap_r[...] = jnp.broadcast_to(pw[PK][0], (8, NST))
        ap_i[...] = jnp.broadcast_to(pw[PK][1], (8, NST))
        if emit_edge:
            a1_r[...] = jnp.broadcast_to(ar, (NPC, NST))
            a1_i[...] = jnp.broadcast_to(ai, (NPC, NST))
        xr = ar - 1.0
        den = lr * lr + li * li
        fr = (xr * lr + ai * li) / den
        fi = (ai * lr - xr * li) / den
        btr = btr_ref[...]
        bti = bti_ref[...]
        bbr = fr * btr - fi * bti
        bbi = fr * bti + fi * btr
        r = lax.broadcasted_iota(jnp.int32, (128, SW), 0)
        c = lax.broadcasted_iota(jnp.int32, (128, SW), 1)
        gmask = ((r >> 4) == (c >> 6)).astype(F32)

        def tiles(vr, vi, m, sign):
            sl = slice(m * SW, (m + 1) * SW)
            tr = jnp.concatenate([vr[:, sl]] * 8, axis=0) * gmask
            ti = jnp.concatenate([vi[:, sl]] * 8, axis=0) * (sign * gmask)
            return jnp.concatenate([tr, ti], axis=1).astype(BF16)

        def times(vr, vi, k):
            pr, pi_ = pw[k]
            return vr * pr - vi * pi_, vr * pi_ + vi * pr

        ab = [times(bbr, bbi, k) for k in range(PK)]
        for m in range(NSLAB):
            for p in range(PK):
                bp_s[m, p * 128:(p + 1) * 128, :] = tiles(*ab[PK - 1 - p], m, 1.0)
        if emit_y:
            ctr = ctr_ref[...]
            cti = cti_ref[...]
            ca = [times(ctr, cti, k) for k in range(PK + 1)]
            for m in range(NSLAB):
                for p in range(PK):
                    cp_s[m, p * 128:(p + 1) * 128, :] = tiles(*ca[p + 1], m, -1.0)
                c0 = tiles(*ca[0], m, -1.0)
                kt = [_nt_dot(tiles(*ab[k], m, 1.0), c0) for k in range(PK)]
                if add_skip:
                    eye = (lax.broadcasted_iota(jnp.int32, (128, 128), 0)
                           == lax.broadcasted_iota(jnp.int32, (128, 128), 1))
                    kt[0] = kt[0] + jnp.where(eye, skip_ref[:, m * 128:(m + 1) * 128], 0.0)
                kt = [k_.astype(BF16) for k_ in kt]
                zero = jnp.zeros((128, 128), BF16)
                for q in range(PK):
                    for p in range(PK):
                        ki_s[m, q * 128:(q + 1) * 128, p * 128:(p + 1) * 128] = kt[p - q] if p >= q else zero
        if init == "carry":
            hs_re[...] = h0r_ref[...]
            hs_im[...] = h0i_ref[...]
        else:
            hs_re[...] = jnp.zeros((NPC, NST), F32)
            hs_im[...] = jnp.zeros((NPC, NST), F32)
            if init == "state":
                seg0 = SEG - 1 if rev else 0
                for b in range(2):
                    row = b * SEG + seg0
                    hs_re[row:row + 1, :] = h0r_ref[b]
                    hs_im[row:row + 1, :] = h0i_ref[b]

    for cp in fetches(i, slot):
        cp.wait()
    for cp in fetches(jnp.minimum(i + 1, NCH - 1), 1 - slot):
        cp.start()

    if u_mode == "make":
        x3 = inbuf[slot]
        ms = jnp.mean(x3 * x3, axis=-1, keepdims=True)
        gain = g_ref[...] * (1.0 + sc_ref[...])
        u3 = x3 * lax.rsqrt(ms + EPS) * gain[None] + sh_ref[...][None]
        u = u3.reshape(TT * NPC, D).astype(BF16)
        ubuf[slot] = u
    else:
        u = inbuf[slot]

    def step_rows(a, b):
        t = TT - 1 - b if rev else b
        return a[t * NPC:(t + 1) * NPC]

    def packed(m):
        lanes = slice(m * 128, (m + 1) * 128)
        return jnp.concatenate(
            [jnp.concatenate([step_rows(u, PK * j + p)[:, lanes] for p in range(PK)], axis=1) for j in range(NG)],
            axis=0)

    def expand(m):
        res = jnp.dot(packed(m), bp_s[m], preferred_element_type=F32)
        d_re[m][...] = res[:, 0:SW]
        d_im[m][...] = res[:, SW:2 * SW]

    def recur(s):
        lanes = slice(s * SW, (s + 1) * SW)
        ar = ap_r[:, lanes]
        ai = ap_i[:, lanes]
        h = [hs_re[0:8, lanes], hs_im[0:8, lanes], hs_re[8:16, lanes], hs_im[8:16, lanes]]
        for j in range(NG):
            if emit_y:
                jrow = slice(j * NPC, (j + 1) * NPC)
                hb[s][jrow, 0:SW] = jnp.concatenate([h[0], h[2]], axis=0).astype(BF16)
                hb[s][jrow, SW:2 * SW] = jnp.concatenate([h[1], h[3]], axis=0).astype(BF16)
            for q in range(2):
                rows = slice(j * NPC + 8 * q, j * NPC + 8 * q + 8)
                hr, hi = h[2 * q], h[2 * q + 1]
                h[2 * q] = ar * hr - ai * hi + d_re[s][rows, :]
                h[2 * q + 1] = ar * hi + ai * hr + d_im[s][rows, :]
        hs_re[0:8, lanes] = h[0]
        hs_im[0:8, lanes] = h[1]
        hs_re[8:16, lanes] = h[2]
        hs_im[8:16, lanes] = h[3]

    def contract(m):
        yp = _nt_dot(hb[m][...], cp_s[m]) + jnp.dot(packed(m), ki_s[m], preferred_element_type=F32)
        for j in range(NG):
            for p in range(PK):
                b = PK * j + p
                t = TT - 1 - b if rev else b
                ybuf[slot, t, :, m * 128:(m + 1) * 128] = yp[j * NPC:(j + 1) * NPC, p * 128:(p + 1) * 128]

    if emit_edge:
        u0 = step_rows(u, 0)
        for m in range(NSLAB):
            lanes = slice(m * SW, (m + 1) * SW)
            bu0 = jnp.dot(u0[:, m * 128:(m + 1) * 128], bp_s[m, (PK - 1) * 128:PK * 128, :],
                          preferred_element_type=F32)
            hr, hi = hs_re[:, lanes], hs_im[:, lanes]
            ed_re[:, lanes] = a1_r[:, lanes] * hr - a1_i[:, lanes] * hi + bu0[:, 0:SW]
            ed_im[:, lanes] = a1_r[:, lanes] * hi + a1_i[:, lanes] * hr + bu0[:, SW:2 * SW]

    expand(0)
    for s in range(NSLAB):
        if s + 1 < NSLAB:
            expand(s + 1)
        if emit_y and s >= 1:
            contract(s - 1)
        recur(s)
    if emit_y:
        contract(NSLAB - 1)
    for cp in stores(i, slot):
        cp.start()

    if emit_edge:
        @pl.when(i == 0)
        def _edge():
            er_ref[...] = ed_re[...]
            ei_ref[...] = ed_im[...]

    @pl.when(i == NCH - 1)
    def _last():
        if emit_fin:
            fr_ref[...] = hs_re[...]
            fi_ref[...] = hs_im[...]
        for cp in fetches(i, 1 - slot) + stores(i, slot) + stores(i, 1 - slot):
            cp.wait()


def _s5_scan(src, col_base, mods_t, layer, g, lam_re, lam_im, dts, bt_re, bt_im, ct_re, ct_im, *,
             jd, rev, trunk, init, h0=None, skip=None, u_mode, emit_y, emit_edge, emit_fin):
    par = lambda i: (jd, 0, 0)
    in_specs = [
        pl.BlockSpec(memory_space=pl.ANY),
        pl.BlockSpec((None, None, NPC, D), lambda i: (layer, 1, trunk, 0)),
        pl.BlockSpec((None, None, NPC, D), lambda i: (layer, 0, trunk, 0)),
        pl.BlockSpec((1, D), lambda i: (0, 0)),
        pl.BlockSpec((None, 1, NST), par),
        pl.BlockSpec((None, 1, NST), par),
        pl.BlockSpec((None, 1, NST), par),
        pl.BlockSpec((None, GC, NST), par),
        pl.BlockSpec((None, GC, NST), par),
        pl.BlockSpec((None, GC, NST), par),
        pl.BlockSpec((None, GC, NST), par),
    ]
    args = [src, mods_t, mods_t, g, lam_re, lam_im, dts, bt_re, bt_im, ct_re, ct_im]
    if init == "state":
        in_specs += [pl.BlockSpec((2, None, 1, NST), lambda i: (0, jd, 0, 0))] * 2
        args += list(h0)
    elif init == "carry":
        in_specs += [pl.BlockSpec((None, NPC, NST), lambda i: (1 if rev else 0, 0, 0))] * 2
        args += list(h0)
    if skip is not None:
        in_specs.append(pl.BlockSpec((1, D), lambda i: (0, 0)))
        args.append(skip)
    out_specs, out_shape = [], []
    slab = pltpu.VMEM((GROWS, SW), F32)
    scratch = [slab] * (2 * NSLAB) + [
        pltpu.VMEM((NSLAB, PK * 128, 2 * SW), BF16),
        pltpu.VMEM((8, NST), F32), pltpu.VMEM((8, NST), F32),
        pltpu.VMEM((NPC, NST), F32), pltpu.VMEM((NPC, NST), F32),
        pltpu.VMEM((2, TT, NPC, D), F32) if u_mode == "make" else pltpu.VMEM((2, TT * NPC, D), BF16),
        pltpu.SemaphoreType.DMA((2,))]
    if emit_y:
        out_specs.append(pl.BlockSpec(memory_space=pl.ANY))
        out_shape.append(jax.ShapeDtypeStruct((NPC, TILE, D), F32))
    if u_mode == "make":
        out_specs.append(pl.BlockSpec(memory_space=pl.ANY))
        out_shape.append(jax.ShapeDtypeStruct((NCH, TT * NPC, D), BF16))
        scratch += [pltpu.VMEM((2, TT * NPC, D), BF16), pltpu.SemaphoreType.DMA((2,))]
    if emit_y:
        scratch += [pltpu.VMEM((2, TT, NPC, D), F32), pltpu.SemaphoreType.DMA((2,)),
                    pltpu.VMEM((NSLAB, PK * 128, 2 * SW), BF16), pltpu.VMEM((NSLAB, PK * 128, PK * 128), BF16)]
        scratch += [pltpu.VMEM((GROWS, 2 * SW), BF16)] * NSLAB
    if emit_edge:
        scratch += [pltpu.VMEM((NPC, NST), F32)] * 4
    n_state_outs = 2 * (int(emit_edge) + int(emit_fin))
    out_specs += [pl.BlockSpec((NPC, NST), lambda i: (0, 0))] * n_state_outs
    out_shape += [jax.ShapeDtypeStruct((NPC, NST), F32)] * n_state_outs
    return pl.pallas_call(
        functools.partial(_s5_kernel, rev=rev, col_base=col_base, init=init, u_mode=u_mode, emit_y=emit_y,
                          emit_edge=emit_edge, emit_fin=emit_fin, add_skip=skip is not None),
        grid=(NCH,),
        in_specs=in_specs,
        out_specs=out_specs,
        out_shape=out_shape,
        scratch_shapes=scratch,
        compiler_params=_cparams(1),
        name=("s5_scan" if emit_y else "s5_states") + ("_bwd" if rev else "_fwd") + str(trunk),
    )(*args)


def _carry_kernel(frf_ref, fif_ref, frb_ref, fib_ref, sr_ref, si_ref, lr_ref, li_ref, dt_ref,
                  or_ref, oi_ref, *, j):
    for d, (fr_ref, fi_ref) in enumerate(((frf_ref, fif_ref), (frb_ref, fib_ref))):
        jd = 2 * j + d
        ar, ai = _abar(lr_ref[jd], li_ref[jd], dt_ref[jd])
        for _ in range(8):
            ar, ai = ar * ar - ai * ai, 2.0 * ar * ai
        for b in range(2):
            order = list(range(SEG)) if d == 0 else list(range(SEG - 1, -1, -1))
            r = b * SEG + order[0]
            or_ref[d, r:r + 1, :] = sr_ref[b, jd]
            oi_ref[d, r:r + 1, :] = si_ref[b, jd]
            tr = fr_ref[r:r + 1, :]
            ti = fi_ref[r:r + 1, :]
            for s in order[1:]:
                r = b * SEG + s
                or_ref[d, r:r + 1, :] = tr
                oi_ref[d, r:r + 1, :] = ti
                tr, ti = (fr_ref[r:r + 1, :] + ar * tr - ai * ti,
                          fi_ref[r:r + 1, :] + ar * ti + ai * tr)


def _carry(frf, fif, frb, fib, st_re4, st_im4, lam_re, lam_im, dts, *, j):
    shp = jax.ShapeDtypeStruct((2, NPC, NST), F32)
    return pl.pallas_call(functools.partial(_carry_kernel, j=j), out_shape=[shp, shp], name="s5_carry")(
        frf, fif, frb, fib, st_re4, st_im4, lam_re, lam_im, dts)


GT = 2 * TILE


def _glu_kernel(xa_ref, xb_ref, yfa_ref, yba_ref, yfb_ref, ybb_ref, mod_ref, w_ref, b_ref,
                o_ref, wbf, ge_s, z_s):
    i = pl.program_id(0)

    @pl.when(i == 0)
    def _():
        wbf[...] = w_ref[...].astype(BF16)

    mod = mod_ref[...]
    first_prompt = i < NPC // 2

    nq = 4
    cw = D // nq
    rw = TILE // nq

    def x_rows(r0, r1, c0=0, c1=D):
        return jnp.where(first_prompt, xa_ref[r0:r1, c0:c1], xb_ref[r0:r1, c0:c1])

    def pre(r0, r1):
        y = jnp.where(first_prompt, yfa_ref[r0:r1, :] + yba_ref[r0:r1, :], yfb_ref[r0:r1, :] + ybb_ref[r0:r1, :])
        k1 = -2.0 * math.sqrt(2.0 / math.pi)
        ge = y / (1.0 + jnp.exp(y * (k1 + (k1 * 0.044715) * (y * y))))
        ge_s[r0:r1, :] = ge.astype(BF16)

    def gate(q, j):
        rows = slice(q * TILE, (q + 1) * TILE)
        for c0 in (j * cw, D + j * cw):
            z_s[rows, c0:c0 + cw] = (jnp.dot(ge_s[rows, :], wbf[:, c0:c0 + cw], preferred_element_type=F32)
                                     + b_ref[:, c0:c0 + cw])

    def post(q, j):
        rows = slice(q * TILE, (q + 1) * TILE)
        c0, c1 = j * cw, (j + 1) * cw
        out = z_s[rows, c0:c1] * _sigmoid(z_s[rows, D + c0:D + c1])
        o_ref[rows, c0:c1] = x_rows(q * TILE, (q + 1) * TILE, c0, c1) + mod[2:3, c0:c1] * out

    for j in range(nq):
        pre(j * rw, (j + 1) * rw)
    for j in range(nq):
        gate(0, j)
        pre(TILE + j * rw, TILE + (j + 1) * rw)
    for j in range(nq):
        gate(1, j)
        post(0, j)
    for j in range(nq):
        post(1, j)


def _glu(xa, xb, off_b, ys, mods, layer, w, b):
    nblk = NCOL // 2
    half = NPC // 2
    pa = lambda i: (jnp.minimum(i, half - 1), 0)
    pb = lambda i: (jnp.maximum(i - half, 0), 0)
    return pl.pallas_call(
        _glu_kernel,
        grid=(nblk,),
        in_specs=[pl.BlockSpec((GT, D), pa),
                  pl.BlockSpec((GT, D), lambda i: (off_b // 2 + jnp.maximum(i - half, 0), 0)),
                  pl.BlockSpec((GT, D), pa),
                  pl.BlockSpec((GT, D), pa),
                  pl.BlockSpec((GT, D), pb),
                  pl.BlockSpec((GT, D), pb),
                  pl.BlockSpec((None, None, 6, D), lambda i: (layer, _cond_of_col(2 * i), 0, 0)),
                  pl.BlockSpec((None, D, 2 * D), lambda i: (layer // 2, 0, 0)),
                  pl.BlockSpec((1, 2 * D), lambda i: (0, 0))],
        out_specs=pl.BlockSpec((GT, D), lambda i: (i, 0)),
        out_shape=jax.ShapeDtypeStruct((NCOL * TILE, D), F32),
        scratch_shapes=[pltpu.VMEM((D, 2 * D), BF16), pltpu.VMEM((GT, D), BF16), pltpu.VMEM((GT, 2 * D), F32)],
        compiler_params=_cparams(1),
        name="s5_glu",
    )(xa, xb, *ys, mods, w, b)


HG = FGC // 2


def _fourier_kernel(x_ref, mod_ref, g_ref, mc_ref, cl_ref, sl_ref, w_ref, b_ref,
                    o_ref, *scratch, scale, nseg):
    mod = mod_ref[...]
    lane = lax.broadcasted_iota(jnp.int32, (TILE, HG), 1)
    lane0 = lane == 0

    def channel_dft(rows):
        h = _normmod(x_ref[rows, :], g_ref[...], mod[1:2], mod[0:1]).astype(BF16)
        zc, zs, zn = [], [], None
        for q in range(FG):
            z = jnp.dot(h[:, q * FGC:(q + 1) * FGC], mc_ref[...], preferred_element_type=F32)
            zc.append(z[:, 0:HG].astype(BF16))
            zs.append(z[:, HG:FGC].astype(BF16))
            nyq = jnp.where(lane0, z[:, HG:FGC], 0.0)
            nyq = pltpu.roll(nyq, q, 1) if q else nyq
            zn = nyq if zn is None else zn + nyq
        return jnp.concatenate(zc, axis=1), jnp.concatenate(zs, axis=1), zn.astype(BF16)

    def position_dft(rows, crow, xc, xs, xn):
        a = jnp.dot(cl_ref[crow, :], xc, preferred_element_type=F32)
        bz = jnp.dot(sl_ref[crow, :], xs, preferred_element_type=F32)
        an = jnp.dot(cl_ref[crow, :], xn, preferred_element_type=F32)
        parts = []
        for q in range(FG):
            aq = a[:, q * HG:(q + 1) * HG]
            bq = jnp.where(lane0, 0.0, bz[:, q * HG:(q + 1) * HG])
            nq = pltpu.roll(an, HG - q, 1) if q else an
            parts += [aq - bq, jnp.where(lane0, nq, aq + bq)]
        f = jnp.concatenate(parts, axis=1) * scale
        o = jnp.dot(f.astype(BF16), w_ref[...], preferred_element_type=F32) + b_ref[...]
        o_ref[rows, :] = x_ref[rows, :] + mod[2:3] * o

    halves = [slice(q * TILE, (q + 1) * TILE) for q in range(2)]
    if nseg == 1:
        z = [channel_dft(r) for r in halves]
        for r, zq in zip(halves, z):
            position_dft(r, slice(0, TILE), *zq)
        return

    xc_s, xs_s, xn_s = scratch
    ph = pl.program_id(1)
    s = pl.program_id(2)

    @pl.when(ph == 0)
    def _():
        for q, r in enumerate(halves):
            r0 = pl.multiple_of((2 * s + q) * TILE, TILE)
            xc, xs, xn = channel_dft(r)
            xc_s[pl.ds(r0, TILE), :] = xc
            xs_s[pl.ds(r0, TILE), :] = xs
            xn_s[pl.ds(r0, TILE), :] = xn

    @pl.when(ph == 1)
    def _():
        for r in halves:
            position_dft(r, r, xc_s[...], xs_s[...], xn_s[...])


def _fourier(x, mods, layer, g, mc, cl, sl, w, b, *, nseq, nseg, col_off, cond_off, cond_stride):
    ln = nseg * TILE
    rows = 2 * TILE
    if nseg == 1:
        crows = TILE
        grid = (nseq // 2,)
        idx = lambda f: (lambda q: f(q, 1, 0))
        x_in = x_out = lambda q: (col_off // 2 + q, 0)
        scratch = []
    else:
        crows = rows
        grid = (nseq, 2, nseg // 2)
        idx = lambda f: f
        x_in = lambda q, ph, s: ((col_off + q * nseg) // 2 + s, 0)
        x_out = lambda q, ph, s: ((col_off + q * nseg) // 2 + s * ph, 0)
        scratch = [pltpu.VMEM((ln, FG * HG), BF16), pltpu.VMEM((ln, FG * HG), BF16), pltpu.VMEM((ln, HG), BF16)]
    return pl.pallas_call(
        functools.partial(_fourier_kernel, scale=1.0 / math.sqrt(ln * FGC), nseg=nseg),
        grid=grid,
        in_specs=[pl.BlockSpec((rows, D), x_in),
                  pl.BlockSpec((None, None, 6, D), idx(lambda q, ph, s: (layer, cond_off + q * cond_stride, 0, 0))),
                  pl.BlockSpec((1, D), idx(lambda q, ph, s: (0, 0))),
                  pl.BlockSpec((FGC, FGC), idx(lambda q, ph, s: (0, 0))),
                  pl.BlockSpec((crows, ln), idx(lambda q, ph, s: (s * ph, 0))),
                  pl.BlockSpec((crows, ln), idx(lambda q, ph, s: (s * ph, 0))),
                  pl.BlockSpec((None, D, D), idx(lambda q, ph, s: (layer // 2, 0, 0))),
                  pl.BlockSpec((1, D), idx(lambda q, ph, s: (0, 0)))],
        out_specs=pl.BlockSpec((rows, D), x_out),
        out_shape=jax.ShapeDtypeStruct((NCOL * TILE, D), F32),
        scratch_shapes=scratch,
        input_output_aliases={0: 0},
        compiler_params=_cparams(len(grid)),
        name="fourier%d" % nseg,
    )(x, mods, g, mc, cl, sl, w, b)


@functools.lru_cache(maxsize=None)
def _dft_mats(n):
    k = np.arange(n, dtype=np.int64)
    ang = ((k[:, None] * k[None, :]) % n).astype(np.float64) * (2.0 * math.pi / n)
    return np.cos(ang).astype(np.float32), np.sin(ang).astype(np.float32)


@functools.lru_cache(maxsize=None)
def _channel_mats():
    c, s = _dft_mats(FGC)
    return np.concatenate([c[:, 0:HG], c[:, HG:HG + 1], s[:, 1:HG]], axis=1)


WCH = DFF // FC


def _ffn_kernel(x_ref, mod_ref, g_ref, wup_hbm, cw_ref, cb_ref, wdn_hbm, gf_ref, *rest, final, layer):
    n_out = 2 if final else 1
    o_refs = rest[:n_out]
    wup_ref, wdn_ref, stg_u, stg_d, sem_u, sem_d = rest[n_out:]
    i = pl.program_id(0)

    @pl.when(i == 0)
    def _load_weights():
        def copies(c):
            slot = c % 2
            return (pltpu.make_async_copy(wup_hbm.at[layer, :, pl.ds(c * 2 * FC, 2 * FC)], stg_u.at[slot],
                                          sem_u.at[slot]),
                    pltpu.make_async_copy(wdn_hbm.at[layer, pl.ds(c * FC, FC), :], stg_d.at[slot],
                                          sem_d.at[slot]))

        for cp in copies(0):
            cp.start()
        for c in range(WCH):
            if c + 1 < WCH:
                for cp in copies(c + 1):
                    cp.start()
            for cp in copies(c):
                cp.wait()
            wup_ref[:, c * 2 * FC:(c + 1) * 2 * FC] = stg_u[c % 2].astype(BF16)
            wdn_ref[c * FC:(c + 1) * FC, :] = stg_d[c % 2].astype(BF16)

    mod = mod_ref[...]
    prompt = i < NPC // 2
    sub = lax.broadcasted_iota(jnp.int32, (8, FC), 0)
    one = jnp.ones((8, FC), F32)
    first0 = (sub != 0).astype(F32)
    last0 = (sub != 7).astype(F32)
    inner_first = jnp.where(prompt, one, first0)
    inner_last = jnp.where(prompt, one, last0)
    prev_slabs = [(0, first0)] + [(r, inner_first) for r in range(GRID_W, TILE, GRID_W)]
    next_slabs = [(r - 8, inner_last) for r in range(GRID_W, TILE, GRID_W)] + [(TILE - 8, last0)]

    def mask_rows(a, slabs):
        parts, last = [], 0
        for r0, m in slabs:
            if r0 > last:
                parts.append(a[last:r0])
            parts.append(a[r0:r0 + 8] * m)
            last = r0 + 8
        if last < TILE:
            parts.append(a[last:])
        return jnp.concatenate(parts, axis=0)

    def conv(up, off):
        w = cw_ref[:, off:off + FC]
        prev = mask_rows(pltpu.roll(up, 1, 0), prev_slabs)
        nxt = mask_rows(pltpu.roll(up, TILE - 1, 0), next_slabs)
        return w[0:1] * prev + w[1:2] * up + w[2:3] * nxt + cb_ref[:, off:off + FC]

    def up_chunk(h, c):
        og = c * FC
        ov = DFF + c * FC
        return (jnp.dot(h, wup_ref[:, og:og + FC], preferred_element_type=F32),
                jnp.dot(h, wup_ref[:, ov:ov + FC], preferred_element_type=F32))

    def down_chunk(act, c):
        return jnp.dot(act, wdn_ref[c * FC:(c + 1) * FC, :], preferred_element_type=F32)

    nchunk = DFF // FC
    cols = (0, 1)
    rows = [slice(q * TILE, (q + 1) * TILE) for q in cols]
    h = [None, None]
    nxt_up = [None, None]
    act = [None, None]
    acc = [jnp.zeros((TILE, D), F32), jnp.zeros((TILE, D), F32)]
    for q in cols:
        h[q] = _normmod(x_ref[rows[q], :], g_ref[...], mod[4:5], mod[3:4]).astype(BF16)
        nxt_up[q] = up_chunk(h[q], 0)
    for c in range(nchunk):
        for q in cols:
            cur_up = nxt_up[q]
            if c + 1 < nchunk:
                nxt_up[q] = up_chunk(h[q], c + 1)
            if act[q] is not None:
                acc[q] = acc[q] + down_chunk(act[q], c - 1)
            gate = conv(cur_up[0], c * FC)
            val = conv(cur_up[1], DFF + c * FC)
            hg = 0.5 * gate
            act[q] = ((hg + hg * jnp.tanh(hg)) * val).astype(BF16)
    ys = []
    for q in cols:
        acc[q] = acc[q] + down_chunk(act[q], nchunk - 1)
        y = x_ref[rows[q], :] + mod[5:6] * acc[q]
        if final:
            ms = jnp.mean(y * y, axis=-1, keepdims=True)
            y = y * lax.rsqrt(ms + EPS) * gf_ref[...]
        ys.append(y)
    if not final:
        for q in cols:
            o_refs[0][rows[q], :] = ys[q]
    else:
        @pl.when(prompt)
        def _():
            for q in cols:
                o_refs[0][rows[q], :] = ys[q]

        @pl.when(jnp.logical_not(prompt))
        def _():
            for q in cols:
                o_refs[1][rows[q], :] = ys[q]


def _ffn(x, mods, layer, g, wup, cw, cb, wdn, gf, *, final):
    nblk = NCOL // 2
    half = NPC // 2
    if final:
        out_specs = [pl.BlockSpec((2 * TILE, D), lambda i: (jnp.minimum(i, half - 1), 0)),
                     pl.BlockSpec((2 * TILE, D), lambda i: (jnp.maximum(i - half, 0), 0))]
        out_shape = [jax.ShapeDtypeStruct((NPC * TILE, D), F32)] * 2
    else:
        out_specs = pl.BlockSpec((2 * TILE, D), lambda i: (i, 0))
        out_shape = jax.ShapeDtypeStruct((NCOL * TILE, D), F32)
    return pl.pallas_call(
        functools.partial(_ffn_kernel, final=final, layer=layer),
        grid=(nblk,),
        in_specs=[pl.BlockSpec((2 * TILE, D), lambda i: (i, 0)),
                  pl.BlockSpec((None, None, 6, D), lambda i: (layer, _cond_of_col(2 * i), 0, 0)),
                  pl.BlockSpec((1, D), lambda i: (0, 0)),
                  pl.BlockSpec(memory_space=pl.ANY),
                  pl.BlockSpec((None, 3, 2 * DFF), lambda i: (layer, 0, 0)),
                  pl.BlockSpec((None, 1, 2 * DFF), lambda i: (layer, 0, 0)),
                  pl.BlockSpec(memory_space=pl.ANY),
                  pl.BlockSpec((1, D), lambda i: (0, 0))],
        out_specs=out_specs,
        out_shape=out_shape,
        scratch_shapes=[pltpu.VMEM((D, 2 * DFF), BF16), pltpu.VMEM((DFF, D), BF16),
                        pltpu.VMEM((2, D, 2 * FC), F32), pltpu.VMEM((2, FC, D), F32),
                        pltpu.SemaphoreType.DMA((2,)), pltpu.SemaphoreType.DMA((2,))],
        compiler_params=_cparams(1),
        name="ffn",
    )(x, mods, g, wup, cw, cb, wdn, gf)


def kernel(x_prompt, x_sample, state_ssm_re, state_ssm_im, c, c_ctx, w_ada, b_ada, g_mix, g_ffn,
           ssm_lam_re, ssm_lam_im, ssm_log_dt, ssm_b_re, ssm_b_im, ssm_c_re, ssm_c_im, ssm_d,
           w_glu, b_glu, w_fourier, b_fourier, w_up, conv_w, conv_b, w_down, g_final):
    nb = x_prompt.shape[0]
    xp = x_prompt.reshape(NPC * TILE, D)
    xs = x_sample.reshape(NPC * TILE, D)
    x = None

    cond8 = jnp.concatenate([c_ctx[None, :], c, jnp.zeros((5, D), F32)], axis=0)
    mods = _ada(cond8, w_ada, b_ada).reshape(DEPTH, 8, 6, D)
    col_cond = jnp.asarray([0] * NPC + [1] * SEG + [2] * SEG, jnp.int32)
    mods_t = mods.transpose(0, 2, 1, 3)[:, :, col_cond]

    mc = jnp.asarray(_channel_mats()).astype(BF16)
    wf4 = w_fourier.reshape(-1, FG, FGC, D)
    wf_perm = jnp.concatenate([wf4[:, :, :HG + 1], jnp.flip(wf4[:, :, HG + 1:], axis=2)], axis=2)
    wf_bf = wf_perm.astype(BF16).reshape(-1, D, D)
    cl1, sl1 = (jnp.asarray(m).astype(BF16) for m in _dft_mats(TILE))
    cl8, sl8 = (jnp.asarray(m).astype(BF16) for m in _dft_mats(SEG * TILE))

    njd = ssm_lam_re.shape[0] * 2
    lam_re = ssm_lam_re.reshape(njd, 1, NST)
    lam_im = ssm_lam_im.reshape(njd, 1, NST)
    dts = jnp.repeat(ssm_log_dt.reshape(njd, GS), PS, axis=-1).reshape(njd, 1, NST)
    bt_re = ssm_b_re.transpose(0, 1, 4, 2, 3).reshape(njd, GC, NST)
    bt_im = ssm_b_im.transpose(0, 1, 4, 2, 3).reshape(njd, GC, NST)
    ct_re = ssm_c_re.transpose(0, 1, 3, 2, 4).reshape(njd, GC, NST)
    ct_im = ssm_c_im.transpose(0, 1, 3, 2, 4).reshape(njd, GC, NST)
    st_re4 = state_ssm_re.reshape(2, njd, 1, NST)
    st_im4 = state_ssm_im.reshape(2, njd, 1, NST)

    edges = []
    for i in range(DEPTH):
        j = i // 2
        if i % 2 == 0:
            g = g_mix[i][None, :]
            xa, xb, off_b = (xp, xs, 0) if x is None else (x, x, NPC)
            xa3 = xa.reshape(-1, TILE, D)
            xb3 = xb.reshape(-1, TILE, D)
            par = (mods_t, i, g, lam_re, lam_im, dts, bt_re, bt_im, ct_re, ct_im)
            kw = dict(emit_edge=False, emit_fin=True, emit_y=False, trunk=1, init="state", h0=(st_re4, st_im4))
            u_s, frf, fif = _s5_scan(xb3, off_b, *par, jd=2 * j, rev=False, u_mode="make", **kw)
            frb, fib = _s5_scan(u_s, 0, *par, jd=2 * j + 1, rev=True, u_mode="load", **kw)
            carry = _carry(frf, fif, frb, fib, st_re4, st_im4, lam_re, lam_im, dts, j=j)
            skip = ssm_d[j][None, :]
            kw = dict(trunk=0, init="zero", emit_y=True, emit_edge=True, emit_fin=False)
            yfa, u_p, er, ei = _s5_scan(xa3, 0, *par, jd=2 * j, rev=False, u_mode="make", skip=skip, **kw)
            edges += [er, ei]
            yba, er, ei = _s5_scan(u_p, 0, *par, jd=2 * j + 1, rev=True, u_mode="load", **kw)
            edges += [er, ei]
            kw = dict(trunk=1, init="carry", h0=carry, u_mode="load", emit_y=True, emit_edge=False, emit_fin=False)
            (yfb,) = _s5_scan(u_s, 0, *par, jd=2 * j, rev=False, skip=skip, **kw)
            (ybb,) = _s5_scan(u_s, 0, *par, jd=2 * j + 1, rev=True, **kw)
            ys = [y.reshape(NPC * TILE, D) for y in (yfa, yba, yfb, ybb)]
            x = _glu(xa, xb, off_b, ys, mods, i, w_glu, b_glu[j][None, :])
        else:
            g = g_mix[i][None, :]
            bf = b_fourier[j][None, :]
            x = _fourier(x, mods, i, g, mc, cl1, sl1, wf_bf, bf,
                         nseq=NPC, nseg=1, col_off=0, cond_off=0, cond_stride=0)
            x = _fourier(x, mods, i, g, mc, cl8, sl8, wf_bf, bf,
                         nseq=2, nseg=SEG, col_off=NPC, cond_off=1, cond_stride=1)
        x = _ffn(x, mods, i, g_ffn[i][None, :], w_up, conv_w, conv_b.reshape(DEPTH, 1, 2 * DFF),
                 w_down, g_final[None, :], final=(i == DEPTH - 1))

    y_prompt = x[0].reshape(NPC, TILE, D)
    y_sample = x[1].reshape(2, SEG * TILE, D)
    ed = jnp.stack(edges, axis=0).reshape(DEPTH // 2, 2, 2, NPC, NST)[:, :, :, :nb]
    new_re = ed[:, :, 0].transpose(2, 0, 1, 3).reshape(nb, DEPTH // 2, 2, GS, PS)
    new_im = ed[:, :, 1].transpose(2, 0, 1, 3).reshape(nb, DEPTH // 2, 2, GS, PS)
    return (y_prompt, y_sample, new_re, new_im)
```
